```python
import math
import jax, jax.numpy as jnp
from jax import lax
import numpy as np

D_MODEL = 2048
BATCH = 4
SEQ = 4096
DEPTH = 2

GRID_W = 64
CTX_LEN = 256
ATTN_WIDTH = D_MODEL // 2
DA_HEAD_DIM = 128
DA_HEADS = ATTN_WIDTH // (2 * DA_HEAD_DIM)
SSM_WIDTH = D_MODEL // 4
SSM_GROUP = 16
SSM_GROUPS = SSM_WIDTH // SSM_GROUP
SSM_STATE = 64
FOURIER_WIDTH = D_MODEL - ATTN_WIDTH - SSM_WIDTH
FOURIER_HEADS = 4
FOURIER_GROUP = FOURIER_WIDTH // FOURIER_HEADS
MIX_WIDTH = ATTN_WIDTH + SSM_WIDTH + FOURIER_WIDTH
IN_WIDTH = 3 * ATTN_WIDTH + SSM_WIDTH + FOURIER_WIDTH
D_FF = -(-8 * D_MODEL // (3 * 256)) * 256
N_MOD = 6
Q_BLOCK = 128
ROPE_BASE = 10000.0
ROPE_PAIRS = DA_HEAD_DIM // 4
NORM_EPS = 1e-6
SUBLN_EPS = 1e-5
DT_MIN = 1e-3
DT_MAX = 1e-1
C_SCALE = 0.5

kernel_name = 'hybrid_diffattn_s5_fnet_dit_block'


def rmsnorm(x, g, eps=NORM_EPS):
    xf = x.astype(jnp.float32)
    y = xf * lax.rsqrt(jnp.mean(xf * xf, axis=-1, keepdims=True) + eps)
    return (y * g.astype(jnp.float32)).astype(x.dtype)


def rope_tables(n_tokens):
    rows = n_tokens // GRID_W
    r = jnp.broadcast_to(jnp.arange(rows, dtype=jnp.float32)[:, None], (rows, GRID_W)).reshape(-1)
    col = jnp.broadcast_to(jnp.arange(GRID_W, dtype=jnp.float32)[None, :], (rows, GRID_W)).reshape(-1)
    inv = ROPE_BASE ** (-jnp.arange(ROPE_PAIRS, dtype=jnp.float32) / ROPE_PAIRS)
    ang = jnp.stack([r[:, None] * inv, col[:, None] * inv], axis=1)
    return jnp.cos(ang), jnp.sin(ang)


def apply_rope2d(x, cos, sin):
    b, t, h, m, d = x.shape
    xf = x.astype(jnp.float32).reshape(b, t, h, m, 2, 2, ROPE_PAIRS)
    x1, x2 = xf[..., 0, :], xf[..., 1, :]
    cs = cos[None, :, None, None]
    sn = sin[None, :, None, None]
    out = jnp.stack([x1 * cs - x2 * sn, x2 * cs + x1 * sn], axis=-2)
    return out.reshape(b, t, h, m, d).astype(x.dtype)


def diff_attend(q, k, v, lam):
    s = jnp.einsum('bqhcd,bkhcd->bhcqk', q, k).astype(jnp.float32) * (DA_HEAD_DIM ** -0.5)
    p = jax.nn.softmax(s, axis=-1)
    a = p[:, :, 0] - lam * p[:, :, 1]
    return jnp.einsum('bhqk,bkhe->bqhe', a.astype(v.dtype), v)


def diff_attention(q, k, v, qc, kc, vc, lam_q1, lam_k1, lam_q2, lam_k2, g_subln, lam_init, cos, sin, need_ctx):
    f32 = jnp.float32
    b, t = q.shape[:2]
    lam = (jnp.exp(jnp.sum(lam_q1.astype(f32) * lam_k1.astype(f32)))
           - jnp.exp(jnp.sum(lam_q2.astype(f32) * lam_k2.astype(f32))) + lam_init)
    q = apply_rope2d(q, cos, sin)
    k = apply_rope2d(k, cos, sin)
    k_all = jnp.concatenate([kc, k], axis=1)
    v_all = jnp.concatenate([vc, v], axis=1)
    nq = t // Q_BLOCK
    qb = q.reshape(b, nq, Q_BLOCK, DA_HEADS, 2, DA_HEAD_DIM).swapaxes(0, 1)
    o = lax.map(lambda blk: diff_attend(blk, k_all, v_all, lam), qb)
    o = o.swapaxes(0, 1).reshape(b, t, DA_HEADS, 2 * DA_HEAD_DIM)

    def finish(o_):
        return (rmsnorm(o_, g_subln, SUBLN_EPS) * (1.0 - lam_init)).reshape(o_.shape[0], o_.shape[1], ATTN_WIDTH)

    out = finish(o)
    out_c = finish(diff_attend(qc, kc, vc, lam)) if need_ctx else None
    return out, out_c


def _scan_combine(left, right):
    a1, b1 = left
    a2, b2 = right
    return a1 * a2, a2 * b1 + b2


def linear_recurrence(a_bar, bu, reverse):
    a_full = jnp.broadcast_to(a_bar, bu.shape)
    return lax.associative_scan(_scan_combine, (a_full, bu), reverse=reverse, axis=1)[1]


def s5_bidirectional(u, uc, a_re, a_im, log_dt, b_re, b_im, c_re, c_im, d_skip, w_glu, b_glu, need_ctx):
    f32 = jnp.float32
    bsz, t = u.shape[:2]
    ug = u.astype(f32).reshape(bsz, t, SSM_GROUPS, SSM_GROUP)
    ucg = uc.astype(f32).reshape(bsz, uc.shape[1], SSM_GROUPS, SSM_GROUP)
    dsk = d_skip.astype(f32).reshape(SSM_GROUPS, SSM_GROUP)
    y = dsk * ug
    yc = dsk * ucg if need_ctx else None
    u_cplx = ug.astype(jnp.complex64)
    uc_cplx = ucg.astype(jnp.complex64)
    for direction in range(2):
        rev = direction == 1
        lam_a = lax.complex(a_re[direction].astype(f32), a_im[direction].astype(f32))
        dt = jnp.exp(log_dt[direction].astype(f32))[:, None]
        a_bar = jnp.exp(lam_a * dt)
        b_bar = ((a_bar - 1.0) / lam_a)[:, :, None] * lax.complex(b_re[direction].astype(f32), b_im[direction].astype(f32))
        c_mat = lax.complex(c_re[direction].astype(f32), c_im[direction].astype(f32))
        h_ctx = linear_recurrence(a_bar, jnp.einsum('btgh,gph->btgp', uc_cplx, b_bar), rev)
        h0 = h_ctx[:, 0] if rev else h_ctx[:, -1]
        bu = jnp.einsum('btgh,gph->btgp', u_cplx, b_bar)
        edge = -1 if rev else 0
        bu = bu.at[:, edge].add(a_bar * h0)
        h = linear_recurrence(a_bar, bu, rev)
        y = y + jnp.real(jnp.einsum('btgp,ghp->btgh', h, c_mat))
        if need_ctx:
            yc = yc + jnp.real(jnp.einsum('btgp,ghp->btgh', h_ctx, c_mat))

    def glu(yy):
        g = jax.nn.gelu(yy.reshape(yy.shape[0], yy.shape[1], SSM_WIDTH))
        return g * jax.nn.sigmoid(g @ w_glu.astype(f32) + b_glu.astype(f32))

    out = glu(y).astype(u.dtype)
    out_c = glu(yc).astype(uc.dtype) if need_ctx else None
    return out, out_c


def fourier_mix(u, w_four, b_four):
    bsz, t = u.shape[:2]
    ug = u.astype(jnp.float32).reshape(bsz, t, FOURIER_HEADS, FOURIER_GROUP)
    mixed = jnp.real(jnp.fft.fft2(ug, axes=(1, 3), norm='ortho')).astype(u.dtype)
    out = jnp.einsum('btgc,gcd->btgd', mixed, w_four) + b_four
    return out.reshape(bsz, t, FOURIER_WIDTH)


def split_in(z):
    b, t = z.shape[:2]
    q = z[..., :ATTN_WIDTH].reshape(b, t, DA_HEADS, 2, DA_HEAD_DIM)
    k = z[..., ATTN_WIDTH:2 * ATTN_WIDTH].reshape(b, t, DA_HEADS, 2, DA_HEAD_DIM)
    v = z[..., 2 * ATTN_WIDTH:3 * ATTN_WIDTH].reshape(b, t, DA_HEADS, 2 * DA_HEAD_DIM)
    u_ssm = z[..., 3 * ATTN_WIDTH:3 * ATTN_WIDTH + SSM_WIDTH]
    u_four = z[..., 3 * ATTN_WIDTH + SSM_WIDTH:]
    return q, k, v, u_ssm, u_four


def swiglu(h, w_gate, w_up, w_down):
    return (jax.nn.silu(h @ w_gate) * (h @ w_up)) @ w_down


def setup_inputs(seed: int = 0) -> dict:
    key = jax.random.key(seed)
    ks = jax.random.split(key, 32)
    f32 = jnp.float32
    L, D, G, P, H = DEPTH, D_MODEL, SSM_GROUPS, SSM_STATE, SSM_GROUP

    def nrm(k, shape, scale):
        return jax.random.normal(k, shape, f32) * scale

    n_idx = jnp.arange(P, dtype=f32)
    return {
        'x': nrm(ks[0], (BATCH, SEQ, D), 1.0),
        'c': nrm(ks[1], (BATCH, D), 1.0),
        'ctx': nrm(ks[2], (BATCH, CTX_LEN, D), 1.0),
        'c_ctx': nrm(ks[3], (D,), 1.0),
        'w_mod': nrm(ks[4], (L, D, N_MOD * D), D ** -0.5),
        'b_mod': nrm(ks[5], (L, N_MOD * D), 0.01),
        'g_mix_pre': 1.0 + nrm(ks[6], (L, D), 0.02),
        'g_mix_post': 1.0 + nrm(ks[7], (L, D), 0.02),
        'g_ffn_pre': 1.0 + nrm(ks[8], (L, D), 0.02),
        'g_ffn_post': 1.0 + nrm(ks[9], (L, D), 0.02),
        'w_in': nrm(ks[10], (L, D, IN_WIDTH), D ** -0.5),
        'w_out': nrm(ks[11], (L, MIX_WIDTH, D), MIX_WIDTH ** -0.5),
        'lam_q1': nrm(ks[12], (L, DA_HEAD_DIM), 0.1),
        'lam_k1': nrm(ks[13], (L, DA_HEAD_DIM), 0.1),
        'lam_q2': nrm(ks[14], (L, DA_HEAD_DIM), 0.1),
        'lam_k2': nrm(ks[15], (L, DA_HEAD_DIM), 0.1),
        'g_subln': 1.0 + nrm(ks[16], (L, 2 * DA_HEAD_DIM), 0.02),
        'ssm_a_re': -0.5 + nrm(ks[17], (L, 2, G, P), 0.01),
        'ssm_a_im': math.pi * n_idx + nrm(ks[18], (L, 2, G, P), 0.01),
        'ssm_log_dt': jax.random.uniform(ks[19], (L, 2, G), f32, minval=math.log(DT_MIN), maxval=math.log(DT_MAX)),
        'ssm_b_re': nrm(ks[20], (L, 2, G, P, H), (2 * H) ** -0.5),
        'ssm_b_im': nrm(ks[21], (L, 2, G, P, H), (2 * H) ** -0.5),
        'ssm_c_re': nrm(ks[22], (L, 2, G, H, P), C_SCALE),
        'ssm_c_im': nrm(ks[23], (L, 2, G, H, P), C_SCALE),
        'ssm_d': nrm(ks[24], (L, SSM_WIDTH), 1.0),
        'w_glu': nrm(ks[25], (L, SSM_WIDTH, SSM_WIDTH), SSM_WIDTH ** -0.5),
        'b_glu': nrm(ks[26], (L, SSM_WIDTH), 0.01),
        'w_four': nrm(ks[27], (L, FOURIER_HEADS, FOURIER_GROUP, FOURIER_GROUP), FOURIER_GROUP ** -0.5),
        'b_four': nrm(ks[28], (L, FOURIER_HEADS, FOURIER_GROUP), 0.01),
        'w_gate': nrm(ks[29], (L, D, D_FF), D ** -0.5),
        'w_up': nrm(ks[30], (L, D, D_FF), D ** -0.5),
        'w_down': nrm(ks[31], (L, D_FF, D), D_FF ** -0.5),
    }


def reference(x, c, ctx, c_ctx, w_mod, b_mod, g_mix_pre, g_mix_post, g_ffn_pre, g_ffn_post, w_in, w_out,
              lam_q1, lam_k1, lam_q2, lam_k2, g_subln, ssm_a_re, ssm_a_im, ssm_log_dt, ssm_b_re, ssm_b_im,
              ssm_c_re, ssm_c_im, ssm_d, w_glu, b_glu, w_four, b_four, w_gate, w_up, w_down):
    n_tok = x.shape[1]
    cos, sin = rope_tables(n_tok)
    xl, xc = x, ctx
    for l in range(DEPTH):
        need_ctx = l < DEPTH - 1
        lam_init = 0.8 - 0.6 * math.exp(-0.3 * l)
        m = (jax.nn.silu(c) @ w_mod[l] + b_mod[l]).reshape(c.shape[0], 1, N_MOD, D_MODEL)
        mc = (jax.nn.silu(c_ctx) @ w_mod[l] + b_mod[l]).reshape(1, 1, N_MOD, D_MODEL)

        h = rmsnorm(xl, g_mix_pre[l]) * (1.0 + m[:, :, 1]) + m[:, :, 0]
        hc = rmsnorm(xc, g_mix_pre[l]) * (1.0 + mc[:, :, 1]) + mc[:, :, 0]
        q, k, v, u_ssm, u_four = split_in(h @ w_in[l])
        qc, kc, vc, uc_ssm, uc_four = split_in(hc @ w_in[l])
        att, att_c = diff_attention(q, k, v, qc, kc, vc, lam_q1[l], lam_k1[l], lam_q2[l], lam_k2[l],
                                    g_subln[l], lam_init, cos, sin, need_ctx)
        ssm, ssm_c = s5_bidirectional(u_ssm, uc_ssm, ssm_a_re[l], ssm_a_im[l], ssm_log_dt[l], ssm_b_re[l],
                                      ssm_b_im[l], ssm_c_re[l], ssm_c_im[l], ssm_d[l], w_glu[l], b_glu[l], need_ctx)
        four = fourier_mix(u_four, w_four[l], b_four[l])
        mix = jnp.concatenate([att, ssm, four], axis=-1) @ w_out[l]
        xl = xl + m[:, :, 2] * rmsnorm(mix, g_mix_post[l])

        f = swiglu(rmsnorm(xl, g_ffn_pre[l]) * (1.0 + m[:, :, 4]) + m[:, :, 3], w_gate[l], w_up[l], w_down[l])
        xl = xl + m[:, :, 5] * rmsnorm(f, g_ffn_post[l])

        if need_ctx:
            four_c = fourier_mix(uc_four, w_four[l], b_four[l])
            mix_c = jnp.concatenate([att_c, ssm_c, four_c], axis=-1) @ w_out[l]
            xc = xc + mc[:, :, 2] * rmsnorm(mix_c, g_mix_post[l])
            fc = swiglu(rmsnorm(xc, g_ffn_pre[l]) * (1.0 + mc[:, :, 4]) + mc[:, :, 3], w_gate[l], w_up[l], w_down[l])
            xc = xc + mc[:, :, 5] * rmsnorm(fc, g_ffn_post[l])
    return xl
```

```python
import functools
import math

import jax
import jax.numpy as jnp
import numpy as np
from jax import lax
from jax.experimental import pallas as pl
from jax.experimental.pallas import tpu as pltpu

F32 = jnp.float32
BF16 = jnp.bfloat16

GRID_W = 64
DA_HEAD_DIM = 128
SSM_GROUP = 16
SSM_STATE = 64
FOURIER_HEADS = 4
N_MOD = 6
ROPE_BASE = 10000.0
ROPE_PAIRS = DA_HEAD_DIM // 4
NORM_EPS = 1e-6
SUBLN_EPS = 1e-5

SSM_CHUNK = 16
MOD_ROWS = 8
VMEM_LIMIT = 56 * 1024 * 1024


def _cparams(sem):
    return pltpu.CompilerParams(dimension_semantics=sem, vmem_limit_bytes=VMEM_LIMIT)


def _rms(x, eps):
    return x * lax.rsqrt(jnp.mean(x * x, axis=-1, keepdims=True) + eps)


def _silu(x):
    return x * jax.nn.sigmoid(x)


def _mod_kernel(c_ref, w_ref, b_ref, o_ref):
    s = _silu(c_ref[...]).astype(BF16)
    o_ref[0] = jnp.dot(s, w_ref[0].astype(BF16), preferred_element_type=F32) + b_ref[0]


def _modulation(cs, w_mod, b_mod, tn=1024):
    depth, d, n = w_mod.shape
    return pl.pallas_call(
        _mod_kernel,
        grid=(depth, n // tn),
        in_specs=[
            pl.BlockSpec((MOD_ROWS, d), lambda l, j: (0, 0)),
            pl.BlockSpec((1, d, tn), lambda l, j: (l, 0, j)),
            pl.BlockSpec((1, 1, tn), lambda l, j: (l, 0, j)),
        ],
        out_specs=pl.BlockSpec((1, MOD_ROWS, tn), lambda l, j: (l, 0, j)),
        out_shape=jax.ShapeDtypeStruct((depth, MOD_ROWS, n), F32),
        compiler_params=_cparams(("arbitrary", "arbitrary")),
        name="modulation",
    )(cs, w_mod, b_mod.reshape(depth, 1, n))


def _inproj_kernel(x_ref, mod_ref, g_ref, w_ref, cq_ref, sq_ref, ck_ref, sk_ref,
                   q_ref, k_ref, v_ref, us_ref, uf_ref, *, aw, sw):
    tm = x_ref.shape[0]
    h = _rms(x_ref[...], NORM_EPS) * g_ref[...]
    h = h * (1.0 + mod_ref[0, 1:2, :]) + mod_ref[0, 0:1, :]
    hb = h.astype(BF16)

    lane = lax.broadcasted_iota(jnp.int32, (tm, DA_HEAD_DIM), 1)
    first_half = (lane & (2 * ROPE_PAIRS - 1)) < ROPE_PAIRS

    def rope(z, c, s):
        partner = jnp.where(first_half, pltpu.roll(z, DA_HEAD_DIM - ROPE_PAIRS, 1), pltpu.roll(z, ROPE_PAIRS, 1))
        return z * c + partner * s

    zq = jnp.dot(hb, w_ref[:, 0:aw], preferred_element_type=F32)
    cq, sq = cq_ref[...], sq_ref[...]
    for j in range(aw // DA_HEAD_DIM):
        sl = slice(j * DA_HEAD_DIM, (j + 1) * DA_HEAD_DIM)
        q_ref[:, sl] = rope(zq[:, sl], cq, sq).astype(BF16)
    zk = jnp.dot(hb, w_ref[:, aw:2 * aw], preferred_element_type=F32)
    ck, sk = ck_ref[...], sk_ref[...]
    for j in range(aw // DA_HEAD_DIM):
        sl = slice(j * DA_HEAD_DIM, (j + 1) * DA_HEAD_DIM)
        k_ref[:, sl] = rope(zk[:, sl], ck, sk).astype(BF16)
    v_ref[...] = jnp.dot(hb, w_ref[:, 2 * aw:3 * aw], preferred_element_type=F32).astype(BF16)
    us_ref[...] = jnp.dot(hb, w_ref[:, 3 * aw:3 * aw + sw], preferred_element_type=F32)
    uf_ref[...] = jnp.dot(hb, w_ref[:, 3 * aw + sw:], preferred_element_type=F32).astype(BF16)


def _inproj(x_all, mod, g, w_in, tabs, *, nl, t, b, aw, sw, fw, tm):
    nr, d = x_all.shape
    n_lat = nl // tm
    per_b = t // tm

    def row(i):
        return (i, 0)

    def mod_idx(i):
        return (jnp.where(i < n_lat, i // per_b, b), 0, 0)

    def tab_idx(i):
        return (jnp.where(i < n_lat, i % per_b, per_b), 0)

    tab_spec = pl.BlockSpec((tm, DA_HEAD_DIM), tab_idx)
    return pl.pallas_call(
        functools.partial(_inproj_kernel, aw=aw, sw=sw),
        grid=(nr // tm,),
        in_specs=[
            pl.BlockSpec((tm, d), row),
            pl.BlockSpec((1, N_MOD, d), mod_idx),
            pl.BlockSpec((1, d), lambda i: (0, 0)),
            pl.BlockSpec(w_in.shape, lambda i: (0, 0), pipeline_mode=pl.Buffered(1)),
            tab_spec, tab_spec, tab_spec, tab_spec,
        ],
        out_specs=[
            pl.BlockSpec((tm, aw), row), pl.BlockSpec((tm, aw), row), pl.BlockSpec((tm, aw), row),
            pl.BlockSpec((tm, sw), row), pl.BlockSpec((tm, fw), row),
        ],
        out_shape=[
            jax.ShapeDtypeStruct((nr, aw), BF16), jax.ShapeDtypeStruct((nr, aw), BF16),
            jax.ShapeDtypeStruct((nr, aw), BF16), jax.ShapeDtypeStruct((nr, sw), F32),
            jax.ShapeDtypeStruct((nr, fw), BF16),
        ],
        compiler_params=_cparams(("arbitrary",)),
        name="inproj",
    )(x_all, mod, g, w_in, *tabs)


def _rope_tables(t, tm):
    rows = t // GRID_W
    r = jnp.broadcast_to(jnp.arange(rows, dtype=F32)[:, None], (rows, GRID_W)).reshape(-1)
    col = jnp.broadcast_to(jnp.arange(GRID_W, dtype=F32)[None, :], (rows, GRID_W)).reshape(-1)
    inv = ROPE_BASE ** (-jnp.arange(ROPE_PAIRS, dtype=F32) / ROPE_PAIRS)
    ar, ac = r[:, None] * inv, col[:, None] * inv
    cos = jnp.concatenate([jnp.cos(ar), jnp.cos(ar), jnp.cos(ac), jnp.cos(ac)], axis=1)
    sin = jnp.concatenate([-jnp.sin(ar), jnp.sin(ar), -jnp.sin(ac), jnp.sin(ac)], axis=1)
    cos = jnp.concatenate([cos, jnp.ones((tm, DA_HEAD_DIM), F32)], axis=0)
    sin = jnp.concatenate([sin, jnp.zeros((tm, DA_HEAD_DIM), F32)], axis=0)
    scale = DA_HEAD_DIM ** -0.5
    return cos * scale, sin * scale, cos, sin


def _attn_kernel(*refs, lam_init, n_lat_chunks, tk):
    if n_lat_chunks:
        lam_ref, gs_ref, q_ref, kc_ref, vc_ref, kl_ref, vl_ref, o_ref, m1, l1, a1, m2, l2, a2 = refs
    else:
        lam_ref, gs_ref, q_ref, kc_ref, vc_ref, o_ref, m1, l1, a1, m2, l2, a2 = refs
        kl_ref = vl_ref = None
    hd = DA_HEAD_DIM
    q = q_ref[...]
    qs = (q[:, :hd], q[:, hd:])
    stats = ((m1, l1, a1), (m2, l2, a2))
    for m, l, a in stats:
        m[...] = jnp.full(m.shape, -jnp.inf, F32)
        l[...] = jnp.zeros(l.shape, F32)
        a[...] = jnp.zeros(a.shape, F32)

    def step(k, v):
        for idx in range(2):
            m, l, a = stats[idx]
            s = lax.dot_general(qs[idx], k[:, idx * hd:(idx + 1) * hd], (((1,), (1,)), ((), ())),
                                preferred_element_type=F32)
            m_old = m[...]
            m_new = jnp.maximum(m_old, jnp.max(s, axis=-1, keepdims=True))
            alpha = jnp.exp(m_old - m_new)
            p = jnp.exp(s - m_new)
            l[...] = alpha * l[...] + jnp.sum(p, axis=-1, keepdims=True)
            a[...] = alpha * a[...] + jnp.dot(p.astype(BF16), v, preferred_element_type=F32)
            m[...] = m_new

    step(kc_ref[...], vc_ref[...])
    if n_lat_chunks:
        def body(i, carry):
            off = pl.multiple_of(i * tk, tk)
            step(kl_ref[pl.ds(off, tk), :], vl_ref[pl.ds(off, tk), :])
            return carry
        lax.fori_loop(0, n_lat_chunks, body, 0)

    s1 = jnp.sum(lam_ref[0:1, :] * lam_ref[1:2, :], axis=-1, keepdims=True)
    s2 = jnp.sum(lam_ref[2:3, :] * lam_ref[3:4, :], axis=-1, keepdims=True)
    lam = jnp.exp(s1) - jnp.exp(s2) + lam_init
    o = a1[...] / l1[...] - lam * (a2[...] / l2[...])
    o_ref[...] = (_rms(o, SUBLN_EPS) * gs_ref[...] * (1.0 - lam_init)).astype(BF16)


def _attention(lam4, gs, q, k, v, prev, *, lam_init, nl, t, b, ctx, heads, latent, tq, tk):
    nr, aw = q.shape
    vw = 2 * DA_HEAD_DIM
    ctx_blk0 = nl // ctx
    if latent:
        per_b = t // tq
        grid = (b, heads, per_b)
        q_spec = pl.BlockSpec((tq, vw), lambda bi, h, i: (bi * per_b + i, h))
    else:
        tq = ctx
        grid = (b, heads, 1)
        q_spec = pl.BlockSpec((tq, vw), lambda bi, h, i: (ctx_blk0 + bi, h))
    c_spec = pl.BlockSpec((ctx, vw), lambda bi, h, i: (ctx_blk0 + bi, h))
    l_spec = pl.BlockSpec((t, vw), lambda bi, h, i: (bi, h))
    in_specs = [pl.BlockSpec((4, DA_HEAD_DIM), lambda bi, h, i: (0, 0)),
                pl.BlockSpec((1, vw), lambda bi, h, i: (0, 0)),
                q_spec, c_spec, c_spec]
    args = [lam4, gs, q, k, v]
    if latent:
        in_specs += [l_spec, l_spec]
        args += [k, v]
    aliases = {}
    if prev is not None:
        in_specs.append(pl.BlockSpec(memory_space=pl.ANY))
        aliases = {len(args): 0}
        args.append(prev)
    kern = functools.partial(_attn_kernel, lam_init=lam_init, n_lat_chunks=(t // tk) if latent else 0, tk=tk)
    if prev is not None:
        kern = _drop_last_input(kern, len(args))
    return pl.pallas_call(
        kern,
        grid=grid,
        in_specs=in_specs,
        out_specs=q_spec,
        out_shape=jax.ShapeDtypeStruct((nr, aw), BF16),
        scratch_shapes=[pltpu.VMEM((tq, 1), F32), pltpu.VMEM((tq, 1), F32), pltpu.VMEM((tq, vw), F32),
                        pltpu.VMEM((tq, 1), F32), pltpu.VMEM((tq, 1), F32), pltpu.VMEM((tq, vw), F32)],
        input_output_aliases=aliases,
        compiler_params=_cparams(("arbitrary", "arbitrary", "arbitrary")),
        name="attn_latent" if latent else "attn_ctx",
    )(*args)


def _drop_last_input(kern, n_in):
    def wrapped(*refs):
        return kern(*refs[:n_in - 1], *refs[n_in:])
    return wrapped


def _ssm_weights(a_re, a_im, log_dt, b_re, b_im, c_re, c_im):
    L, H, P = SSM_CHUNK, SSM_GROUP, SSM_STATE
    g = a_re.shape[1]
    npair = g // 2
    lam_a = lax.complex(a_re.astype(F32), a_im.astype(F32))
    dt = jnp.exp(log_dt.astype(F32))[..., None]
    ldt = lam_a * dt
    a_bar = jnp.exp(ldt)
    b_bar = ((a_bar - 1.0) / lam_a)[..., None] * lax.complex(b_re.astype(F32), b_im.astype(F32))
    c_mat = lax.complex(c_re.astype(F32), c_im.astype(F32))
    n = jnp.arange(L + 1, dtype=F32)
    apow = jnp.exp(ldt[..., None] * n)
    hp = lax.Precision.HIGHEST

    kt = jnp.real(jnp.einsum('dghp,dgpt,dgpk->dgthk', c_mat, apow[..., :L], b_bar, precision=hp))
    s_idx = jnp.arange(L)[:, None]
    t_idx = jnp.arange(L)[None, :]
    kf = kt[0][:, jnp.clip(t_idx - s_idx, 0, L - 1)] * (t_idx >= s_idx)[None, :, :, None, None]
    kb = kt[1][:, jnp.clip(s_idx - t_idx, 0, L - 1)] * (s_idx >= t_idx)[None, :, :, None, None]
    w_intra = jnp.transpose(kf + kb, (0, 1, 4, 2, 3)).reshape(g, L * H, L * H)

    steps_in = jnp.stack([L - 1 - jnp.arange(L), jnp.arange(L)])
    ap_in = jnp.take_along_axis(apow, jnp.broadcast_to(steps_in[:, None, None, :], apow.shape[:3] + (L,)), axis=3)
    w_sin = jnp.einsum('dgps,dgph->dgshp', ap_in, b_bar)
    w_sin = jnp.stack([jnp.real(w_sin), jnp.imag(w_sin)], axis=-2)
    w_sin = w_sin.reshape(2, npair, 2, L * H, 2, P)
    eye = jnp.eye(2, dtype=F32)
    w_sin = w_sin[:, :, :, :, :, None, :] * eye[None, None, :, None, None, :, None]
    w_sin = w_sin.reshape(2, npair, 2, L * H, 4 * P)

    steps_out = jnp.stack([jnp.arange(L) + 1, L - jnp.arange(L)])
    ap_out = jnp.take_along_axis(apow, jnp.broadcast_to(steps_out[:, None, None, :], apow.shape[:3] + (L,)), axis=3)
    cn = jnp.einsum('dghp,dgpt->dgpth', c_mat, ap_out).reshape(2, g, P, L * H)
    w_so = jnp.stack([jnp.real(cn), -jnp.imag(cn)], axis=2)
    w_so = w_so.reshape(2, npair, 2, 2, P, L * H)
    w_so = jnp.transpose(w_so, (0, 1, 3, 2, 4, 5))
    w_so = w_so[:, :, :, :, :, None, :] * eye[None, None, None, :, None, :, None]
    w_so = w_so.reshape(2, npair, 4 * P, 2 * L * H)

    return w_intra.astype(BF16), w_sin.astype(BF16), w_so.astype(BF16)


def _ssm_al_layout(a_re, a_im, log_dt):
    P = SSM_STATE
    g = a_re.shape[1]
    lam_a = lax.complex(a_re.astype(F32), a_im.astype(F32))
    dt = jnp.exp(log_dt.astype(F32))[..., None]
    al = jnp.exp(lam_a * dt * float(SSM_CHUNK))
    al = jnp.stack([jnp.real(al), jnp.imag(al)], axis=1)
    al = al.reshape(2, 2, g // 2, 2, P)
    al = jnp.transpose(al, (0, 2, 1, 3, 4))
    return al.reshape(2, 1, g * 2 * P)


def _ssm_in_kernel(u_ref, w_ref, s_ref):
    for d in range(2):
        s_ref[d] = (jnp.dot(u_ref[0], w_ref[d, 0, 0], preferred_element_type=F32)
                    + jnp.dot(u_ref[1], w_ref[d, 0, 1], preferred_element_type=F32))


def _ssm_scan_kernel(s_ref, al_ref, h_ref, *, n_ctx_chunks, n_chunks):
    d = pl.program_id(0)
    nb = s_ref.shape[2]
    width = s_ref.shape[3]
    npair = width // 256
    ars = [al_ref[0, :, k * 256:k * 256 + 128] for k in range(npair)]
    ais = [al_ref[0, :, k * 256 + 128:(k + 1) * 256] for k in range(npair)]

    def body(i, carry):
        c_b = jnp.where(i < n_ctx_chunks, n_ctx_chunks - 1 - i, n_chunks + n_ctx_chunks - 1 - i)
        c = jnp.where(d == 0, i, c_b)
        new = []
        for k in range(npair):
            hr, hi = carry[2 * k], carry[2 * k + 1]
            h_ref[0, c, :, k * 256:k * 256 + 128] = hr
            h_ref[0, c, :, k * 256 + 128:(k + 1) * 256] = hi
            sr = s_ref[0, c, :, k * 256:k * 256 + 128]
            si = s_ref[0, c, :, k * 256 + 128:(k + 1) * 256]
            new.append(ars[k] * hr - ais[k] * hi + sr)
            new.append(ars[k] * hi + ais[k] * hr + si)
        return tuple(new)

    init = tuple(jnp.zeros((nb, 128), F32) for _ in range(2 * npair))
    lax.fori_loop(0, n_chunks, body, init)


def _ssm_out_kernel(u_ref, wi_ref, h_ref, wo_ref, y_ref):
    hf = h_ref[0].astype(BF16)
    hb = h_ref[1].astype(BF16)
    w = SSM_CHUNK * SSM_GROUP
    for e in range(2):
        y_ref[e] = (jnp.dot(u_ref[e], wi_ref[e], preferred_element_type=F32)
                    + jnp.dot(hf, wo_ref[0, 0, :, e * w:(e + 1) * w], preferred_element_type=F32)
                    + jnp.dot(hb, wo_ref[1, 0, :, e * w:(e + 1) * w], preferred_element_type=F32))


def _ssm_conv(us, w_intra, w_sin, w_so, al, *, nl, t, b, ctx):
    L, H, P = SSM_CHUNK, SSM_GROUP, SSM_STATE
    nr, sw = us.shape
    g = sw // H
    npair = g // 2
    ncl, ncc = t // L, ctx // L
    nch = ncl + ncc
    r = nch * b
    cw = L * H
    ul = jnp.transpose(us[:nl].reshape(b, ncl, L, g, H), (3, 1, 0, 2, 4)).reshape(g, ncl, b, cw)
    uc = jnp.transpose(us[nl:].reshape(b, ncc, L, g, H), (3, 1, 0, 2, 4)).reshape(g, ncc, b, cw)
    ut = jnp.concatenate([uc, ul], axis=1).reshape(g, r, cw).astype(BF16)

    s = pl.pallas_call(
        _ssm_in_kernel,
        grid=(npair,),
        in_specs=[pl.BlockSpec((2, r, cw), lambda j: (j, 0, 0)),
                  pl.BlockSpec((2, 1, 2, cw, 4 * P), lambda j: (0, j, 0, 0, 0))],
        out_specs=pl.BlockSpec((2, r, 4 * P), lambda j: (0, 0, j)),
        out_shape=jax.ShapeDtypeStruct((2, r, npair * 4 * P), F32),
        compiler_params=_cparams(("arbitrary",)),
        name="ssm_in",
    )(ut, w_sin)

    cols = npair * 4 * P
    cb = 1024
    hin = pl.pallas_call(
        functools.partial(_ssm_scan_kernel, n_ctx_chunks=ncc, n_chunks=nch),
        grid=(2, cols // cb),
        in_specs=[pl.BlockSpec((1, nch, b, cb), lambda d, j: (d, 0, 0, j)),
                  pl.BlockSpec((1, 1, cb), lambda d, j: (d, 0, j))],
        out_specs=pl.BlockSpec((1, nch, b, cb), lambda d, j: (d, 0, 0, j)),
        out_shape=jax.ShapeDtypeStruct((2, nch, b, cols), F32),
        compiler_params=_cparams(("arbitrary", "arbitrary")),
        name="ssm_scan",
    )(s.reshape(2, nch, b, cols), al)

    y = pl.pallas_call(
        _ssm_out_kernel,
        grid=(npair,),
        in_specs=[pl.BlockSpec((2, r, cw), lambda j: (j, 0, 0)),
                  pl.BlockSpec((2, cw, cw), lambda j: (j, 0, 0)),
                  pl.BlockSpec((2, r, 4 * P), lambda j: (0, 0, j)),
                  pl.BlockSpec((2, 1, 4 * P, 2 * cw), lambda j: (0, j, 0, 0))],
        out_specs=pl.BlockSpec((2, r, cw), lambda j: (j, 0, 0)),
        out_shape=jax.ShapeDtypeStruct((g, r, cw), F32),
        compiler_params=_cparams(("arbitrary",)),
        name="ssm_out",
    )(ut, w_intra, hin.reshape(2, r, cols), w_so)

    y = y.reshape(g, nch, b, L, H)
    yl = jnp.transpose(y[:, ncc:], (2, 1, 3, 0, 4)).reshape(nl, sw)
    yc = jnp.transpose(y[:, :ncc], (2, 1, 3, 0, 4)).reshape(nr - nl, sw)
    return jnp.concatenate([yl, yc], axis=0)


def _fourier_kernel(*refs, groups):
    u_ref, w_ref, cos_ref, sin_ref, b_ref = refs[:5]
    o_ref, a_s, b_s = refs[-3:]
    gc = u_ref.shape[1] // groups

    @pl.when(pl.program_id(1) == 0)
    def _():
        for g in range(groups):
            ab = jnp.dot(u_ref[:, g * gc:(g + 1) * gc], w_ref[g], preferred_element_type=F32)
            a_s[:, g * gc:(g + 1) * gc] = ab[:, :gc].astype(BF16)
            b_s[:, g * gc:(g + 1) * gc] = ab[:, gc:].astype(BF16)

    o_ref[...] = (jnp.dot(cos_ref[...], a_s[...], preferred_element_type=F32)
                  - jnp.dot(sin_ref[...], b_s[...], preferred_element_type=F32)
                  + b_ref[...]).astype(BF16)


def _dft_tables(n):
    k = jnp.arange(n, dtype=jnp.int32)
    ang = ((k[:, None] * k[None, :]) % n).astype(F32) * (2.0 * math.pi / n)
    return jnp.cos(ang).astype(BF16), jnp.sin(ang).astype(BF16)


def _fourier(uf, w_cs, bias, tables, prev, *, row0, seq, b, tm):
    nr, fw = uf.shape
    groups = w_cs.shape[0]
    cos_t, sin_t = tables
    tm = min(tm, seq)
    per_b = seq // tm
    blk0 = row0 // tm
    seq_blk0 = row0 // seq
    in_specs = [
        pl.BlockSpec((seq, fw), lambda bi, i: (seq_blk0 + bi, 0)),
        pl.BlockSpec(w_cs.shape, lambda bi, i: (0, 0, 0)),
        pl.BlockSpec((tm, seq), lambda bi, i: (i, 0)),
        pl.BlockSpec((tm, seq), lambda bi, i: (i, 0)),
        pl.BlockSpec((1, fw), lambda bi, i: (0, 0)),
    ]
    args = [uf, w_cs, cos_t, sin_t, bias]
    aliases = {}
    kern = functools.partial(_fourier_kernel, groups=groups)
    if prev is not None:
        in_specs.append(pl.BlockSpec(memory_space=pl.ANY))
        aliases = {len(args): 0}
        args.append(prev)
    return pl.pallas_call(
        kern,
        grid=(b, per_b),
        in_specs=in_specs,
        out_specs=pl.BlockSpec((tm, fw), lambda bi, i: (blk0 + bi * per_b + i, 0)),
        out_shape=jax.ShapeDtypeStruct((nr, fw), BF16),
        scratch_shapes=[pltpu.VMEM((seq, fw), BF16), pltpu.VMEM((seq, fw), BF16)],
        input_output_aliases=aliases,
        compiler_params=_cparams(("arbitrary", "arbitrary")),
        name="fourier_%d" % seq,
    )(*args)


def _fourier_weights(w_four, seq):
    groups, gc, _ = w_four.shape
    k = np.arange(gc)
    ang = ((k[:, None] * k[None, :]) % gc).astype(np.float64) * (2.0 * np.pi / gc)
    norm = 1.0 / math.sqrt(seq * gc)
    cc = jnp.asarray(np.cos(ang) * norm, F32)
    sc = jnp.asarray(np.sin(ang) * norm, F32)
    hp = lax.Precision.HIGHEST
    wc = jnp.einsum('ck,gkd->gcd', cc, w_four.astype(F32), precision=hp)
    ws = jnp.einsum('ck,gkd->gcd', sc, w_four.astype(F32), precision=hp)
    return jnp.concatenate([wc, ws], axis=-1).astype(BF16)


def _outproj_kernel(att_ref, us_ref, yc_ref, uf_ref, x_ref, mod_ref, wo_ref, wg_ref, bg_ref, dsk_ref,
                    gpost_ref, gpre_ref, xo_ref, h2_ref, *, aw, sw):
    y = dsk_ref[...] * us_ref[...] + yc_ref[...]
    g = jax.nn.gelu(y)
    z = jnp.dot(g.astype(BF16), wg_ref[...], preferred_element_type=F32) + bg_ref[...]
    ssm = (g * jax.nn.sigmoid(z)).astype(BF16)
    mix = (jnp.dot(att_ref[...], wo_ref[0:aw, :], preferred_element_type=F32)
           + jnp.dot(ssm, wo_ref[aw:aw + sw, :], preferred_element_type=F32)
           + jnp.dot(uf_ref[...], wo_ref[aw + sw:, :], preferred_element_type=F32))
    xn = x_ref[...] + mod_ref[0, 2:3, :] * (_rms(mix, NORM_EPS) * gpost_ref[...])
    xo_ref[...] = xn
    h2 = _rms(xn, NORM_EPS) * gpre_ref[...]
    h2_ref[...] = (h2 * (1.0 + mod_ref[0, 4:5, :]) + mod_ref[0, 3:4, :]).astype(BF16)


def _outproj(att, us, yc, four, x_all, mod, w_out, w_glu, b_glu, dsk, g_post, g_pre, *, n_rows, nl, t, b, tm):
    d = x_all.shape[1]
    aw, sw, fw = att.shape[1], us.shape[1], four.shape[1]
    n_lat = nl // tm
    per_b = t // tm

    def row(i):
        return (i, 0)

    def const(i):
        return (0, 0)

    def mod_idx(i):
        return (jnp.where(i < n_lat, i // per_b, b), 0, 0)

    return pl.pallas_call(
        functools.partial(_outproj_kernel, aw=aw, sw=sw),
        grid=(n_rows // tm,),
        in_specs=[
            pl.BlockSpec((tm, aw), row), pl.BlockSpec((tm, sw), row), pl.BlockSpec((tm, sw), row),
            pl.BlockSpec((tm, fw), row), pl.BlockSpec((tm, d), row),
            pl.BlockSpec((1, N_MOD, d), mod_idx),
            pl.BlockSpec(w_out.shape, const, pipeline_mode=pl.Buffered(1)),
            pl.BlockSpec(w_glu.shape, const), pl.BlockSpec((1, sw), const), pl.BlockSpec((1, sw), const),
            pl.BlockSpec((1, d), const), pl.BlockSpec((1, d), const),
        ],
        out_specs=[pl.BlockSpec((tm, d), row), pl.BlockSpec((tm, d), row)],
        out_shape=[jax.ShapeDtypeStruct((n_rows, d), F32), jax.ShapeDtypeStruct((n_rows, d), BF16)],
        compiler_params=_cparams(("arbitrary",)),
        name="outproj",
    )(att, us, yc, four, x_all, mod, w_out, w_glu, b_glu, dsk, g_post, g_pre)


def _ffn_kernel(h_ref, x_ref, mod_ref, g_ref, wg_ref, wu_ref, wd_ref, o_ref):
    k = pl.program_id(1)
    h = h_ref[...]
    a = jnp.dot(h, wg_ref[...], preferred_element_type=F32)
    u = jnp.dot(h, wu_ref[...], preferred_element_type=F32)
    part = jnp.dot((_silu(a) * u).astype(BF16), wd_ref[...], preferred_element_type=F32)

    @pl.when(k == 0)
    def _():
        o_ref[...] = part

    @pl.when(k > 0)
    def _():
        o_ref[...] += part

    @pl.when(k == pl.num_programs(1) - 1)
    def _():
        o_ref[...] = x_ref[...] + mod_ref[0, 5:6, :] * (_rms(o_ref[...], NORM_EPS) * g_ref[...])


def _ffn(h2, x_mid, mod, g_post, w_gate, w_up, w_down, *, nl, t, b, tm, tf):
    n_rows, d = h2.shape
    dff = w_gate.shape[1]
    n_lat = nl // tm
    per_b = t // tm

    def mod_idx(i, k):
        return (jnp.where(i < n_lat, i // per_b, b), 0, 0)

    return pl.pallas_call(
        _ffn_kernel,
        grid=(n_rows // tm, dff // tf),
        in_specs=[
            pl.BlockSpec((tm, d), lambda i, k: (i, 0)),
            pl.BlockSpec((tm, d), lambda i, k: (i, 0)),
            pl.BlockSpec((1, N_MOD, d), mod_idx),
            pl.BlockSpec((1, d), lambda i, k: (0, 0)),
            pl.BlockSpec((d, tf), lambda i, k: (0, k)),
            pl.BlockSpec((d, tf), lambda i, k: (0, k)),
            pl.BlockSpec((tf, d), lambda i, k: (k, 0)),
        ],
        out_specs=pl.BlockSpec((tm, d), lambda i, k: (i, 0)),
        out_shape=jax.ShapeDtypeStruct((n_rows, d), F32),
        compiler_params=_cparams(("arbitrary", "arbitrary")),
        name="ffn",
    )(h2, x_mid, mod, g_post, w_gate, w_up, w_down)


def kernel(x, c, ctx, c_ctx, w_mod, b_mod, g_mix_pre, g_mix_post, g_ffn_pre, g_ffn_post, w_in, w_out, lam_q1, lam_k1, lam_q2, lam_k2, g_subln, ssm_a_re, ssm_a_im, ssm_log_dt, ssm_b_re, ssm_b_im, ssm_c_re, ssm_c_im, ssm_d, w_glu, b_glu, w_four, b_four, w_gate, w_up, w_down):
    b, t, d = x.shape
    n_ctx = ctx.shape[1]
    depth = w_mod.shape[0]
    nl, nc = b * t, b * n_ctx
    aw = d // 2
    sw = ssm_d.shape[1]
    fw = d - aw - sw
    heads = aw // (2 * DA_HEAD_DIM)
    tm = 512
    assert t % tm == 0 and nc % tm == 0 and n_ctx % SSM_CHUNK == 0 and t % GRID_W == 0 and b + 1 <= MOD_ROWS

    cs = jnp.concatenate([c, c_ctx[None, :], jnp.zeros((MOD_ROWS - b - 1, d), F32)], axis=0)
    mod_all = _modulation(cs, w_mod, b_mod).reshape(depth, MOD_ROWS, N_MOD, d)

    tabs = _rope_tables(t, tm)
    dft_lat = _dft_tables(t)
    dft_ctx = _dft_tables(n_ctx)
    x_all = jnp.concatenate([x.reshape(nl, d), ctx.reshape(nc, d)], axis=0)

    for l in range(depth):
        need_ctx = l < depth - 1
        lam_init = 0.8 - 0.6 * math.exp(-0.3 * l)
        mod = mod_all[l]
        q, k, v, us, uf = _inproj(x_all, mod, g_mix_pre[l][None, :], w_in[l].astype(BF16), tabs,
                                  nl=nl, t=t, b=b, aw=aw, sw=sw, fw=fw, tm=tm)

        lam4 = jnp.stack([lam_q1[l], lam_k1[l], lam_q2[l], lam_k2[l]]).astype(F32)
        gs = g_subln[l][None, :].astype(F32)
        att = _attention(lam4, gs, q, k, v, None, lam_init=lam_init, nl=nl, t=t, b=b, ctx=n_ctx, heads=heads,
                         latent=True, tq=512, tk=512)
        if need_ctx:
            att = _attention(lam4, gs, q, k, v, att, lam_init=lam_init, nl=nl, t=t, b=b, ctx=n_ctx, heads=heads,
                             latent=False, tq=n_ctx, tk=512)

        w_intra, w_sin, w_so = _ssm_weights(ssm_a_re[l], ssm_a_im[l], ssm_log_dt[l], ssm_b_re[l], ssm_b_im[l],
                                               ssm_c_re[l], ssm_c_im[l])
        al = _ssm_al_layout(ssm_a_re[l], ssm_a_im[l], ssm_log_dt[l])
        yc = _ssm_conv(us, w_intra, w_sin, w_so, al, nl=nl, t=t, b=b, ctx=n_ctx)

        bias = b_four[l].reshape(1, fw).astype(F32)
        four = _fourier(uf, _fourier_weights(w_four[l], t), bias, dft_lat, None, row0=0, seq=t, b=b, tm=tm)
        if need_ctx:
            four = _fourier(uf, _fourier_weights(w_four[l], n_ctx), bias, dft_ctx, four, row0=nl, seq=n_ctx, b=b,
                            tm=tm)

        n_rows = nl + nc if need_ctx else nl
        x_mid, h2 = _outproj(att, us, yc, four, x_all, mod, w_out[l].astype(BF16), w_glu[l].astype(BF16),
                             b_glu[l][None, :].astype(F32), ssm_d[l][None, :].astype(F32),
                             g_mix_post[l][None, :], g_ffn_pre[l][None, :], n_rows=n_rows, nl=nl, t=t, b=b, tm=tm)
        x_all = _ffn(h2, x_mid, mod, g_ffn_post[l][None, :], w_gate[l].astype(BF16), w_up[l].astype(BF16),
                     w_down[l].astype(BF16), nl=nl, t=t, b=b, tm=tm, tf=512)
    return x_all[:nl].reshape(b, t, d)
```

```python
import functools
import math

import jax
import jax.numpy as jnp
import numpy as np
from jax import lax
from jax.experimental import pallas as pl
from jax.experimental.pallas import tpu as pltpu

F32 = jnp.float32
BF16 = jnp.bfloat16

GRID_W = 64
DA_HEAD_DIM = 128
SSM_GROUP = 16
SSM_STATE = 64
FOURIER_HEADS = 4
N_MOD = 6
ROPE_BASE = 10000.0
ROPE_PAIRS = DA_HEAD_DIM // 4
NORM_EPS = 1e-6
SUBLN_EPS = 1e-5

SSM_CHUNK = 16
MOD_ROWS = 8
VMEM_LIMIT = 56 * 1024 * 1024


def _cparams(sem):
    return pltpu.CompilerParams(dimension_semantics=sem, vmem_limit_bytes=VMEM_LIMIT)


def _rms(x, eps):
    return x * lax.rsqrt(jnp.mean(x * x, axis=-1, keepdims=True) + eps)


def _silu(x):
    return x * jax.nn.sigmoid(x)


def _mod_kernel(c_ref, w_ref, b_ref, o_ref):
    s = _silu(c_ref[...]).astype(BF16)
    o_ref[0] = jnp.dot(s, w_ref[0].astype(BF16), preferred_element_type=F32) + b_ref[0]


def _modulation(cs, w_mod, b_mod, tn=1024):
    depth, d, n = w_mod.shape
    return pl.pallas_call(
        _mod_kernel,
        grid=(depth, n // tn),
        in_specs=[
            pl.BlockSpec((MOD_ROWS, d), lambda l, j: (0, 0)),
            pl.BlockSpec((1, d, tn), lambda l, j: (l, 0, j)),
            pl.BlockSpec((1, 1, tn), lambda l, j: (l, 0, j)),
        ],
        out_specs=pl.BlockSpec((1, MOD_ROWS, tn), lambda l, j: (l, 0, j)),
        out_shape=jax.ShapeDtypeStruct((depth, MOD_ROWS, n), F32),
        compiler_params=_cparams(("arbitrary", "arbitrary")),
        name="modulation",
    )(cs, w_mod, b_mod.reshape(depth, 1, n))


def _inproj_kernel(x_ref, mod_ref, g_ref, wqk_ref, wvt_ref, wsf_ref, cq_ref, sq_ref, ck_ref, sk_ref,
                   q_ref, k_ref, vt_ref, us_ref, uf_ref, *, aw, sw):
    tm = x_ref.shape[0]
    h = _rms(x_ref[...], NORM_EPS) * g_ref[...]
    h = h * (1.0 + mod_ref[0, 1:2, :]) + mod_ref[0, 0:1, :]
    hb = h.astype(BF16)

    lane = lax.broadcasted_iota(jnp.int32, (tm, DA_HEAD_DIM), 1)
    first_half = (lane & (2 * ROPE_PAIRS - 1)) < ROPE_PAIRS

    def rope(z, c, s):
        partner = jnp.where(first_half, pltpu.roll(z, DA_HEAD_DIM - ROPE_PAIRS, 1), pltpu.roll(z, ROPE_PAIRS, 1))
        return z * c + partner * s

    zq = jnp.dot(hb, wqk_ref[:, 0:aw], preferred_element_type=F32)
    cq, sq = cq_ref[...], sq_ref[...]
    for j in range(aw // DA_HEAD_DIM):
        sl = slice(j * DA_HEAD_DIM, (j + 1) * DA_HEAD_DIM)
        q_ref[:, sl] = rope(zq[:, sl], cq, sq).astype(BF16)
    zk = jnp.dot(hb, wqk_ref[:, aw:2 * aw], preferred_element_type=F32)
    ck, sk = ck_ref[...], sk_ref[...]
    for j in range(aw // DA_HEAD_DIM):
        sl = slice(j * DA_HEAD_DIM, (j + 1) * DA_HEAD_DIM)
        k_ref[:, sl] = rope(zk[:, sl], ck, sk).astype(BF16)
    vt_ref[...] = lax.dot_general(wvt_ref[...], hb, (((1,), (1,)), ((), ())),
                                  preferred_element_type=F32).astype(BF16)
    us_ref[...] = jnp.dot(hb, wsf_ref[:, 0:sw], preferred_element_type=F32)
    uf_ref[...] = jnp.dot(hb, wsf_ref[:, sw:], preferred_element_type=F32).astype(BF16)


def _inproj(x_all, mod, g, w_qk, w_vt, w_sf, tabs, *, nl, t, b, aw, sw, fw, tm):
    nr, d = x_all.shape
    n_lat = nl // tm
    per_b = t // tm

    def row(i):
        return (i, 0)

    def mod_idx(i):
        return (jnp.where(i < n_lat, i // per_b, b), 0, 0)

    def tab_idx(i):
        return (jnp.where(i < n_lat, i % per_b, per_b), 0)

    tab_spec = pl.BlockSpec((tm, DA_HEAD_DIM), tab_idx)
    return pl.pallas_call(
        functools.partial(_inproj_kernel, aw=aw, sw=sw),
        grid=(nr // tm,),
        in_specs=[
            pl.BlockSpec((tm, d), row),
            pl.BlockSpec((1, N_MOD, d), mod_idx),
            pl.BlockSpec((1, d), lambda i: (0, 0)),
            pl.BlockSpec(w_qk.shape, lambda i: (0, 0), pipeline_mode=pl.Buffered(1)),
            pl.BlockSpec(w_vt.shape, lambda i: (0, 0), pipeline_mode=pl.Buffered(1)),
            pl.BlockSpec(w_sf.shape, lambda i: (0, 0), pipeline_mode=pl.Buffered(1)),
            tab_spec, tab_spec, tab_spec, tab_spec,
        ],
        out_specs=[
            pl.BlockSpec((tm, aw), row), pl.BlockSpec((tm, aw), row), pl.BlockSpec((aw, tm), lambda i: (0, i)),
            pl.BlockSpec((tm, sw), row), pl.BlockSpec((tm, fw), row),
        ],
        out_shape=[
            jax.ShapeDtypeStruct((nr, aw), BF16), jax.ShapeDtypeStruct((nr, aw), BF16),
            jax.ShapeDtypeStruct((aw, nr), BF16), jax.ShapeDtypeStruct((nr, sw), F32),
            jax.ShapeDtypeStruct((nr, fw), BF16),
        ],
        compiler_params=_cparams(("arbitrary",)),
        name="inproj",
    )(x_all, mod, g, w_qk, w_vt, w_sf, *tabs)


def _rope_tables(t, tm):
    rows = t // GRID_W
    r = jnp.broadcast_to(jnp.arange(rows, dtype=F32)[:, None], (rows, GRID_W)).reshape(-1)
    col = jnp.broadcast_to(jnp.arange(GRID_W, dtype=F32)[None, :], (rows, GRID_W)).reshape(-1)
    inv = ROPE_BASE ** (-jnp.arange(ROPE_PAIRS, dtype=F32) / ROPE_PAIRS)
    ar, ac = r[:, None] * inv, col[:, None] * inv
    cos = jnp.concatenate([jnp.cos(ar), jnp.cos(ar), jnp.cos(ac), jnp.cos(ac)], axis=1)
    sin = jnp.concatenate([-jnp.sin(ar), jnp.sin(ar), -jnp.sin(ac), jnp.sin(ac)], axis=1)
    cos = jnp.concatenate([cos, jnp.ones((tm, DA_HEAD_DIM), F32)], axis=0)
    sin = jnp.concatenate([sin, jnp.zeros((tm, DA_HEAD_DIM), F32)], axis=0)
    scale = DA_HEAD_DIM ** -0.5 * math.log2(math.e)
    return cos * scale, sin * scale, cos, sin


def _attn_kernel(*refs, lam_init, n_lat_chunks, tk):
    if n_lat_chunks:
        lam_ref, gs_ref, q_ref, kc_ref, vc_ref, kl_ref, vl_ref, o_ref = refs
        k_refs, vt_refs = (kc_ref, kl_ref), (vc_ref, vl_ref)
    else:
        lam_ref, gs_ref, q_ref, kc_ref, vc_ref, o_ref = refs
        k_refs, vt_refs = (kc_ref,), (vc_ref,)
    hd = DA_HEAD_DIM
    nt = (((1,), (1,)), ((), ()))
    q = q_ref[...]
    outs = []
    for idx in range(2):
        qi = q[:, idx * hd:(idx + 1) * hd]
        ss = [lax.dot_general(kr[:, idx * hd:(idx + 1) * hd], qi, nt, preferred_element_type=F32) for kr in k_refs]
        m = functools.reduce(jnp.maximum, [jnp.max(s, axis=0, keepdims=True) for s in ss])
        ps = [jnp.exp2(s - m) for s in ss]
        l = functools.reduce(jnp.add, [jnp.sum(p, axis=0, keepdims=True) for p in ps])
        acc = functools.reduce(jnp.add, [jnp.dot(vr[...], p.astype(BF16), preferred_element_type=F32)
                                         for vr, p in zip(vt_refs, ps)])
        outs.append(acc * (1.0 / l))

    s1 = jnp.sum(lam_ref[0:1, :] * lam_ref[1:2, :], axis=-1, keepdims=True)
    s2 = jnp.sum(lam_ref[2:3, :] * lam_ref[3:4, :], axis=-1, keepdims=True)
    lam = jnp.exp(s1) - jnp.exp(s2) + lam_init
    o = jnp.transpose(outs[0] - outs[1] * lam)
    o_ref[...] = (_rms(o, SUBLN_EPS) * gs_ref[...] * (1.0 - lam_init)).astype(BF16)


def _attention(lam4, gs, q, k, vt, prev, *, lam_init, nl, t, b, ctx, heads, latent, tq, tk):
    nr, aw = q.shape
    vw = 2 * DA_HEAD_DIM
    ctx_blk0 = nl // ctx
    if latent:
        per_b = t // tq
        grid = (b, heads, per_b)
        q_spec = pl.BlockSpec((tq, vw), lambda bi, h, i: (bi * per_b + i, h))
    else:
        tq = ctx
        grid = (b, heads, 1)
        q_spec = pl.BlockSpec((tq, vw), lambda bi, h, i: (ctx_blk0 + bi, h))
    in_specs = [pl.BlockSpec((4, DA_HEAD_DIM), lambda bi, h, i: (0, 0)),
                pl.BlockSpec((1, vw), lambda bi, h, i: (0, 0)),
                q_spec,
                pl.BlockSpec((ctx, vw), lambda bi, h, i: (ctx_blk0 + bi, h)),
                pl.BlockSpec((vw, ctx), lambda bi, h, i: (h, ctx_blk0 + bi))]
    args = [lam4, gs, q, k, vt]
    if latent:
        in_specs += [pl.BlockSpec((t, vw), lambda bi, h, i: (bi, h)),
                     pl.BlockSpec((vw, t), lambda bi, h, i: (h, bi))]
        args += [k, vt]
    aliases = {}
    kern = functools.partial(_attn_kernel, lam_init=lam_init, n_lat_chunks=(t // tk) if latent else 0, tk=tk)
    if prev is not None:
        in_specs.append(pl.BlockSpec(memory_space=pl.ANY))
        aliases = {len(args): 0}
        args.append(prev)
        kern = _drop_last_input(kern, len(args))
    return pl.pallas_call(
        kern,
        grid=grid,
        in_specs=in_specs,
        out_specs=q_spec,
        out_shape=jax.ShapeDtypeStruct((nr, aw), BF16),
        input_output_aliases=aliases,
        compiler_params=_cparams(("arbitrary", "arbitrary", "arbitrary")),
        name="attn_latent" if latent else "attn_ctx",
    )(*args)


def _drop_last_input(kern, n_in):
    def wrapped(*refs):
        return kern(*refs[:n_in - 1], *refs[n_in:])
    return wrapped


def _ssm_weights(a_re, a_im, log_dt, b_re, b_im, c_re, c_im):
    L, H, P = SSM_CHUNK, SSM_GROUP, SSM_STATE
    g = a_re.shape[1]
    npair = g // 2
    lam_a = lax.complex(a_re.astype(F32), a_im.astype(F32))
    dt = jnp.exp(log_dt.astype(F32))[..., None]
    ldt = lam_a * dt
    a_bar = jnp.exp(ldt)
    b_bar = ((a_bar - 1.0) / lam_a)[..., None] * lax.complex(b_re.astype(F32), b_im.astype(F32))
    c_mat = lax.complex(c_re.astype(F32), c_im.astype(F32))
    n = jnp.arange(L + 1, dtype=F32)
    apow = jnp.exp(ldt[..., None] * n)
    hp = lax.Precision.HIGHEST

    kt = jnp.real(jnp.einsum('dghp,dgpt,dgpk->dgthk', c_mat, apow[..., :L], b_bar, precision=hp))
    s_idx = jnp.arange(L)[:, None]
    t_idx = jnp.arange(L)[None, :]
    kf = kt[0][:, jnp.clip(t_idx - s_idx, 0, L - 1)] * (t_idx >= s_idx)[None, :, :, None, None]
    kb = kt[1][:, jnp.clip(s_idx - t_idx, 0, L - 1)] * (s_idx >= t_idx)[None, :, :, None, None]
    w_intra = jnp.transpose(kf + kb, (0, 1, 4, 2, 3)).reshape(g, L * H, L * H)

    steps_in = jnp.stack([L - 1 - jnp.arange(L), jnp.arange(L)])
    ap_in = jnp.take_along_axis(apow, jnp.broadcast_to(steps_in[:, None, None, :], apow.shape[:3] + (L,)), axis=3)
    w_sin = jnp.einsum('dgps,dgph->dgshp', ap_in, b_bar)
    w_sin = jnp.stack([jnp.real(w_sin), jnp.imag(w_sin)], axis=-2)
    w_sin = w_sin.reshape(2, npair, 2, L * H, 2, P)
    eye = jnp.eye(2, dtype=F32)
    w_sin = w_sin[:, :, :, :, :, None, :] * eye[None, None, :, None, None, :, None]
    w_sin = w_sin.reshape(2, npair, 2, L * H, 4 * P)

    steps_out = jnp.stack([jnp.arange(L) + 1, L - jnp.arange(L)])
    ap_out = jnp.take_along_axis(apow, jnp.broadcast_to(steps_out[:, None, None, :], apow.shape[:3] + (L,)), axis=3)
    cn = jnp.einsum('dghp,dgpt->dgpth', c_mat, ap_out).reshape(2, g, P, L * H)
    w_so = jnp.stack([jnp.real(cn), -jnp.imag(cn)], axis=2)
    w_so = w_so.reshape(2, npair, 2, 2, P, L * H)
    w_so = jnp.transpose(w_so, (0, 1, 3, 2, 4, 5))
    w_so = w_so[:, :, :, :, :, None, :] * eye[None, None, None, :, None, :, None]
    w_so = w_so.reshape(2, npair, 4 * P, 2 * L * H)

    return w_intra.astype(BF16), w_sin.astype(BF16), w_so.astype(BF16)


def _ssm_al_layout(a_re, a_im, log_dt):
    P = SSM_STATE
    g = a_re.shape[1]
    lam_a = lax.complex(a_re.astype(F32), a_im.astype(F32))
    dt = jnp.exp(log_dt.astype(F32))[..., None]
    al = jnp.exp(lam_a * dt * float(SSM_CHUNK))
    al = jnp.stack([jnp.real(al), jnp.imag(al)], axis=1)
    al = al.reshape(2, 2, g // 2, 2, P)
    al = jnp.transpose(al, (0, 2, 1, 3, 4))
    return al.reshape(2, 1, g * 2 * P)


def _ssm_in_kernel(u_ref, w_ref, s_ref):
    for d in range(2):
        s_ref[d] = (jnp.dot(u_ref[0], w_ref[d, 0, 0], preferred_element_type=F32)
                    + jnp.dot(u_ref[1], w_ref[d, 0, 1], preferred_element_type=F32))


def _ssm_scan_kernel(s_ref, al_ref, h_ref, *, n_ctx_chunks, n_chunks):
    d = pl.program_id(0)
    nb = s_ref.shape[2]
    width = s_ref.shape[3]
    npair = width // 256
    ars = [al_ref[0, :, k * 256:k * 256 + 128] for k in range(npair)]
    ais = [al_ref[0, :, k * 256 + 128:(k + 1) * 256] for k in range(npair)]

    def body(i, carry):
        c_b = jnp.where(i < n_ctx_chunks, n_ctx_chunks - 1 - i, n_chunks + n_ctx_chunks - 1 - i)
        c = jnp.where(d == 0, i, c_b)
        new = []
        for k in range(npair):
            hr, hi = carry[2 * k], carry[2 * k + 1]
            h_ref[0, c, :, k * 256:k * 256 + 128] = hr
            h_ref[0, c, :, k * 256 + 128:(k + 1) * 256] = hi
            sr = s_ref[0, c, :, k * 256:k * 256 + 128]
            si = s_ref[0, c, :, k * 256 + 128:(k + 1) * 256]
            new.append(ars[k] * hr - ais[k] * hi + sr)
            new.append(ars[k] * hi + ais[k] * hr + si)
        return tuple(new)

    init = tuple(jnp.zeros((nb, 128), F32) for _ in range(2 * npair))
    lax.fori_loop(0, n_chunks, body, init)


def _ssm_out_kernel(u_ref, wi_ref, h_ref, wo_ref, y_ref):
    hf = h_ref[0].astype(BF16)
    hb = h_ref[1].astype(BF16)
    w = SSM_CHUNK * SSM_GROUP
    for e in range(2):
        y_ref[e] = (jnp.dot(u_ref[e], wi_ref[e], preferred_element_type=F32)
                    + jnp.dot(hf, wo_ref[0, 0, :, e * w:(e + 1) * w], preferred_element_type=F32)
                    + jnp.dot(hb, wo_ref[1, 0, :, e * w:(e + 1) * w], preferred_element_type=F32))


def _ssm_conv(us, w_intra, w_sin, w_so, al, *, nl, t, b, ctx):
    L, H, P = SSM_CHUNK, SSM_GROUP, SSM_STATE
    nr, sw = us.shape
    g = sw // H
    npair = g // 2
    ncl, ncc = t // L, ctx // L
    nch = ncl + ncc
    r = nch * b
    cw = L * H
    ul = jnp.transpose(us[:nl].reshape(b, ncl, L, g, H), (3, 1, 0, 2, 4)).reshape(g, ncl, b, cw)
    uc = jnp.transpose(us[nl:].reshape(b, ncc, L, g, H), (3, 1, 0, 2, 4)).reshape(g, ncc, b, cw)
    ut = jnp.concatenate([uc, ul], axis=1).reshape(g, r, cw).astype(BF16)

    s = pl.pallas_call(
        _ssm_in_kernel,
        grid=(npair,),
        in_specs=[pl.BlockSpec((2, r, cw), lambda j: (j, 0, 0)),
                  pl.BlockSpec((2, 1, 2, cw, 4 * P), lambda j: (0, j, 0, 0, 0))],
        out_specs=pl.BlockSpec((2, r, 4 * P), lambda j: (0, 0, j)),
        out_shape=jax.ShapeDtypeStruct((2, r, npair * 4 * P), F32),
        compiler_params=_cparams(("arbitrary",)),
        name="ssm_in",
    )(ut, w_sin)

    cols = npair * 4 * P
    cb = 1024
    hin = pl.pallas_call(
        functools.partial(_ssm_scan_kernel, n_ctx_chunks=ncc, n_chunks=nch),
        grid=(2, cols // cb),
        in_specs=[pl.BlockSpec((1, nch, b, cb), lambda d, j: (d, 0, 0, j)),
                  pl.BlockSpec((1, 1, cb), lambda d, j: (d, 0, j))],
        out_specs=pl.BlockSpec((1, nch, b, cb), lambda d, j: (d, 0, 0, j)),
        out_shape=jax.ShapeDtypeStruct((2, nch, b, cols), F32),
        compiler_params=_cparams(("arbitrary", "arbitrary")),
        name="ssm_scan",
    )(s.reshape(2, nch, b, cols), al)

    y = pl.pallas_call(
        _ssm_out_kernel,
        grid=(npair,),
        in_specs=[pl.BlockSpec((2, r, cw), lambda j: (j, 0, 0)),
                  pl.BlockSpec((2, cw, cw), lambda j: (j, 0, 0)),
                  pl.BlockSpec((2, r, 4 * P), lambda j: (0, 0, j)),
                  pl.BlockSpec((2, 1, 4 * P, 2 * cw), lambda j: (0, j, 0, 0))],
        out_specs=pl.BlockSpec((2, r, cw), lambda j: (j, 0, 0)),
        out_shape=jax.ShapeDtypeStruct((g, r, cw), F32),
        compiler_params=_cparams(("arbitrary",)),
        name="ssm_out",
    )(ut, w_intra, hin.reshape(2, r, cols), w_so)

    y = y.reshape(g, nch, b, L, H)
    yl = jnp.transpose(y[:, ncc:], (2, 1, 3, 0, 4)).reshape(nl, sw)
    yc = jnp.transpose(y[:, :ncc], (2, 1, 3, 0, 4)).reshape(nr - nl, sw)
    return jnp.concatenate([yl, yc], axis=0)


def _fourier_kernel(*refs, groups):
    u_ref, w_ref, cos_ref, sin_ref, b_ref = refs[:5]
    o_ref, a_s, b_s = refs[-3:]
    gc = u_ref.shape[1] // groups

    @pl.when(pl.program_id(1) == 0)
    def _():
        for g in range(groups):
            ab = jnp.dot(u_ref[:, g * gc:(g + 1) * gc], w_ref[g], preferred_element_type=F32)
            a_s[:, g * gc:(g + 1) * gc] = ab[:, :gc].astype(BF16)
            b_s[:, g * gc:(g + 1) * gc] = ab[:, gc:].astype(BF16)

    o_ref[...] = (jnp.dot(cos_ref[...], a_s[...], preferred_element_type=F32)
                  - jnp.dot(sin_ref[...], b_s[...], preferred_element_type=F32)
                  + b_ref[...]).astype(BF16)


def _dft_tables(n):
    k = jnp.arange(n, dtype=jnp.int32)
    ang = ((k[:, None] * k[None, :]) % n).astype(F32) * (2.0 * math.pi / n)
    return jnp.cos(ang).astype(BF16), jnp.sin(ang).astype(BF16)


def _fourier(uf, w_cs, bias, tables, prev, *, row0, seq, b, tm):
    nr, fw = uf.shape
    groups = w_cs.shape[0]
    cos_t, sin_t = tables
    tm = min(tm, seq)
    per_b = seq // tm
    blk0 = row0 // tm
    seq_blk0 = row0 // seq
    in_specs = [
        pl.BlockSpec((seq, fw), lambda bi, i: (seq_blk0 + bi, 0)),
        pl.BlockSpec(w_cs.shape, lambda bi, i: (0, 0, 0)),
        pl.BlockSpec((tm, seq), lambda bi, i: (i, 0)),
        pl.BlockSpec((tm, seq), lambda bi, i: (i, 0)),
        pl.BlockSpec((1, fw), lambda bi, i: (0, 0)),
    ]
    args = [uf, w_cs, cos_t, sin_t, bias]
    aliases = {}
    kern = functools.partial(_fourier_kernel, groups=groups)
    if prev is not None:
        in_specs.append(pl.BlockSpec(memory_space=pl.ANY))
        aliases = {len(args): 0}
        args.append(prev)
    return pl.pallas_call(
        kern,
        grid=(b, per_b),
        in_specs=in_specs,
        out_specs=pl.BlockSpec((tm, fw), lambda bi, i: (blk0 + bi * per_b + i, 0)),
        out_shape=jax.ShapeDtypeStruct((nr, fw), BF16),
        scratch_shapes=[pltpu.VMEM((seq, fw), BF16), pltpu.VMEM((seq, fw), BF16)],
        input_output_aliases=aliases,
        compiler_params=_cparams(("arbitrary", "arbitrary")),
        name="fourier_%d" % seq,
    )(*args)


def _fourier_weights(w_four, seq):
    groups, gc, _ = w_four.shape
    k = np.arange(gc)
    ang = ((k[:, None] * k[None, :]) % gc).astype(np.float64) * (2.0 * np.pi / gc)
    norm = 1.0 / math.sqrt(seq * gc)
    cc = jnp.asarray(np.cos(ang) * norm, F32)
    sc = jnp.asarray(np.sin(ang) * norm, F32)
    hp = lax.Precision.HIGHEST
    wc = jnp.einsum('ck,gkd->gcd', cc, w_four.astype(F32), precision=hp)
    ws = jnp.einsum('ck,gkd->gcd', sc, w_four.astype(F32), precision=hp)
    return jnp.concatenate([wc, ws], axis=-1).astype(BF16)


def _outproj_kernel(att_ref, us_ref, yc_ref, uf_ref, x_ref, mod_ref, wo_ref, wg_ref, bg_ref, dsk_ref,
                    gpost_ref, gpre_ref, xo_ref, h2_ref, *, aw, sw):
    y = dsk_ref[...] * us_ref[...] + yc_ref[...]
    g = jax.nn.gelu(y)
    z = jnp.dot(g.astype(BF16), wg_ref[...], preferred_element_type=F32) + bg_ref[...]
    ssm = (g * jax.nn.sigmoid(z)).astype(BF16)
    mix = (jnp.dot(att_ref[...], wo_ref[0:aw, :], preferred_element_type=F32)
           + jnp.dot(ssm, wo_ref[aw:aw + sw, :], preferred_element_type=F32)
           + jnp.dot(uf_ref[...], wo_ref[aw + sw:, :], preferred_element_type=F32))
    xn = x_ref[...] + mod_ref[0, 2:3, :] * (_rms(mix, NORM_EPS) * gpost_ref[...])
    xo_ref[...] = xn
    h2 = _rms(xn, NORM_EPS) * gpre_ref[...]
    h2_ref[...] = (h2 * (1.0 + mod_ref[0, 4:5, :]) + mod_ref[0, 3:4, :]).astype(BF16)


def _outproj(att, us, yc, four, x_all, mod, w_out, w_glu, b_glu, dsk, g_post, g_pre, *, n_rows, nl, t, b, tm):
    d = x_all.shape[1]
    aw, sw, fw = att.shape[1], us.shape[1], four.shape[1]
    n_lat = nl // tm
    per_b = t // tm

    def row(i):
        return (i, 0)

    def const(i):
        return (0, 0)

    def mod_idx(i):
        return (jnp.where(i < n_lat, i // per_b, b), 0, 0)

    return pl.pallas_call(
        functools.partial(_outproj_kernel, aw=aw, sw=sw),
        grid=(n_rows // tm,),
        in_specs=[
            pl.BlockSpec((tm, aw), row), pl.BlockSpec((tm, sw), row), pl.BlockSpec((tm, sw), row),
            pl.BlockSpec((tm, fw), row), pl.BlockSpec((tm, d), row),
            pl.BlockSpec((1, N_MOD, d), mod_idx),
            pl.BlockSpec(w_out.shape, const, pipeline_mode=pl.Buffered(1)),
            pl.BlockSpec(w_glu.shape, const), pl.BlockSpec((1, sw), const), pl.BlockSpec((1, sw), const),
            pl.BlockSpec((1, d), const), pl.BlockSpec((1, d), const),
        ],
        out_specs=[pl.BlockSpec((tm, d), row), pl.BlockSpec((tm, d), row)],
        out_shape=[jax.ShapeDtypeStruct((n_rows, d), F32), jax.ShapeDtypeStruct((n_rows, d), BF16)],
        compiler_params=_cparams(("arbitrary",)),
        name="outproj",
    )(att, us, yc, four, x_all, mod, w_out, w_glu, b_glu, dsk, g_post, g_pre)


def _ffn_kernel(h_ref, x_ref, mod_ref, g_ref, wg_ref, wu_ref, wd_ref, o_ref):
    k = pl.program_id(1)
    h = h_ref[...]
    a = jnp.dot(h, wg_ref[...], preferred_element_type=F32)
    u = jnp.dot(h, wu_ref[...], preferred_element_type=F32)
    part = jnp.dot((_silu(a) * u).astype(BF16), wd_ref[...], preferred_element_type=F32)

    @pl.when(k == 0)
    def _():
        o_ref[...] = part

    @pl.when(k > 0)
    def _():
        o_ref[...] += part

    @pl.when(k == pl.num_programs(1) - 1)
    def _():
        o_ref[...] = x_ref[...] + mod_ref[0, 5:6, :] * (_rms(o_ref[...], NORM_EPS) * g_ref[...])


def _ffn(h2, x_mid, mod, g_post, w_gate, w_up, w_down, *, nl, t, b, tm, tf):
    n_rows, d = h2.shape
    dff = w_gate.shape[1]
    n_lat = nl // tm
    per_b = t // tm

    def mod_idx(i, k):
        return (jnp.where(i < n_lat, i // per_b, b), 0, 0)

    return pl.pallas_call(
        _ffn_kernel,
        grid=(n_rows // tm, dff // tf),
        in_specs=[
            pl.BlockSpec((tm, d), lambda i, k: (i, 0)),
            pl.BlockSpec((tm, d), lambda i, k: (i, 0)),
            pl.BlockSpec((1, N_MOD, d), mod_idx),
            pl.BlockSpec((1, d), lambda i, k: (0, 0)),
            pl.BlockSpec((d, tf), lambda i, k: (0, k)),
            pl.BlockSpec((d, tf), lambda i, k: (0, k)),
            pl.BlockSpec((tf, d), lambda i, k: (k, 0)),
        ],
        out_specs=pl.BlockSpec((tm, d), lambda i, k: (i, 0)),
        out_shape=jax.ShapeDtypeStruct((n_rows, d), F32),
        compiler_params=_cparams(("arbitrary", "arbitrary")),
        name="ffn",
    )(h2, x_mid, mod, g_post, w_gate, w_up, w_down)


def kernel(x, c, ctx, c_ctx, w_mod, b_mod, g_mix_pre, g_mix_post, g_ffn_pre, g_ffn_post, w_in, w_out, lam_q1, lam_k1, lam_q2, lam_k2, g_subln, ssm_a_re, ssm_a_im, ssm_log_dt, ssm_b_re, ssm_b_im, ssm_c_re, ssm_c_im, ssm_d, w_glu, b_glu, w_four, b_four, w_gate, w_up, w_down):
    b, t, d = x.shape
    n_ctx = ctx.shape[1]
    depth = w_mod.shape[0]
    nl, nc = b * t, b * n_ctx
    aw = d // 2
    sw = ssm_d.shape[1]
    fw = d - aw - sw
    heads = aw // (2 * DA_HEAD_DIM)
    tm = 512
    assert t % tm == 0 and nc % tm == 0 and n_ctx % SSM_CHUNK == 0 and t % GRID_W == 0 and b + 1 <= MOD_ROWS

    cs = jnp.concatenate([c, c_ctx[None, :], jnp.zeros((MOD_ROWS - b - 1, d), F32)], axis=0)
    mod_all = _modulation(cs, w_mod, b_mod).reshape(depth, MOD_ROWS, N_MOD, d)

    tabs = _rope_tables(t, tm)
    dft_lat = _dft_tables(t)
    dft_ctx = _dft_tables(n_ctx)
    x_all = jnp.concatenate([x.reshape(nl, d), ctx.reshape(nc, d)], axis=0)

    for l in range(depth):
        need_ctx = l < depth - 1
        lam_init = 0.8 - 0.6 * math.exp(-0.3 * l)
        mod = mod_all[l]
        w_inl = w_in[l].astype(BF16)
        q, k, v, us, uf = _inproj(x_all, mod, g_mix_pre[l][None, :], w_inl[:, :2 * aw],
                                  jnp.transpose(w_inl[:, 2 * aw:3 * aw]), w_inl[:, 3 * aw:], tabs,
                                  nl=nl, t=t, b=b, aw=aw, sw=sw, fw=fw, tm=tm)

        lam4 = jnp.stack([lam_q1[l], lam_k1[l], lam_q2[l], lam_k2[l]]).astype(F32)
        gs = g_subln[l][None, :].astype(F32)
        att = _attention(lam4, gs, q, k, v, None, lam_init=lam_init, nl=nl, t=t, b=b, ctx=n_ctx, heads=heads,
                         latent=True, tq=512, tk=512)
        if need_ctx:
            att = _attention(lam4, gs, q, k, v, att, lam_init=lam_init, nl=nl, t=t, b=b, ctx=n_ctx, heads=heads,
                             latent=False, tq=n_ctx, tk=512)

        w_intra, w_sin, w_so = _ssm_weights(ssm_a_re[l], ssm_a_im[l], ssm_log_dt[l], ssm_b_re[l], ssm_b_im[l],
                                               ssm_c_re[l], ssm_c_im[l])
        al = _ssm_al_layout(ssm_a_re[l], ssm_a_im[l], ssm_log_dt[l])
        yc = _ssm_conv(us, w_intra, w_sin, w_so, al, nl=nl, t=t, b=b, ctx=n_ctx)

        bias = b_four[l].reshape(1, fw).astype(F32)
        four = _fourier(uf, _fourier_weights(w_four[l], t), bias, dft_lat, None, row0=0, seq=t, b=b, tm=tm)
        if need_ctx:
            four = _fourier(uf, _fourier_weights(w_four[l], n_ctx), bias, dft_ctx, four, row0=nl, seq=n_ctx, b=b,
                            tm=tm)

        n_rows = nl + nc if need_ctx else nl
        x_mid, h2 = _outproj(att, us, yc, four, x_all, mod, w_out[l].astype(BF16), w_glu[l].astype(BF16),
                             b_glu[l][None, :].astype(F32), ssm_d[l][None, :].astype(F32),
                             g_mix_post[l][None, :], g_ffn_pre[l][None, :], n_rows=n_rows, nl=nl, t=t, b=b, tm=tm)
        x_all = _ffn(h2, x_mid, mod, g_ffn_post[l][None, :], w_gate[l].astype(BF16), w_up[l].astype(BF16),
                     w_down[l].astype(BF16), nl=nl, t=t, b=b, tm=tm, tf=512)
    return x_all[:nl].reshape(b, t, d)
```

```python
import functools
import math

import jax
import jax.numpy as jnp
import numpy as np
from jax import lax
from jax.experimental import pallas as pl
from jax.experimental.pallas import tpu as pltpu

F32 = jnp.float32
BF16 = jnp.bfloat16

LANES = 128
GRID_W = 64
DA_HEAD_DIM = 128
SSM_GROUP = 16
SSM_STATE = 64
N_MOD = 6
ROPE_BASE = 10000.0
ROPE_PAIRS = DA_HEAD_DIM // 4
NORM_EPS = 1e-6
SUBLN_EPS = 1e-5

SSM_CHUNK = 16
SSM_CW = SSM_CHUNK * SSM_GROUP
OCT = LANES // SSM_GROUP
MOD_ROWS = 8
VMEM_LIMIT = 56 * 1024 * 1024
NT_DIMS = (((1,), (1,)), ((), ()))


def _cparams(sem):
    return pltpu.CompilerParams(dimension_semantics=sem, vmem_limit_bytes=VMEM_LIMIT)


def _rms(x, eps):
    return x * lax.rsqrt(jnp.mean(x * x, axis=-1, keepdims=True) + eps)


def _silu(x):
    return x * jax.nn.sigmoid(x)


def _resident(shape):
    nd = len(shape)
    return pl.BlockSpec(shape, lambda *_: (0,) * nd, pipeline_mode=pl.Buffered(1))


def _mod_kernel(c_ref, w_ref, b_ref, o_ref):
    s = _silu(c_ref[...]).astype(BF16)
    o_ref[0] = jnp.dot(s, w_ref[0].astype(BF16), preferred_element_type=F32) + b_ref[0]


def _modulation(cs, w_mod, b_mod, tn=1024):
    depth, d, n = w_mod.shape
    return pl.pallas_call(
        _mod_kernel,
        grid=(depth, n // tn),
        in_specs=[
            pl.BlockSpec((MOD_ROWS, d), lambda l, j: (0, 0)),
            pl.BlockSpec((1, d, tn), lambda l, j: (l, 0, j)),
            pl.BlockSpec((1, 1, tn), lambda l, j: (l, 0, j)),
        ],
        out_specs=pl.BlockSpec((1, MOD_ROWS, tn), lambda l, j: (l, 0, j)),
        out_shape=jax.ShapeDtypeStruct((depth, MOD_ROWS, n), F32),
        compiler_params=_cparams(("arbitrary", "arbitrary")),
        name="modulation",
    )(cs, w_mod, b_mod.reshape(depth, 1, n))


def _inproj_kernel(x_ref, mod_ref, g_ref, wqk_ref, wvt_ref, wsf_ref, cq_ref, sq_ref, ck_ref, sk_ref,
                   q_ref, k_ref, vt_ref, us_ref, uf_ref, *, aw, sw):
    tm = x_ref.shape[0]
    h = _rms(x_ref[...], NORM_EPS) * g_ref[...]
    h = h * (1.0 + mod_ref[0, 1:2, :]) + mod_ref[0, 0:1, :]
    hb = h.astype(BF16)

    lane = lax.broadcasted_iota(jnp.int32, (tm, DA_HEAD_DIM), 1)
    first_half = (lane & (2 * ROPE_PAIRS - 1)) < ROPE_PAIRS

    def rope(z, c, s):
        partner = jnp.where(first_half, pltpu.roll(z, DA_HEAD_DIM - ROPE_PAIRS, 1), pltpu.roll(z, ROPE_PAIRS, 1))
        return z * c + partner * s

    zq = jnp.dot(hb, wqk_ref[:, 0:aw], preferred_element_type=F32)
    cq, sq = cq_ref[...], sq_ref[...]
    for j in range(aw // DA_HEAD_DIM):
        sl = slice(j * DA_HEAD_DIM, (j + 1) * DA_HEAD_DIM)
        q_ref[:, sl] = rope(zq[:, sl], cq, sq).astype(BF16)
    zk = jnp.dot(hb, wqk_ref[:, aw:2 * aw], preferred_element_type=F32)
    ck, sk = ck_ref[...], sk_ref[...]
    for j in range(aw // DA_HEAD_DIM):
        sl = slice(j * DA_HEAD_DIM, (j + 1) * DA_HEAD_DIM)
        k_ref[:, sl] = rope(zk[:, sl], ck, sk).astype(BF16)
    vt_ref[...] = lax.dot_general(wvt_ref[...], hb, NT_DIMS, preferred_element_type=F32).astype(BF16)
    zs = jnp.dot(hb, wsf_ref[:, 0:sw], preferred_element_type=F32)
    for j in range(sw // LANES):
        us_ref[j] = zs[:, j * LANES:(j + 1) * LANES]
    uf_ref[...] = jnp.dot(hb, wsf_ref[:, sw:], preferred_element_type=F32).astype(BF16)


def _inproj(x, mod, g, w_qk, w_vt, w_sf, tabs, *, mod_row, tab_blk, aw, sw, fw, tm):
    nr, d = x.shape

    def row(i):
        return (i, 0)

    tab_spec = pl.BlockSpec((tm, DA_HEAD_DIM), lambda i: (tab_blk(i), 0))
    return pl.pallas_call(
        functools.partial(_inproj_kernel, aw=aw, sw=sw),
        grid=(nr // tm,),
        in_specs=[
            pl.BlockSpec((tm, d), row),
            pl.BlockSpec((1, N_MOD, d), lambda i: (mod_row(i), 0, 0)),
            pl.BlockSpec((1, d), lambda i: (0, 0)),
            _resident(w_qk.shape), _resident(w_vt.shape), _resident(w_sf.shape),
            tab_spec, tab_spec, tab_spec, tab_spec,
        ],
        out_specs=[
            pl.BlockSpec((tm, aw), row), pl.BlockSpec((tm, aw), row), pl.BlockSpec((aw, tm), lambda i: (0, i)),
            pl.BlockSpec((sw // LANES, tm, LANES), lambda i: (0, i, 0)), pl.BlockSpec((tm, fw), row),
        ],
        out_shape=[
            jax.ShapeDtypeStruct((nr, aw), BF16), jax.ShapeDtypeStruct((nr, aw), BF16),
            jax.ShapeDtypeStruct((aw, nr), BF16), jax.ShapeDtypeStruct((sw // LANES, nr, LANES), F32),
            jax.ShapeDtypeStruct((nr, fw), BF16),
        ],
        compiler_params=_cparams(("arbitrary",)),
        name="inproj",
    )(x, mod, g, w_qk, w_vt, w_sf, *tabs)


def _rope_tables(t, tm):
    rows = t // GRID_W
    r = jnp.broadcast_to(jnp.arange(rows, dtype=F32)[:, None], (rows, GRID_W)).reshape(-1)
    col = jnp.broadcast_to(jnp.arange(GRID_W, dtype=F32)[None, :], (rows, GRID_W)).reshape(-1)
    inv = ROPE_BASE ** (-jnp.arange(ROPE_PAIRS, dtype=F32) / ROPE_PAIRS)
    ar, ac = r[:, None] * inv, col[:, None] * inv
    cos = jnp.concatenate([jnp.cos(ar), jnp.cos(ar), jnp.cos(ac), jnp.cos(ac)], axis=1)
    sin = jnp.concatenate([-jnp.sin(ar), jnp.sin(ar), -jnp.sin(ac), jnp.sin(ac)], axis=1)
    cos = jnp.concatenate([cos, jnp.ones((tm, DA_HEAD_DIM), F32)], axis=0)
    sin = jnp.concatenate([sin, jnp.zeros((tm, DA_HEAD_DIM), F32)], axis=0)
    scale = DA_HEAD_DIM ** -0.5 * math.log2(math.e)
    return cos * scale, sin * scale, cos, sin


def _attn_kernel(lam_ref, gs_ref, q_ref, *refs, lam_init):
    o_ref = refs[-1]
    k_refs, vt_refs = refs[0:-1:2], refs[1:-1:2]
    hd = DA_HEAD_DIM
    q = q_ref[...]
    outs = []
    for idx in range(2):
        qi = q[:, idx * hd:(idx + 1) * hd]
        ss = [lax.dot_general(kr[:, idx * hd:(idx + 1) * hd], qi, NT_DIMS, preferred_element_type=F32)
              for kr in k_refs]
        m = functools.reduce(jnp.maximum, [jnp.max(s, axis=0, keepdims=True) for s in ss])
        ps = [jnp.exp2(s - m) for s in ss]
        l = functools.reduce(jnp.add, [jnp.sum(p, axis=0, keepdims=True) for p in ps])
        acc = functools.reduce(jnp.add, [jnp.dot(vr[...], p.astype(BF16), preferred_element_type=F32)
                                         for vr, p in zip(vt_refs, ps)])
        outs.append(acc * (1.0 / l))

    s1 = jnp.sum(lam_ref[0:1, :] * lam_ref[1:2, :], axis=-1, keepdims=True)
    s2 = jnp.sum(lam_ref[2:3, :] * lam_ref[3:4, :], axis=-1, keepdims=True)
    lam = jnp.exp(s1) - jnp.exp(s2) + lam_init
    o = jnp.transpose(outs[0] - outs[1] * lam)
    o_ref[...] = (_rms(o, SUBLN_EPS) * gs_ref[...] * (1.0 - lam_init)).astype(BF16)


def _attention(lam4, gs, q, kv_sets, *, lam_init, b, heads, tq, name):
    nq, aw = q.shape
    vw = 2 * DA_HEAD_DIM
    per_b = nq // b // tq
    q_spec = pl.BlockSpec((tq, vw), lambda bi, h, i: (bi * per_b + i, h))
    in_specs = [pl.BlockSpec((4, DA_HEAD_DIM), lambda bi, h, i: (0, 0)),
                pl.BlockSpec((1, vw), lambda bi, h, i: (0, 0)),
                q_spec]
    args = [lam4, gs, q]
    for k, vt in kv_sets:
        sk = k.shape[0] // b
        in_specs += [pl.BlockSpec((sk, vw), lambda bi, h, i: (bi, h)),
                     pl.BlockSpec((vw, sk), lambda bi, h, i: (h, bi))]
        args += [k, vt]
    return pl.pallas_call(
        functools.partial(_attn_kernel, lam_init=lam_init),
        grid=(b, heads, per_b),
        in_specs=in_specs,
        out_specs=q_spec,
        out_shape=jax.ShapeDtypeStruct((nq, aw), BF16),
        compiler_params=_cparams(("arbitrary", "arbitrary", "arbitrary")),
        name=name,
    )(*args)


def _ssm_weights(a_re, a_im, log_dt, b_re, b_im, c_re, c_im):
    L, H, P = SSM_CHUNK, SSM_GROUP, SSM_STATE
    g = a_re.shape[1]
    npair = g // 2
    hp = lax.Precision.HIGHEST
    ar, ai = a_re.astype(F32), a_im.astype(F32)
    dt = jnp.exp(log_dt.astype(F32))[..., None]
    n = jnp.arange(L + 1, dtype=F32)
    mag = jnp.exp((ar * dt)[..., None] * n)
    ang = (ai * dt)[..., None] * n
    apr, api = mag * jnp.cos(ang), mag * jnp.sin(ang)
    xr, xi = apr[..., 1] - 1.0, api[..., 1]
    den = ar * ar + ai * ai
    qr, qi = (xr * ar + xi * ai) / den, (xi * ar - xr * ai) / den
    br, bi = b_re.astype(F32), b_im.astype(F32)
    bbr = qr[..., None] * br - qi[..., None] * bi
    bbi = qr[..., None] * bi + qi[..., None] * br
    cr, ci = c_re.astype(F32), c_im.astype(F32)

    car = cr[..., None] * apr[:, :, None, :, :L] - ci[..., None] * api[:, :, None, :, :L]
    cai = cr[..., None] * api[:, :, None, :, :L] + ci[..., None] * apr[:, :, None, :, :L]
    kt = (jnp.einsum('dghpt,dgpk->dgkth', car, bbr, precision=hp)
          - jnp.einsum('dghpt,dgpk->dgkth', cai, bbi, precision=hp))
    kc = jnp.concatenate([jnp.flip(kt[1][:, :, 1:], axis=2), kt[0][:, :, :1] + kt[1][:, :, :1], kt[0][:, :, 1:]],
                         axis=2).reshape(g, H, (2 * L - 1) * H)
    w_intra = jnp.stack([kc[:, :, (L - 1 - s) * H:(2 * L - 1 - s) * H] for s in range(L)], axis=1)
    w_intra = w_intra.reshape(g, L * H, L * H)

    def pair_cols(x_re, x_im):
        z = jnp.zeros_like(x_re[:, 0::2])
        e0 = jnp.concatenate([x_re[:, 0::2], z, x_im[:, 0::2], z], axis=-1)
        e1 = jnp.concatenate([z, x_re[:, 1::2], z, x_im[:, 1::2]], axis=-1)
        return jnp.stack([e0, e1], axis=2)

    sir = jnp.stack([jnp.flip(apr[0, ..., :L], axis=-1), apr[1, ..., :L]])
    sii = jnp.stack([jnp.flip(api[0, ..., :L], axis=-1), api[1, ..., :L]])
    sir, sii = jnp.swapaxes(sir, 2, 3)[:, :, :, None, :], jnp.swapaxes(sii, 2, 3)[:, :, :, None, :]
    tbr, tbi = jnp.swapaxes(bbr, 2, 3)[:, :, None], jnp.swapaxes(bbi, 2, 3)[:, :, None]
    w_sin = pair_cols(sir * tbr - sii * tbi, sir * tbi + sii * tbr)
    w_sin = w_sin.reshape(2, npair, 2 * L * H, 4 * P)

    sor = jnp.stack([apr[0, ..., 1:], jnp.flip(apr[1, ..., 1:], axis=-1)])
    soi = jnp.stack([api[0, ..., 1:], jnp.flip(api[1, ..., 1:], axis=-1)])
    tcr, tci = jnp.swapaxes(cr, 2, 3)[:, :, :, None, :], jnp.swapaxes(ci, 2, 3)[:, :, :, None, :]
    cnr = (tcr * sor[..., None] - tci * soi[..., None]).reshape(2, g, P, L * H)
    cni = (tcr * soi[..., None] + tci * sor[..., None]).reshape(2, g, P, L * H)
    zo = jnp.zeros_like(cnr[:, 0::2])
    w_so = jnp.stack([jnp.concatenate([cnr[:, 0::2], zo], axis=-1), jnp.concatenate([zo, cnr[:, 1::2]], axis=-1),
                      jnp.concatenate([-cni[:, 0::2], zo], axis=-1), jnp.concatenate([zo, -cni[:, 1::2]], axis=-1)],
                     axis=2).reshape(2, npair, 4 * P, 2 * L * H)

    al = jnp.stack([apr[..., L].reshape(2, npair, 2 * P), api[..., L].reshape(2, npair, 2 * P)], axis=2)
    al = al.reshape(2, 2 * npair, 1, 2 * P)
    return w_intra.astype(BF16), w_sin.astype(BF16), w_so.astype(BF16), al


def _atom_transpose_matrix():
    n = OCT * OCT * SSM_GROUP
    i = np.arange(n)
    j = ((i // SSM_GROUP) % OCT) * LANES + (i // LANES) * SSM_GROUP + i % SSM_GROUP
    p = np.zeros((n, n), np.float32)
    p[i, j] = 1.0
    return jnp.asarray(p, BF16)


def _ssm_in_kernel(us_ref, p_ref, w_ref, ut_ref, s_ref):
    n = ut_ref.shape[1]
    noct = us_ref.shape[0]
    half = SSM_CHUNK // 2
    src = []
    for o in range(noct):
        for hf in range(2):
            rows = [us_ref[o, pl.ds(half * hf + sl, n, stride=SSM_CHUNK), :] for sl in range(half)]
            src.append(jnp.concatenate(rows, axis=1).astype(BF16))
    perm = jnp.dot(jnp.concatenate(src, axis=0), p_ref[...], preferred_element_type=F32).astype(BF16)

    def group_chunk(gi):
        o, gl = gi // OCT, gi % OCT
        return jnp.concatenate([perm[(2 * o + hf) * n:(2 * o + hf + 1) * n, gl * LANES:(gl + 1) * LANES]
                                for hf in range(2)], axis=1)

    for j in range(noct * OCT // 2):
        ug = [group_chunk(2 * j), group_chunk(2 * j + 1)]
        ut_ref[2 * j] = ug[0]
        ut_ref[2 * j + 1] = ug[1]
        up = jnp.concatenate(ug, axis=1)
        for d in range(2):
            s = jnp.dot(up, w_ref[d, j], preferred_element_type=F32)
            s_ref[d, 2 * j] = s[:, :LANES]
            s_ref[d, 2 * j + 1] = s[:, LANES:]


def _ssm_scan_kernel(sc_ref, sl_ref, al_ref, hc_ref, hl_ref, *, b):
    d = pl.program_id(0)
    nblk = sc_ref.shape[1]
    ars = [al_ref[0, 2 * k] for k in range(nblk // 2)]
    ais = [al_ref[0, 2 * k + 1] for k in range(nblk // 2)]

    def phase(s_ref, h_ref, carry):
        nch = s_ref.shape[2] // b

        def body(i, carry):
            c = jnp.where(d == 0, i, nch - 1 - i)
            rows = pl.ds(c, b, stride=nch)
            new = []
            for k in range(nblk // 2):
                hr, hi = carry[2 * k], carry[2 * k + 1]
                h_ref[0, 2 * k, rows, :] = hr
                h_ref[0, 2 * k + 1, rows, :] = hi
                sr = s_ref[0, 2 * k, rows, :]
                si = s_ref[0, 2 * k + 1, rows, :]
                new.append(ars[k] * hr - ais[k] * hi + sr)
                new.append(ars[k] * hi + ais[k] * hr + si)
            return tuple(new)

        return lax.fori_loop(0, nch, body, carry)

    carry = tuple(jnp.zeros((b, LANES), F32) for _ in range(nblk))
    carry = phase(sc_ref, hc_ref, carry)
    phase(sl_ref, hl_ref, carry)


def _ssm_out_kernel(ut_ref, h_ref, wi_ref, wo_ref, p_ref, y_ref):
    n = ut_ref.shape[1]
    ng = ut_ref.shape[0]
    noct = ng // OCT
    half = SSM_CHUNK // 2
    ys = []
    for j in range(ng // 2):
        hp = [jnp.concatenate([h_ref[d, 2 * j], h_ref[d, 2 * j + 1]], axis=1).astype(BF16) for d in range(2)]
        for e in range(2):
            gi = 2 * j + e
            cols = slice(e * SSM_CW, (e + 1) * SSM_CW)
            ys.append(jnp.dot(ut_ref[gi], wi_ref[gi], preferred_element_type=F32)
                      + jnp.dot(hp[0], wo_ref[0, j, :, cols], preferred_element_type=F32)
                      + jnp.dot(hp[1], wo_ref[1, j, :, cols], preferred_element_type=F32))
    rows = []
    for o in range(noct):
        for hf in range(2):
            rows.append(jnp.concatenate([ys[OCT * o + gl][:, hf * LANES:(hf + 1) * LANES] for gl in range(OCT)],
                                        axis=1))
    ycat = jnp.concatenate(rows, axis=0)
    hi = ycat.astype(BF16)
    lo = (ycat - hi.astype(F32)).astype(BF16)
    res = (jnp.dot(hi, p_ref[...], preferred_element_type=F32)
           + jnp.dot(lo, p_ref[...], preferred_element_type=F32))
    for o in range(noct):
        for hf in range(2):
            blk = 2 * o + hf
            for tl in range(half):
                y_ref[o, pl.ds(half * hf + tl, n, stride=SSM_CHUNK), :] = (
                    res[blk * n:(blk + 1) * n, tl * LANES:(tl + 1) * LANES])


def _ssm_conv(us_lat, us_ctx, weights, perm, *, b):
    w_intra, w_sin, w_so, al = weights
    P = SSM_STATE
    ng = w_intra.shape[0]
    nblk = ng

    def stage_in(us, steps):
        noct, rows, _ = us.shape
        nchunks = rows // SSM_CHUNK
        n = nchunks // steps
        return pl.pallas_call(
            _ssm_in_kernel,
            grid=(steps,),
            in_specs=[pl.BlockSpec((noct, n * SSM_CHUNK, LANES), lambda i: (0, i, 0)),
                      _resident(perm.shape), _resident(w_sin.shape)],
            out_specs=[pl.BlockSpec((ng, n, SSM_CW), lambda i: (0, i, 0)),
                       pl.BlockSpec((2, nblk, n, LANES), lambda i: (0, 0, i, 0))],
            out_shape=[jax.ShapeDtypeStruct((ng, nchunks, SSM_CW), BF16),
                       jax.ShapeDtypeStruct((2, nblk, nchunks, LANES), F32)],
            compiler_params=_cparams(("arbitrary",)),
            name="ssm_in",
        )(us, perm, w_sin)

    lat_steps = 2 * b
    ut_lat, s_lat = stage_in(us_lat, lat_steps)
    ut_ctx, s_ctx = stage_in(us_ctx, 1)

    cb = 8
    nc_rows, nl_rows = s_ctx.shape[2], s_lat.shape[2]
    h_ctx, h_lat = pl.pallas_call(
        functools.partial(_ssm_scan_kernel, b=b),
        grid=(2, nblk // cb),
        in_specs=[pl.BlockSpec((1, cb, nc_rows, LANES), lambda d, j: (d, j, 0, 0)),
                  pl.BlockSpec((1, cb, nl_rows, LANES), lambda d, j: (d, j, 0, 0)),
                  pl.BlockSpec((1, cb, 1, 2 * P), lambda d, j: (d, j, 0, 0))],
        out_specs=[pl.BlockSpec((1, cb, nc_rows, LANES), lambda d, j: (d, j, 0, 0)),
                   pl.BlockSpec((1, cb, nl_rows, LANES), lambda d, j: (d, j, 0, 0))],
        out_shape=[jax.ShapeDtypeStruct(s_ctx.shape, F32), jax.ShapeDtypeStruct(s_lat.shape, F32)],
        compiler_params=_cparams(("arbitrary", "arbitrary")),
        name="ssm_scan",
    )(s_ctx, s_lat, al)

    def stage_out(ut, hin, steps):
        _, nchunks, _ = ut.shape
        n = nchunks // steps
        noct = ng // OCT
        return pl.pallas_call(
            _ssm_out_kernel,
            grid=(steps,),
            in_specs=[pl.BlockSpec((ng, n, SSM_CW), lambda i: (0, i, 0)),
                      pl.BlockSpec((2, nblk, n, LANES), lambda i: (0, 0, i, 0)),
                      _resident(w_intra.shape), _resident(w_so.shape), _resident(perm.shape)],
            out_specs=pl.BlockSpec((noct, n * SSM_CHUNK, LANES), lambda i: (0, i, 0)),
            out_shape=jax.ShapeDtypeStruct((noct, nchunks * SSM_CHUNK, LANES), F32),
            compiler_params=_cparams(("arbitrary",)),
            name="ssm_out",
        )(ut, hin, w_intra, w_so, perm)

    return stage_out(ut_lat, h_lat, lat_steps), stage_out(ut_ctx, h_ctx, 1)


def _fourier_kernel(u_ref, w_ref, cos_ref, sin_ref, b_ref, o_ref, a_s, b_s, *, groups):
    gc = u_ref.shape[1] // groups

    @pl.when(pl.program_id(1) == 0)
    def _():
        for g in range(groups):
            ab = jnp.dot(u_ref[:, g * gc:(g + 1) * gc], w_ref[g], preferred_element_type=F32)
            a_s[:, g * gc:(g + 1) * gc] = ab[:, :gc].astype(BF16)
            b_s[:, g * gc:(g + 1) * gc] = ab[:, gc:].astype(BF16)

    o_ref[...] = (jnp.dot(cos_ref[...], a_s[...], preferred_element_type=F32)
                  - jnp.dot(sin_ref[...], b_s[...], preferred_element_type=F32)
                  + b_ref[...]).astype(BF16)


def _dft_tables(n):
    k = jnp.arange(n, dtype=jnp.int32)
    ang = ((k[:, None] * k[None, :]) % n).astype(F32) * (2.0 * math.pi / n)
    return jnp.cos(ang).astype(BF16), jnp.sin(ang).astype(BF16)


def _fourier(uf, w_cs, bias, tables, *, b, tm):
    nr, fw = uf.shape
    seq = nr // b
    groups = w_cs.shape[0]
    cos_t, sin_t = tables
    tm = min(tm, seq)
    per_b = seq // tm
    return pl.pallas_call(
        functools.partial(_fourier_kernel, groups=groups),
        grid=(b, per_b),
        in_specs=[
            pl.BlockSpec((seq, fw), lambda bi, i: (bi, 0)),
            pl.BlockSpec(w_cs.shape, lambda bi, i: (0, 0, 0)),
            pl.BlockSpec((tm, seq), lambda bi, i: (i, 0)),
            pl.BlockSpec((tm, seq), lambda bi, i: (i, 0)),
            pl.BlockSpec((1, fw), lambda bi, i: (0, 0)),
        ],
        out_specs=pl.BlockSpec((tm, fw), lambda bi, i: (bi * per_b + i, 0)),
        out_shape=jax.ShapeDtypeStruct((nr, fw), BF16),
        scratch_shapes=[pltpu.VMEM((seq, fw), BF16), pltpu.VMEM((seq, fw), BF16)],
        compiler_params=_cparams(("arbitrary", "arbitrary")),
        name="fourier_%d" % seq,
    )(uf, w_cs, cos_t, sin_t, bias)


def _fourier_weights(w_four, seq):
    groups, gc, _ = w_four.shape
    k = np.arange(gc)
    ang = ((k[:, None] * k[None, :]) % gc).astype(np.float64) * (2.0 * np.pi / gc)
    norm = 1.0 / math.sqrt(seq * gc)
    cc = jnp.asarray(np.cos(ang) * norm, F32)
    sc = jnp.asarray(np.sin(ang) * norm, F32)
    hp = lax.Precision.HIGHEST
    wc = jnp.einsum('ck,gkd->gcd', cc, w_four.astype(F32), precision=hp)
    ws = jnp.einsum('ck,gkd->gcd', sc, w_four.astype(F32), precision=hp)
    return jnp.concatenate([wc, ws], axis=-1).astype(BF16)


def _outproj_kernel(att_ref, us_ref, yc_ref, uf_ref, x_ref, mod_ref, wo_ref, wg_ref, bg_ref, dsk_ref,
                    gpost_ref, gpre_ref, xo_ref, h2_ref, *, aw, sw):
    nblk = us_ref.shape[0]
    us = jnp.concatenate([us_ref[j] for j in range(nblk)], axis=1)
    yc = jnp.concatenate([yc_ref[j] for j in range(nblk)], axis=1)
    g = jax.nn.gelu(dsk_ref[...] * us + yc)
    z = jnp.dot(g.astype(BF16), wg_ref[...], preferred_element_type=F32) + bg_ref[...]
    ssm = (g * jax.nn.sigmoid(z)).astype(BF16)
    mix = (jnp.dot(att_ref[...], wo_ref[0:aw, :], preferred_element_type=F32)
           + jnp.dot(ssm, wo_ref[aw:aw + sw, :], preferred_element_type=F32)
           + jnp.dot(uf_ref[...], wo_ref[aw + sw:, :], preferred_element_type=F32))
    xn = x_ref[...] + mod_ref[0, 2:3, :] * (_rms(mix, NORM_EPS) * gpost_ref[...])
    xo_ref[...] = xn
    h2 = _rms(xn, NORM_EPS) * gpre_ref[...]
    h2_ref[...] = (h2 * (1.0 + mod_ref[0, 4:5, :]) + mod_ref[0, 3:4, :]).astype(BF16)


def _outproj(att, us, yc, four, x, mod, w_out, w_glu, b_glu, dsk, g_post, g_pre, *, mod_row, tm):
    n_rows, d = x.shape
    aw, fw = att.shape[1], four.shape[1]
    nblk = us.shape[0]
    sw = nblk * LANES

    def row(i):
        return (i, 0)

    def const(i):
        return (0, 0)

    blk3 = pl.BlockSpec((nblk, tm, LANES), lambda i: (0, i, 0))
    return pl.pallas_call(
        functools.partial(_outproj_kernel, aw=aw, sw=sw),
        grid=(n_rows // tm,),
        in_specs=[
            pl.BlockSpec((tm, aw), row), blk3, blk3,
            pl.BlockSpec((tm, fw), row), pl.BlockSpec((tm, d), row),
            pl.BlockSpec((1, N_MOD, d), lambda i: (mod_row(i), 0, 0)),
            _resident(w_out.shape),
            pl.BlockSpec(w_glu.shape, const), pl.BlockSpec((1, sw), const), pl.BlockSpec((1, sw), const),
            pl.BlockSpec((1, d), const), pl.BlockSpec((1, d), const),
        ],
        out_specs=[pl.BlockSpec((tm, d), row), pl.BlockSpec((tm, d), row)],
        out_shape=[jax.ShapeDtypeStruct((n_rows, d), F32), jax.ShapeDtypeStruct((n_rows, d), BF16)],
        compiler_params=_cparams(("arbitrary",)),
        name="outproj",
    )(att, us, yc, four, x, mod, w_out, w_glu, b_glu, dsk, g_post, g_pre)


def _ffn_kernel(h_ref, x_ref, mod_ref, g_ref, wg_ref, wu_ref, wd_ref, o_ref):
    k = pl.program_id(1)
    h = h_ref[...]
    a = jnp.dot(h, wg_ref[...], preferred_element_type=F32)
    u = jnp.dot(h, wu_ref[...], preferred_element_type=F32)
    part = jnp.dot((_silu(a) * u).astype(BF16), wd_ref[...], preferred_element_type=F32)

    @pl.when(k == 0)
    def _():
        o_ref[...] = part

    @pl.when(k > 0)
    def _():
        o_ref[...] += part

    @pl.when(k == pl.num_programs(1) - 1)
    def _():
        o_ref[...] = x_ref[...] + mod_ref[0, 5:6, :] * (_rms(o_ref[...], NORM_EPS) * g_ref[...])


def _ffn(h2, x_mid, mod, g_post, w_gate, w_up, w_down, *, mod_row, tm, tf):
    n_rows, d = h2.shape
    dff = w_gate.shape[1]
    return pl.pallas_call(
        _ffn_kernel,
        grid=(n_rows // tm, dff // tf),
        in_specs=[
            pl.BlockSpec((tm, d), lambda i, k: (i, 0)),
            pl.BlockSpec((tm, d), lambda i, k: (i, 0)),
            pl.BlockSpec((1, N_MOD, d), lambda i, k: (mod_row(i), 0, 0)),
            pl.BlockSpec((1, d), lambda i, k: (0, 0)),
            pl.BlockSpec((d, tf), lambda i, k: (0, k)),
            pl.BlockSpec((d, tf), lambda i, k: (0, k)),
            pl.BlockSpec((tf, d), lambda i, k: (k, 0)),
        ],
        out_specs=pl.BlockSpec((tm, d), lambda i, k: (i, 0)),
        out_shape=jax.ShapeDtypeStruct((n_rows, d), F32),
        compiler_params=_cparams(("arbitrary", "arbitrary")),
        name="ffn",
    )(h2, x_mid, mod, g_post, w_gate, w_up, w_down)


def kernel(x, c, ctx, c_ctx, w_mod, b_mod, g_mix_pre, g_mix_post, g_ffn_pre, g_ffn_post, w_in, w_out, lam_q1, lam_k1, lam_q2, lam_k2, g_subln, ssm_a_re, ssm_a_im, ssm_log_dt, ssm_b_re, ssm_b_im, ssm_c_re, ssm_c_im, ssm_d, w_glu, b_glu, w_four, b_four, w_gate, w_up, w_down):
    b, t, d = x.shape
    n_ctx = ctx.shape[1]
    depth = w_mod.shape[0]
    nl, nc = b * t, b * n_ctx
    aw = d // 2
    sw = ssm_d.shape[1]
    fw = d - aw - sw
    heads = aw // (2 * DA_HEAD_DIM)
    tm = 512
    assert t % tm == 0 and nc % tm == 0 and n_ctx % SSM_CHUNK == 0 and t % GRID_W == 0 and b + 1 <= MOD_ROWS
    assert sw % LANES == 0 and (sw // SSM_GROUP) % (2 * OCT) == 0

    cs = jnp.concatenate([c, c_ctx[None, :], jnp.zeros((MOD_ROWS - b - 1, d), F32)], axis=0)
    mod_all = _modulation(cs, w_mod, b_mod).reshape(depth, MOD_ROWS, N_MOD, d)

    per_b = t // tm
    lat_mod = lambda i: i // per_b
    ctx_mod = lambda i: b
    tabs = _rope_tables(t, tm)
    dft_lat = _dft_tables(t)
    dft_ctx = _dft_tables(n_ctx)
    perm = _atom_transpose_matrix()
    x_lat, x_ctx = x.reshape(nl, d), ctx.reshape(nc, d)

    for l in range(depth):
        need_ctx = l < depth - 1
        lam_init = 0.8 - 0.6 * math.exp(-0.3 * l)
        mod = mod_all[l]
        w_inl = w_in[l].astype(BF16)
        w_parts = (w_inl[:, :2 * aw], jnp.transpose(w_inl[:, 2 * aw:3 * aw]), w_inl[:, 3 * aw:])
        g_pre = g_mix_pre[l][None, :]
        q, k, vt, us, uf = _inproj(x_lat, mod, g_pre, *w_parts, tabs, mod_row=lat_mod,
                                   tab_blk=lambda i: i % per_b, aw=aw, sw=sw, fw=fw, tm=tm)
        qc, kc, vtc, usc, ufc = _inproj(x_ctx, mod, g_pre, *w_parts, tabs, mod_row=ctx_mod,
                                        tab_blk=lambda i: per_b, aw=aw, sw=sw, fw=fw, tm=tm)

        lam4 = jnp.stack([lam_q1[l], lam_k1[l], lam_q2[l], lam_k2[l]]).astype(F32)
        gs = g_subln[l][None, :].astype(F32)
        att = _attention(lam4, gs, q, [(kc, vtc), (k, vt)], lam_init=lam_init, b=b, heads=heads, tq=512,
                         name="attn_latent")
        ssm_w = _ssm_weights(ssm_a_re[l], ssm_a_im[l], ssm_log_dt[l], ssm_b_re[l], ssm_b_im[l],
                             ssm_c_re[l], ssm_c_im[l])
        yc, ycc = _ssm_conv(us, usc, ssm_w, perm, b=b)
        bias = b_four[l].reshape(1, fw).astype(F32)
        four = _fourier(uf, _fourier_weights(w_four[l], t), bias, dft_lat, b=b, tm=tm)

        wo, wgl = w_out[l].astype(BF16), w_glu[l].astype(BF16)
        small = (b_glu[l][None, :].astype(F32), ssm_d[l][None, :].astype(F32), g_mix_post[l][None, :],
                 g_ffn_pre[l][None, :])
        ffn_w = (g_ffn_post[l][None, :], w_gate[l].astype(BF16), w_up[l].astype(BF16), w_down[l].astype(BF16))
        x_mid, h2 = _outproj(att, us, yc, four, x_lat, mod, wo, wgl, *small, mod_row=lat_mod, tm=tm)
        x_lat = _ffn(h2, x_mid, mod, *ffn_w, mod_row=lat_mod, tm=tm, tf=512)

        if need_ctx:
            att_c = _attention(lam4, gs, qc, [(kc, vtc)], lam_init=lam_init, b=b, heads=heads, tq=n_ctx,
                               name="attn_ctx")
            four_c = _fourier(ufc, _fourier_weights(w_four[l], n_ctx), bias, dft_ctx, b=b, tm=tm)
            xc_mid, h2c = _outproj(att_c, usc, ycc, four_c, x_ctx, mod, wo, wgl, *small, mod_row=ctx_mod, tm=tm)
            x_ctx = _ffn(h2c, xc_mid, mod, *ffn_w, mod_row=ctx_mod, tm=tm, tf=512)
    return x_lat.reshape(b, t, d)
```

```python
import functools
import math

import jax
import jax.numpy as jnp
import numpy as np
from jax import lax
from jax.experimental import pallas as pl
from jax.experimental.pallas import tpu as pltpu

F32 = jnp.float32
BF16 = jnp.bfloat16

LANES = 128
GRID_W = 64
DA_HEAD_DIM = 128
SSM_GROUP = 16
SSM_STATE = 64
N_MOD = 6
ROPE_BASE = 10000.0
ROPE_PAIRS = DA_HEAD_DIM // 4
NORM_EPS = 1e-6
SUBLN_EPS = 1e-5

SSM_CHUNK = 16
SSM_CW = SSM_CHUNK * SSM_GROUP
OCT = LANES // SSM_GROUP
MOD_ROWS = 8
VMEM_LIMIT = 56 * 1024 * 1024
NT_DIMS = (((1,), (1,)), ((), ()))


def _cparams(sem):
    return pltpu.CompilerParams(dimension_semantics=sem, vmem_limit_bytes=VMEM_LIMIT)


def _rms(x, eps):
    return x * lax.rsqrt(jnp.mean(x * x, axis=-1, keepdims=True) + eps)


def _silu(x):
    return x * jax.nn.sigmoid(x)


def _resident(shape):
    nd = len(shape)
    return pl.BlockSpec(shape, lambda *_: (0,) * nd, pipeline_mode=pl.Buffered(1))


def _mod_kernel(c_ref, w_ref, b_ref, o_ref):
    s = _silu(c_ref[...]).astype(BF16)
    o_ref[0] = jnp.dot(s, w_ref[0].astype(BF16), preferred_element_type=F32) + b_ref[0]


def _modulation(cs, w_mod, b_mod, tn=1024):
    depth, d, n = w_mod.shape
    return pl.pallas_call(
        _mod_kernel,
        grid=(depth, n // tn),
        in_specs=[
            pl.BlockSpec((MOD_ROWS, d), lambda l, j: (0, 0)),
            pl.BlockSpec((1, d, tn), lambda l, j: (l, 0, j)),
            pl.BlockSpec((1, 1, tn), lambda l, j: (l, 0, j)),
        ],
        out_specs=pl.BlockSpec((1, MOD_ROWS, tn), lambda l, j: (l, 0, j)),
        out_shape=jax.ShapeDtypeStruct((depth, MOD_ROWS, n), F32),
        compiler_params=_cparams(("arbitrary", "arbitrary")),
        name="modulation",
    )(cs, w_mod, b_mod.reshape(depth, 1, n))


def _inproj_kernel(x_ref, mod_ref, g_ref, wqk_ref, wvt_ref, wsf_ref, cq_ref, sq_ref, ck_ref, sk_ref,
                   q_ref, k_ref, vt_ref, us_ref, uf_ref, *, aw, sw):
    tm = x_ref.shape[0]
    h = _rms(x_ref[...], NORM_EPS) * g_ref[...]
    h = h * (1.0 + mod_ref[0, 1:2, :]) + mod_ref[0, 0:1, :]
    hb = h.astype(BF16)

    lane = lax.broadcasted_iota(jnp.int32, (tm, DA_HEAD_DIM), 1)
    first_half = (lane & (2 * ROPE_PAIRS - 1)) < ROPE_PAIRS

    def rope(z, c, s):
        partner = jnp.where(first_half, pltpu.roll(z, DA_HEAD_DIM - ROPE_PAIRS, 1), pltpu.roll(z, ROPE_PAIRS, 1))
        return z * c + partner * s

    zq = jnp.dot(hb, wqk_ref[:, 0:aw], preferred_element_type=F32)
    cq, sq = cq_ref[...], sq_ref[...]
    for j in range(aw // DA_HEAD_DIM):
        sl = slice(j * DA_HEAD_DIM, (j + 1) * DA_HEAD_DIM)
        q_ref[:, sl] = rope(zq[:, sl], cq, sq).astype(BF16)
    zk = jnp.dot(hb, wqk_ref[:, aw:2 * aw], preferred_element_type=F32)
    ck, sk = ck_ref[...], sk_ref[...]
    for j in range(aw // DA_HEAD_DIM):
        sl = slice(j * DA_HEAD_DIM, (j + 1) * DA_HEAD_DIM)
        k_ref[:, sl] = rope(zk[:, sl], ck, sk).astype(BF16)
    vt_ref[...] = lax.dot_general(wvt_ref[...], hb, NT_DIMS, preferred_element_type=F32).astype(BF16)
    zs = jnp.dot(hb, wsf_ref[:, 0:sw], preferred_element_type=F32)
    for j in range(sw // LANES):
        us_ref[j] = zs[:, j * LANES:(j + 1) * LANES]
    uf_ref[...] = jnp.dot(hb, wsf_ref[:, sw:], preferred_element_type=F32).astype(BF16)


def _inproj(x, mod, g, w_qk, w_vt, w_sf, tabs, *, mod_row, tab_blk, aw, sw, fw, tm):
    nr, d = x.shape

    def row(i):
        return (i, 0)

    tab_spec = pl.BlockSpec((tm, DA_HEAD_DIM), lambda i: (tab_blk(i), 0))
    return pl.pallas_call(
        functools.partial(_inproj_kernel, aw=aw, sw=sw),
        grid=(nr // tm,),
        in_specs=[
            pl.BlockSpec((tm, d), row),
            pl.BlockSpec((1, N_MOD, d), lambda i: (mod_row(i), 0, 0)),
            pl.BlockSpec((1, d), lambda i: (0, 0)),
            _resident(w_qk.shape), _resident(w_vt.shape), _resident(w_sf.shape),
            tab_spec, tab_spec, tab_spec, tab_spec,
        ],
        out_specs=[
            pl.BlockSpec((tm, aw), row), pl.BlockSpec((tm, aw), row), pl.BlockSpec((aw, tm), lambda i: (0, i)),
            pl.BlockSpec((sw // LANES, tm, LANES), lambda i: (0, i, 0)), pl.BlockSpec((tm, fw), row),
        ],
        out_shape=[
            jax.ShapeDtypeStruct((nr, aw), BF16), jax.ShapeDtypeStruct((nr, aw), BF16),
            jax.ShapeDtypeStruct((aw, nr), BF16), jax.ShapeDtypeStruct((sw // LANES, nr, LANES), F32),
            jax.ShapeDtypeStruct((nr, fw), BF16),
        ],
        compiler_params=_cparams(("arbitrary",)),
        name="inproj",
    )(x, mod, g, w_qk, w_vt, w_sf, *tabs)


def _rope_tables(t, tm):
    rows = t // GRID_W
    r = jnp.broadcast_to(jnp.arange(rows, dtype=F32)[:, None], (rows, GRID_W)).reshape(-1)
    col = jnp.broadcast_to(jnp.arange(GRID_W, dtype=F32)[None, :], (rows, GRID_W)).reshape(-1)
    inv = ROPE_BASE ** (-jnp.arange(ROPE_PAIRS, dtype=F32) / ROPE_PAIRS)
    ar, ac = r[:, None] * inv, col[:, None] * inv
    cos = jnp.concatenate([jnp.cos(ar), jnp.cos(ar), jnp.cos(ac), jnp.cos(ac)], axis=1)
    sin = jnp.concatenate([-jnp.sin(ar), jnp.sin(ar), -jnp.sin(ac), jnp.sin(ac)], axis=1)
    cos = jnp.concatenate([cos, jnp.ones((tm, DA_HEAD_DIM), F32)], axis=0)
    sin = jnp.concatenate([sin, jnp.zeros((tm, DA_HEAD_DIM), F32)], axis=0)
    scale = DA_HEAD_DIM ** -0.5 * math.log2(math.e)
    return cos * scale, sin * scale, cos, sin


def _attn_kernel(lam_ref, gs_ref, q_ref, *refs, lam_init):
    o_ref = refs[-1]
    k_refs, vt_refs = refs[0:-1:2], refs[1:-1:2]
    hd = DA_HEAD_DIM
    q = q_ref[...]
    outs = []
    for idx in range(2):
        qi = q[:, idx * hd:(idx + 1) * hd]
        ss = [lax.dot_general(kr[:, idx * hd:(idx + 1) * hd], qi, NT_DIMS, preferred_element_type=F32)
              for kr in k_refs]
        m = functools.reduce(jnp.maximum, [jnp.max(s, axis=0, keepdims=True) for s in ss])
        ps = [jnp.exp2(s - m) for s in ss]
        l = functools.reduce(jnp.add, [jnp.sum(p, axis=0, keepdims=True) for p in ps])
        acc = functools.reduce(jnp.add, [jnp.dot(vr[...], p.astype(BF16), preferred_element_type=F32)
                                         for vr, p in zip(vt_refs, ps)])
        outs.append(acc * (1.0 / l))

    s1 = jnp.sum(lam_ref[0:1, :] * lam_ref[1:2, :], axis=-1, keepdims=True)
    s2 = jnp.sum(lam_ref[2:3, :] * lam_ref[3:4, :], axis=-1, keepdims=True)
    lam = jnp.exp(s1) - jnp.exp(s2) + lam_init
    o = jnp.transpose(outs[0] - outs[1] * lam)
    o_ref[...] = (_rms(o, SUBLN_EPS) * gs_ref[...] * (1.0 - lam_init)).astype(BF16)


def _attention(lam4, gs, q, kv_sets, *, lam_init, b, heads, tq, name):
    nq, aw = q.shape
    vw = 2 * DA_HEAD_DIM
    per_b = nq // b // tq
    q_spec = pl.BlockSpec((tq, vw), lambda bi, h, i: (bi * per_b + i, h))
    in_specs = [pl.BlockSpec((4, DA_HEAD_DIM), lambda bi, h, i: (0, 0)),
                pl.BlockSpec((1, vw), lambda bi, h, i: (0, 0)),
                q_spec]
    args = [lam4, gs, q]
    for k, vt in kv_sets:
        sk = k.shape[0] // b
        in_specs += [pl.BlockSpec((sk, vw), lambda bi, h, i: (bi, h)),
                     pl.BlockSpec((vw, sk), lambda bi, h, i: (h, bi))]
        args += [k, vt]
    return pl.pallas_call(
        functools.partial(_attn_kernel, lam_init=lam_init),
        grid=(b, heads, per_b),
        in_specs=in_specs,
        out_specs=q_spec,
        out_shape=jax.ShapeDtypeStruct((nq, aw), BF16),
        compiler_params=_cparams(("arbitrary", "arbitrary", "arbitrary")),
        name=name,
    )(*args)


def _ssm_weights(a_re, a_im, log_dt, b_re, b_im, c_re, c_im):
    L, H, P = SSM_CHUNK, SSM_GROUP, SSM_STATE
    g = a_re.shape[1]
    npair = g // 2
    hp = lax.Precision.HIGHEST
    ar, ai = a_re.astype(F32), a_im.astype(F32)
    dt = jnp.exp(log_dt.astype(F32))[..., None]
    n = jnp.arange(L + 1, dtype=F32)
    mag = jnp.exp((ar * dt)[..., None] * n)
    ang = (ai * dt)[..., None] * n
    apr, api = mag * jnp.cos(ang), mag * jnp.sin(ang)
    xr, xi = apr[..., 1] - 1.0, api[..., 1]
    den = ar * ar + ai * ai
    qr, qi = (xr * ar + xi * ai) / den, (xi * ar - xr * ai) / den
    br, bi = b_re.astype(F32), b_im.astype(F32)
    bbr = qr[..., None] * br - qi[..., None] * bi
    bbi = qr[..., None] * bi + qi[..., None] * br
    cr, ci = c_re.astype(F32), c_im.astype(F32)

    car = cr[..., None] * apr[:, :, None, :, :L] - ci[..., None] * api[:, :, None, :, :L]
    cai = cr[..., None] * api[:, :, None, :, :L] + ci[..., None] * apr[:, :, None, :, :L]
    kt = (jnp.einsum('dghpt,dgpk->dgkth', car, bbr, precision=hp)
          - jnp.einsum('dghpt,dgpk->dgkth', cai, bbi, precision=hp))
    kc = jnp.concatenate([jnp.flip(kt[1][:, :, 1:], axis=2), kt[0][:, :, :1] + kt[1][:, :, :1], kt[0][:, :, 1:]],
                         axis=2).reshape(g, H, (2 * L - 1) * H)
    w_intra = jnp.stack([kc[:, :, (L - 1 - s) * H:(2 * L - 1 - s) * H] for s in range(L)], axis=1)
    w_intra = w_intra.reshape(g, L * H, L * H)

    def pair_cols(x_re, x_im):
        z = jnp.zeros_like(x_re[:, 0::2])
        e0 = jnp.concatenate([x_re[:, 0::2], z, x_im[:, 0::2], z], axis=-1)
        e1 = jnp.concatenate([z, x_re[:, 1::2], z, x_im[:, 1::2]], axis=-1)
        return jnp.stack([e0, e1], axis=2)

    sir = jnp.stack([jnp.flip(apr[0, ..., :L], axis=-1), apr[1, ..., :L]])
    sii = jnp.stack([jnp.flip(api[0, ..., :L], axis=-1), api[1, ..., :L]])
    sir, sii = jnp.swapaxes(sir, 2, 3)[:, :, :, None, :], jnp.swapaxes(sii, 2, 3)[:, :, :, None, :]
    tbr, tbi = jnp.swapaxes(bbr, 2, 3)[:, :, None], jnp.swapaxes(bbi, 2, 3)[:, :, None]
    w_sin = pair_cols(sir * tbr - sii * tbi, sir * tbi + sii * tbr)
    w_sin = w_sin.reshape(2, npair, 2 * L * H, 4 * P)

    sor = jnp.stack([apr[0, ..., 1:], jnp.flip(apr[1, ..., 1:], axis=-1)])
    soi = jnp.stack([api[0, ..., 1:], jnp.flip(api[1, ..., 1:], axis=-1)])
    tcr, tci = jnp.swapaxes(cr, 2, 3)[:, :, :, None, :], jnp.swapaxes(ci, 2, 3)[:, :, :, None, :]
    cnr = (tcr * sor[..., None] - tci * soi[..., None]).reshape(2, g, P, L * H)
    cni = (tcr * soi[..., None] + tci * sor[..., None]).reshape(2, g, P, L * H)
    zo = jnp.zeros_like(cnr[:, 0::2])
    w_so = jnp.stack([jnp.concatenate([cnr[:, 0::2], zo], axis=-1), jnp.concatenate([zo, cnr[:, 1::2]], axis=-1),
                      jnp.concatenate([-cni[:, 0::2], zo], axis=-1), jnp.concatenate([zo, -cni[:, 1::2]], axis=-1)],
                     axis=2).reshape(2, npair, 4 * P, 2 * L * H)

    al = jnp.stack([apr[..., L].reshape(2, npair, 2 * P), api[..., L].reshape(2, npair, 2 * P)], axis=2)
    al = al.reshape(2, 2 * npair, 1, 2 * P)
    return w_intra.astype(BF16), w_sin.astype(BF16), w_so.astype(BF16), al


def _atom_transpose_matrix():
    n = OCT * OCT * SSM_GROUP
    i = np.arange(n)
    j = ((i // SSM_GROUP) % OCT) * LANES + (i // LANES) * SSM_GROUP + i % SSM_GROUP
    p = np.zeros((n, n), np.float32)
    p[i, j] = 1.0
    return jnp.asarray(p, BF16)


def _ssm_in_kernel(us_ref, p_ref, w_ref, ut_ref, s_ref):
    n = ut_ref.shape[1]
    noct = us_ref.shape[0]
    half = SSM_CHUNK // 2
    src = []
    for o in range(noct):
        for hf in range(2):
            rows = [us_ref[o, pl.ds(half * hf + sl, n, stride=SSM_CHUNK), :] for sl in range(half)]
            src.append(jnp.concatenate(rows, axis=1).astype(BF16))
    perm = jnp.dot(jnp.concatenate(src, axis=0), p_ref[...], preferred_element_type=F32).astype(BF16)

    def group_chunk(gi):
        o, gl = gi // OCT, gi % OCT
        return jnp.concatenate([perm[(2 * o + hf) * n:(2 * o + hf + 1) * n, gl * LANES:(gl + 1) * LANES]
                                for hf in range(2)], axis=1)

    for j in range(noct * OCT // 2):
        ug = [group_chunk(2 * j), group_chunk(2 * j + 1)]
        ut_ref[2 * j] = ug[0]
        ut_ref[2 * j + 1] = ug[1]
        up = jnp.concatenate(ug, axis=1)
        for d in range(2):
            s = jnp.dot(up, w_ref[d, j], preferred_element_type=F32)
            s_ref[d, 2 * j] = s[:, :LANES]
            s_ref[d, 2 * j + 1] = s[:, LANES:]


def _ssm_scan_kernel(sc_ref, sl_ref, al_ref, hc_ref, hl_ref, *, b):
    d = pl.program_id(0)
    nblk = sc_ref.shape[1]
    ars = [al_ref[0, 2 * k] for k in range(nblk // 2)]
    ais = [al_ref[0, 2 * k + 1] for k in range(nblk // 2)]

    def phase(s_ref, h_ref, carry):
        nch = s_ref.shape[2] // b

        def body(i, carry):
            c = jnp.where(d == 0, i, nch - 1 - i)
            rows = pl.ds(c, b, stride=nch)
            new = []
            for k in range(nblk // 2):
                hr, hi = carry[2 * k], carry[2 * k + 1]
                h_ref[0, 2 * k, rows, :] = hr
                h_ref[0, 2 * k + 1, rows, :] = hi
                sr = s_ref[0, 2 * k, rows, :]
                si = s_ref[0, 2 * k + 1, rows, :]
                new.append(ars[k] * hr - ais[k] * hi + sr)
                new.append(ars[k] * hi + ais[k] * hr + si)
            return tuple(new)

        return lax.fori_loop(0, nch, body, carry, unroll=4)

    carry = tuple(jnp.zeros((b, LANES), F32) for _ in range(nblk))
    carry = phase(sc_ref, hc_ref, carry)
    phase(sl_ref, hl_ref, carry)


def _ssm_out_kernel(ut_ref, h_ref, wi_ref, wo_ref, p_ref, y_ref):
    n = ut_ref.shape[1]
    ng = ut_ref.shape[0]
    noct = ng // OCT
    half = SSM_CHUNK // 2
    ys = []
    for j in range(ng // 2):
        hp = [jnp.concatenate([h_ref[d, 2 * j], h_ref[d, 2 * j + 1]], axis=1).astype(BF16) for d in range(2)]
        for e in range(2):
            gi = 2 * j + e
            cols = slice(e * SSM_CW, (e + 1) * SSM_CW)
            ys.append(jnp.dot(ut_ref[gi], wi_ref[gi], preferred_element_type=F32)
                      + jnp.dot(hp[0], wo_ref[0, j, :, cols], preferred_element_type=F32)
                      + jnp.dot(hp[1], wo_ref[1, j, :, cols], preferred_element_type=F32))
    rows = []
    for o in range(noct):
        for hf in range(2):
            rows.append(jnp.concatenate([ys[OCT * o + gl][:, hf * LANES:(hf + 1) * LANES] for gl in range(OCT)],
                                        axis=1))
    ycat = jnp.concatenate(rows, axis=0)
    hi = ycat.astype(BF16)
    lo = (ycat - hi.astype(F32)).astype(BF16)
    res = (jnp.dot(hi, p_ref[...], preferred_element_type=F32)
           + jnp.dot(lo, p_ref[...], preferred_element_type=F32))
    for o in range(noct):
        for hf in range(2):
            blk = 2 * o + hf
            for tl in range(half):
                y_ref[o, pl.ds(half * hf + tl, n, stride=SSM_CHUNK), :] = (
                    res[blk * n:(blk + 1) * n, tl * LANES:(tl + 1) * LANES])


def _ssm_conv(us_lat, us_ctx, weights, perm, *, b):
    w_intra, w_sin, w_so, al = weights
    P = SSM_STATE
    ng = w_intra.shape[0]
    nblk = ng

    def stage_in(us, steps):
        noct, rows, _ = us.shape
        nchunks = rows // SSM_CHUNK
        n = nchunks // steps
        return pl.pallas_call(
            _ssm_in_kernel,
            grid=(steps,),
            in_specs=[pl.BlockSpec((noct, n * SSM_CHUNK, LANES), lambda i: (0, i, 0)),
                      _resident(perm.shape), _resident(w_sin.shape)],
            out_specs=[pl.BlockSpec((ng, n, SSM_CW), lambda i: (0, i, 0)),
                       pl.BlockSpec((2, nblk, n, LANES), lambda i: (0, 0, i, 0))],
            out_shape=[jax.ShapeDtypeStruct((ng, nchunks, SSM_CW), BF16),
                       jax.ShapeDtypeStruct((2, nblk, nchunks, LANES), F32)],
            compiler_params=_cparams(("arbitrary",)),
            name="ssm_in",
        )(us, perm, w_sin)

    lat_steps = 2 * b
    ut_lat, s_lat = stage_in(us_lat, lat_steps)
    ut_ctx, s_ctx = stage_in(us_ctx, 1)

    cb = 16
    nc_rows, nl_rows = s_ctx.shape[2], s_lat.shape[2]
    h_ctx, h_lat = pl.pallas_call(
        functools.partial(_ssm_scan_kernel, b=b),
        grid=(2, nblk // cb),
        in_specs=[pl.BlockSpec((1, cb, nc_rows, LANES), lambda d, j: (d, j, 0, 0)),
                  pl.BlockSpec((1, cb, nl_rows, LANES), lambda d, j: (d, j, 0, 0)),
                  pl.BlockSpec((1, cb, 1, 2 * P), lambda d, j: (d, j, 0, 0))],
        out_specs=[pl.BlockSpec((1, cb, nc_rows, LANES), lambda d, j: (d, j, 0, 0)),
                   pl.BlockSpec((1, cb, nl_rows, LANES), lambda d, j: (d, j, 0, 0))],
        out_shape=[jax.ShapeDtypeStruct(s_ctx.shape, F32), jax.ShapeDtypeStruct(s_lat.shape, F32)],
        compiler_params=_cparams(("arbitrary", "arbitrary")),
        name="ssm_scan",
    )(s_ctx, s_lat, al)

    def stage_out(ut, hin, steps):
        _, nchunks, _ = ut.shape
        n = nchunks // steps
        noct = ng // OCT
        return pl.pallas_call(
            _ssm_out_kernel,
            grid=(steps,),
            in_specs=[pl.BlockSpec((ng, n, SSM_CW), lambda i: (0, i, 0)),
                      pl.BlockSpec((2, nblk, n, LANES), lambda i: (0, 0, i, 0)),
                      _resident(w_intra.shape), _resident(w_so.shape), _resident(perm.shape)],
            out_specs=pl.BlockSpec((noct, n * SSM_CHUNK, LANES), lambda i: (0, i, 0)),
            out_shape=jax.ShapeDtypeStruct((noct, nchunks * SSM_CHUNK, LANES), F32),
            compiler_params=_cparams(("arbitrary",)),
            name="ssm_out",
        )(ut, hin, w_intra, w_so, perm)

    return stage_out(ut_lat, h_lat, lat_steps), stage_out(ut_ctx, h_ctx, 1)


def _fourier_kernel(u_ref, w_ref, cos_ref, sin_ref, b_ref, o_ref, a_s, b_s, *, groups):
    gc = u_ref.shape[1] // groups

    @pl.when(pl.program_id(1) == 0)
    def _():
        for g in range(groups):
            ab = jnp.dot(u_ref[:, g * gc:(g + 1) * gc], w_ref[g], preferred_element_type=F32)
            a_s[:, g * gc:(g + 1) * gc] = ab[:, :gc].astype(BF16)
            b_s[:, g * gc:(g + 1) * gc] = ab[:, gc:].astype(BF16)

    o_ref[...] = (jnp.dot(cos_ref[...], a_s[...], preferred_element_type=F32)
                  - jnp.dot(sin_ref[...], b_s[...], preferred_element_type=F32)
                  + b_ref[...]).astype(BF16)


def _dft_tables(n):
    m = math.isqrt(n)
    assert m * m == n
    k = jnp.arange(m, dtype=jnp.int32)[:, None]
    t = jnp.arange(n, dtype=jnp.int32)[None, :]
    alpha = ((k * t) % m).astype(F32) * (2.0 * math.pi / m)
    beta = ((k * t) % n).astype(F32) * (2.0 * math.pi / n)
    ca, sa = jnp.cos(alpha)[:, None, :], jnp.sin(alpha)[:, None, :]
    cb, sb = jnp.cos(beta)[None, :, :], jnp.sin(beta)[None, :, :]
    cos = (ca * cb - sa * sb).reshape(n, n)
    sin = (sa * cb + ca * sb).reshape(n, n)
    return cos.astype(BF16), sin.astype(BF16)


def _fourier(uf, w_cs, bias, tables, *, b, tm):
    nr, fw = uf.shape
    seq = nr // b
    groups = w_cs.shape[0]
    cos_t, sin_t = tables
    tm = min(tm, seq)
    per_b = seq // tm
    return pl.pallas_call(
        functools.partial(_fourier_kernel, groups=groups),
        grid=(b, per_b),
        in_specs=[
            pl.BlockSpec((seq, fw), lambda bi, i: (bi, 0)),
            pl.BlockSpec(w_cs.shape, lambda bi, i: (0, 0, 0)),
            pl.BlockSpec((tm, seq), lambda bi, i: (i, 0)),
            pl.BlockSpec((tm, seq), lambda bi, i: (i, 0)),
            pl.BlockSpec((1, fw), lambda bi, i: (0, 0)),
        ],
        out_specs=pl.BlockSpec((tm, fw), lambda bi, i: (bi * per_b + i, 0)),
        out_shape=jax.ShapeDtypeStruct((nr, fw), BF16),
        scratch_shapes=[pltpu.VMEM((seq, fw), BF16), pltpu.VMEM((seq, fw), BF16)],
        compiler_params=_cparams(("arbitrary", "arbitrary")),
        name="fourier_%d" % seq,
    )(uf, w_cs, cos_t, sin_t, bias)


def _fourier_weights(w_four, seq):
    groups, gc, _ = w_four.shape
    k = np.arange(gc)
    ang = ((k[:, None] * k[None, :]) % gc).astype(np.float64) * (2.0 * np.pi / gc)
    norm = 1.0 / math.sqrt(seq * gc)
    cc = jnp.asarray(np.cos(ang) * norm, F32)
    sc = jnp.asarray(np.sin(ang) * norm, F32)
    hp = lax.Precision.HIGHEST
    wc = jnp.einsum('ck,gkd->gcd', cc, w_four.astype(F32), precision=hp)
    ws = jnp.einsum('ck,gkd->gcd', sc, w_four.astype(F32), precision=hp)
    return jnp.concatenate([wc, ws], axis=-1).astype(BF16)


def _outproj_kernel(att_ref, us_ref, yc_ref, uf_ref, x_ref, mod_ref, wo_ref, wg_ref, bg_ref, dsk_ref,
                    gpost_ref, gpre_ref, xo_ref, h2_ref, *, aw, sw):
    nblk = us_ref.shape[0]
    us = jnp.concatenate([us_ref[j] for j in range(nblk)], axis=1)
    yc = jnp.concatenate([yc_ref[j] for j in range(nblk)], axis=1)
    g = jax.nn.gelu(dsk_ref[...] * us + yc)
    z = jnp.dot(g.astype(BF16), wg_ref[...], preferred_element_type=F32) + bg_ref[...]
    ssm = (g * jax.nn.sigmoid(z)).astype(BF16)
    mix = (jnp.dot(att_ref[...], wo_ref[0:aw, :], preferred_element_type=F32)
           + jnp.dot(ssm, wo_ref[aw:aw + sw, :], preferred_element_type=F32)
           + jnp.dot(uf_ref[...], wo_ref[aw + sw:, :], preferred_element_type=F32))
    xn = x_ref[...] + mod_ref[0, 2:3, :] * (_rms(mix, NORM_EPS) * gpost_ref[...])
    xo_ref[...] = xn
    h2 = _rms(xn, NORM_EPS) * gpre_ref[...]
    h2_ref[...] = (h2 * (1.0 + mod_ref[0, 4:5, :]) + mod_ref[0, 3:4, :]).astype(BF16)


def _outproj(att, us, yc, four, x, mod, w_out, w_glu, b_glu, dsk, g_post, g_pre, *, mod_row, tm):
    n_rows, d = x.shape
    aw, fw = att.shape[1], four.shape[1]
    nblk = us.shape[0]
    sw = nblk * LANES

    def row(i):
        return (i, 0)

    def const(i):
        return (0, 0)

    blk3 = pl.BlockSpec((nblk, tm, LANES), lambda i: (0, i, 0))
    return pl.pallas_call(
        functools.partial(_outproj_kernel, aw=aw, sw=sw),
        grid=(n_rows // tm,),
        in_specs=[
            pl.BlockSpec((tm, aw), row), blk3, blk3,
            pl.BlockSpec((tm, fw), row), pl.BlockSpec((tm, d), row),
            pl.BlockSpec((1, N_MOD, d), lambda i: (mod_row(i), 0, 0)),
            _resident(w_out.shape),
            pl.BlockSpec(w_glu.shape, const), pl.BlockSpec((1, sw), const), pl.BlockSpec((1, sw), const),
            pl.BlockSpec((1, d), const), pl.BlockSpec((1, d), const),
        ],
        out_specs=[pl.BlockSpec((tm, d), row), pl.BlockSpec((tm, d), row)],
        out_shape=[jax.ShapeDtypeStruct((n_rows, d), F32), jax.ShapeDtypeStruct((n_rows, d), BF16)],
        compiler_params=_cparams(("arbitrary",)),
        name="outproj",
    )(att, us, yc, four, x, mod, w_out, w_glu, b_glu, dsk, g_post, g_pre)


def _ffn_kernel(h_ref, x_ref, mod_ref, g_ref, wg_ref, wu_ref, wd_ref, o_ref):
    k = pl.program_id(1)

    @pl.when(k == 0)
    def _():
        o_ref[...] = jnp.zeros(o_ref.shape, F32)

    h = h_ref[...]
    tf = wg_ref.shape[1]
    part = None
    for c0 in range(0, tf, tf // 2):
        cols = slice(c0, c0 + tf // 2)
        a = jnp.dot(h, wg_ref[:, cols], preferred_element_type=F32)
        u = jnp.dot(h, wu_ref[:, cols], preferred_element_type=F32)
        p = jnp.dot((_silu(a) * u).astype(BF16), wd_ref[cols, :], preferred_element_type=F32)
        part = p if part is None else part + p
    o_ref[...] += part

    @pl.when(k == pl.num_programs(1) - 1)
    def _():
        o_ref[...] = x_ref[...] + mod_ref[0, 5:6, :] * (_rms(o_ref[...], NORM_EPS) * g_ref[...])


def _ffn(h2, x_mid, mod, g_post, w_gate, w_up, w_down, *, mod_row, tm, tf):
    n_rows, d = h2.shape
    dff = w_gate.shape[1]
    return pl.pallas_call(
        _ffn_kernel,
        grid=(n_rows // tm, dff // tf),
        in_specs=[
            pl.BlockSpec((tm, d), lambda i, k: (i, 0)),
            pl.BlockSpec((tm, d), lambda i, k: (i, 0)),
            pl.BlockSpec((1, N_MOD, d), lambda i, k: (mod_row(i), 0, 0)),
            pl.BlockSpec((1, d), lambda i, k: (0, 0)),
            pl.BlockSpec((d, tf), lambda i, k: (0, k)),
            pl.BlockSpec((d, tf), lambda i, k: (0, k)),
            pl.BlockSpec((tf, d), lambda i, k: (k, 0)),
        ],
        out_specs=pl.BlockSpec((tm, d), lambda i, k: (i, 0)),
        out_shape=jax.ShapeDtypeStruct((n_rows, d), F32),
        compiler_params=_cparams(("arbitrary", "arbitrary")),
        name="ffn",
    )(h2, x_mid, mod, g_post, w_gate, w_up, w_down)


def kernel(x, c, ctx, c_ctx, w_mod, b_mod, g_mix_pre, g_mix_post, g_ffn_pre, g_ffn_post, w_in, w_out, lam_q1, lam_k1, lam_q2, lam_k2, g_subln, ssm_a_re, ssm_a_im, ssm_log_dt, ssm_b_re, ssm_b_im, ssm_c_re, ssm_c_im, ssm_d, w_glu, b_glu, w_four, b_four, w_gate, w_up, w_down):
    b, t, d = x.shape
    n_ctx = ctx.shape[1]
    depth = w_mod.shape[0]
    nl, nc = b * t, b * n_ctx
    aw = d // 2
    sw = ssm_d.shape[1]
    fw = d - aw - sw
    heads = aw // (2 * DA_HEAD_DIM)
    tm = 512
    assert t % tm == 0 and nc % tm == 0 and n_ctx % SSM_CHUNK == 0 and t % GRID_W == 0 and b + 1 <= MOD_ROWS
    assert sw % LANES == 0 and (sw // SSM_GROUP) % (2 * OCT) == 0

    cs = jnp.concatenate([c, c_ctx[None, :], jnp.zeros((MOD_ROWS - b - 1, d), F32)], axis=0)
    mod_all = _modulation(cs, w_mod, b_mod).reshape(depth, MOD_ROWS, N_MOD, d)

    per_b = t // tm
    lat_mod = lambda i: i // per_b
    ctx_mod = lambda i: b
    tabs = _rope_tables(t, tm)
    dft_lat = _dft_tables(t)
    dft_ctx = _dft_tables(n_ctx)
    perm = _atom_transpose_matrix()
    x_lat, x_ctx = x.reshape(nl, d), ctx.reshape(nc, d)

    for l in range(depth):
        need_ctx = l < depth - 1
        lam_init = 0.8 - 0.6 * math.exp(-0.3 * l)
        mod = mod_all[l]
        w_inl = w_in[l].astype(BF16)
        w_parts = (w_inl[:, :2 * aw], jnp.transpose(w_inl[:, 2 * aw:3 * aw]), w_inl[:, 3 * aw:])
        g_pre = g_mix_pre[l][None, :]
        q, k, vt, us, uf = _inproj(x_lat, mod, g_pre, *w_parts, tabs, mod_row=lat_mod,
                                   tab_blk=lambda i: i % per_b, aw=aw, sw=sw, fw=fw, tm=tm)
        qc, kc, vtc, usc, ufc = _inproj(x_ctx, mod, g_pre, *w_parts, tabs, mod_row=ctx_mod,
                                        tab_blk=lambda i: per_b, aw=aw, sw=sw, fw=fw, tm=tm)

        lam4 = jnp.stack([lam_q1[l], lam_k1[l], lam_q2[l], lam_k2[l]]).astype(F32)
        gs = g_subln[l][None, :].astype(F32)
        att = _attention(lam4, gs, q, [(kc, vtc), (k, vt)], lam_init=lam_init, b=b, heads=heads, tq=512,
                         name="attn_latent")
        ssm_w = _ssm_weights(ssm_a_re[l], ssm_a_im[l], ssm_log_dt[l], ssm_b_re[l], ssm_b_im[l],
                             ssm_c_re[l], ssm_c_im[l])
        yc, ycc = _ssm_conv(us, usc, ssm_w, perm, b=b)
        bias = b_four[l].reshape(1, fw).astype(F32)
        four = _fourier(uf, _fourier_weights(w_four[l], t), bias, dft_lat, b=b, tm=tm)

        wo, wgl = w_out[l].astype(BF16), w_glu[l].astype(BF16)
        small = (b_glu[l][None, :].astype(F32), ssm_d[l][None, :].astype(F32), g_mix_post[l][None, :],
                 g_ffn_pre[l][None, :])
        ffn_w = (g_ffn_post[l][None, :], w_gate[l].astype(BF16), w_up[l].astype(BF16), w_down[l].astype(BF16))
        x_mid, h2 = _outproj(att, us, yc, four, x_lat, mod, wo, wgl, *small, mod_row=lat_mod, tm=tm)
        x_lat = _ffn(h2, x_mid, mod, *ffn_w, mod_row=lat_mod, tm=tm, tf=512)

        if need_ctx:
            att_c = _attention(lam4, gs, qc, [(kc, vtc)], lam_init=lam_init, b=b, heads=heads, tq=n_ctx,
                               name="attn_ctx")
            four_c = _fourier(ufc, _fourier_weights(w_four[l], n_ctx), bias, dft_ctx, b=b, tm=tm)
            xc_mid, h2c = _outproj(att_c, usc, ycc, four_c, x_ctx, mod, wo, wgl, *small, mod_row=ctx_mod, tm=tm)
            x_ctx = _ffn(h2c, xc_mid, mod, *ffn_w, mod_row=ctx_mod, tm=tm, tf=512)
    return x_lat.reshape(b, t, d)
```

```python
import functools
import math

import jax
import jax.numpy as jnp
import numpy as np
from jax import lax
from jax.experimental import pallas as pl
from jax.experimental.pallas import tpu as pltpu

F32 = jnp.float32
BF16 = jnp.bfloat16

LANES = 128
GRID_W = 64
DA_HEAD_DIM = 128
SSM_GROUP = 16
SSM_STATE = 64
N_MOD = 6
ROPE_BASE = 10000.0
ROPE_PAIRS = DA_HEAD_DIM // 4
NORM_EPS = 1e-6
SUBLN_EPS = 1e-5

SSM_CHUNK = 16
SSM_CW = SSM_CHUNK * SSM_GROUP
OCT = LANES // SSM_GROUP
MOD_ROWS = 8
VMEM_LIMIT = 56 * 1024 * 1024
NT_DIMS = (((1,), (1,)), ((), ()))


def _cparams(sem):
    return pltpu.CompilerParams(dimension_semantics=sem, vmem_limit_bytes=VMEM_LIMIT)


def _rms(x, eps):
    return x * lax.rsqrt(jnp.mean(x * x, axis=-1, keepdims=True) + eps)


def _silu(x):
    return x * jax.nn.sigmoid(x)


def _resident(shape):
    nd = len(shape)
    return pl.BlockSpec(shape, lambda *_: (0,) * nd, pipeline_mode=pl.Buffered(1))


def _resident_layer(shape, layer):
    nd = len(shape)
    return pl.BlockSpec((None,) + tuple(shape[1:]), lambda *_: (layer,) + (0,) * (nd - 1),
                        pipeline_mode=pl.Buffered(1))


def _mod_kernel(c_ref, w_ref, b_ref, o_ref):
    s = _silu(c_ref[...]).astype(BF16)
    o_ref[0] = jnp.dot(s, w_ref[0].astype(BF16), preferred_element_type=F32) + b_ref[0]


def _modulation(cs, w_mod, b_mod, tn=1024):
    depth, d, n = w_mod.shape
    return pl.pallas_call(
        _mod_kernel,
        grid=(depth, n // tn),
        in_specs=[
            pl.BlockSpec((MOD_ROWS, d), lambda l, j: (0, 0)),
            pl.BlockSpec((1, d, tn), lambda l, j: (l, 0, j)),
            pl.BlockSpec((1, 1, tn), lambda l, j: (l, 0, j)),
        ],
        out_specs=pl.BlockSpec((1, MOD_ROWS, tn), lambda l, j: (l, 0, j)),
        out_shape=jax.ShapeDtypeStruct((depth, MOD_ROWS, n), F32),
        compiler_params=_cparams(("arbitrary", "arbitrary")),
        name="modulation",
    )(cs, w_mod, b_mod.reshape(depth, 1, n))


def _inproj_kernel(x_ref, mod_ref, g_ref, w_ref, wvt_ref, cq_ref, sq_ref, ck_ref, sk_ref,
                   q_ref, k_ref, vt_ref, us_ref, uf_ref, *, aw, sw):
    tm = x_ref.shape[0]
    h = _rms(x_ref[...], NORM_EPS) * g_ref[...]
    h = h * (1.0 + mod_ref[0, 1:2, :]) + mod_ref[0, 0:1, :]
    hb = h.astype(BF16)

    lane = lax.broadcasted_iota(jnp.int32, (tm, DA_HEAD_DIM), 1)
    first_half = (lane & (2 * ROPE_PAIRS - 1)) < ROPE_PAIRS

    def rope(z, c, s):
        partner = jnp.where(first_half, pltpu.roll(z, DA_HEAD_DIM - ROPE_PAIRS, 1), pltpu.roll(z, ROPE_PAIRS, 1))
        return z * c + partner * s

    zq = jnp.dot(hb, w_ref[:, 0:aw], preferred_element_type=F32)
    cq, sq = cq_ref[...], sq_ref[...]
    for j in range(aw // DA_HEAD_DIM):
        sl = slice(j * DA_HEAD_DIM, (j + 1) * DA_HEAD_DIM)
        q_ref[:, sl] = rope(zq[:, sl], cq, sq).astype(BF16)
    zk = jnp.dot(hb, w_ref[:, aw:2 * aw], preferred_element_type=F32)
    ck, sk = ck_ref[...], sk_ref[...]
    for j in range(aw // DA_HEAD_DIM):
        sl = slice(j * DA_HEAD_DIM, (j + 1) * DA_HEAD_DIM)
        k_ref[:, sl] = rope(zk[:, sl], ck, sk).astype(BF16)
    vt_ref[...] = lax.dot_general(wvt_ref[...], hb, NT_DIMS, preferred_element_type=F32).astype(BF16)
    zs = jnp.dot(hb, w_ref[:, 3 * aw:3 * aw + sw], preferred_element_type=F32)
    for j in range(sw // LANES):
        us_ref[j] = zs[:, j * LANES:(j + 1) * LANES]
    uf_ref[...] = jnp.dot(hb, w_ref[:, 3 * aw + sw:], preferred_element_type=F32).astype(BF16)


def _inproj(x, mod, g, w_in, w_vt, tabs, *, layer, mod_row, tab_blk, aw, sw, fw, tm):
    nr, d = x.shape

    def row(i):
        return (i, 0)

    tab_spec = pl.BlockSpec((tm, DA_HEAD_DIM), lambda i: (tab_blk(i), 0))
    return pl.pallas_call(
        functools.partial(_inproj_kernel, aw=aw, sw=sw),
        grid=(nr // tm,),
        in_specs=[
            pl.BlockSpec((tm, d), row),
            pl.BlockSpec((1, N_MOD, d), lambda i: (mod_row(i), 0, 0)),
            pl.BlockSpec((1, d), lambda i: (0, 0)),
            _resident_layer(w_in.shape, layer), _resident_layer(w_vt.shape, layer),
            tab_spec, tab_spec, tab_spec, tab_spec,
        ],
        out_specs=[
            pl.BlockSpec((tm, aw), row), pl.BlockSpec((tm, aw), row), pl.BlockSpec((aw, tm), lambda i: (0, i)),
            pl.BlockSpec((sw // LANES, tm, LANES), lambda i: (0, i, 0)), pl.BlockSpec((tm, fw), row),
        ],
        out_shape=[
            jax.ShapeDtypeStruct((nr, aw), BF16), jax.ShapeDtypeStruct((nr, aw), BF16),
            jax.ShapeDtypeStruct((aw, nr), BF16), jax.ShapeDtypeStruct((sw // LANES, nr, LANES), F32),
            jax.ShapeDtypeStruct((nr, fw), BF16),
        ],
        compiler_params=_cparams(("arbitrary",)),
        name="inproj",
    )(x, mod, g, w_in, w_vt, *tabs)


def _rope_tables(t, tm):
    rows = t // GRID_W
    r = jnp.broadcast_to(jnp.arange(rows, dtype=F32)[:, None], (rows, GRID_W)).reshape(-1)
    col = jnp.broadcast_to(jnp.arange(GRID_W, dtype=F32)[None, :], (rows, GRID_W)).reshape(-1)
    inv = ROPE_BASE ** (-jnp.arange(ROPE_PAIRS, dtype=F32) / ROPE_PAIRS)
    ar, ac = r[:, None] * inv, col[:, None] * inv
    cos = jnp.concatenate([jnp.cos(ar), jnp.cos(ar), jnp.cos(ac), jnp.cos(ac)], axis=1)
    sin = jnp.concatenate([-jnp.sin(ar), jnp.sin(ar), -jnp.sin(ac), jnp.sin(ac)], axis=1)
    cos = jnp.concatenate([cos, jnp.ones((tm, DA_HEAD_DIM), F32)], axis=0)
    sin = jnp.concatenate([sin, jnp.zeros((tm, DA_HEAD_DIM), F32)], axis=0)
    scale = DA_HEAD_DIM ** -0.5 * math.log2(math.e)
    return cos * scale, sin * scale, cos, sin


def _attn_kernel(lam_ref, gs_ref, q_ref, *refs, lam_init):
    o_ref = refs[-1]
    k_refs, vt_refs = refs[0:-1:2], refs[1:-1:2]
    hd = DA_HEAD_DIM
    q = q_ref[...]
    outs = []
    for idx in range(2):
        qi = q[:, idx * hd:(idx + 1) * hd]
        ss = [lax.dot_general(kr[:, idx * hd:(idx + 1) * hd], qi, NT_DIMS, preferred_element_type=F32)
              for kr in k_refs]
        m = functools.reduce(jnp.maximum, [jnp.max(s, axis=0, keepdims=True) for s in ss])
        ps = [jnp.exp2(s - m) for s in ss]
        l = functools.reduce(jnp.add, [jnp.sum(p, axis=0, keepdims=True) for p in ps])
        acc = functools.reduce(jnp.add, [jnp.dot(vr[...], p.astype(BF16), preferred_element_type=F32)
                                         for vr, p in zip(vt_refs, ps)])
        outs.append(acc * (1.0 / l))

    s1 = jnp.sum(lam_ref[0:1, :] * lam_ref[1:2, :], axis=-1, keepdims=True)
    s2 = jnp.sum(lam_ref[2:3, :] * lam_ref[3:4, :], axis=-1, keepdims=True)
    lam = jnp.exp(s1) - jnp.exp(s2) + lam_init
    o = jnp.transpose(outs[0] - outs[1] * lam)
    o_ref[...] = (_rms(o, SUBLN_EPS) * gs_ref[...] * (1.0 - lam_init)).astype(BF16)


def _attention(lam4, gs, q, kv_sets, *, lam_init, b, heads, tq, name):
    nq, aw = q.shape
    vw = 2 * DA_HEAD_DIM
    per_b = nq // b // tq
    q_spec = pl.BlockSpec((tq, vw), lambda bi, h, i: (bi * per_b + i, h))
    in_specs = [pl.BlockSpec((4, DA_HEAD_DIM), lambda bi, h, i: (0, 0)),
                pl.BlockSpec((1, vw), lambda bi, h, i: (0, 0)),
                q_spec]
    args = [lam4, gs, q]
    for k, vt in kv_sets:
        sk = k.shape[0] // b
        in_specs += [pl.BlockSpec((sk, vw), lambda bi, h, i: (bi, h)),
                     pl.BlockSpec((vw, sk), lambda bi, h, i: (h, bi))]
        args += [k, vt]
    return pl.pallas_call(
        functools.partial(_attn_kernel, lam_init=lam_init),
        grid=(b, heads, per_b),
        in_specs=in_specs,
        out_specs=q_spec,
        out_shape=jax.ShapeDtypeStruct((nq, aw), BF16),
        compiler_params=_cparams(("arbitrary", "arbitrary", "arbitrary")),
        name=name,
    )(*args)


def _ssm_weights(a_re, a_im, log_dt, b_re, b_im, c_re, c_im):
    L, H, P = SSM_CHUNK, SSM_GROUP, SSM_STATE
    g = a_re.shape[1]
    npair = g // 2
    hp = lax.Precision.HIGHEST
    ar, ai = a_re.astype(F32), a_im.astype(F32)
    dt = jnp.exp(log_dt.astype(F32))[..., None]
    n = jnp.arange(L + 1, dtype=F32)
    mag = jnp.exp((ar * dt)[..., None] * n)
    ang = (ai * dt)[..., None] * n
    apr, api = mag * jnp.cos(ang), mag * jnp.sin(ang)
    xr, xi = apr[..., 1] - 1.0, api[..., 1]
    den = ar * ar + ai * ai
    qr, qi = (xr * ar + xi * ai) / den, (xi * ar - xr * ai) / den
    br, bi = b_re.astype(F32), b_im.astype(F32)
    bbr = qr[..., None] * br - qi[..., None] * bi
    bbi = qr[..., None] * bi + qi[..., None] * br
    cr, ci = c_re.astype(F32), c_im.astype(F32)

    car = cr[..., None] * apr[:, :, None, :, :L] - ci[..., None] * api[:, :, None, :, :L]
    cai = cr[..., None] * api[:, :, None, :, :L] + ci[..., None] * apr[:, :, None, :, :L]
    kt = (jnp.einsum('dghpt,dgpk->dgkth', car, bbr, precision=hp)
          - jnp.einsum('dghpt,dgpk->dgkth', cai, bbi, precision=hp))
    kc = jnp.concatenate([jnp.flip(kt[1][:, :, 1:], axis=2), kt[0][:, :, :1] + kt[1][:, :, :1], kt[0][:, :, 1:]],
                         axis=2).reshape(g, H, (2 * L - 1) * H)
    w_intra = jnp.stack([kc[:, :, (L - 1 - s) * H:(2 * L - 1 - s) * H] for s in range(L)], axis=1)
    w_intra = w_intra.reshape(g, L * H, L * H)

    def pair_cols(x_re, x_im):
        z = jnp.zeros_like(x_re[:, 0::2])
        e0 = jnp.concatenate([x_re[:, 0::2], z, x_im[:, 0::2], z], axis=-1)
        e1 = jnp.concatenate([z, x_re[:, 1::2], z, x_im[:, 1::2]], axis=-1)
        return jnp.stack([e0, e1], axis=2)

    sir = jnp.stack([jnp.flip(apr[0, ..., :L], axis=-1), apr[1, ..., :L]])
    sii = jnp.stack([jnp.flip(api[0, ..., :L], axis=-1), api[1, ..., :L]])
    sir, sii = jnp.swapaxes(sir, 2, 3)[:, :, :, None, :], jnp.swapaxes(sii, 2, 3)[:, :, :, None, :]
    tbr, tbi = jnp.swapaxes(bbr, 2, 3)[:, :, None], jnp.swapaxes(bbi, 2, 3)[:, :, None]
    w_sin = pair_cols(sir * tbr - sii * tbi, sir * tbi + sii * tbr)
    w_sin = w_sin.reshape(2, npair, 2 * L * H, 4 * P)

    sor = jnp.stack([apr[0, ..., 1:], jnp.flip(apr[1, ..., 1:], axis=-1)])
    soi = jnp.stack([api[0, ..., 1:], jnp.flip(api[1, ..., 1:], axis=-1)])
    tcr, tci = jnp.swapaxes(cr, 2, 3)[:, :, :, None, :], jnp.swapaxes(ci, 2, 3)[:, :, :, None, :]
    cnr = (tcr * sor[..., None] - tci * soi[..., None]).reshape(2, g, P, L * H)
    cni = (tcr * soi[..., None] + tci * sor[..., None]).reshape(2, g, P, L * H)
    zo = jnp.zeros_like(cnr[:, 0::2])
    w_so = jnp.stack([jnp.concatenate([cnr[:, 0::2], zo], axis=-1), jnp.concatenate([zo, cnr[:, 1::2]], axis=-1),
                      jnp.concatenate([-cni[:, 0::2], zo], axis=-1), jnp.concatenate([zo, -cni[:, 1::2]], axis=-1)],
                     axis=2).reshape(2, npair, 4 * P, 2 * L * H)

    al = jnp.stack([apr[..., L].reshape(2, npair, 2 * P), api[..., L].reshape(2, npair, 2 * P)], axis=2)
    al = al.reshape(2, 2 * npair, 1, 2 * P)
    return w_intra.astype(BF16), w_sin.astype(BF16), w_so.astype(BF16), al


def _atom_transpose_matrix():
    n = OCT * OCT * SSM_GROUP
    i = np.arange(n)
    j = ((i // SSM_GROUP) % OCT) * LANES + (i // LANES) * SSM_GROUP + i % SSM_GROUP
    p = np.zeros((n, n), np.float32)
    p[i, j] = 1.0
    return jnp.asarray(p, BF16)


def _ssm_in_kernel(us_ref, p_ref, w_ref, ut_ref, s_ref):
    n = ut_ref.shape[1]
    noct = us_ref.shape[0]
    half = SSM_CHUNK // 2
    src = []
    for o in range(noct):
        for hf in range(2):
            rows = [us_ref[o, pl.ds(half * hf + sl, n, stride=SSM_CHUNK), :] for sl in range(half)]
            src.append(jnp.concatenate(rows, axis=1).astype(BF16))
    perm = jnp.dot(jnp.concatenate(src, axis=0), p_ref[...], preferred_element_type=F32).astype(BF16)

    def group_chunk(gi):
        o, gl = gi // OCT, gi % OCT
        return jnp.concatenate([perm[(2 * o + hf) * n:(2 * o + hf + 1) * n, gl * LANES:(gl + 1) * LANES]
                                for hf in range(2)], axis=1)

    for j in range(noct * OCT // 2):
        ug = [group_chunk(2 * j), group_chunk(2 * j + 1)]
        ut_ref[2 * j] = ug[0]
        ut_ref[2 * j + 1] = ug[1]
        up = jnp.concatenate(ug, axis=1)
        for d in range(2):
            s = jnp.dot(up, w_ref[d, j], preferred_element_type=F32)
            s_ref[d, 2 * j] = s[:, :LANES]
            s_ref[d, 2 * j + 1] = s[:, LANES:]


def _ssm_scan_kernel(sc_ref, sl_ref, al_ref, hc_ref, hl_ref, *, b):
    d = pl.program_id(0)
    nblk = sc_ref.shape[1]
    ars = [al_ref[0, 2 * k] for k in range(nblk // 2)]
    ais = [al_ref[0, 2 * k + 1] for k in range(nblk // 2)]

    def phase(s_ref, h_ref, carry):
        nch = s_ref.shape[2] // b

        def body(i, carry):
            c = jnp.where(d == 0, i, nch - 1 - i)
            rows = pl.ds(c, b, stride=nch)
            new = []
            for k in range(nblk // 2):
                hr, hi = carry[2 * k], carry[2 * k + 1]
                h_ref[0, 2 * k, rows, :] = hr
                h_ref[0, 2 * k + 1, rows, :] = hi
                sr = s_ref[0, 2 * k, rows, :]
                si = s_ref[0, 2 * k + 1, rows, :]
                new.append(ars[k] * hr - ais[k] * hi + sr)
                new.append(ars[k] * hi + ais[k] * hr + si)
            return tuple(new)

        return lax.fori_loop(0, nch, body, carry, unroll=4)

    carry = tuple(jnp.zeros((b, LANES), F32) for _ in range(nblk))
    carry = phase(sc_ref, hc_ref, carry)
    phase(sl_ref, hl_ref, carry)


def _ssm_out_kernel(ut_ref, h_ref, wi_ref, wo_ref, p_ref, y_ref):
    n = ut_ref.shape[1]
    ng = ut_ref.shape[0]
    noct = ng // OCT
    half = SSM_CHUNK // 2
    ys = []
    for j in range(ng // 2):
        hp = [jnp.concatenate([h_ref[d, 2 * j], h_ref[d, 2 * j + 1]], axis=1).astype(BF16) for d in range(2)]
        for e in range(2):
            gi = 2 * j + e
            cols = slice(e * SSM_CW, (e + 1) * SSM_CW)
            ys.append(jnp.dot(ut_ref[gi], wi_ref[gi], preferred_element_type=F32)
                      + jnp.dot(hp[0], wo_ref[0, j, :, cols], preferred_element_type=F32)
                      + jnp.dot(hp[1], wo_ref[1, j, :, cols], preferred_element_type=F32))
    rows = []
    for o in range(noct):
        for hf in range(2):
            rows.append(jnp.concatenate([ys[OCT * o + gl][:, hf * LANES:(hf + 1) * LANES] for gl in range(OCT)],
                                        axis=1))
    ycat = jnp.concatenate(rows, axis=0)
    hi = ycat.astype(BF16)
    lo = (ycat - hi.astype(F32)).astype(BF16)
    res = (jnp.dot(hi, p_ref[...], preferred_element_type=F32)
           + jnp.dot(lo, p_ref[...], preferred_element_type=F32))
    for o in range(noct):
        for hf in range(2):
            blk = 2 * o + hf
            for tl in range(half):
                y_ref[o, pl.ds(half * hf + tl, n, stride=SSM_CHUNK), :] = (
                    res[blk * n:(blk + 1) * n, tl * LANES:(tl + 1) * LANES])


def _ssm_conv(us_lat, us_ctx, weights, perm, *, b):
    w_intra, w_sin, w_so, al = weights
    P = SSM_STATE
    ng = w_intra.shape[0]
    nblk = ng

    def stage_in(us, steps):
        noct, rows, _ = us.shape
        nchunks = rows // SSM_CHUNK
        n = nchunks // steps
        return pl.pallas_call(
            _ssm_in_kernel,
            grid=(steps,),
            in_specs=[pl.BlockSpec((noct, n * SSM_CHUNK, LANES), lambda i: (0, i, 0)),
                      _resident(perm.shape), _resident(w_sin.shape)],
            out_specs=[pl.BlockSpec((ng, n, SSM_CW), lambda i: (0, i, 0)),
                       pl.BlockSpec((2, nblk, n, LANES), lambda i: (0, 0, i, 0))],
            out_shape=[jax.ShapeDtypeStruct((ng, nchunks, SSM_CW), BF16),
                       jax.ShapeDtypeStruct((2, nblk, nchunks, LANES), F32)],
            compiler_params=_cparams(("arbitrary",)),
            name="ssm_in",
        )(us, perm, w_sin)

    lat_steps = 2 * b
    ut_lat, s_lat = stage_in(us_lat, lat_steps)
    ut_ctx, s_ctx = stage_in(us_ctx, 1)

    cb = 16
    nc_rows, nl_rows = s_ctx.shape[2], s_lat.shape[2]
    h_ctx, h_lat = pl.pallas_call(
        functools.partial(_ssm_scan_kernel, b=b),
        grid=(2, nblk // cb),
        in_specs=[pl.BlockSpec((1, cb, nc_rows, LANES), lambda d, j: (d, j, 0, 0)),
                  pl.BlockSpec((1, cb, nl_rows, LANES), lambda d, j: (d, j, 0, 0)),
                  pl.BlockSpec((1, cb, 1, 2 * P), lambda d, j: (d, j, 0, 0))],
        out_specs=[pl.BlockSpec((1, cb, nc_rows, LANES), lambda d, j: (d, j, 0, 0)),
                   pl.BlockSpec((1, cb, nl_rows, LANES), lambda d, j: (d, j, 0, 0))],
        out_shape=[jax.ShapeDtypeStruct(s_ctx.shape, F32), jax.ShapeDtypeStruct(s_lat.shape, F32)],
        compiler_params=_cparams(("arbitrary", "arbitrary")),
        name="ssm_scan",
    )(s_ctx, s_lat, al)

    def stage_out(ut, hin, steps):
        _, nchunks, _ = ut.shape
        n = nchunks // steps
        noct = ng // OCT
        return pl.pallas_call(
            _ssm_out_kernel,
            grid=(steps,),
            in_specs=[pl.BlockSpec((ng, n, SSM_CW), lambda i: (0, i, 0)),
                      pl.BlockSpec((2, nblk, n, LANES), lambda i: (0, 0, i, 0)),
                      _resident(w_intra.shape), _resident(w_so.shape), _resident(perm.shape)],
            out_specs=pl.BlockSpec((noct, n * SSM_CHUNK, LANES), lambda i: (0, i, 0)),
            out_shape=jax.ShapeDtypeStruct((noct, nchunks * SSM_CHUNK, LANES), F32),
            compiler_params=_cparams(("arbitrary",)),
            name="ssm_out",
        )(ut, hin, w_intra, w_so, perm)

    return stage_out(ut_lat, h_lat, lat_steps), stage_out(ut_ctx, h_ctx, 1)


def _fourier_kernel(u_ref, w_ref, cos_ref, sin_ref, b_ref, o_ref, a_s, b_s, *, groups):
    gc = u_ref.shape[1] // groups

    @pl.when(pl.program_id(1) == 0)
    def _():
        for g in range(groups):
            ab = jnp.dot(u_ref[:, g * gc:(g + 1) * gc], w_ref[g], preferred_element_type=F32)
            a_s[:, g * gc:(g + 1) * gc] = ab[:, :gc].astype(BF16)
            b_s[:, g * gc:(g + 1) * gc] = ab[:, gc:].astype(BF16)

    o_ref[...] = (jnp.dot(cos_ref[...], a_s[...], preferred_element_type=F32)
                  - jnp.dot(sin_ref[...], b_s[...], preferred_element_type=F32)
                  + b_ref[...]).astype(BF16)


def _dft_tables(n):
    m = math.isqrt(n)
    assert m * m == n
    k = jnp.arange(m, dtype=jnp.int32)[:, None]
    t = jnp.arange(n, dtype=jnp.int32)[None, :]
    alpha = ((k * t) % m).astype(F32) * (2.0 * math.pi / m)
    beta = ((k * t) % n).astype(F32) * (2.0 * math.pi / n)
    ca, sa = jnp.cos(alpha)[:, None, :], jnp.sin(alpha)[:, None, :]
    cb, sb = jnp.cos(beta)[None, :, :], jnp.sin(beta)[None, :, :]
    cos = (ca * cb - sa * sb).reshape(n, n)
    sin = (sa * cb + ca * sb).reshape(n, n)
    return cos.astype(BF16), sin.astype(BF16)


def _fourier(uf, w_cs, bias, tables, *, b, tm):
    nr, fw = uf.shape
    seq = nr // b
    groups = w_cs.shape[0]
    cos_t, sin_t = tables
    tm = min(tm, seq)
    per_b = seq // tm
    return pl.pallas_call(
        functools.partial(_fourier_kernel, groups=groups),
        grid=(b, per_b),
        in_specs=[
            pl.BlockSpec((seq, fw), lambda bi, i: (bi, 0)),
            pl.BlockSpec(w_cs.shape, lambda bi, i: (0, 0, 0)),
            pl.BlockSpec((tm, seq), lambda bi, i: (i, 0)),
            pl.BlockSpec((tm, seq), lambda bi, i: (i, 0)),
            pl.BlockSpec((1, fw), lambda bi, i: (0, 0)),
        ],
        out_specs=pl.BlockSpec((tm, fw), lambda bi, i: (bi * per_b + i, 0)),
        out_shape=jax.ShapeDtypeStruct((nr, fw), BF16),
        scratch_shapes=[pltpu.VMEM((seq, fw), BF16), pltpu.VMEM((seq, fw), BF16)],
        compiler_params=_cparams(("arbitrary", "arbitrary")),
        name="fourier_%d" % seq,
    )(uf, w_cs, cos_t, sin_t, bias)


def _fourier_weights(w_four, seq):
    groups, gc, _ = w_four.shape
    k = np.arange(gc)
    ang = ((k[:, None] * k[None, :]) % gc).astype(np.float64) * (2.0 * np.pi / gc)
    norm = 1.0 / math.sqrt(seq * gc)
    cc = jnp.asarray(np.cos(ang) * norm, F32)
    sc = jnp.asarray(np.sin(ang) * norm, F32)
    hp = lax.Precision.HIGHEST
    wc = jnp.einsum('ck,gkd->gcd', cc, w_four.astype(F32), precision=hp)
    ws = jnp.einsum('ck,gkd->gcd', sc, w_four.astype(F32), precision=hp)
    return jnp.concatenate([wc, ws], axis=-1).astype(BF16)


def _outproj_kernel(att_ref, us_ref, yc_ref, uf_ref, x_ref, mod_ref, wo_ref, wg_ref, bg_ref, dsk_ref,
                    gpost_ref, gpre_ref, xo_ref, h2_ref, *, aw, sw):
    nblk = us_ref.shape[0]
    tm = x_ref.shape[0]
    for r0 in range(0, tm, tm // 2):
        rows = slice(r0, r0 + tm // 2)
        us = jnp.concatenate([us_ref[j, rows, :] for j in range(nblk)], axis=1)
        yc = jnp.concatenate([yc_ref[j, rows, :] for j in range(nblk)], axis=1)
        g = jax.nn.gelu(dsk_ref[...] * us + yc)
        z = jnp.dot(g.astype(BF16), wg_ref[...], preferred_element_type=F32) + bg_ref[...]
        ssm = (g * jax.nn.sigmoid(z)).astype(BF16)
        mix = (jnp.dot(att_ref[rows, :], wo_ref[0:aw, :], preferred_element_type=F32)
               + jnp.dot(ssm, wo_ref[aw:aw + sw, :], preferred_element_type=F32)
               + jnp.dot(uf_ref[rows, :], wo_ref[aw + sw:, :], preferred_element_type=F32))
        xn = x_ref[rows, :] + mod_ref[0, 2:3, :] * (_rms(mix, NORM_EPS) * gpost_ref[...])
        xo_ref[rows, :] = xn
        h2 = _rms(xn, NORM_EPS) * gpre_ref[...]
        h2_ref[rows, :] = (h2 * (1.0 + mod_ref[0, 4:5, :]) + mod_ref[0, 3:4, :]).astype(BF16)


def _outproj(att, us, yc, four, x, mod, w_out, w_glu, b_glu, dsk, g_post, g_pre, *, layer, mod_row, tm):
    n_rows, d = x.shape
    aw, fw = att.shape[1], four.shape[1]
    nblk = us.shape[0]
    sw = nblk * LANES

    def row(i):
        return (i, 0)

    def const(i):
        return (0, 0)

    blk3 = pl.BlockSpec((nblk, tm, LANES), lambda i: (0, i, 0))
    return pl.pallas_call(
        functools.partial(_outproj_kernel, aw=aw, sw=sw),
        grid=(n_rows // tm,),
        in_specs=[
            pl.BlockSpec((tm, aw), row), blk3, blk3,
            pl.BlockSpec((tm, fw), row), pl.BlockSpec((tm, d), row),
            pl.BlockSpec((1, N_MOD, d), lambda i: (mod_row(i), 0, 0)),
            _resident_layer(w_out.shape, layer), _resident_layer(w_glu.shape, layer),
            pl.BlockSpec((1, sw), const), pl.BlockSpec((1, sw), const),
            pl.BlockSpec((1, d), const), pl.BlockSpec((1, d), const),
        ],
        out_specs=[pl.BlockSpec((tm, d), row), pl.BlockSpec((tm, d), row)],
        out_shape=[jax.ShapeDtypeStruct((n_rows, d), F32), jax.ShapeDtypeStruct((n_rows, d), BF16)],
        compiler_params=_cparams(("arbitrary",)),
        name="outproj",
    )(att, us, yc, four, x, mod, w_out, w_glu, b_glu, dsk, g_post, g_pre)


def _ffn_kernel(h_ref, x_ref, mod_ref, g_ref, wg_ref, wu_ref, wd_ref, o_ref):
    k = pl.program_id(1)

    @pl.when(k == 0)
    def _():
        o_ref[...] = jnp.zeros(o_ref.shape, F32)

    h = h_ref[...]
    tf = wg_ref.shape[1]
    part = None
    for c0 in range(0, tf, tf // 2):
        cols = slice(c0, c0 + tf // 2)
        a = jnp.dot(h, wg_ref[:, cols], preferred_element_type=F32)
        u = jnp.dot(h, wu_ref[:, cols], preferred_element_type=F32)
        p = jnp.dot((_silu(a) * u).astype(BF16), wd_ref[cols, :], preferred_element_type=F32)
        part = p if part is None else part + p
    o_ref[...] += part

    @pl.when(k == pl.num_programs(1) - 1)
    def _():
        o_ref[...] = x_ref[...] + mod_ref[0, 5:6, :] * (_rms(o_ref[...], NORM_EPS) * g_ref[...])


def _ffn(h2, x_mid, mod, g_post, w_gate, w_up, w_down, *, layer, mod_row, tm, tf):
    n_rows, d = h2.shape
    dff = w_gate.shape[2]
    return pl.pallas_call(
        _ffn_kernel,
        grid=(n_rows // tm, dff // tf),
        in_specs=[
            pl.BlockSpec((tm, d), lambda i, k: (i, 0)),
            pl.BlockSpec((tm, d), lambda i, k: (i, 0)),
            pl.BlockSpec((1, N_MOD, d), lambda i, k: (mod_row(i), 0, 0)),
            pl.BlockSpec((1, d), lambda i, k: (0, 0)),
            pl.BlockSpec((None, d, tf), lambda i, k: (layer, 0, k)),
            pl.BlockSpec((None, d, tf), lambda i, k: (layer, 0, k)),
            pl.BlockSpec((None, tf, d), lambda i, k: (layer, k, 0)),
        ],
        out_specs=pl.BlockSpec((tm, d), lambda i, k: (i, 0)),
        out_shape=jax.ShapeDtypeStruct((n_rows, d), F32),
        compiler_params=_cparams(("arbitrary", "arbitrary")),
        name="ffn",
    )(h2, x_mid, mod, g_post, w_gate, w_up, w_down)


def kernel(x, c, ctx, c_ctx, w_mod, b_mod, g_mix_pre, g_mix_post, g_ffn_pre, g_ffn_post, w_in, w_out, lam_q1, lam_k1, lam_q2, lam_k2, g_subln, ssm_a_re, ssm_a_im, ssm_log_dt, ssm_b_re, ssm_b_im, ssm_c_re, ssm_c_im, ssm_d, w_glu, b_glu, w_four, b_four, w_gate, w_up, w_down):
    b, t, d = x.shape
    n_ctx = ctx.shape[1]
    depth = w_mod.shape[0]
    nl, nc = b * t, b * n_ctx
    aw = d // 2
    sw = ssm_d.shape[1]
    fw = d - aw - sw
    heads = aw // (2 * DA_HEAD_DIM)
    tm = 512
    assert t % tm == 0 and nc % tm == 0 and n_ctx % SSM_CHUNK == 0 and t % GRID_W == 0 and b + 1 <= MOD_ROWS
    assert sw % LANES == 0 and (sw // SSM_GROUP) % (2 * OCT) == 0

    cs = jnp.concatenate([c, c_ctx[None, :], jnp.zeros((MOD_ROWS - b - 1, d), F32)], axis=0)
    mod_all = _modulation(cs, w_mod, b_mod).reshape(depth, MOD_ROWS, N_MOD, d)

    per_b = t // tm
    lat_mod = lambda i: i // per_b
    ctx_mod = lambda i: b
    tabs = _rope_tables(t, tm)
    dft_lat = _dft_tables(t)
    dft_ctx = _dft_tables(n_ctx)
    perm = _atom_transpose_matrix()
    x_lat, x_ctx = x.reshape(nl, d), ctx.reshape(nc, d)
    w_in_b, w_out_b, w_glu_b = w_in.astype(BF16), w_out.astype(BF16), w_glu.astype(BF16)
    w_vt_b = jnp.transpose(w_in_b[:, :, 2 * aw:3 * aw], (0, 2, 1))
    w_gate_b, w_up_b, w_down_b = w_gate.astype(BF16), w_up.astype(BF16), w_down.astype(BF16)

    for l in range(depth):
        need_ctx = l < depth - 1
        lam_init = 0.8 - 0.6 * math.exp(-0.3 * l)
        mod = mod_all[l]
        g_pre = g_mix_pre[l][None, :]
        q, k, vt, us, uf = _inproj(x_lat, mod, g_pre, w_in_b, w_vt_b, tabs, layer=l, mod_row=lat_mod,
                                   tab_blk=lambda i: i % per_b, aw=aw, sw=sw, fw=fw, tm=tm)
        qc, kc, vtc, usc, ufc = _inproj(x_ctx, mod, g_pre, w_in_b, w_vt_b, tabs, layer=l, mod_row=ctx_mod,
                                        tab_blk=lambda i: per_b, aw=aw, sw=sw, fw=fw, tm=tm)

        lam4 = jnp.stack([lam_q1[l], lam_k1[l], lam_q2[l], lam_k2[l]]).astype(F32)
        gs = g_subln[l][None, :].astype(F32)
        att = _attention(lam4, gs, q, [(kc, vtc), (k, vt)], lam_init=lam_init, b=b, heads=heads, tq=512,
                         name="attn_latent")
        ssm_w = _ssm_weights(ssm_a_re[l], ssm_a_im[l], ssm_log_dt[l], ssm_b_re[l], ssm_b_im[l],
                             ssm_c_re[l], ssm_c_im[l])
        yc, ycc = _ssm_conv(us, usc, ssm_w, perm, b=b)
        bias = b_four[l].reshape(1, fw).astype(F32)
        four = _fourier(uf, _fourier_weights(w_four[l], t), bias, dft_lat, b=b, tm=tm)

        small = (b_glu[l][None, :].astype(F32), ssm_d[l][None, :].astype(F32), g_mix_post[l][None, :],
                 g_ffn_pre[l][None, :])
        ffn_w = (g_ffn_post[l][None, :], w_gate_b, w_up_b, w_down_b)
        x_mid, h2 = _outproj(att, us, yc, four, x_lat, mod, w_out_b, w_glu_b, *small, layer=l, mod_row=lat_mod,
                             tm=tm)
        x_lat = _ffn(h2, x_mid, mod, *ffn_w, layer=l, mod_row=lat_mod, tm=tm, tf=512)

        if need_ctx:
            att_c = _attention(lam4, gs, qc, [(kc, vtc)], lam_init=lam_init, b=b, heads=heads, tq=n_ctx,
                               name="attn_ctx")
            four_c = _fourier(ufc, _fourier_weights(w_four[l], n_ctx), bias, dft_ctx, b=b, tm=tm)
            xc_mid, h2c = _outproj(att_c, usc, ycc, four_c, x_ctx, mod, w_out_b, w_glu_b, *small, layer=l,
                                   mod_row=ctx_mod, tm=tm)
            x_ctx = _ffn(h2c, xc_mid, mod, *ffn_w, layer=l, mod_row=ctx_mod, tm=tm, tf=512)
    return x_lat.reshape(b, t, d)
```

```python
import functools
import math

import jax
import jax.numpy as jnp
import numpy as np
from jax import lax
from jax.experimental import pallas as pl
from jax.experimental.pallas import tpu as pltpu

F32 = jnp.float32
BF16 = jnp.bfloat16

LANES = 128
GRID_W = 64
DA_HEAD_DIM = 128
SSM_GROUP = 16
SSM_STATE = 64
N_MOD = 6
ROPE_BASE = 10000.0
ROPE_PAIRS = DA_HEAD_DIM // 4
NORM_EPS = 1e-6
SUBLN_EPS = 1e-5

SSM_CHUNK = 16
SSM_CW = SSM_CHUNK * SSM_GROUP
OCT = LANES // SSM_GROUP
MOD_ROWS = 8
VMEM_LIMIT = 56 * 1024 * 1024
NT_DIMS = (((1,), (1,)), ((), ()))


def _cparams(sem):
    return pltpu.CompilerParams(dimension_semantics=sem, vmem_limit_bytes=VMEM_LIMIT)


def _rms(x, eps):
    return x * lax.rsqrt(jnp.mean(x * x, axis=-1, keepdims=True) + eps)


def _silu(x):
    return x * jax.nn.sigmoid(x)


def _resident(shape):
    nd = len(shape)
    return pl.BlockSpec(shape, lambda *_: (0,) * nd, pipeline_mode=pl.Buffered(1))


def _resident_layer(shape, layer):
    nd = len(shape)
    return pl.BlockSpec((None,) + tuple(shape[1:]), lambda *_: (layer,) + (0,) * (nd - 1),
                        pipeline_mode=pl.Buffered(1))


def _mod_kernel(c_ref, w_ref, b_ref, o_ref):
    s = _silu(c_ref[...]).astype(BF16)
    o_ref[0] = jnp.dot(s, w_ref[0].astype(BF16), preferred_element_type=F32) + b_ref[0]


def _modulation(cs, w_mod, b_mod, tn=1024):
    depth, d, n = w_mod.shape
    return pl.pallas_call(
        _mod_kernel,
        grid=(depth, n // tn),
        in_specs=[
            pl.BlockSpec((MOD_ROWS, d), lambda l, j: (0, 0)),
            pl.BlockSpec((1, d, tn), lambda l, j: (l, 0, j)),
            pl.BlockSpec((1, 1, tn), lambda l, j: (l, 0, j)),
        ],
        out_specs=pl.BlockSpec((1, MOD_ROWS, tn), lambda l, j: (l, 0, j)),
        out_shape=jax.ShapeDtypeStruct((depth, MOD_ROWS, n), F32),
        compiler_params=_cparams(("arbitrary", "arbitrary")),
        name="modulation",
    )(cs, w_mod, b_mod.reshape(depth, 1, n))


def _inproj_kernel(x_ref, mod_ref, g_ref, w_ref, wvt_ref, cq_ref, sq_ref, ck_ref, sk_ref,
                   q_ref, k_ref, vt_ref, us_ref, uf_ref, *, aw, sw):
    tm = x_ref.shape[0]
    h = _rms(x_ref[...], NORM_EPS) * g_ref[...]
    h = h * (1.0 + mod_ref[0, 1:2, :]) + mod_ref[0, 0:1, :]
    hb = h.astype(BF16)

    lane = lax.broadcasted_iota(jnp.int32, (tm, DA_HEAD_DIM), 1)
    first_half = (lane & (2 * ROPE_PAIRS - 1)) < ROPE_PAIRS

    def rope(z, c, s):
        partner = jnp.where(first_half, pltpu.roll(z, DA_HEAD_DIM - ROPE_PAIRS, 1), pltpu.roll(z, ROPE_PAIRS, 1))
        return z * c + partner * s

    zq = jnp.dot(hb, w_ref[:, 0:aw], preferred_element_type=F32)
    cq, sq = cq_ref[...], sq_ref[...]
    for j in range(aw // DA_HEAD_DIM):
        sl = slice(j * DA_HEAD_DIM, (j + 1) * DA_HEAD_DIM)
        q_ref[:, sl] = rope(zq[:, sl], cq, sq).astype(BF16)
    zk = jnp.dot(hb, w_ref[:, aw:2 * aw], preferred_element_type=F32)
    ck, sk = ck_ref[...], sk_ref[...]
    for j in range(aw // DA_HEAD_DIM):
        sl = slice(j * DA_HEAD_DIM, (j + 1) * DA_HEAD_DIM)
        k_ref[:, sl] = rope(zk[:, sl], ck, sk).astype(BF16)
    vt_ref[...] = lax.dot_general(wvt_ref[...], hb, NT_DIMS, preferred_element_type=F32).astype(BF16)
    zs = jnp.dot(hb, w_ref[:, 3 * aw:3 * aw + sw], preferred_element_type=F32)
    for j in range(sw // LANES):
        us_ref[j] = zs[:, j * LANES:(j + 1) * LANES]
    uf_ref[...] = jnp.dot(hb, w_ref[:, 3 * aw + sw:], preferred_element_type=F32).astype(BF16)


def _inproj(x, mod, g, w_in, w_vt, tabs, *, layer, mod_row, tab_blk, aw, sw, fw, tm):
    nr, d = x.shape

    def row(i):
        return (i, 0)

    tab_spec = pl.BlockSpec((tm, DA_HEAD_DIM), lambda i: (tab_blk(i), 0))
    return pl.pallas_call(
        functools.partial(_inproj_kernel, aw=aw, sw=sw),
        grid=(nr // tm,),
        in_specs=[
            pl.BlockSpec((tm, d), row),
            pl.BlockSpec((1, N_MOD, d), lambda i: (mod_row(i), 0, 0)),
            pl.BlockSpec((1, d), lambda i: (0, 0)),
            _resident_layer(w_in.shape, layer), _resident_layer(w_vt.shape, layer),
            tab_spec, tab_spec, tab_spec, tab_spec,
        ],
        out_specs=[
            pl.BlockSpec((tm, aw), row), pl.BlockSpec((tm, aw), row), pl.BlockSpec((aw, tm), lambda i: (0, i)),
            pl.BlockSpec((sw // LANES, tm, LANES), lambda i: (0, i, 0)), pl.BlockSpec((tm, fw), row),
        ],
        out_shape=[
            jax.ShapeDtypeStruct((nr, aw), BF16), jax.ShapeDtypeStruct((nr, aw), BF16),
            jax.ShapeDtypeStruct((aw, nr), BF16), jax.ShapeDtypeStruct((sw // LANES, nr, LANES), F32),
            jax.ShapeDtypeStruct((nr, fw), BF16),
        ],
        compiler_params=_cparams(("arbitrary",)),
        name="inproj",
    )(x, mod, g, w_in, w_vt, *tabs)


def _rope_tables(t, tm):
    rows = t // GRID_W
    r = jnp.broadcast_to(jnp.arange(rows, dtype=F32)[:, None], (rows, GRID_W)).reshape(-1)
    col = jnp.broadcast_to(jnp.arange(GRID_W, dtype=F32)[None, :], (rows, GRID_W)).reshape(-1)
    inv = ROPE_BASE ** (-jnp.arange(ROPE_PAIRS, dtype=F32) / ROPE_PAIRS)
    ar, ac = r[:, None] * inv, col[:, None] * inv
    cos = jnp.concatenate([jnp.cos(ar), jnp.cos(ar), jnp.cos(ac), jnp.cos(ac)], axis=1)
    sin = jnp.concatenate([-jnp.sin(ar), jnp.sin(ar), -jnp.sin(ac), jnp.sin(ac)], axis=1)
    cos = jnp.concatenate([cos, jnp.ones((tm, DA_HEAD_DIM), F32)], axis=0)
    sin = jnp.concatenate([sin, jnp.zeros((tm, DA_HEAD_DIM), F32)], axis=0)
    scale = DA_HEAD_DIM ** -0.5 * math.log2(math.e)
    return cos * scale, sin * scale, cos, sin


def _attn_kernel(lam_ref, gs_ref, q_ref, *refs, lam_init, n_split):
    o_ref = refs[-1]
    k_refs, vt_refs = refs[0:-1:2], refs[1:-1:2]
    hd = DA_HEAD_DIM
    s1 = jnp.sum(lam_ref[0:1, :] * lam_ref[1:2, :], axis=-1, keepdims=True)
    s2 = jnp.sum(lam_ref[2:3, :] * lam_ref[3:4, :], axis=-1, keepdims=True)
    lam = jnp.exp(s1) - jnp.exp(s2) + lam_init
    tq = q_ref.shape[0]
    for r0 in range(0, tq, tq // n_split):
        rows = slice(r0, r0 + tq // n_split)
        outs = []
        for idx in range(2):
            qi = q_ref[rows, idx * hd:(idx + 1) * hd]
            ss = [lax.dot_general(kr[:, idx * hd:(idx + 1) * hd], qi, NT_DIMS, preferred_element_type=F32)
                  for kr in k_refs]
            m = functools.reduce(jnp.maximum, [jnp.max(s, axis=0, keepdims=True) for s in ss])
            ps = [jnp.exp2(s - m) for s in ss]
            l = functools.reduce(jnp.add, [jnp.sum(p, axis=0, keepdims=True) for p in ps])
            acc = functools.reduce(jnp.add, [jnp.dot(vr[...], p.astype(BF16), preferred_element_type=F32)
                                             for vr, p in zip(vt_refs, ps)])
            outs.append(acc * (1.0 / l))
        o = jnp.transpose(outs[0] - outs[1] * lam)
        o_ref[rows, :] = (_rms(o, SUBLN_EPS) * gs_ref[...] * (1.0 - lam_init)).astype(BF16)


def _attention(lam4, gs, q, kv_sets, *, lam_init, b, heads, tq, n_split, name):
    nq, aw = q.shape
    vw = 2 * DA_HEAD_DIM
    per_b = nq // b // tq
    q_spec = pl.BlockSpec((tq, vw), lambda bi, h, i: (bi * per_b + i, h))
    in_specs = [pl.BlockSpec((4, DA_HEAD_DIM), lambda bi, h, i: (0, 0)),
                pl.BlockSpec((1, vw), lambda bi, h, i: (0, 0)),
                q_spec]
    args = [lam4, gs, q]
    for k, vt in kv_sets:
        sk = k.shape[0] // b
        in_specs += [pl.BlockSpec((sk, vw), lambda bi, h, i: (bi, h)),
                     pl.BlockSpec((vw, sk), lambda bi, h, i: (h, bi))]
        args += [k, vt]
    return pl.pallas_call(
        functools.partial(_attn_kernel, lam_init=lam_init, n_split=n_split),
        grid=(b, heads, per_b),
        in_specs=in_specs,
        out_specs=q_spec,
        out_shape=jax.ShapeDtypeStruct((nq, aw), BF16),
        compiler_params=_cparams(("arbitrary", "arbitrary", "arbitrary")),
        name=name,
    )(*args)


def _ssm_weights(a_re, a_im, log_dt, b_re, b_im, c_re, c_im):
    L, H, P = SSM_CHUNK, SSM_GROUP, SSM_STATE
    g = a_re.shape[1]
    npair = g // 2
    hp = lax.Precision.HIGHEST
    ar, ai = a_re.astype(F32), a_im.astype(F32)
    dt = jnp.exp(log_dt.astype(F32))[..., None]
    n = jnp.arange(L + 1, dtype=F32)
    mag = jnp.exp((ar * dt)[..., None] * n)
    ang = (ai * dt)[..., None] * n
    apr, api = mag * jnp.cos(ang), mag * jnp.sin(ang)
    xr, xi = apr[..., 1] - 1.0, api[..., 1]
    den = ar * ar + ai * ai
    qr, qi = (xr * ar + xi * ai) / den, (xi * ar - xr * ai) / den
    br, bi = b_re.astype(F32), b_im.astype(F32)
    bbr = qr[..., None] * br - qi[..., None] * bi
    bbi = qr[..., None] * bi + qi[..., None] * br
    cr, ci = c_re.astype(F32), c_im.astype(F32)

    car = cr[..., None] * apr[:, :, None, :, :L] - ci[..., None] * api[:, :, None, :, :L]
    cai = cr[..., None] * api[:, :, None, :, :L] + ci[..., None] * apr[:, :, None, :, :L]
    kt = (jnp.einsum('dghpt,dgpk->dgkth', car, bbr, precision=hp)
          - jnp.einsum('dghpt,dgpk->dgkth', cai, bbi, precision=hp))
    kc = jnp.concatenate([jnp.flip(kt[1][:, :, 1:], axis=2), kt[0][:, :, :1] + kt[1][:, :, :1], kt[0][:, :, 1:]],
                         axis=2).reshape(g, H, (2 * L - 1) * H)
    w_intra = jnp.stack([kc[:, :, (L - 1 - s) * H:(2 * L - 1 - s) * H] for s in range(L)], axis=1)
    w_intra = w_intra.reshape(g, L * H, L * H)

    sir = jnp.stack([jnp.flip(apr[0, ..., :L], axis=-1), apr[1, ..., :L]])
    sii = jnp.stack([jnp.flip(api[0, ..., :L], axis=-1), api[1, ..., :L]])
    sir, sii = jnp.swapaxes(sir, 2, 3)[:, :, :, None, :], jnp.swapaxes(sii, 2, 3)[:, :, :, None, :]
    tbr, tbi = jnp.swapaxes(bbr, 2, 3)[:, :, None], jnp.swapaxes(bbi, 2, 3)[:, :, None]
    w_sin = jnp.concatenate([sir * tbr - sii * tbi, sir * tbi + sii * tbr], axis=-1)
    w_sin = w_sin.reshape(2, g, L * H, 2 * P)

    sor = jnp.stack([apr[0, ..., 1:], jnp.flip(apr[1, ..., 1:], axis=-1)])
    soi = jnp.stack([api[0, ..., 1:], jnp.flip(api[1, ..., 1:], axis=-1)])
    tcr, tci = jnp.swapaxes(cr, 2, 3)[:, :, :, None, :], jnp.swapaxes(ci, 2, 3)[:, :, :, None, :]
    cnr = (tcr * sor[..., None] - tci * soi[..., None]).reshape(2, g, P, L * H)
    cni = (tcr * soi[..., None] + tci * sor[..., None]).reshape(2, g, P, L * H)
    w_so = jnp.concatenate([cnr, -cni], axis=2)

    al = jnp.stack([apr[..., L].reshape(2, npair, 2 * P), api[..., L].reshape(2, npair, 2 * P)], axis=2)
    al = al.reshape(2, 2 * npair, 1, 2 * P)
    return w_intra.astype(BF16), w_sin.astype(BF16), w_so.astype(BF16), al


def _atom_transpose_matrix():
    n = OCT * OCT * SSM_GROUP
    i = np.arange(n)
    j = ((i // SSM_GROUP) % OCT) * LANES + (i // LANES) * SSM_GROUP + i % SSM_GROUP
    p = np.zeros((n, n), np.float32)
    p[i, j] = 1.0
    return jnp.asarray(p, BF16)


def _ssm_in_kernel(us_ref, p_ref, w_ref, ut_ref, s_ref):
    n = ut_ref.shape[1]
    noct = us_ref.shape[0]
    half = SSM_CHUNK // 2
    src = []
    for o in range(noct):
        for hf in range(2):
            rows = [us_ref[o, pl.ds(half * hf + sl, n, stride=SSM_CHUNK), :] for sl in range(half)]
            src.append(jnp.concatenate(rows, axis=1).astype(BF16))
    perm = jnp.dot(jnp.concatenate(src, axis=0), p_ref[...], preferred_element_type=F32).astype(BF16)

    def group_chunk(gi):
        o, gl = gi // OCT, gi % OCT
        return jnp.concatenate([perm[(2 * o + hf) * n:(2 * o + hf + 1) * n, gl * LANES:(gl + 1) * LANES]
                                for hf in range(2)], axis=1)

    low = lax.broadcasted_iota(jnp.int32, (n, LANES), 1) < SSM_STATE
    for j in range(noct * OCT // 2):
        ug = [group_chunk(2 * j), group_chunk(2 * j + 1)]
        ut_ref[2 * j] = ug[0]
        ut_ref[2 * j + 1] = ug[1]
        for d in range(2):
            s0 = jnp.dot(ug[0], w_ref[d, 2 * j], preferred_element_type=F32)
            s1 = jnp.dot(ug[1], w_ref[d, 2 * j + 1], preferred_element_type=F32)
            s_ref[d, 2 * j] = jnp.where(low, s0, pltpu.roll(s1, SSM_STATE, 1))
            s_ref[d, 2 * j + 1] = jnp.where(low, pltpu.roll(s0, SSM_STATE, 1), s1)


def _ssm_scan_kernel(sc_ref, sl_ref, al_ref, hc_ref, hl_ref, *, b):
    d = pl.program_id(0)
    nblk = sc_ref.shape[1]
    ars = [al_ref[0, 2 * k] for k in range(nblk // 2)]
    ais = [al_ref[0, 2 * k + 1] for k in range(nblk // 2)]

    def phase(s_ref, h_ref, carry):
        nch = s_ref.shape[2] // b

        def body(i, carry):
            c = jnp.where(d == 0, i, nch - 1 - i)
            rows = pl.ds(c, b, stride=nch)
            new = []
            for k in range(nblk // 2):
                hr, hi = carry[2 * k], carry[2 * k + 1]
                h_ref[0, 2 * k, rows, :] = hr
                h_ref[0, 2 * k + 1, rows, :] = hi
                sr = s_ref[0, 2 * k, rows, :]
                si = s_ref[0, 2 * k + 1, rows, :]
                new.append(ars[k] * hr - ais[k] * hi + sr)
                new.append(ars[k] * hi + ais[k] * hr + si)
            return tuple(new)

        return lax.fori_loop(0, nch, body, carry, unroll=4)

    carry = tuple(jnp.zeros((b, LANES), F32) for _ in range(nblk))
    carry = phase(sc_ref, hc_ref, carry)
    phase(sl_ref, hl_ref, carry)


def _ssm_out_kernel(ut_ref, h_ref, wi_ref, wo_ref, p_ref, y_ref):
    n = ut_ref.shape[1]
    ng = ut_ref.shape[0]
    noct = ng // OCT
    half = SSM_CHUNK // 2
    low = lax.broadcasted_iota(jnp.int32, (n, LANES), 1) < SSM_STATE
    ys = []
    for j in range(ng // 2):
        hg = [[], []]
        for d in range(2):
            hr, hi = h_ref[d, 2 * j], h_ref[d, 2 * j + 1]
            hg[0].append(jnp.where(low, hr, pltpu.roll(hi, SSM_STATE, 1)).astype(BF16))
            hg[1].append(jnp.where(low, pltpu.roll(hr, SSM_STATE, 1), hi).astype(BF16))
        for e in range(2):
            gi = 2 * j + e
            ys.append(jnp.dot(ut_ref[gi], wi_ref[gi], preferred_element_type=F32)
                      + jnp.dot(hg[e][0], wo_ref[0, gi], preferred_element_type=F32)
                      + jnp.dot(hg[e][1], wo_ref[1, gi], preferred_element_type=F32))
    rows = []
    for o in range(noct):
        for hf in range(2):
            rows.append(jnp.concatenate([ys[OCT * o + gl][:, hf * LANES:(hf + 1) * LANES] for gl in range(OCT)],
                                        axis=1))
    ycat = jnp.concatenate(rows, axis=0)
    hi = ycat.astype(BF16)
    lo = (ycat - hi.astype(F32)).astype(BF16)
    res = (jnp.dot(hi, p_ref[...], preferred_element_type=F32)
           + jnp.dot(lo, p_ref[...], preferred_element_type=F32))
    for o in range(noct):
        for hf in range(2):
            blk = 2 * o + hf
            for tl in range(half):
                y_ref[o, pl.ds(half * hf + tl, n, stride=SSM_CHUNK), :] = (
                    res[blk * n:(blk + 1) * n, tl * LANES:(tl + 1) * LANES])


def _ssm_conv(us_lat, us_ctx, weights, perm, *, b):
    w_intra, w_sin, w_so, al = weights
    P = SSM_STATE
    ng = w_intra.shape[0]
    nblk = ng

    def stage_in(us, steps):
        noct, rows, _ = us.shape
        nchunks = rows // SSM_CHUNK
        n = nchunks // steps
        return pl.pallas_call(
            _ssm_in_kernel,
            grid=(steps,),
            in_specs=[pl.BlockSpec((noct, n * SSM_CHUNK, LANES), lambda i: (0, i, 0)),
                      _resident(perm.shape), _resident(w_sin.shape)],
            out_specs=[pl.BlockSpec((ng, n, SSM_CW), lambda i: (0, i, 0)),
                       pl.BlockSpec((2, nblk, n, LANES), lambda i: (0, 0, i, 0))],
            out_shape=[jax.ShapeDtypeStruct((ng, nchunks, SSM_CW), BF16),
                       jax.ShapeDtypeStruct((2, nblk, nchunks, LANES), F32)],
            compiler_params=_cparams(("arbitrary",)),
            name="ssm_in",
        )(us, perm, w_sin)

    lat_steps = 2 * b
    ut_lat, s_lat = stage_in(us_lat, lat_steps)
    ut_ctx, s_ctx = stage_in(us_ctx, 1)

    cb = 16
    nc_rows, nl_rows = s_ctx.shape[2], s_lat.shape[2]
    h_ctx, h_lat = pl.pallas_call(
        functools.partial(_ssm_scan_kernel, b=b),
        grid=(2, nblk // cb),
        in_specs=[pl.BlockSpec((1, cb, nc_rows, LANES), lambda d, j: (d, j, 0, 0)),
                  pl.BlockSpec((1, cb, nl_rows, LANES), lambda d, j: (d, j, 0, 0)),
                  pl.BlockSpec((1, cb, 1, 2 * P), lambda d, j: (d, j, 0, 0))],
        out_specs=[pl.BlockSpec((1, cb, nc_rows, LANES), lambda d, j: (d, j, 0, 0)),
                   pl.BlockSpec((1, cb, nl_rows, LANES), lambda d, j: (d, j, 0, 0))],
        out_shape=[jax.ShapeDtypeStruct(s_ctx.shape, F32), jax.ShapeDtypeStruct(s_lat.shape, F32)],
        compiler_params=_cparams(("arbitrary", "arbitrary")),
        name="ssm_scan",
    )(s_ctx, s_lat, al)

    def stage_out(ut, hin, steps):
        _, nchunks, _ = ut.shape
        n = nchunks // steps
        noct = ng // OCT
        return pl.pallas_call(
            _ssm_out_kernel,
            grid=(steps,),
            in_specs=[pl.BlockSpec((ng, n, SSM_CW), lambda i: (0, i, 0)),
                      pl.BlockSpec((2, nblk, n, LANES), lambda i: (0, 0, i, 0)),
                      _resident(w_intra.shape), _resident(w_so.shape), _resident(perm.shape)],
            out_specs=pl.BlockSpec((noct, n * SSM_CHUNK, LANES), lambda i: (0, i, 0)),
            out_shape=jax.ShapeDtypeStruct((noct, nchunks * SSM_CHUNK, LANES), F32),
            compiler_params=_cparams(("arbitrary",)),
            name="ssm_out",
        )(ut, hin, w_intra, w_so, perm)

    return stage_out(ut_lat, h_lat, lat_steps), stage_out(ut_ctx, h_ctx, 1)


def _fourier_kernel(u_ref, w_ref, cos_ref, sin_ref, b_ref, o_ref, a_s, b_s, *, groups):
    gc = u_ref.shape[1] // groups

    @pl.when(pl.program_id(1) == 0)
    def _():
        for g in range(groups):
            ab = jnp.dot(u_ref[:, g * gc:(g + 1) * gc], w_ref[g], preferred_element_type=F32)
            a_s[:, g * gc:(g + 1) * gc] = ab[:, :gc].astype(BF16)
            b_s[:, g * gc:(g + 1) * gc] = ab[:, gc:].astype(BF16)

    o_ref[...] = (jnp.dot(cos_ref[...], a_s[...], preferred_element_type=F32)
                  - jnp.dot(sin_ref[...], b_s[...], preferred_element_type=F32)
                  + b_ref[...]).astype(BF16)


def _dft_tables(n):
    m = math.isqrt(n)
    assert m * m == n
    k = jnp.arange(m, dtype=jnp.int32)[:, None]
    t = jnp.arange(n, dtype=jnp.int32)[None, :]
    alpha = ((k * t) % m).astype(F32) * (2.0 * math.pi / m)
    beta = ((k * t) % n).astype(F32) * (2.0 * math.pi / n)
    ca, sa = jnp.cos(alpha)[:, None, :], jnp.sin(alpha)[:, None, :]
    cb, sb = jnp.cos(beta)[None, :, :], jnp.sin(beta)[None, :, :]
    cos = (ca * cb - sa * sb).reshape(n, n)
    sin = (sa * cb + ca * sb).reshape(n, n)
    return cos.astype(BF16), sin.astype(BF16)


def _fourier(uf, w_cs, bias, tables, *, b, tm):
    nr, fw = uf.shape
    seq = nr // b
    groups = w_cs.shape[0]
    cos_t, sin_t = tables
    tm = min(tm, seq)
    per_b = seq // tm
    return pl.pallas_call(
        functools.partial(_fourier_kernel, groups=groups),
        grid=(b, per_b),
        in_specs=[
            pl.BlockSpec((seq, fw), lambda bi, i: (bi, 0)),
            pl.BlockSpec(w_cs.shape, lambda bi, i: (0, 0, 0)),
            pl.BlockSpec((tm, seq), lambda bi, i: (i, 0)),
            pl.BlockSpec((tm, seq), lambda bi, i: (i, 0)),
            pl.BlockSpec((1, fw), lambda bi, i: (0, 0)),
        ],
        out_specs=pl.BlockSpec((tm, fw), lambda bi, i: (bi * per_b + i, 0)),
        out_shape=jax.ShapeDtypeStruct((nr, fw), BF16),
        scratch_shapes=[pltpu.VMEM((seq, fw), BF16), pltpu.VMEM((seq, fw), BF16)],
        compiler_params=_cparams(("arbitrary", "arbitrary")),
        name="fourier_%d" % seq,
    )(uf, w_cs, cos_t, sin_t, bias)


def _fourier_weights(w_four, seq):
    groups, gc, _ = w_four.shape
    k = np.arange(gc)
    ang = ((k[:, None] * k[None, :]) % gc).astype(np.float64) * (2.0 * np.pi / gc)
    norm = 1.0 / math.sqrt(seq * gc)
    cc = jnp.asarray(np.cos(ang) * norm, F32)
    sc = jnp.asarray(np.sin(ang) * norm, F32)
    hp = lax.Precision.HIGHEST
    wc = jnp.einsum('ck,gkd->gcd', cc, w_four.astype(F32), precision=hp)
    ws = jnp.einsum('ck,gkd->gcd', sc, w_four.astype(F32), precision=hp)
    return jnp.concatenate([wc, ws], axis=-1).astype(BF16)


def _outproj_kernel(att_ref, us_ref, yc_ref, uf_ref, x_ref, mod_ref, wo_ref, wg_ref, bg_ref, dsk_ref,
                    gpost_ref, gpre_ref, xo_ref, h2_ref, *, aw, sw):
    nblk = us_ref.shape[0]
    tm = x_ref.shape[0]
    for r0 in range(0, tm, tm // 2):
        rows = slice(r0, r0 + tm // 2)
        us = jnp.concatenate([us_ref[j, rows, :] for j in range(nblk)], axis=1)
        yc = jnp.concatenate([yc_ref[j, rows, :] for j in range(nblk)], axis=1)
        g = jax.nn.gelu(dsk_ref[...] * us + yc)
        z = jnp.dot(g.astype(BF16), wg_ref[...], preferred_element_type=F32) + bg_ref[...]
        ssm = (g * jax.nn.sigmoid(z)).astype(BF16)
        mix = (jnp.dot(att_ref[rows, :], wo_ref[0:aw, :], preferred_element_type=F32)
               + jnp.dot(ssm, wo_ref[aw:aw + sw, :], preferred_element_type=F32)
               + jnp.dot(uf_ref[rows, :], wo_ref[aw + sw:, :], preferred_element_type=F32))
        xn = x_ref[rows, :] + mod_ref[0, 2:3, :] * (_rms(mix, NORM_EPS) * gpost_ref[...])
        xo_ref[rows, :] = xn
        h2 = _rms(xn, NORM_EPS) * gpre_ref[...]
        h2_ref[rows, :] = (h2 * (1.0 + mod_ref[0, 4:5, :]) + mod_ref[0, 3:4, :]).astype(BF16)


def _outproj(att, us, yc, four, x, mod, w_out, w_glu, b_glu, dsk, g_post, g_pre, *, layer, mod_row, tm):
    n_rows, d = x.shape
    aw, fw = att.shape[1], four.shape[1]
    nblk = us.shape[0]
    sw = nblk * LANES

    def row(i):
        return (i, 0)

    def const(i):
        return (0, 0)

    blk3 = pl.BlockSpec((nblk, tm, LANES), lambda i: (0, i, 0))
    return pl.pallas_call(
        functools.partial(_outproj_kernel, aw=aw, sw=sw),
        grid=(n_rows // tm,),
        in_specs=[
            pl.BlockSpec((tm, aw), row), blk3, blk3,
            pl.BlockSpec((tm, fw), row), pl.BlockSpec((tm, d), row),
            pl.BlockSpec((1, N_MOD, d), lambda i: (mod_row(i), 0, 0)),
            _resident_layer(w_out.shape, layer), _resident_layer(w_glu.shape, layer),
            pl.BlockSpec((1, sw), const), pl.BlockSpec((1, sw), const),
            pl.BlockSpec((1, d), const), pl.BlockSpec((1, d), const),
        ],
        out_specs=[pl.BlockSpec((tm, d), row), pl.BlockSpec((tm, d), row)],
        out_shape=[jax.ShapeDtypeStruct((n_rows, d), F32), jax.ShapeDtypeStruct((n_rows, d), BF16)],
        compiler_params=_cparams(("arbitrary",)),
        name="outproj",
    )(att, us, yc, four, x, mod, w_out, w_glu, b_glu, dsk, g_post, g_pre)


def _ffn_kernel(h_ref, x_ref, mod_ref, g_ref, wg_ref, wu_ref, wd_ref, o_ref):
    k = pl.program_id(1)

    @pl.when(k == 0)
    def _():
        o_ref[...] = jnp.zeros(o_ref.shape, F32)

    h = h_ref[...]
    tf = wg_ref.shape[1]
    part = None
    for c0 in range(0, tf, tf // 2):
        cols = slice(c0, c0 + tf // 2)
        a = jnp.dot(h, wg_ref[:, cols], preferred_element_type=F32)
        u = jnp.dot(h, wu_ref[:, cols], preferred_element_type=F32)
        p = jnp.dot((_silu(a) * u).astype(BF16), wd_ref[cols, :], preferred_element_type=F32)
        part = p if part is None else part + p
    o_ref[...] += part

    @pl.when(k == pl.num_programs(1) - 1)
    def _():
        o_ref[...] = x_ref[...] + mod_ref[0, 5:6, :] * (_rms(o_ref[...], NORM_EPS) * g_ref[...])


def _ffn(h2, x_mid, mod, g_post, w_gate, w_up, w_down, *, layer, mod_row, tm, tf):
    n_rows, d = h2.shape
    dff = w_gate.shape[2]
    return pl.pallas_call(
        _ffn_kernel,
        grid=(n_rows // tm, dff // tf),
        in_specs=[
            pl.BlockSpec((tm, d), lambda i, k: (i, 0)),
            pl.BlockSpec((tm, d), lambda i, k: (i, 0)),
            pl.BlockSpec((1, N_MOD, d), lambda i, k: (mod_row(i), 0, 0)),
            pl.BlockSpec((1, d), lambda i, k: (0, 0)),
            pl.BlockSpec((None, d, tf), lambda i, k: (layer, 0, k)),
            pl.BlockSpec((None, d, tf), lambda i, k: (layer, 0, k)),
            pl.BlockSpec((None, tf, d), lambda i, k: (layer, k, 0)),
        ],
        out_specs=pl.BlockSpec((tm, d), lambda i, k: (i, 0)),
        out_shape=jax.ShapeDtypeStruct((n_rows, d), F32),
        compiler_params=_cparams(("arbitrary", "arbitrary")),
        name="ffn",
    )(h2, x_mid, mod, g_post, w_gate, w_up, w_down)


def kernel(x, c, ctx, c_ctx, w_mod, b_mod, g_mix_pre, g_mix_post, g_ffn_pre, g_ffn_post, w_in, w_out, lam_q1, lam_k1, lam_q2, lam_k2, g_subln, ssm_a_re, ssm_a_im, ssm_log_dt, ssm_b_re, ssm_b_im, ssm_c_re, ssm_c_im, ssm_d, w_glu, b_glu, w_four, b_four, w_gate, w_up, w_down):
    b, t, d = x.shape
    n_ctx = ctx.shape[1]
    depth = w_mod.shape[0]
    nl, nc = b * t, b * n_ctx
    aw = d // 2
    sw = ssm_d.shape[1]
    fw = d - aw - sw
    heads = aw // (2 * DA_HEAD_DIM)
    tm = 512
    assert t % tm == 0 and nc % tm == 0 and n_ctx % SSM_CHUNK == 0 and t % GRID_W == 0 and b + 1 <= MOD_ROWS
    assert sw % LANES == 0 and (sw // SSM_GROUP) % (2 * OCT) == 0

    cs = jnp.concatenate([c, c_ctx[None, :], jnp.zeros((MOD_ROWS - b - 1, d), F32)], axis=0)
    mod_all = _modulation(cs, w_mod, b_mod).reshape(depth, MOD_ROWS, N_MOD, d)

    per_b = t // tm
    lat_mod = lambda i: i // per_b
    ctx_mod = lambda i: b
    tabs = _rope_tables(t, tm)
    dft_lat = _dft_tables(t)
    dft_ctx = _dft_tables(n_ctx)
    perm = _atom_transpose_matrix()
    x_lat, x_ctx = x.reshape(nl, d), ctx.reshape(nc, d)
    w_in_b, w_out_b, w_glu_b = w_in.astype(BF16), w_out.astype(BF16), w_glu.astype(BF16)
    w_vt_b = jnp.transpose(w_in_b[:, :, 2 * aw:3 * aw], (0, 2, 1))
    w_gate_b, w_up_b, w_down_b = w_gate.astype(BF16), w_up.astype(BF16), w_down.astype(BF16)

    for l in range(depth):
        need_ctx = l < depth - 1
        lam_init = 0.8 - 0.6 * math.exp(-0.3 * l)
        mod = mod_all[l]
        g_pre = g_mix_pre[l][None, :]
        q, k, vt, us, uf = _inproj(x_lat, mod, g_pre, w_in_b, w_vt_b, tabs, layer=l, mod_row=lat_mod,
                                   tab_blk=lambda i: i % per_b, aw=aw, sw=sw, fw=fw, tm=tm)
        qc, kc, vtc, usc, ufc = _inproj(x_ctx, mod, g_pre, w_in_b, w_vt_b, tabs, layer=l, mod_row=ctx_mod,
                                        tab_blk=lambda i: per_b, aw=aw, sw=sw, fw=fw, tm=tm)

        lam4 = jnp.stack([lam_q1[l], lam_k1[l], lam_q2[l], lam_k2[l]]).astype(F32)
        gs = g_subln[l][None, :].astype(F32)
        att = _attention(lam4, gs, q, [(kc, vtc), (k, vt)], lam_init=lam_init, b=b, heads=heads, tq=1024,
                         n_split=1, name="attn_latent")
        ssm_w = _ssm_weights(ssm_a_re[l], ssm_a_im[l], ssm_log_dt[l], ssm_b_re[l], ssm_b_im[l],
                             ssm_c_re[l], ssm_c_im[l])
        yc, ycc = _ssm_conv(us, usc, ssm_w, perm, b=b)
        bias = b_four[l].reshape(1, fw).astype(F32)
        four = _fourier(uf, _fourier_weights(w_four[l], t), bias, dft_lat, b=b, tm=tm)

        small = (b_glu[l][None, :].astype(F32), ssm_d[l][None, :].astype(F32), g_mix_post[l][None, :],
                 g_ffn_pre[l][None, :])
        ffn_w = (g_ffn_post[l][None, :], w_gate_b, w_up_b, w_down_b)
        x_mid, h2 = _outproj(att, us, yc, four, x_lat, mod, w_out_b, w_glu_b, *small, layer=l, mod_row=lat_mod,
                             tm=tm)
        x_lat = _ffn(h2, x_mid, mod, *ffn_w, layer=l, mod_row=lat_mod, tm=tm, tf=512)

        if need_ctx:
            att_c = _attention(lam4, gs, qc, [(kc, vtc)], lam_init=lam_init, b=b, heads=heads, tq=n_ctx,
                               n_split=1, name="attn_ctx")
            four_c = _fourier(ufc, _fourier_weights(w_four[l], n_ctx), bias, dft_ctx, b=b, tm=tm)
            xc_mid, h2c = _outproj(att_c, usc, ycc, four_c, x_ctx, mod, w_out_b, w_glu_b, *small, layer=l,
                                   mod_row=ctx_mod, tm=tm)
            x_ctx = _ffn(h2c, xc_mid, mod, *ffn_w, layer=l, mod_row=ctx_mod, tm=tm, tf=512)
    return x_lat.reshape(b, t, d)
```

```python
import functools
import math

import jax
import jax.numpy as jnp
import numpy as np
from jax import lax
from jax.experimental import pallas as pl
from jax.experimental.pallas import tpu as pltpu

F32 = jnp.float32
BF16 = jnp.bfloat16

LANES = 128
GRID_W = 64
DA_HEAD_DIM = 128
SSM_GROUP = 16
SSM_STATE = 64
N_MOD = 6
ROPE_BASE = 10000.0
ROPE_PAIRS = DA_HEAD_DIM // 4
NORM_EPS = 1e-6
SUBLN_EPS = 1e-5

SSM_CHUNK = 16
SSM_CW = SSM_CHUNK * SSM_GROUP
OCT = LANES // SSM_GROUP
MOD_ROWS = 8
VMEM_LIMIT = 56 * 1024 * 1024
NT_DIMS = (((1,), (1,)), ((), ()))


def _cparams(sem):
    return pltpu.CompilerParams(dimension_semantics=sem, vmem_limit_bytes=VMEM_LIMIT)


def _rms(x, eps):
    return x * lax.rsqrt(jnp.mean(x * x, axis=-1, keepdims=True) + eps)


def _silu(x):
    return x * jax.nn.sigmoid(x)


def _resident(shape):
    nd = len(shape)
    return pl.BlockSpec(shape, lambda *_: (0,) * nd, pipeline_mode=pl.Buffered(1))


def _resident_layer(shape, layer):
    nd = len(shape)
    return pl.BlockSpec((None,) + tuple(shape[1:]), lambda *_: (layer,) + (0,) * (nd - 1),
                        pipeline_mode=pl.Buffered(1))


def _mod_kernel(c_ref, w_ref, b_ref, o_ref):
    s = _silu(c_ref[...]).astype(BF16)
    o_ref[0] = jnp.dot(s, w_ref[0].astype(BF16), preferred_element_type=F32) + b_ref[0]


def _modulation(cs, w_mod, b_mod, tn=1024):
    depth, d, n = w_mod.shape
    return pl.pallas_call(
        _mod_kernel,
        grid=(depth, n // tn),
        in_specs=[
            pl.BlockSpec((MOD_ROWS, d), lambda l, j: (0, 0)),
            pl.BlockSpec((1, d, tn), lambda l, j: (l, 0, j)),
            pl.BlockSpec((1, 1, tn), lambda l, j: (l, 0, j)),
        ],
        out_specs=pl.BlockSpec((1, MOD_ROWS, tn), lambda l, j: (l, 0, j)),
        out_shape=jax.ShapeDtypeStruct((depth, MOD_ROWS, n), F32),
        compiler_params=_cparams(("arbitrary", "arbitrary")),
        name="modulation",
    )(cs, w_mod, b_mod.reshape(depth, 1, n))


def _inproj_kernel(x_ref, mod_ref, g_ref, w_ref, wvt_ref, cq_ref, sq_ref, ck_ref, sk_ref,
                   q_ref, k_ref, vt_ref, us_ref, uf_ref, *, aw, sw):
    tm = x_ref.shape[0]
    h = _rms(x_ref[...], NORM_EPS) * g_ref[...]
    h = h * (1.0 + mod_ref[0, 1:2, :]) + mod_ref[0, 0:1, :]
    hb = h.astype(BF16)

    lane = lax.broadcasted_iota(jnp.int32, (tm, DA_HEAD_DIM), 1)
    first_half = (lane & (2 * ROPE_PAIRS - 1)) < ROPE_PAIRS

    def rope(z, c, s):
        partner = jnp.where(first_half, pltpu.roll(z, DA_HEAD_DIM - ROPE_PAIRS, 1), pltpu.roll(z, ROPE_PAIRS, 1))
        return z * c + partner * s

    zq = jnp.dot(hb, w_ref[:, 0:aw], preferred_element_type=F32)
    cq, sq = cq_ref[...], sq_ref[...]
    for j in range(aw // DA_HEAD_DIM):
        sl = slice(j * DA_HEAD_DIM, (j + 1) * DA_HEAD_DIM)
        q_ref[:, sl] = rope(zq[:, sl], cq, sq).astype(BF16)
    zk = jnp.dot(hb, w_ref[:, aw:2 * aw], preferred_element_type=F32)
    ck, sk = ck_ref[...], sk_ref[...]
    for j in range(aw // DA_HEAD_DIM):
        sl = slice(j * DA_HEAD_DIM, (j + 1) * DA_HEAD_DIM)
        k_ref[:, sl] = rope(zk[:, sl], ck, sk).astype(BF16)
    vt_ref[...] = lax.dot_general(wvt_ref[...], hb, NT_DIMS, preferred_element_type=F32).astype(BF16)
    zs = jnp.dot(hb, w_ref[:, 3 * aw:3 * aw + sw], preferred_element_type=F32)
    for j in range(sw // LANES):
        us_ref[j] = zs[:, j * LANES:(j + 1) * LANES]
    uf_ref[...] = jnp.dot(hb, w_ref[:, 3 * aw + sw:], preferred_element_type=F32).astype(BF16)


def _inproj(x, mod, g, w_in, w_vt, tabs, *, layer, mod_row, tab_blk, aw, sw, fw, tm):
    nr, d = x.shape

    def row(i):
        return (i, 0)

    tab_spec = pl.BlockSpec((tm, DA_HEAD_DIM), lambda i: (tab_blk(i), 0))
    return pl.pallas_call(
        functools.partial(_inproj_kernel, aw=aw, sw=sw),
        grid=(nr // tm,),
        in_specs=[
            pl.BlockSpec((tm, d), row),
            pl.BlockSpec((1, N_MOD, d), lambda i: (mod_row(i), 0, 0)),
            pl.BlockSpec((1, d), lambda i: (0, 0)),
            _resident_layer(w_in.shape, layer), _resident_layer(w_vt.shape, layer),
            tab_spec, tab_spec, tab_spec, tab_spec,
        ],
        out_specs=[
            pl.BlockSpec((tm, aw), row), pl.BlockSpec((tm, aw), row), pl.BlockSpec((aw, tm), lambda i: (0, i)),
            pl.BlockSpec((sw // LANES, tm, LANES), lambda i: (0, i, 0)), pl.BlockSpec((tm, fw), row),
        ],
        out_shape=[
            jax.ShapeDtypeStruct((nr, aw), BF16), jax.ShapeDtypeStruct((nr, aw), BF16),
            jax.ShapeDtypeStruct((aw, nr), BF16), jax.ShapeDtypeStruct((sw // LANES, nr, LANES), F32),
            jax.ShapeDtypeStruct((nr, fw), BF16),
        ],
        compiler_params=_cparams(("arbitrary",)),
        name="inproj",
    )(x, mod, g, w_in, w_vt, *tabs)


def _rope_tables(t, tm):
    rows = t // GRID_W
    r = jnp.broadcast_to(jnp.arange(rows, dtype=F32)[:, None], (rows, GRID_W)).reshape(-1)
    col = jnp.broadcast_to(jnp.arange(GRID_W, dtype=F32)[None, :], (rows, GRID_W)).reshape(-1)
    inv = ROPE_BASE ** (-jnp.arange(ROPE_PAIRS, dtype=F32) / ROPE_PAIRS)
    ar, ac = r[:, None] * inv, col[:, None] * inv
    cos = jnp.concatenate([jnp.cos(ar), jnp.cos(ar), jnp.cos(ac), jnp.cos(ac)], axis=1)
    sin = jnp.concatenate([-jnp.sin(ar), jnp.sin(ar), -jnp.sin(ac), jnp.sin(ac)], axis=1)
    cos = jnp.concatenate([cos, jnp.ones((tm, DA_HEAD_DIM), F32)], axis=0)
    sin = jnp.concatenate([sin, jnp.zeros((tm, DA_HEAD_DIM), F32)], axis=0)
    scale = DA_HEAD_DIM ** -0.5 * math.log2(math.e)
    return cos * scale, sin * scale, cos, sin


def _attn_kernel(lam_ref, gs_ref, q_ref, *refs, lam_init, n_split):
    o_ref = refs[-1]
    k_refs, vt_refs = refs[0:-1:2], refs[1:-1:2]
    hd = DA_HEAD_DIM
    s1 = jnp.sum(lam_ref[0:1, :] * lam_ref[1:2, :], axis=-1, keepdims=True)
    s2 = jnp.sum(lam_ref[2:3, :] * lam_ref[3:4, :], axis=-1, keepdims=True)
    lam = jnp.exp(s1) - jnp.exp(s2) + lam_init
    tq = q_ref.shape[0]
    for r0 in range(0, tq, tq // n_split):
        rows = slice(r0, r0 + tq // n_split)
        outs = []
        for idx in range(2):
            qi = q_ref[rows, idx * hd:(idx + 1) * hd]
            ss = [lax.dot_general(kr[:, idx * hd:(idx + 1) * hd], qi, NT_DIMS, preferred_element_type=F32)
                  for kr in k_refs]
            m = functools.reduce(jnp.maximum, [jnp.max(s, axis=0, keepdims=True) for s in ss])
            ps = [jnp.exp2(s - m) for s in ss]
            l = functools.reduce(jnp.add, [jnp.sum(p, axis=0, keepdims=True) for p in ps])
            acc = functools.reduce(jnp.add, [jnp.dot(vr[...], p.astype(BF16), preferred_element_type=F32)
                                             for vr, p in zip(vt_refs, ps)])
            outs.append(acc * (1.0 / l))
        o = jnp.transpose(outs[0] - outs[1] * lam)
        o_ref[rows, :] = (_rms(o, SUBLN_EPS) * gs_ref[...] * (1.0 - lam_init)).astype(BF16)


def _attention(lam4, gs, q, kv_sets, *, lam_init, b, heads, tq, n_split, name):
    nq, aw = q.shape
    vw = 2 * DA_HEAD_DIM
    per_b = nq // b // tq
    q_spec = pl.BlockSpec((tq, vw), lambda bi, h, i: (bi * per_b + i, h))
    in_specs = [pl.BlockSpec((4, DA_HEAD_DIM), lambda bi, h, i: (0, 0)),
                pl.BlockSpec((1, vw), lambda bi, h, i: (0, 0)),
                q_spec]
    args = [lam4, gs, q]
    for k, vt in kv_sets:
        sk = k.shape[0] // b
        in_specs += [pl.BlockSpec((sk, vw), lambda bi, h, i: (bi, h)),
                     pl.BlockSpec((vw, sk), lambda bi, h, i: (h, bi))]
        args += [k, vt]
    return pl.pallas_call(
        functools.partial(_attn_kernel, lam_init=lam_init, n_split=n_split),
        grid=(b, heads, per_b),
        in_specs=in_specs,
        out_specs=q_spec,
        out_shape=jax.ShapeDtypeStruct((nq, aw), BF16),
        compiler_params=_cparams(("arbitrary", "arbitrary", "arbitrary")),
        name=name,
    )(*args)


def _ssm_weights(a_re, a_im, log_dt, b_re, b_im, c_re, c_im):
    L, H, P = SSM_CHUNK, SSM_GROUP, SSM_STATE
    g = a_re.shape[1]
    npair = g // 2
    hp = lax.Precision.HIGHEST
    ar, ai = a_re.astype(F32), a_im.astype(F32)
    dt = jnp.exp(log_dt.astype(F32))[..., None]
    n = jnp.arange(L + 1, dtype=F32)
    mag = jnp.exp((ar * dt)[..., None] * n)
    ang = (ai * dt)[..., None] * n
    apr, api = mag * jnp.cos(ang), mag * jnp.sin(ang)
    xr, xi = apr[..., 1] - 1.0, api[..., 1]
    den = ar * ar + ai * ai
    qr, qi = (xr * ar + xi * ai) / den, (xi * ar - xr * ai) / den
    br, bi = b_re.astype(F32), b_im.astype(F32)
    bbr = qr[..., None] * br - qi[..., None] * bi
    bbi = qr[..., None] * bi + qi[..., None] * br
    cr, ci = c_re.astype(F32), c_im.astype(F32)

    car = cr[..., None] * apr[:, :, None, :, :L] - ci[..., None] * api[:, :, None, :, :L]
    cai = cr[..., None] * api[:, :, None, :, :L] + ci[..., None] * apr[:, :, None, :, :L]
    kt = (jnp.einsum('dghpt,dgpk->dgkth', car, bbr, precision=hp)
          - jnp.einsum('dghpt,dgpk->dgkth', cai, bbi, precision=hp))
    kc = jnp.concatenate([jnp.flip(kt[1][:, :, 1:], axis=2), kt[0][:, :, :1] + kt[1][:, :, :1], kt[0][:, :, 1:]],
                         axis=2).reshape(g, H, (2 * L - 1) * H)
    w_intra = jnp.stack([kc[:, :, (L - 1 - s) * H:(2 * L - 1 - s) * H] for s in range(L)], axis=1)
    w_intra = w_intra.reshape(g, L * H, L * H)

    sir = jnp.stack([jnp.flip(apr[0, ..., :L], axis=-1), apr[1, ..., :L]])
    sii = jnp.stack([jnp.flip(api[0, ..., :L], axis=-1), api[1, ..., :L]])
    sir, sii = jnp.swapaxes(sir, 2, 3)[:, :, :, None, :], jnp.swapaxes(sii, 2, 3)[:, :, :, None, :]
    tbr, tbi = jnp.swapaxes(bbr, 2, 3)[:, :, None], jnp.swapaxes(bbi, 2, 3)[:, :, None]
    w_sin = jnp.concatenate([sir * tbr - sii * tbi, sir * tbi + sii * tbr], axis=-1)
    w_sin = w_sin.reshape(2, g, L * H, 2 * P)

    sor = jnp.stack([apr[0, ..., 1:], jnp.flip(apr[1, ..., 1:], axis=-1)])
    soi = jnp.stack([api[0, ..., 1:], jnp.flip(api[1, ..., 1:], axis=-1)])
    tcr, tci = jnp.swapaxes(cr, 2, 3)[:, :, :, None, :], jnp.swapaxes(ci, 2, 3)[:, :, :, None, :]
    cnr = (tcr * sor[..., None] - tci * soi[..., None]).reshape(2, g, P, L * H)
    cni = (tcr * soi[..., None] + tci * sor[..., None]).reshape(2, g, P, L * H)
    w_so = jnp.concatenate([cnr, -cni], axis=2)

    al = jnp.stack([apr[..., L].reshape(2, npair, 2 * P), api[..., L].reshape(2, npair, 2 * P)], axis=2)
    al = al.reshape(2, 2 * npair, 1, 2 * P)
    return w_intra.astype(BF16), w_sin.astype(BF16), w_so.astype(BF16), al


def _atom_transpose_matrix():
    n = OCT * OCT * SSM_GROUP
    i = np.arange(n)
    j = ((i // SSM_GROUP) % OCT) * LANES + (i // LANES) * SSM_GROUP + i % SSM_GROUP
    p = np.zeros((n, n), np.float32)
    p[i, j] = 1.0
    return jnp.asarray(p, BF16)


def _ssm_in_kernel(us_ref, p_ref, w_ref, ut_ref, s_ref):
    n = ut_ref.shape[1]
    noct = us_ref.shape[0]
    half = SSM_CHUNK // 2
    src = []
    for o in range(noct):
        for hf in range(2):
            rows = [us_ref[o, pl.ds(half * hf + sl, n, stride=SSM_CHUNK), :] for sl in range(half)]
            src.append(jnp.concatenate(rows, axis=1).astype(BF16))
    perm = jnp.dot(jnp.concatenate(src, axis=0), p_ref[...], preferred_element_type=F32).astype(BF16)

    def group_chunk(gi):
        o, gl = gi // OCT, gi % OCT
        return jnp.concatenate([perm[(2 * o + hf) * n:(2 * o + hf + 1) * n, gl * LANES:(gl + 1) * LANES]
                                for hf in range(2)], axis=1)

    low = lax.broadcasted_iota(jnp.int32, (n, LANES), 1) < SSM_STATE
    for j in range(noct * OCT // 2):
        ug = [group_chunk(2 * j), group_chunk(2 * j + 1)]
        ut_ref[2 * j] = ug[0]
        ut_ref[2 * j + 1] = ug[1]
        for d in range(2):
            s0 = jnp.dot(ug[0], w_ref[d, 2 * j], preferred_element_type=F32)
            s1 = jnp.dot(ug[1], w_ref[d, 2 * j + 1], preferred_element_type=F32)
            s_ref[d, 2 * j] = jnp.where(low, s0, pltpu.roll(s1, SSM_STATE, 1))
            s_ref[d, 2 * j + 1] = jnp.where(low, pltpu.roll(s0, SSM_STATE, 1), s1)


def _ssm_scan_kernel(sc_ref, sl_ref, al_ref, hc_ref, hl_ref, *, b):
    d = pl.program_id(0)
    nblk = sc_ref.shape[1]
    ars = [al_ref[0, 2 * k] for k in range(nblk // 2)]
    ais = [al_ref[0, 2 * k + 1] for k in range(nblk // 2)]

    def phase(s_ref, h_ref, carry):
        nch = s_ref.shape[2] // b

        def body(i, carry):
            c = jnp.where(d == 0, i, nch - 1 - i)
            rows = pl.ds(c, b, stride=nch)
            new = []
            for k in range(nblk // 2):
                hr, hi = carry[2 * k], carry[2 * k + 1]
                h_ref[0, 2 * k, rows, :] = hr
                h_ref[0, 2 * k + 1, rows, :] = hi
                sr = s_ref[0, 2 * k, rows, :]
                si = s_ref[0, 2 * k + 1, rows, :]
                new.append(ars[k] * hr - ais[k] * hi + sr)
                new.append(ars[k] * hi + ais[k] * hr + si)
            return tuple(new)

        return lax.fori_loop(0, nch, body, carry, unroll=4)

    carry = tuple(jnp.zeros((b, LANES), F32) for _ in range(nblk))
    carry = phase(sc_ref, hc_ref, carry)
    phase(sl_ref, hl_ref, carry)


def _ssm_out_kernel(ut_ref, h_ref, wi_ref, wo_ref, p_ref, y_ref):
    n = ut_ref.shape[1]
    ng = ut_ref.shape[0]
    noct = ng // OCT
    half = SSM_CHUNK // 2
    low = lax.broadcasted_iota(jnp.int32, (n, LANES), 1) < SSM_STATE
    ys = []
    for j in range(ng // 2):
        hg = [[], []]
        for d in range(2):
            hr, hi = h_ref[d, 2 * j], h_ref[d, 2 * j + 1]
            hg[0].append(jnp.where(low, hr, pltpu.roll(hi, SSM_STATE, 1)).astype(BF16))
            hg[1].append(jnp.where(low, pltpu.roll(hr, SSM_STATE, 1), hi).astype(BF16))
        for e in range(2):
            gi = 2 * j + e
            ys.append(jnp.dot(ut_ref[gi], wi_ref[gi], preferred_element_type=F32)
                      + jnp.dot(hg[e][0], wo_ref[0, gi], preferred_element_type=F32)
                      + jnp.dot(hg[e][1], wo_ref[1, gi], preferred_element_type=F32))
    rows = []
    for o in range(noct):
        for hf in range(2):
            rows.append(jnp.concatenate([ys[OCT * o + gl][:, hf * LANES:(hf + 1) * LANES] for gl in range(OCT)],
                                        axis=1))
    ycat = jnp.concatenate(rows, axis=0)
    hi = ycat.astype(BF16)
    lo = (ycat - hi.astype(F32)).astype(BF16)
    res = (jnp.dot(hi, p_ref[...], preferred_element_type=F32)
           + jnp.dot(lo, p_ref[...], preferred_element_type=F32))
    for o in range(noct):
        for hf in range(2):
            blk = 2 * o + hf
            for tl in range(half):
                y_ref[o, pl.ds(half * hf + tl, n, stride=SSM_CHUNK), :] = (
                    res[blk * n:(blk + 1) * n, tl * LANES:(tl + 1) * LANES])


def _ssm_conv(us_lat, us_ctx, weights, perm, *, b):
    w_intra, w_sin, w_so, al = weights
    P = SSM_STATE
    ng = w_intra.shape[0]
    nblk = ng

    def stage_in(us, steps):
        noct, rows, _ = us.shape
        nchunks = rows // SSM_CHUNK
        n = nchunks // steps
        return pl.pallas_call(
            _ssm_in_kernel,
            grid=(steps,),
            in_specs=[pl.BlockSpec((noct, n * SSM_CHUNK, LANES), lambda i: (0, i, 0)),
                      _resident(perm.shape), _resident(w_sin.shape)],
            out_specs=[pl.BlockSpec((ng, n, SSM_CW), lambda i: (0, i, 0)),
                       pl.BlockSpec((2, nblk, n, LANES), lambda i: (0, 0, i, 0))],
            out_shape=[jax.ShapeDtypeStruct((ng, nchunks, SSM_CW), BF16),
                       jax.ShapeDtypeStruct((2, nblk, nchunks, LANES), F32)],
            compiler_params=_cparams(("arbitrary",)),
            name="ssm_in",
        )(us, perm, w_sin)

    lat_steps = 2 * b
    ut_lat, s_lat = stage_in(us_lat, lat_steps)
    ut_ctx, s_ctx = stage_in(us_ctx, 1)

    cb = 16
    nc_rows, nl_rows = s_ctx.shape[2], s_lat.shape[2]
    h_ctx, h_lat = pl.pallas_call(
        functools.partial(_ssm_scan_kernel, b=b),
        grid=(2, nblk // cb),
        in_specs=[pl.BlockSpec((1, cb, nc_rows, LANES), lambda d, j: (d, j, 0, 0)),
                  pl.BlockSpec((1, cb, nl_rows, LANES), lambda d, j: (d, j, 0, 0)),
                  pl.BlockSpec((1, cb, 1, 2 * P), lambda d, j: (d, j, 0, 0))],
        out_specs=[pl.BlockSpec((1, cb, nc_rows, LANES), lambda d, j: (d, j, 0, 0)),
                   pl.BlockSpec((1, cb, nl_rows, LANES), lambda d, j: (d, j, 0, 0))],
        out_shape=[jax.ShapeDtypeStruct(s_ctx.shape, F32), jax.ShapeDtypeStruct(s_lat.shape, F32)],
        compiler_params=_cparams(("arbitrary", "arbitrary")),
        name="ssm_scan",
    )(s_ctx, s_lat, al)

    def stage_out(ut, hin, steps):
        _, nchunks, _ = ut.shape
        n = nchunks // steps
        noct = ng // OCT
        return pl.pallas_call(
            _ssm_out_kernel,
            grid=(steps,),
            in_specs=[pl.BlockSpec((ng, n, SSM_CW), lambda i: (0, i, 0)),
                      pl.BlockSpec((2, nblk, n, LANES), lambda i: (0, 0, i, 0)),
                      _resident(w_intra.shape), _resident(w_so.shape), _resident(perm.shape)],
            out_specs=pl.BlockSpec((noct, n * SSM_CHUNK, LANES), lambda i: (0, i, 0)),
            out_shape=jax.ShapeDtypeStruct((noct, nchunks * SSM_CHUNK, LANES), F32),
            compiler_params=_cparams(("arbitrary",)),
            name="ssm_out",
        )(ut, hin, w_intra, w_so, perm)

    return stage_out(ut_lat, h_lat, lat_steps), stage_out(ut_ctx, h_ctx, 1)


def _fourier_kernel(ulo_ref, upr_ref, umid_ref, w_ref, cos_ref, sin_ref, alt_ref, b_ref,
                    lo_ref, hi_ref, mid_ref, a_s, b_s, am_s, *, groups):
    gc = ulo_ref.shape[1] // groups

    @pl.when(pl.program_id(1) == 0)
    def _():
        for g in range(groups):
            cols = slice(g * gc, (g + 1) * gc)
            ab_lo = jnp.dot(ulo_ref[:, cols], w_ref[g], preferred_element_type=F32)
            ab_pr = jnp.dot(upr_ref[:, cols], w_ref[g], preferred_element_type=F32)
            a_s[:, cols] = (ab_lo[:, :gc] + ab_pr[:, :gc]).astype(BF16)
            b_s[:, cols] = (ab_lo[:, gc:] - ab_pr[:, gc:]).astype(BF16)
            am_s[:, cols] = jnp.dot(umid_ref[:, cols], w_ref[g, :, 0:gc], preferred_element_type=F32)
        mid = jnp.dot(alt_ref[...], a_s[...], preferred_element_type=F32)
        mid_ref[0] = (mid + am_s[0:mid.shape[0], :] + b_ref[...]).astype(BF16)

    row = lax.broadcasted_iota(jnp.int32, lo_ref.shape, 0)
    a_mid = jnp.where((row & 1) == 0, am_s[0:1, :], -am_s[0:1, :])
    p = jnp.dot(cos_ref[...], a_s[...], preferred_element_type=F32) + a_mid + b_ref[...]
    q = jnp.dot(sin_ref[...], b_s[...], preferred_element_type=F32)
    lo_ref[...] = (p - q).astype(BF16)
    hi_ref[...] = (p + q).astype(BF16)


def _dft_half_tables(n):
    m = math.isqrt(n)
    half = n // 2
    assert m * m == n and half % m == 0
    t = jnp.arange(half, dtype=jnp.int32)[None, :]
    k2 = jnp.arange(half // m, dtype=jnp.int32)[:, None]
    k1 = jnp.arange(m, dtype=jnp.int32)[:, None]
    alpha = ((m * k2 * t) % n).astype(F32) * (2.0 * math.pi / n)
    beta = ((k1 * t) % n).astype(F32) * (2.0 * math.pi / n)
    ca, sa = jnp.cos(alpha)[:, None, :], jnp.sin(alpha)[:, None, :]
    cb, sb = jnp.cos(beta)[None, :, :], jnp.sin(beta)[None, :, :]
    cos = (ca * cb - sa * sb).reshape(half, half)
    sin = (sa * cb + ca * sb).reshape(half, half)
    alt = jnp.broadcast_to(jnp.where((t & 1) == 0, 1.0, -1.0), (8, half))
    return cos.astype(BF16), sin.astype(BF16), alt.astype(BF16)


def _fourier(uf, w_cs, bias, tables, *, b, tm):
    nr, fw = uf.shape
    seq = nr // b
    half = seq // 2
    groups = w_cs.shape[0]
    cos_t, sin_t, alt = tables
    tm = min(tm, half)
    per_b = half // tm
    mid_rows = 16
    u3 = uf.reshape(b, seq, fw)
    upr = jnp.concatenate([jnp.zeros((b, 1, fw), uf.dtype), jnp.flip(u3[:, half + 1:], axis=1)], axis=1)
    lo, hi, mid = pl.pallas_call(
        functools.partial(_fourier_kernel, groups=groups),
        grid=(b, per_b),
        in_specs=[
            pl.BlockSpec((half, fw), lambda bi, i: (2 * bi, 0)),
            pl.BlockSpec((half, fw), lambda bi, i: (bi, 0)),
            pl.BlockSpec((mid_rows, fw), lambda bi, i: ((bi * seq + half) // mid_rows, 0)),
            pl.BlockSpec(w_cs.shape, lambda bi, i: (0, 0, 0)),
            pl.BlockSpec((tm, half), lambda bi, i: (i, 0)),
            pl.BlockSpec((tm, half), lambda bi, i: (i, 0)),
            pl.BlockSpec(alt.shape, lambda bi, i: (0, 0)),
            pl.BlockSpec((1, fw), lambda bi, i: (0, 0)),
        ],
        out_specs=[pl.BlockSpec((tm, fw), lambda bi, i: (bi * per_b + i, 0)),
                   pl.BlockSpec((tm, fw), lambda bi, i: (bi * per_b + i, 0)),
                   pl.BlockSpec((1, alt.shape[0], fw), lambda bi, i: (bi, 0, 0))],
        out_shape=[jax.ShapeDtypeStruct((b * half, fw), BF16), jax.ShapeDtypeStruct((b * half, fw), BF16),
                   jax.ShapeDtypeStruct((b, alt.shape[0], fw), BF16)],
        scratch_shapes=[pltpu.VMEM((half, fw), BF16), pltpu.VMEM((half, fw), BF16),
                        pltpu.VMEM((mid_rows, fw), F32)],
        compiler_params=_cparams(("arbitrary", "arbitrary")),
        name="fourier_%d" % seq,
    )(uf, upr.reshape(b * half, fw), uf, w_cs, cos_t, sin_t, alt, bias)
    out = jnp.concatenate([lo.reshape(b, half, fw), mid[:, 0:1], jnp.flip(hi.reshape(b, half, fw)[:, 1:], axis=1)],
                          axis=1)
    return out.reshape(nr, fw)


def _fourier_weights(w_four, seq):
    groups, gc, _ = w_four.shape
    k = np.arange(gc)
    ang = ((k[:, None] * k[None, :]) % gc).astype(np.float64) * (2.0 * np.pi / gc)
    norm = 1.0 / math.sqrt(seq * gc)
    cc = jnp.asarray(np.cos(ang) * norm, F32)
    sc = jnp.asarray(np.sin(ang) * norm, F32)
    hp = lax.Precision.HIGHEST
    wc = jnp.einsum('ck,gkd->gcd', cc, w_four.astype(F32), precision=hp)
    ws = jnp.einsum('ck,gkd->gcd', sc, w_four.astype(F32), precision=hp)
    return jnp.concatenate([wc, ws], axis=-1).astype(BF16)


def _outproj_kernel(att_ref, us_ref, yc_ref, uf_ref, x_ref, mod_ref, wo_ref, wg_ref, bg_ref, dsk_ref,
                    gpost_ref, gpre_ref, xo_ref, h2_ref, *, aw, sw):
    nblk = us_ref.shape[0]
    tm = x_ref.shape[0]
    for r0 in range(0, tm, tm // 2):
        rows = slice(r0, r0 + tm // 2)
        us = jnp.concatenate([us_ref[j, rows, :] for j in range(nblk)], axis=1)
        yc = jnp.concatenate([yc_ref[j, rows, :] for j in range(nblk)], axis=1)
        g = jax.nn.gelu(dsk_ref[...] * us + yc)
        z = jnp.dot(g.astype(BF16), wg_ref[...], preferred_element_type=F32) + bg_ref[...]
        ssm = (g * jax.nn.sigmoid(z)).astype(BF16)
        mix = (jnp.dot(att_ref[rows, :], wo_ref[0:aw, :], preferred_element_type=F32)
               + jnp.dot(ssm, wo_ref[aw:aw + sw, :], preferred_element_type=F32)
               + jnp.dot(uf_ref[rows, :], wo_ref[aw + sw:, :], preferred_element_type=F32))
        xn = x_ref[rows, :] + mod_ref[0, 2:3, :] * (_rms(mix, NORM_EPS) * gpost_ref[...])
        xo_ref[rows, :] = xn
        h2 = _rms(xn, NORM_EPS) * gpre_ref[...]
        h2_ref[rows, :] = (h2 * (1.0 + mod_ref[0, 4:5, :]) + mod_ref[0, 3:4, :]).astype(BF16)


def _outproj(att, us, yc, four, x, mod, w_out, w_glu, b_glu, dsk, g_post, g_pre, *, layer, mod_row, tm):
    n_rows, d = x.shape
    aw, fw = att.shape[1], four.shape[1]
    nblk = us.shape[0]
    sw = nblk * LANES

    def row(i):
        return (i, 0)

    def const(i):
        return (0, 0)

    blk3 = pl.BlockSpec((nblk, tm, LANES), lambda i: (0, i, 0))
    return pl.pallas_call(
        functools.partial(_outproj_kernel, aw=aw, sw=sw),
        grid=(n_rows // tm,),
        in_specs=[
            pl.BlockSpec((tm, aw), row), blk3, blk3,
            pl.BlockSpec((tm, fw), row), pl.BlockSpec((tm, d), row),
            pl.BlockSpec((1, N_MOD, d), lambda i: (mod_row(i), 0, 0)),
            _resident_layer(w_out.shape, layer), _resident_layer(w_glu.shape, layer),
            pl.BlockSpec((1, sw), const), pl.BlockSpec((1, sw), const),
            pl.BlockSpec((1, d), const), pl.BlockSpec((1, d), const),
        ],
        out_specs=[pl.BlockSpec((tm, d), row), pl.BlockSpec((tm, d), row)],
        out_shape=[jax.ShapeDtypeStruct((n_rows, d), F32), jax.ShapeDtypeStruct((n_rows, d), BF16)],
        compiler_params=_cparams(("arbitrary",)),
        name="outproj",
    )(att, us, yc, four, x, mod, w_out, w_glu, b_glu, dsk, g_post, g_pre)


def _ffn_kernel(h_ref, x_ref, mod_ref, g_ref, wg_ref, wu_ref, wd_ref, o_ref):
    k = pl.program_id(1)

    @pl.when(k == 0)
    def _():
        o_ref[...] = jnp.zeros(o_ref.shape, F32)

    h = h_ref[...]
    tf = wg_ref.shape[1]
    part = None
    for c0 in range(0, tf, tf // 2):
        cols = slice(c0, c0 + tf // 2)
        a = jnp.dot(h, wg_ref[:, cols], preferred_element_type=F32)
        u = jnp.dot(h, wu_ref[:, cols], preferred_element_type=F32)
        p = jnp.dot((_silu(a) * u).astype(BF16), wd_ref[cols, :], preferred_element_type=F32)
        part = p if part is None else part + p
    o_ref[...] += part

    @pl.when(k == pl.num_programs(1) - 1)
    def _():
        o_ref[...] = x_ref[...] + mod_ref[0, 5:6, :] * (_rms(o_ref[...], NORM_EPS) * g_ref[...])


def _ffn(h2, x_mid, mod, g_post, w_gate, w_up, w_down, *, layer, mod_row, tm, tf):
    n_rows, d = h2.shape
    dff = w_gate.shape[2]
    return pl.pallas_call(
        _ffn_kernel,
        grid=(n_rows // tm, dff // tf),
        in_specs=[
            pl.BlockSpec((tm, d), lambda i, k: (i, 0)),
            pl.BlockSpec((tm, d), lambda i, k: (i, 0)),
            pl.BlockSpec((1, N_MOD, d), lambda i, k: (mod_row(i), 0, 0)),
            pl.BlockSpec((1, d), lambda i, k: (0, 0)),
            pl.BlockSpec((None, d, tf), lambda i, k: (layer, 0, k)),
            pl.BlockSpec((None, d, tf), lambda i, k: (layer, 0, k)),
            pl.BlockSpec((None, tf, d), lambda i, k: (layer, k, 0)),
        ],
        out_specs=pl.BlockSpec((tm, d), lambda i, k: (i, 0)),
        out_shape=jax.ShapeDtypeStruct((n_rows, d), F32),
        compiler_params=_cparams(("arbitrary", "arbitrary")),
        name="ffn",
    )(h2, x_mid, mod, g_post, w_gate, w_up, w_down)


def kernel(x, c, ctx, c_ctx, w_mod, b_mod, g_mix_pre, g_mix_post, g_ffn_pre, g_ffn_post, w_in, w_out, lam_q1, lam_k1, lam_q2, lam_k2, g_subln, ssm_a_re, ssm_a_im, ssm_log_dt, ssm_b_re, ssm_b_im, ssm_c_re, ssm_c_im, ssm_d, w_glu, b_glu, w_four, b_four, w_gate, w_up, w_down):
    b, t, d = x.shape
    n_ctx = ctx.shape[1]
    depth = w_mod.shape[0]
    nl, nc = b * t, b * n_ctx
    aw = d // 2
    sw = ssm_d.shape[1]
    fw = d - aw - sw
    heads = aw // (2 * DA_HEAD_DIM)
    tm = 512
    assert t % tm == 0 and nc % tm == 0 and n_ctx % SSM_CHUNK == 0 and t % GRID_W == 0 and b + 1 <= MOD_ROWS
    assert sw % LANES == 0 and (sw // SSM_GROUP) % (2 * OCT) == 0

    cs = jnp.concatenate([c, c_ctx[None, :], jnp.zeros((MOD_ROWS - b - 1, d), F32)], axis=0)
    mod_all = _modulation(cs, w_mod, b_mod).reshape(depth, MOD_ROWS, N_MOD, d)

    per_b = t // tm
    lat_mod = lambda i: i // per_b
    ctx_mod = lambda i: b
    tabs = _rope_tables(t, tm)
    dft_lat = _dft_half_tables(t)
    dft_ctx = _dft_half_tables(n_ctx)
    perm = _atom_transpose_matrix()
    x_lat, x_ctx = x.reshape(nl, d), ctx.reshape(nc, d)
    w_in_b, w_out_b, w_glu_b = w_in.astype(BF16), w_out.astype(BF16), w_glu.astype(BF16)
    w_vt_b = jnp.transpose(w_in[:, :, 2 * aw:3 * aw], (0, 2, 1)).astype(BF16)
    w_gate_b, w_up_b, w_down_b = w_gate.astype(BF16), w_up.astype(BF16), w_down.astype(BF16)

    for l in range(depth):
        need_ctx = l < depth - 1
        lam_init = 0.8 - 0.6 * math.exp(-0.3 * l)
        mod = mod_all[l]
        g_pre = g_mix_pre[l][None, :]
        q, k, vt, us, uf = _inproj(x_lat, mod, g_pre, w_in_b, w_vt_b, tabs, layer=l, mod_row=lat_mod,
                                   tab_blk=lambda i: i % per_b, aw=aw, sw=sw, fw=fw, tm=tm)
        qc, kc, vtc, usc, ufc = _inproj(x_ctx, mod, g_pre, w_in_b, w_vt_b, tabs, layer=l, mod_row=ctx_mod,
                                        tab_blk=lambda i: per_b, aw=aw, sw=sw, fw=fw, tm=tm)

        lam4 = jnp.stack([lam_q1[l], lam_k1[l], lam_q2[l], lam_k2[l]]).astype(F32)
        gs = g_subln[l][None, :].astype(F32)
        att = _attention(lam4, gs, q, [(kc, vtc), (k, vt)], lam_init=lam_init, b=b, heads=heads, tq=1024,
                         n_split=1, name="attn_latent")
        ssm_w = _ssm_weights(ssm_a_re[l], ssm_a_im[l], ssm_log_dt[l], ssm_b_re[l], ssm_b_im[l],
                             ssm_c_re[l], ssm_c_im[l])
        yc, ycc = _ssm_conv(us, usc, ssm_w, perm, b=b)
        bias = b_four[l].reshape(1, fw).astype(F32)
        four = _fourier(uf, _fourier_weights(w_four[l], t), bias, dft_lat, b=b, tm=tm)

        small = (b_glu[l][None, :].astype(F32), ssm_d[l][None, :].astype(F32), g_mix_post[l][None, :],
                 g_ffn_pre[l][None, :])
        ffn_w = (g_ffn_post[l][None, :], w_gate_b, w_up_b, w_down_b)
        x_mid, h2 = _outproj(att, us, yc, four, x_lat, mod, w_out_b, w_glu_b, *small, layer=l, mod_row=lat_mod,
                             tm=tm)
        x_lat = _ffn(h2, x_mid, mod, *ffn_w, layer=l, mod_row=lat_mod, tm=tm, tf=512)

        if need_ctx:
            att_c = _attention(lam4, gs, qc, [(kc, vtc)], lam_init=lam_init, b=b, heads=heads, tq=n_ctx,
                               n_split=1, name="attn_ctx")
            four_c = _fourier(ufc, _fourier_weights(w_four[l], n_ctx), bias, dft_ctx, b=b, tm=tm)
            xc_mid, h2c = _outproj(att_c, usc, ycc, four_c, x_ctx, mod, w_out_b, w_glu_b, *small, layer=l,
                                   mod_row=ctx_mod, tm=tm)
            x_ctx = _ffn(h2c, xc_mid, mod, *ffn_w, layer=l, mod_row=ctx_mod, tm=tm, tf=512)
    return x_lat.reshape(b, t, d)
```

```python
import functools
import math

import jax
import jax.numpy as jnp
import numpy as np
from jax import lax
from jax.experimental import pallas as pl
from jax.experimental.pallas import tpu as pltpu

F32 = jnp.float32
BF16 = jnp.bfloat16

LANES = 128
GRID_W = 64
DA_HEAD_DIM = 128
SSM_GROUP = 16
SSM_STATE = 64
N_MOD = 6
ROPE_BASE = 10000.0
ROPE_PAIRS = DA_HEAD_DIM // 4
NORM_EPS = 1e-6
SUBLN_EPS = 1e-5

SSM_CHUNK = 16
SSM_CW = SSM_CHUNK * SSM_GROUP
OCT = LANES // SSM_GROUP
MOD_ROWS = 8
VMEM_LIMIT = 56 * 1024 * 1024
NT_DIMS = (((1,), (1,)), ((), ()))


def _cparams(sem):
    return pltpu.CompilerParams(dimension_semantics=sem, vmem_limit_bytes=VMEM_LIMIT)


def _rms(x, eps):
    return x * lax.rsqrt(jnp.mean(x * x, axis=-1, keepdims=True) + eps)


def _silu(x):
    return x * jax.nn.sigmoid(x)


def _resident(shape):
    nd = len(shape)
    return pl.BlockSpec(shape, lambda *_: (0,) * nd, pipeline_mode=pl.Buffered(1))


def _resident_layer(shape, layer):
    nd = len(shape)
    return pl.BlockSpec((None,) + tuple(shape[1:]), lambda *_: (layer,) + (0,) * (nd - 1),
                        pipeline_mode=pl.Buffered(1))


def _mod_kernel(c_ref, w_ref, b_ref, o_ref):
    s = _silu(c_ref[...]).astype(BF16)
    o_ref[0] = jnp.dot(s, w_ref[0].astype(BF16), preferred_element_type=F32) + b_ref[0]


def _modulation(cs, w_mod, b_mod, tn=1024):
    depth, d, n = w_mod.shape
    return pl.pallas_call(
        _mod_kernel,
        grid=(depth, n // tn),
        in_specs=[
            pl.BlockSpec((MOD_ROWS, d), lambda l, j: (0, 0)),
            pl.BlockSpec((1, d, tn), lambda l, j: (l, 0, j)),
            pl.BlockSpec((1, 1, tn), lambda l, j: (l, 0, j)),
        ],
        out_specs=pl.BlockSpec((1, MOD_ROWS, tn), lambda l, j: (l, 0, j)),
        out_shape=jax.ShapeDtypeStruct((depth, MOD_ROWS, n), F32),
        compiler_params=_cparams(("arbitrary", "arbitrary")),
        name="modulation",
    )(cs, w_mod, b_mod.reshape(depth, 1, n))


def _inproj_kernel(x_ref, mod_ref, g_ref, w_ref, wvt_ref, cq_ref, sq_ref, ck_ref, sk_ref,
                   q_ref, k_ref, vt_ref, us_ref, uf_ref, *, aw, sw):
    tm = x_ref.shape[0]
    h = _rms(x_ref[...], NORM_EPS) * g_ref[...]
    h = h * (1.0 + mod_ref[0, 1:2, :]) + mod_ref[0, 0:1, :]
    hb = h.astype(BF16)

    lane = lax.broadcasted_iota(jnp.int32, (tm, DA_HEAD_DIM), 1)
    first_half = (lane & (2 * ROPE_PAIRS - 1)) < ROPE_PAIRS

    def rope(z, c, s):
        partner = jnp.where(first_half, pltpu.roll(z, DA_HEAD_DIM - ROPE_PAIRS, 1), pltpu.roll(z, ROPE_PAIRS, 1))
        return z * c + partner * s

    zq = jnp.dot(hb, w_ref[:, 0:aw], preferred_element_type=F32)
    cq, sq = cq_ref[...], sq_ref[...]
    for j in range(aw // DA_HEAD_DIM):
        sl = slice(j * DA_HEAD_DIM, (j + 1) * DA_HEAD_DIM)
        q_ref[:, sl] = rope(zq[:, sl], cq, sq).astype(BF16)
    zk = jnp.dot(hb, w_ref[:, aw:2 * aw], preferred_element_type=F32)
    ck, sk = ck_ref[...], sk_ref[...]
    for j in range(aw // DA_HEAD_DIM):
        sl = slice(j * DA_HEAD_DIM, (j + 1) * DA_HEAD_DIM)
        k_ref[:, sl] = rope(zk[:, sl], ck, sk).astype(BF16)
    vt_ref[...] = lax.dot_general(wvt_ref[...], hb, NT_DIMS, preferred_element_type=F32).astype(BF16)
    zs = jnp.dot(hb, w_ref[:, 3 * aw:3 * aw + sw], preferred_element_type=F32)
    for j in range(sw // LANES):
        us_ref[j] = zs[:, j * LANES:(j + 1) * LANES]
    uf_ref[...] = jnp.dot(hb, w_ref[:, 3 * aw + sw:], preferred_element_type=F32).astype(BF16)


def _inproj(x, mod, g, w_in, w_vt, tabs, *, layer, mod_row, tab_blk, aw, sw, fw, tm):
    nr, d = x.shape

    def row(i):
        return (i, 0)

    tab_spec = pl.BlockSpec((tm, DA_HEAD_DIM), lambda i: (tab_blk(i), 0))
    return pl.pallas_call(
        functools.partial(_inproj_kernel, aw=aw, sw=sw),
        grid=(nr // tm,),
        in_specs=[
            pl.BlockSpec((tm, d), row),
            pl.BlockSpec((1, N_MOD, d), lambda i: (mod_row(i), 0, 0)),
            pl.BlockSpec((1, d), lambda i: (0, 0)),
            _resident_layer(w_in.shape, layer), _resident_layer(w_vt.shape, layer),
            tab_spec, tab_spec, tab_spec, tab_spec,
        ],
        out_specs=[
            pl.BlockSpec((tm, aw), row), pl.BlockSpec((tm, aw), row), pl.BlockSpec((aw, tm), lambda i: (0, i)),
            pl.BlockSpec((sw // LANES, tm, LANES), lambda i: (0, i, 0)), pl.BlockSpec((tm, fw), row),
        ],
        out_shape=[
            jax.ShapeDtypeStruct((nr, aw), BF16), jax.ShapeDtypeStruct((nr, aw), BF16),
            jax.ShapeDtypeStruct((aw, nr), BF16), jax.ShapeDtypeStruct((sw // LANES, nr, LANES), F32),
            jax.ShapeDtypeStruct((nr, fw), BF16),
        ],
        compiler_params=_cparams(("arbitrary",)),
        name="inproj",
    )(x, mod, g, w_in, w_vt, *tabs)


def _rope_tables(t, tm):
    rows = t // GRID_W
    r = jnp.broadcast_to(jnp.arange(rows, dtype=F32)[:, None], (rows, GRID_W)).reshape(-1)
    col = jnp.broadcast_to(jnp.arange(GRID_W, dtype=F32)[None, :], (rows, GRID_W)).reshape(-1)
    inv = ROPE_BASE ** (-jnp.arange(ROPE_PAIRS, dtype=F32) / ROPE_PAIRS)
    ar, ac = r[:, None] * inv, col[:, None] * inv
    cos = jnp.concatenate([jnp.cos(ar), jnp.cos(ar), jnp.cos(ac), jnp.cos(ac)], axis=1)
    sin = jnp.concatenate([-jnp.sin(ar), jnp.sin(ar), -jnp.sin(ac), jnp.sin(ac)], axis=1)
    cos = jnp.concatenate([cos, jnp.ones((tm, DA_HEAD_DIM), F32)], axis=0)
    sin = jnp.concatenate([sin, jnp.zeros((tm, DA_HEAD_DIM), F32)], axis=0)
    scale = DA_HEAD_DIM ** -0.5 * math.log2(math.e)
    return cos * scale, sin * scale, cos, sin


def _attn_kernel(lam_ref, gs_ref, q_ref, *refs, lam_init, n_split):
    o_ref = refs[-1]
    k_refs, vt_refs = refs[0:-1:2], refs[1:-1:2]
    hd = DA_HEAD_DIM
    s1 = jnp.sum(lam_ref[0:1, :] * lam_ref[1:2, :], axis=-1, keepdims=True)
    s2 = jnp.sum(lam_ref[2:3, :] * lam_ref[3:4, :], axis=-1, keepdims=True)
    lam = jnp.exp(s1) - jnp.exp(s2) + lam_init
    tq = q_ref.shape[0]
    for r0 in range(0, tq, tq // n_split):
        rows = slice(r0, r0 + tq // n_split)
        outs = []
        for idx in range(2):
            qi = q_ref[rows, idx * hd:(idx + 1) * hd]
            ss = [lax.dot_general(kr[:, idx * hd:(idx + 1) * hd], qi, NT_DIMS, preferred_element_type=F32)
                  for kr in k_refs]
            m = functools.reduce(jnp.maximum, [jnp.max(s, axis=0, keepdims=True) for s in ss])
            ps = [jnp.exp2(s - m) for s in ss]
            l = functools.reduce(jnp.add, [jnp.sum(p, axis=0, keepdims=True) for p in ps])
            acc = functools.reduce(jnp.add, [jnp.dot(vr[...], p.astype(BF16), preferred_element_type=F32)
                                             for vr, p in zip(vt_refs, ps)])
            outs.append(acc * (1.0 / l))
        o = jnp.transpose(outs[0] - outs[1] * lam)
        o_ref[rows, :] = (_rms(o, SUBLN_EPS) * gs_ref[...] * (1.0 - lam_init)).astype(BF16)


def _attention(lam4, gs, q, kv_sets, *, lam_init, b, heads, tq, n_split, name):
    nq, aw = q.shape
    vw = 2 * DA_HEAD_DIM
    per_b = nq // b // tq
    q_spec = pl.BlockSpec((tq, vw), lambda bi, h, i: (bi * per_b + i, h))
    in_specs = [pl.BlockSpec((4, DA_HEAD_DIM), lambda bi, h, i: (0, 0)),
                pl.BlockSpec((1, vw), lambda bi, h, i: (0, 0)),
                q_spec]
    args = [lam4, gs, q]
    for k, vt in kv_sets:
        sk = k.shape[0] // b
        in_specs += [pl.BlockSpec((sk, vw), lambda bi, h, i: (bi, h)),
                     pl.BlockSpec((vw, sk), lambda bi, h, i: (h, bi))]
        args += [k, vt]
    return pl.pallas_call(
        functools.partial(_attn_kernel, lam_init=lam_init, n_split=n_split),
        grid=(b, heads, per_b),
        in_specs=in_specs,
        out_specs=q_spec,
        out_shape=jax.ShapeDtypeStruct((nq, aw), BF16),
        compiler_params=_cparams(("arbitrary", "arbitrary", "arbitrary")),
        name=name,
    )(*args)


def _ssm_weights(a_re, a_im, log_dt, b_re, b_im, c_re, c_im):
    L, H, P = SSM_CHUNK, SSM_GROUP, SSM_STATE
    g = a_re.shape[1]
    npair = g // 2
    hp = lax.Precision.HIGHEST
    ar, ai = a_re.astype(F32), a_im.astype(F32)
    dt = jnp.exp(log_dt.astype(F32))[..., None]
    n = jnp.arange(L + 1, dtype=F32)
    mag = jnp.exp((ar * dt)[..., None] * n)
    ang = (ai * dt)[..., None] * n
    apr, api = mag * jnp.cos(ang), mag * jnp.sin(ang)
    xr, xi = apr[..., 1] - 1.0, api[..., 1]
    den = ar * ar + ai * ai
    qr, qi = (xr * ar + xi * ai) / den, (xi * ar - xr * ai) / den
    br, bi = b_re.astype(F32), b_im.astype(F32)
    bbr = qr[..., None] * br - qi[..., None] * bi
    bbi = qr[..., None] * bi + qi[..., None] * br
    cr, ci = c_re.astype(F32), c_im.astype(F32)

    car = cr[..., None] * apr[:, :, None, :, :L] - ci[..., None] * api[:, :, None, :, :L]
    cai = cr[..., None] * api[:, :, None, :, :L] + ci[..., None] * apr[:, :, None, :, :L]
    kt = (jnp.einsum('dghpt,dgpk->dgkth', car, bbr, precision=hp)
          - jnp.einsum('dghpt,dgpk->dgkth', cai, bbi, precision=hp))
    kc = jnp.concatenate([jnp.flip(kt[1][:, :, 1:], axis=2), kt[0][:, :, :1] + kt[1][:, :, :1], kt[0][:, :, 1:]],
                         axis=2).reshape(g, H, (2 * L - 1) * H)
    w_intra = jnp.stack([kc[:, :, (L - 1 - s) * H:(2 * L - 1 - s) * H] for s in range(L)], axis=1)
    w_intra = w_intra.reshape(g, L * H, L * H)

    sir = jnp.stack([jnp.flip(apr[0, ..., :L], axis=-1), apr[1, ..., :L]])
    sii = jnp.stack([jnp.flip(api[0, ..., :L], axis=-1), api[1, ..., :L]])
    sir, sii = jnp.swapaxes(sir, 2, 3)[:, :, :, None, :], jnp.swapaxes(sii, 2, 3)[:, :, :, None, :]
    tbr, tbi = jnp.swapaxes(bbr, 2, 3)[:, :, None], jnp.swapaxes(bbi, 2, 3)[:, :, None]
    w_sin = jnp.concatenate([sir * tbr - sii * tbi, sir * tbi + sii * tbr], axis=-1)
    w_sin = w_sin.reshape(2, g, L * H, 2 * P)

    sor = jnp.stack([apr[0, ..., 1:], jnp.flip(apr[1, ..., 1:], axis=-1)])
    soi = jnp.stack([api[0, ..., 1:], jnp.flip(api[1, ..., 1:], axis=-1)])
    tcr, tci = jnp.swapaxes(cr, 2, 3)[:, :, :, None, :], jnp.swapaxes(ci, 2, 3)[:, :, :, None, :]
    cnr = (tcr * sor[..., None] - tci * soi[..., None]).reshape(2, g, P, L * H)
    cni = (tcr * soi[..., None] + tci * sor[..., None]).reshape(2, g, P, L * H)
    w_so = jnp.concatenate([cnr, -cni], axis=2)

    al = jnp.stack([apr[..., L].reshape(2, npair, 2 * P), api[..., L].reshape(2, npair, 2 * P)], axis=2)
    al = al.reshape(2, 2 * npair, 1, 2 * P)
    return w_intra.astype(BF16), w_sin.astype(BF16), w_so.astype(BF16), al


def _atom_transpose_matrix():
    n = OCT * OCT * SSM_GROUP
    i = np.arange(n)
    j = ((i // SSM_GROUP) % OCT) * LANES + (i // LANES) * SSM_GROUP + i % SSM_GROUP
    p = np.zeros((n, n), np.float32)
    p[i, j] = 1.0
    return jnp.asarray(p, BF16)


def _ssm_in_kernel(us_ref, p_ref, w_ref, ut_ref, s_ref):
    n = ut_ref.shape[1]
    noct = us_ref.shape[0]
    half = SSM_CHUNK // 2
    src = []
    for o in range(noct):
        for hf in range(2):
            rows = [us_ref[o, pl.ds(half * hf + sl, n, stride=SSM_CHUNK), :] for sl in range(half)]
            src.append(jnp.concatenate(rows, axis=1).astype(BF16))
    perm = jnp.dot(jnp.concatenate(src, axis=0), p_ref[...], preferred_element_type=F32).astype(BF16)

    def group_chunk(gi):
        o, gl = gi // OCT, gi % OCT
        return jnp.concatenate([perm[(2 * o + hf) * n:(2 * o + hf + 1) * n, gl * LANES:(gl + 1) * LANES]
                                for hf in range(2)], axis=1)

    low = lax.broadcasted_iota(jnp.int32, (n, LANES), 1) < SSM_STATE
    for j in range(noct * OCT // 2):
        ug = [group_chunk(2 * j), group_chunk(2 * j + 1)]
        ut_ref[2 * j] = ug[0]
        ut_ref[2 * j + 1] = ug[1]
        for d in range(2):
            s0 = jnp.dot(ug[0], w_ref[d, 2 * j], preferred_element_type=F32)
            s1 = jnp.dot(ug[1], w_ref[d, 2 * j + 1], preferred_element_type=F32)
            s_ref[d, 2 * j] = jnp.where(low, s0, pltpu.roll(s1, SSM_STATE, 1))
            s_ref[d, 2 * j + 1] = jnp.where(low, pltpu.roll(s0, SSM_STATE, 1), s1)


def _ssm_scan_kernel(sc_ref, sl_ref, al_ref, hc_ref, hl_ref, *, b):
    d = pl.program_id(0)
    nblk = sc_ref.shape[1]
    ars = [al_ref[0, 2 * k] for k in range(nblk // 2)]
    ais = [al_ref[0, 2 * k + 1] for k in range(nblk // 2)]

    def phase(s_ref, h_ref, carry):
        nch = s_ref.shape[2] // b

        def body(i, carry):
            c = jnp.where(d == 0, i, nch - 1 - i)
            rows = pl.ds(c, b, stride=nch)
            new = []
            for k in range(nblk // 2):
                hr, hi = carry[2 * k], carry[2 * k + 1]
                h_ref[0, 2 * k, rows, :] = hr
                h_ref[0, 2 * k + 1, rows, :] = hi
                sr = s_ref[0, 2 * k, rows, :]
                si = s_ref[0, 2 * k + 1, rows, :]
                new.append(ars[k] * hr - ais[k] * hi + sr)
                new.append(ars[k] * hi + ais[k] * hr + si)
            return tuple(new)

        return lax.fori_loop(0, nch, body, carry, unroll=4)

    carry = tuple(jnp.zeros((b, LANES), F32) for _ in range(nblk))
    carry = phase(sc_ref, hc_ref, carry)
    phase(sl_ref, hl_ref, carry)


def _ssm_out_kernel(ut_ref, h_ref, wi_ref, wo_ref, p_ref, y_ref):
    n = ut_ref.shape[1]
    ng = ut_ref.shape[0]
    noct = ng // OCT
    half = SSM_CHUNK // 2
    low = lax.broadcasted_iota(jnp.int32, (n, LANES), 1) < SSM_STATE
    ys = []
    for j in range(ng // 2):
        hg = [[], []]
        for d in range(2):
            hr, hi = h_ref[d, 2 * j], h_ref[d, 2 * j + 1]
            hg[0].append(jnp.where(low, hr, pltpu.roll(hi, SSM_STATE, 1)).astype(BF16))
            hg[1].append(jnp.where(low, pltpu.roll(hr, SSM_STATE, 1), hi).astype(BF16))
        for e in range(2):
            gi = 2 * j + e
            ys.append(jnp.dot(ut_ref[gi], wi_ref[gi], preferred_element_type=F32)
                      + jnp.dot(hg[e][0], wo_ref[0, gi], preferred_element_type=F32)
                      + jnp.dot(hg[e][1], wo_ref[1, gi], preferred_element_type=F32))
    rows = []
    for o in range(noct):
        for hf in range(2):
            rows.append(jnp.concatenate([ys[OCT * o + gl][:, hf * LANES:(hf + 1) * LANES] for gl in range(OCT)],
                                        axis=1))
    ycat = jnp.concatenate(rows, axis=0)
    hi = ycat.astype(BF16)
    lo = (ycat - hi.astype(F32)).astype(BF16)
    res = (jnp.dot(hi, p_ref[...], preferred_element_type=F32)
           + jnp.dot(lo, p_ref[...], preferred_element_type=F32))
    for o in range(noct):
        for hf in range(2):
            blk = 2 * o + hf
            for tl in range(half):
                y_ref[o, pl.ds(half * hf + tl, n, stride=SSM_CHUNK), :] = (
                    res[blk * n:(blk + 1) * n, tl * LANES:(tl + 1) * LANES])


def _ssm_conv(us_lat, us_ctx, weights, perm, *, b):
    w_intra, w_sin, w_so, al = weights
    P = SSM_STATE
    ng = w_intra.shape[0]
    nblk = ng

    def stage_in(us, steps):
        noct, rows, _ = us.shape
        nchunks = rows // SSM_CHUNK
        n = nchunks // steps
        return pl.pallas_call(
            _ssm_in_kernel,
            grid=(steps,),
            in_specs=[pl.BlockSpec((noct, n * SSM_CHUNK, LANES), lambda i: (0, i, 0)),
                      _resident(perm.shape), _resident(w_sin.shape)],
            out_specs=[pl.BlockSpec((ng, n, SSM_CW), lambda i: (0, i, 0)),
                       pl.BlockSpec((2, nblk, n, LANES), lambda i: (0, 0, i, 0))],
            out_shape=[jax.ShapeDtypeStruct((ng, nchunks, SSM_CW), BF16),
                       jax.ShapeDtypeStruct((2, nblk, nchunks, LANES), F32)],
            compiler_params=_cparams(("arbitrary",)),
            name="ssm_in",
        )(us, perm, w_sin)

    lat_steps = 2 * b
    ut_lat, s_lat = stage_in(us_lat, lat_steps)
    ut_ctx, s_ctx = stage_in(us_ctx, 1)

    cb = 16
    nc_rows, nl_rows = s_ctx.shape[2], s_lat.shape[2]
    h_ctx, h_lat = pl.pallas_call(
        functools.partial(_ssm_scan_kernel, b=b),
        grid=(2, nblk // cb),
        in_specs=[pl.BlockSpec((1, cb, nc_rows, LANES), lambda d, j: (d, j, 0, 0)),
                  pl.BlockSpec((1, cb, nl_rows, LANES), lambda d, j: (d, j, 0, 0)),
                  pl.BlockSpec((1, cb, 1, 2 * P), lambda d, j: (d, j, 0, 0))],
        out_specs=[pl.BlockSpec((1, cb, nc_rows, LANES), lambda d, j: (d, j, 0, 0)),
                   pl.BlockSpec((1, cb, nl_rows, LANES), lambda d, j: (d, j, 0, 0))],
        out_shape=[jax.ShapeDtypeStruct(s_ctx.shape, F32), jax.ShapeDtypeStruct(s_lat.shape, F32)],
        compiler_params=_cparams(("arbitrary", "arbitrary")),
        name="ssm_scan",
    )(s_ctx, s_lat, al)

    def stage_out(ut, hin, steps):
        _, nchunks, _ = ut.shape
        n = nchunks // steps
        noct = ng // OCT
        return pl.pallas_call(
            _ssm_out_kernel,
            grid=(steps,),
            in_specs=[pl.BlockSpec((ng, n, SSM_CW), lambda i: (0, i, 0)),
                      pl.BlockSpec((2, nblk, n, LANES), lambda i: (0, 0, i, 0)),
                      _resident(w_intra.shape), _resident(w_so.shape), _resident(perm.shape)],
            out_specs=pl.BlockSpec((noct, n * SSM_CHUNK, LANES), lambda i: (0, i, 0)),
            out_shape=jax.ShapeDtypeStruct((noct, nchunks * SSM_CHUNK, LANES), F32),
            compiler_params=_cparams(("arbitrary",)),
            name="ssm_out",
        )(ut, hin, w_intra, w_so, perm)

    return stage_out(ut_lat, h_lat, lat_steps), stage_out(ut_ctx, h_ctx, 1)


def _fourier_kernel(ulo_ref, uup_ref, w_ref, cos_ref, sin_ref, alt_ref, s1_ref, b_ref,
                    lo_ref, up_ref, a_s, b_s, am_s, carry, *, groups):
    gc = ulo_ref.shape[1] // groups
    tm = lo_ref.shape[0]
    nt = a_s.shape[0] // tm
    row = lax.broadcasted_iota(jnp.int32, lo_ref.shape, 0)

    @pl.when(pl.program_id(1) == 0)
    def _():
        for jj in range(nt):
            pr = jnp.dot(s1_ref[...], uup_ref[(nt - 1 - jj) * tm:(nt - jj) * tm, :], preferred_element_type=F32)
            if jj > 0:
                first = uup_ref[(nt - jj) * tm:(nt - jj) * tm + am_s.shape[0], :][0:1, :].astype(F32)
                pr = jnp.where(row == 0, first, pr)
            pr = pr.astype(BF16)
            rows = slice(jj * tm, (jj + 1) * tm)
            for g in range(groups):
                cols = slice(g * gc, (g + 1) * gc)
                ab_lo = jnp.dot(ulo_ref[rows, cols], w_ref[g], preferred_element_type=F32)
                ab_pr = jnp.dot(pr[:, cols], w_ref[g], preferred_element_type=F32)
                a_s[rows, cols] = (ab_lo[:, :gc] + ab_pr[:, :gc]).astype(BF16)
                b_s[rows, cols] = (ab_lo[:, gc:] - ab_pr[:, gc:]).astype(BF16)
        for g in range(groups):
            cols = slice(g * gc, (g + 1) * gc)
            am_s[:, cols] = jnp.dot(uup_ref[0:am_s.shape[0], cols], w_ref[g, :, 0:gc], preferred_element_type=F32)
        mid = jnp.dot(alt_ref[...], a_s[...], preferred_element_type=F32)
        carry[...] = mid + am_s[0:carry.shape[0], :] + b_ref[...]

    a_mid = jnp.where((row & 1) == 0, am_s[0:1, :], -am_s[0:1, :])
    p = jnp.dot(cos_ref[...], a_s[...], preferred_element_type=F32) + a_mid + b_ref[...]
    q = jnp.dot(sin_ref[...], b_s[...], preferred_element_type=F32)
    lo_ref[...] = (p - q).astype(BF16)
    hi = (p + q).astype(BF16)
    rev = jnp.dot(s1_ref[...], hi, preferred_element_type=F32)
    up_ref[...] = jnp.where(row == 0, carry[0:1, :], rev).astype(BF16)
    carry[0:1, :] = hi[0:1, :].astype(F32)


def _dft_half_tables(n):
    m = math.isqrt(n)
    half = n // 2
    assert m * m == n and half % m == 0
    t = jnp.arange(half, dtype=jnp.int32)[None, :]
    k2 = jnp.arange(half // m, dtype=jnp.int32)[:, None]
    k1 = jnp.arange(m, dtype=jnp.int32)[:, None]
    alpha = ((m * k2 * t) % n).astype(F32) * (2.0 * math.pi / n)
    beta = ((k1 * t) % n).astype(F32) * (2.0 * math.pi / n)
    ca, sa = jnp.cos(alpha)[:, None, :], jnp.sin(alpha)[:, None, :]
    cb, sb = jnp.cos(beta)[None, :, :], jnp.sin(beta)[None, :, :]
    cos = (ca * cb - sa * sb).reshape(half, half)
    sin = (sa * cb + ca * sb).reshape(half, half)
    alt = jnp.broadcast_to(jnp.where((t & 1) == 0, 1.0, -1.0), (8, half))
    return cos.astype(BF16), sin.astype(BF16), alt.astype(BF16)


def _fourier(uf, w_cs, bias, tables, *, b, tm):
    nr, fw = uf.shape
    seq = nr // b
    half = seq // 2
    groups = w_cs.shape[0]
    cos_t, sin_t, alt = tables
    tm = min(tm, half)
    nt = half // tm
    r = np.arange(1, tm)
    s1 = np.zeros((tm, tm), np.float32)
    s1[r, tm - r] = 1.0
    s1 = jnp.asarray(s1, BF16)
    lo, up = pl.pallas_call(
        functools.partial(_fourier_kernel, groups=groups),
        grid=(b, nt),
        in_specs=[
            pl.BlockSpec((half, fw), lambda bi, i: (2 * bi, 0)),
            pl.BlockSpec((half, fw), lambda bi, i: (2 * bi + 1, 0)),
            pl.BlockSpec(w_cs.shape, lambda bi, i: (0, 0, 0)),
            pl.BlockSpec((tm, half), lambda bi, i: (nt - 1 - i, 0)),
            pl.BlockSpec((tm, half), lambda bi, i: (nt - 1 - i, 0)),
            pl.BlockSpec(alt.shape, lambda bi, i: (0, 0)),
            pl.BlockSpec((tm, tm), lambda bi, i: (0, 0)),
            pl.BlockSpec((1, fw), lambda bi, i: (0, 0)),
        ],
        out_specs=[pl.BlockSpec((tm, fw), lambda bi, i: (bi * nt + nt - 1 - i, 0)),
                   pl.BlockSpec((tm, fw), lambda bi, i: (bi * nt + i, 0))],
        out_shape=[jax.ShapeDtypeStruct((b * half, fw), BF16), jax.ShapeDtypeStruct((b * half, fw), BF16)],
        scratch_shapes=[pltpu.VMEM((half, fw), BF16), pltpu.VMEM((half, fw), BF16),
                        pltpu.VMEM((16, fw), F32), pltpu.VMEM((alt.shape[0], fw), F32)],
        compiler_params=_cparams(("arbitrary", "arbitrary")),
        name="fourier_%d" % seq,
    )(uf, uf, w_cs, cos_t, sin_t, alt, s1, bias)
    return jnp.concatenate([lo.reshape(b, half, fw), up.reshape(b, half, fw)], axis=1).reshape(nr, fw)


def _fourier_weights(w_four, seq):
    groups, gc, _ = w_four.shape
    k = np.arange(gc)
    ang = ((k[:, None] * k[None, :]) % gc).astype(np.float64) * (2.0 * np.pi / gc)
    norm = 1.0 / math.sqrt(seq * gc)
    cc = jnp.asarray(np.cos(ang) * norm, F32)
    sc = jnp.asarray(np.sin(ang) * norm, F32)
    hp = lax.Precision.HIGHEST
    wc = jnp.einsum('ck,gkd->gcd', cc, w_four.astype(F32), precision=hp)
    ws = jnp.einsum('ck,gkd->gcd', sc, w_four.astype(F32), precision=hp)
    return jnp.concatenate([wc, ws], axis=-1).astype(BF16)


def _outproj_kernel(att_ref, us_ref, yc_ref, uf_ref, x_ref, mod_ref, wo_ref, wg_ref, bg_ref, dsk_ref,
                    gpost_ref, gpre_ref, xo_ref, h2_ref, *, aw, sw):
    nblk = us_ref.shape[0]
    tm = x_ref.shape[0]
    for r0 in range(0, tm, tm // 2):
        rows = slice(r0, r0 + tm // 2)
        us = jnp.concatenate([us_ref[j, rows, :] for j in range(nblk)], axis=1)
        yc = jnp.concatenate([yc_ref[j, rows, :] for j in range(nblk)], axis=1)
        g = jax.nn.gelu(dsk_ref[...] * us + yc)
        z = jnp.dot(g.astype(BF16), wg_ref[...], preferred_element_type=F32) + bg_ref[...]
        ssm = (g * jax.nn.sigmoid(z)).astype(BF16)
        mix = (jnp.dot(att_ref[rows, :], wo_ref[0:aw, :], preferred_element_type=F32)
               + jnp.dot(ssm, wo_ref[aw:aw + sw, :], preferred_element_type=F32)
               + jnp.dot(uf_ref[rows, :], wo_ref[aw + sw:, :], preferred_element_type=F32))
        xn = x_ref[rows, :] + mod_ref[0, 2:3, :] * (_rms(mix, NORM_EPS) * gpost_ref[...])
        xo_ref[rows, :] = xn
        h2 = _rms(xn, NORM_EPS) * gpre_ref[...]
        h2_ref[rows, :] = (h2 * (1.0 + mod_ref[0, 4:5, :]) + mod_ref[0, 3:4, :]).astype(BF16)


def _outproj(att, us, yc, four, x, mod, w_out, w_glu, b_glu, dsk, g_post, g_pre, *, layer, mod_row, tm):
    n_rows, d = x.shape
    aw, fw = att.shape[1], four.shape[1]
    nblk = us.shape[0]
    sw = nblk * LANES

    def row(i):
        return (i, 0)

    def const(i):
        return (0, 0)

    blk3 = pl.BlockSpec((nblk, tm, LANES), lambda i: (0, i, 0))
    return pl.pallas_call(
        functools.partial(_outproj_kernel, aw=aw, sw=sw),
        grid=(n_rows // tm,),
        in_specs=[
            pl.BlockSpec((tm, aw), row), blk3, blk3,
            pl.BlockSpec((tm, fw), row), pl.BlockSpec((tm, d), row),
            pl.BlockSpec((1, N_MOD, d), lambda i: (mod_row(i), 0, 0)),
            _resident_layer(w_out.shape, layer), _resident_layer(w_glu.shape, layer),
            pl.BlockSpec((1, sw), const), pl.BlockSpec((1, sw), const),
            pl.BlockSpec((1, d), const), pl.BlockSpec((1, d), const),
        ],
        out_specs=[pl.BlockSpec((tm, d), row), pl.BlockSpec((tm, d), row)],
        out_shape=[jax.ShapeDtypeStruct((n_rows, d), F32), jax.ShapeDtypeStruct((n_rows, d), BF16)],
        compiler_params=_cparams(("arbitrary",)),
        name="outproj",
    )(att, us, yc, four, x, mod, w_out, w_glu, b_glu, dsk, g_post, g_pre)


def _ffn_kernel(h_ref, x_ref, mod_ref, g_ref, wg_ref, wu_ref, wd_ref, o_ref):
    k = pl.program_id(1)

    @pl.when(k == 0)
    def _():
        o_ref[...] = jnp.zeros(o_ref.shape, F32)

    h = h_ref[...]
    tf = wg_ref.shape[1]
    part = None
    for c0 in range(0, tf, tf // 2):
        cols = slice(c0, c0 + tf // 2)
        a = jnp.dot(h, wg_ref[:, cols], preferred_element_type=F32)
        u = jnp.dot(h, wu_ref[:, cols], preferred_element_type=F32)
        p = jnp.dot((_silu(a) * u).astype(BF16), wd_ref[cols, :], preferred_element_type=F32)
        part = p if part is None else part + p
    o_ref[...] += part

    @pl.when(k == pl.num_programs(1) - 1)
    def _():
        o_ref[...] = x_ref[...] + mod_ref[0, 5:6, :] * (_rms(o_ref[...], NORM_EPS) * g_ref[...])


def _ffn(h2, x_mid, mod, g_post, w_gate, w_up, w_down, *, layer, mod_row, tm, tf):
    n_rows, d = h2.shape
    dff = w_gate.shape[2]
    return pl.pallas_call(
        _ffn_kernel,
        grid=(n_rows // tm, dff // tf),
        in_specs=[
            pl.BlockSpec((tm, d), lambda i, k: (i, 0)),
            pl.BlockSpec((tm, d), lambda i, k: (i, 0)),
            pl.BlockSpec((1, N_MOD, d), lambda i, k: (mod_row(i), 0, 0)),
            pl.BlockSpec((1, d), lambda i, k: (0, 0)),
            pl.BlockSpec((None, d, tf), lambda i, k: (layer, 0, k)),
            pl.BlockSpec((None, d, tf), lambda i, k: (layer, 0, k)),
            pl.BlockSpec((None, tf, d), lambda i, k: (layer, k, 0)),
        ],
        out_specs=pl.BlockSpec((tm, d), lambda i, k: (i, 0)),
        out_shape=jax.ShapeDtypeStruct((n_rows, d), F32),
        compiler_params=_cparams(("arbitrary", "arbitrary")),
        name="ffn",
    )(h2, x_mid, mod, g_post, w_gate, w_up, w_down)


def kernel(x, c, ctx, c_ctx, w_mod, b_mod, g_mix_pre, g_mix_post, g_ffn_pre, g_ffn_post, w_in, w_out, lam_q1, lam_k1, lam_q2, lam_k2, g_subln, ssm_a_re, ssm_a_im, ssm_log_dt, ssm_b_re, ssm_b_im, ssm_c_re, ssm_c_im, ssm_d, w_glu, b_glu, w_four, b_four, w_gate, w_up, w_down):
    b, t, d = x.shape
    n_ctx = ctx.shape[1]
    depth = w_mod.shape[0]
    nl, nc = b * t, b * n_ctx
    aw = d // 2
    sw = ssm_d.shape[1]
    fw = d - aw - sw
    heads = aw // (2 * DA_HEAD_DIM)
    tm = 512
    assert t % tm == 0 and nc % tm == 0 and n_ctx % SSM_CHUNK == 0 and t % GRID_W == 0 and b + 1 <= MOD_ROWS
    assert sw % LANES == 0 and (sw // SSM_GROUP) % (2 * OCT) == 0

    cs = jnp.concatenate([c, c_ctx[None, :], jnp.zeros((MOD_ROWS - b - 1, d), F32)], axis=0)
    mod_all = _modulation(cs, w_mod, b_mod).reshape(depth, MOD_ROWS, N_MOD, d)

    per_b = t // tm
    lat_mod = lambda i: i // per_b
    ctx_mod = lambda i: b
    tabs = _rope_tables(t, tm)
    dft_lat = _dft_half_tables(t)
    dft_ctx = _dft_half_tables(n_ctx)
    perm = _atom_transpose_matrix()
    x_lat, x_ctx = x.reshape(nl, d), ctx.reshape(nc, d)
    w_in_b, w_out_b, w_glu_b = w_in.astype(BF16), w_out.astype(BF16), w_glu.astype(BF16)
    w_vt_b = jnp.transpose(w_in[:, :, 2 * aw:3 * aw], (0, 2, 1)).astype(BF16)
    w_gate_b, w_up_b, w_down_b = w_gate.astype(BF16), w_up.astype(BF16), w_down.astype(BF16)

    for l in range(depth):
        need_ctx = l < depth - 1
        lam_init = 0.8 - 0.6 * math.exp(-0.3 * l)
        mod = mod_all[l]
        g_pre = g_mix_pre[l][None, :]
        q, k, vt, us, uf = _inproj(x_lat, mod, g_pre, w_in_b, w_vt_b, tabs, layer=l, mod_row=lat_mod,
                                   tab_blk=lambda i: i % per_b, aw=aw, sw=sw, fw=fw, tm=tm)
        qc, kc, vtc, usc, ufc = _inproj(x_ctx, mod, g_pre, w_in_b, w_vt_b, tabs, layer=l, mod_row=ctx_mod,
                                        tab_blk=lambda i: per_b, aw=aw, sw=sw, fw=fw, tm=tm)

        lam4 = jnp.stack([lam_q1[l], lam_k1[l], lam_q2[l], lam_k2[l]]).astype(F32)
        gs = g_subln[l][None, :].astype(F32)
        att = _attention(lam4, gs, q, [(kc, vtc), (k, vt)], lam_init=lam_init, b=b, heads=heads, tq=1024,
                         n_split=1, name="attn_latent")
        ssm_w = _ssm_weights(ssm_a_re[l], ssm_a_im[l], ssm_log_dt[l], ssm_b_re[l], ssm_b_im[l],
                             ssm_c_re[l], ssm_c_im[l])
        yc, ycc = _ssm_conv(us, usc, ssm_w, perm, b=b)
        bias = b_four[l].reshape(1, fw).astype(F32)
        four = _fourier(uf, _fourier_weights(w_four[l], t), bias, dft_lat, b=b, tm=tm)

        small = (b_glu[l][None, :].astype(F32), ssm_d[l][None, :].astype(F32), g_mix_post[l][None, :],
                 g_ffn_pre[l][None, :])
        ffn_w = (g_ffn_post[l][None, :], w_gate_b, w_up_b, w_down_b)
        x_mid, h2 = _outproj(att, us, yc, four, x_lat, mod, w_out_b, w_glu_b, *small, layer=l, mod_row=lat_mod,
                             tm=tm)
        x_lat = _ffn(h2, x_mid, mod, *ffn_w, layer=l, mod_row=lat_mod, tm=tm, tf=512)

        if need_ctx:
            att_c = _attention(lam4, gs, qc, [(kc, vtc)], lam_init=lam_init, b=b, heads=heads, tq=n_ctx,
                               n_split=1, name="attn_ctx")
            four_c = _fourier(ufc, _fourier_weights(w_four[l], n_ctx), bias, dft_ctx, b=b, tm=tm)
            xc_mid, h2c = _outproj(att_c, usc, ycc, four_c, x_ctx, mod, w_out_b, w_glu_b, *small, layer=l,
                                   mod_row=ctx_mod, tm=tm)
            x_ctx = _ffn(h2c, xc_mid, mod, *ffn_w, layer=l, mod_row=ctx_mod, tm=tm, tf=512)
    return x_lat.reshape(b, t, d)
```

```python
import functools
import math

import jax
import jax.numpy as jnp
import numpy as np
from jax import lax
from jax.experimental import pallas as pl
from jax.experimental.pallas import tpu as pltpu

F32 = jnp.float32
BF16 = jnp.bfloat16

LANES = 128
GRID_W = 64
DA_HEAD_DIM = 128
SSM_GROUP = 16
SSM_STATE = 64
N_MOD = 6
ROPE_BASE = 10000.0
ROPE_PAIRS = DA_HEAD_DIM // 4
NORM_EPS = 1e-6
SUBLN_EPS = 1e-5

SSM_CHUNK = 16
SSM_CW = SSM_CHUNK * SSM_GROUP
OCT = LANES // SSM_GROUP
MOD_ROWS = 8
VMEM_LIMIT = 56 * 1024 * 1024
NT_DIMS = (((1,), (1,)), ((), ()))


def _cparams(sem):
    return pltpu.CompilerParams(dimension_semantics=sem, vmem_limit_bytes=VMEM_LIMIT)


def _rms(x, eps):
    return x * lax.rsqrt(jnp.mean(x * x, axis=-1, keepdims=True) + eps)


def _silu(x):
    return x * jax.nn.sigmoid(x)


def _resident(shape):
    nd = len(shape)
    return pl.BlockSpec(shape, lambda *_: (0,) * nd, pipeline_mode=pl.Buffered(1))


def _resident_layer(shape, layer):
    nd = len(shape)
    return pl.BlockSpec((None,) + tuple(shape[1:]), lambda *_: (layer,) + (0,) * (nd - 1),
                        pipeline_mode=pl.Buffered(1))


def _mod_kernel(c_ref, w_ref, b_ref, o_ref):
    s = _silu(c_ref[...]).astype(BF16)
    o_ref[0] = jnp.dot(s, w_ref[0].astype(BF16), preferred_element_type=F32) + b_ref[0]


def _modulation(cs, w_mod, b_mod, tn=1024):
    depth, d, n = w_mod.shape
    return pl.pallas_call(
        _mod_kernel,
        grid=(depth, n // tn),
        in_specs=[
            pl.BlockSpec((MOD_ROWS, d), lambda l, j: (0, 0)),
            pl.BlockSpec((1, d, tn), lambda l, j: (l, 0, j)),
            pl.BlockSpec((1, 1, tn), lambda l, j: (l, 0, j)),
        ],
        out_specs=pl.BlockSpec((1, MOD_ROWS, tn), lambda l, j: (l, 0, j)),
        out_shape=jax.ShapeDtypeStruct((depth, MOD_ROWS, n), F32),
        compiler_params=_cparams(("arbitrary", "arbitrary")),
        name="modulation",
    )(cs, w_mod, b_mod.reshape(depth, 1, n))


def _inproj_kernel(x_ref, mod_ref, g_ref, w_ref, cq_ref, sq_ref, ck_ref, sk_ref,
                   q_ref, k_ref, vt_ref, us_ref, uf_ref, *, aw, sw):
    tm = x_ref.shape[0]
    h = _rms(x_ref[...], NORM_EPS) * g_ref[...]
    h = h * (1.0 + mod_ref[0, 1:2, :]) + mod_ref[0, 0:1, :]
    hb = h.astype(BF16)

    lane = lax.broadcasted_iota(jnp.int32, (tm, DA_HEAD_DIM), 1)
    first_half = (lane & (2 * ROPE_PAIRS - 1)) < ROPE_PAIRS

    def rope(z, c, s):
        partner = jnp.where(first_half, pltpu.roll(z, DA_HEAD_DIM - ROPE_PAIRS, 1), pltpu.roll(z, ROPE_PAIRS, 1))
        return z * c + partner * s

    zq = jnp.dot(hb, w_ref[:, 0:aw], preferred_element_type=F32)
    cq, sq = cq_ref[...], sq_ref[...]
    for j in range(aw // DA_HEAD_DIM):
        sl = slice(j * DA_HEAD_DIM, (j + 1) * DA_HEAD_DIM)
        q_ref[:, sl] = rope(zq[:, sl], cq, sq).astype(BF16)
    zk = jnp.dot(hb, w_ref[:, aw:2 * aw], preferred_element_type=F32)
    ck, sk = ck_ref[...], sk_ref[...]
    for j in range(aw // DA_HEAD_DIM):
        sl = slice(j * DA_HEAD_DIM, (j + 1) * DA_HEAD_DIM)
        k_ref[:, sl] = rope(zk[:, sl], ck, sk).astype(BF16)
    zv = jnp.dot(hb, w_ref[:, 2 * aw:3 * aw], preferred_element_type=F32)
    vt_ref[...] = jnp.transpose(zv.astype(BF16))
    zs = jnp.dot(hb, w_ref[:, 3 * aw:3 * aw + sw], preferred_element_type=F32)
    for j in range(sw // LANES):
        us_ref[j] = zs[:, j * LANES:(j + 1) * LANES]
    uf_ref[...] = jnp.dot(hb, w_ref[:, 3 * aw + sw:], preferred_element_type=F32).astype(BF16)


def _inproj(x, mod, g, w_in, tabs, *, layer, mod_row, tab_blk, aw, sw, fw, tm):
    nr, d = x.shape

    def row(i):
        return (i, 0)

    tab_spec = pl.BlockSpec((tm, DA_HEAD_DIM), lambda i: (tab_blk(i), 0))
    return pl.pallas_call(
        functools.partial(_inproj_kernel, aw=aw, sw=sw),
        grid=(nr // tm,),
        in_specs=[
            pl.BlockSpec((tm, d), row),
            pl.BlockSpec((1, N_MOD, d), lambda i: (mod_row(i), 0, 0)),
            pl.BlockSpec((1, d), lambda i: (0, 0)),
            _resident_layer(w_in.shape, layer),
            tab_spec, tab_spec, tab_spec, tab_spec,
        ],
        out_specs=[
            pl.BlockSpec((tm, aw), row), pl.BlockSpec((tm, aw), row), pl.BlockSpec((aw, tm), lambda i: (0, i)),
            pl.BlockSpec((sw // LANES, tm, LANES), lambda i: (0, i, 0)), pl.BlockSpec((tm, fw), row),
        ],
        out_shape=[
            jax.ShapeDtypeStruct((nr, aw), BF16), jax.ShapeDtypeStruct((nr, aw), BF16),
            jax.ShapeDtypeStruct((aw, nr), BF16), jax.ShapeDtypeStruct((sw // LANES, nr, LANES), F32),
            jax.ShapeDtypeStruct((nr, fw), BF16),
        ],
        compiler_params=_cparams(("arbitrary",)),
        name="inproj",
    )(x, mod, g, w_in, *tabs)


def _rope_tables(t, tm):
    rows = t // GRID_W
    r = jnp.broadcast_to(jnp.arange(rows, dtype=F32)[:, None], (rows, GRID_W)).reshape(-1)
    col = jnp.broadcast_to(jnp.arange(GRID_W, dtype=F32)[None, :], (rows, GRID_W)).reshape(-1)
    inv = ROPE_BASE ** (-jnp.arange(ROPE_PAIRS, dtype=F32) / ROPE_PAIRS)
    ar, ac = r[:, None] * inv, col[:, None] * inv
    cos = jnp.concatenate([jnp.cos(ar), jnp.cos(ar), jnp.cos(ac), jnp.cos(ac)], axis=1)
    sin = jnp.concatenate([-jnp.sin(ar), jnp.sin(ar), -jnp.sin(ac), jnp.sin(ac)], axis=1)
    cos = jnp.concatenate([cos, jnp.ones((tm, DA_HEAD_DIM), F32)], axis=0)
    sin = jnp.concatenate([sin, jnp.zeros((tm, DA_HEAD_DIM), F32)], axis=0)
    scale = DA_HEAD_DIM ** -0.5 * math.log2(math.e)
    return cos * scale, sin * scale, cos, sin


def _attn_kernel(lam_ref, gs_ref, q_ref, *refs, lam_init, n_split):
    o_ref = refs[-1]
    k_refs, vt_refs = refs[0:-1:2], refs[1:-1:2]
    hd = DA_HEAD_DIM
    s1 = jnp.sum(lam_ref[0:1, :] * lam_ref[1:2, :], axis=-1, keepdims=True)
    s2 = jnp.sum(lam_ref[2:3, :] * lam_ref[3:4, :], axis=-1, keepdims=True)
    lam = jnp.exp(s1) - jnp.exp(s2) + lam_init
    tq = q_ref.shape[0]
    for r0 in range(0, tq, tq // n_split):
        rows = slice(r0, r0 + tq // n_split)
        outs = []
        for idx in range(2):
            qi = q_ref[rows, idx * hd:(idx + 1) * hd]
            ss = [lax.dot_general(kr[:, idx * hd:(idx + 1) * hd], qi, NT_DIMS, preferred_element_type=F32)
                  for kr in k_refs]
            m = functools.reduce(jnp.maximum, [jnp.max(s, axis=0, keepdims=True) for s in ss])
            ps = [jnp.exp2(s - m) for s in ss]
            l = functools.reduce(jnp.add, [jnp.sum(p, axis=0, keepdims=True) for p in ps])
            acc = functools.reduce(jnp.add, [jnp.dot(vr[...], p.astype(BF16), preferred_element_type=F32)
                                             for vr, p in zip(vt_refs, ps)])
            outs.append(acc * (1.0 / l))
        o = jnp.transpose(outs[0] - outs[1] * lam)
        o_ref[rows, :] = (_rms(o, SUBLN_EPS) * gs_ref[...] * (1.0 - lam_init)).astype(BF16)


def _attention(lam4, gs, q, kv_sets, *, lam_init, b, heads, tq, n_split, name):
    nq, aw = q.shape
    vw = 2 * DA_HEAD_DIM
    per_b = nq // b // tq
    q_spec = pl.BlockSpec((tq, vw), lambda bi, h, i: (bi * per_b + i, h))
    in_specs = [pl.BlockSpec((4, DA_HEAD_DIM), lambda bi, h, i: (0, 0)),
                pl.BlockSpec((1, vw), lambda bi, h, i: (0, 0)),
                q_spec]
    args = [lam4, gs, q]
    for k, vt in kv_sets:
        sk = k.shape[0] // b
        in_specs += [pl.BlockSpec((sk, vw), lambda bi, h, i: (bi, h)),
                     pl.BlockSpec((vw, sk), lambda bi, h, i: (h, bi))]
        args += [k, vt]
    return pl.pallas_call(
        functools.partial(_attn_kernel, lam_init=lam_init, n_split=n_split),
        grid=(b, heads, per_b),
        in_specs=in_specs,
        out_specs=q_spec,
        out_shape=jax.ShapeDtypeStruct((nq, aw), BF16),
        compiler_params=_cparams(("arbitrary", "arbitrary", "arbitrary")),
        name=name,
    )(*args)


def _ssm_weights(a_re, a_im, log_dt, b_re, b_im, c_re, c_im):
    L, H, P = SSM_CHUNK, SSM_GROUP, SSM_STATE
    g = a_re.shape[1]
    npair = g // 2
    hp = lax.Precision.HIGHEST
    ar, ai = a_re.astype(F32), a_im.astype(F32)
    dt = jnp.exp(log_dt.astype(F32))[..., None]
    n = jnp.arange(L + 1, dtype=F32)
    mag = jnp.exp((ar * dt)[..., None] * n)
    ang = (ai * dt)[..., None] * n
    apr, api = mag * jnp.cos(ang), mag * jnp.sin(ang)
    xr, xi = apr[..., 1] - 1.0, api[..., 1]
    den = ar * ar + ai * ai
    qr, qi = (xr * ar + xi * ai) / den, (xi * ar - xr * ai) / den
    br, bi = b_re.astype(F32), b_im.astype(F32)
    bbr = qr[..., None] * br - qi[..., None] * bi
    bbi = qr[..., None] * bi + qi[..., None] * br
    cr, ci = c_re.astype(F32), c_im.astype(F32)

    car = cr[..., None] * apr[:, :, None, :, :L] - ci[..., None] * api[:, :, None, :, :L]
    cai = cr[..., None] * api[:, :, None, :, :L] + ci[..., None] * apr[:, :, None, :, :L]
    kt = (jnp.einsum('dghpt,dgpk->dgkth', car, bbr, precision=hp)
          - jnp.einsum('dghpt,dgpk->dgkth', cai, bbi, precision=hp))
    kc = jnp.concatenate([jnp.flip(kt[1][:, :, 1:], axis=2), kt[0][:, :, :1] + kt[1][:, :, :1], kt[0][:, :, 1:]],
                         axis=2).reshape(g, H, (2 * L - 1) * H)
    w_intra = jnp.stack([kc[:, :, (L - 1 - s) * H:(2 * L - 1 - s) * H] for s in range(L)], axis=1)
    w_intra = w_intra.reshape(g, L * H, L * H)

    sir = jnp.stack([jnp.flip(apr[0, ..., :L], axis=-1), apr[1, ..., :L]])
    sii = jnp.stack([jnp.flip(api[0, ..., :L], axis=-1), api[1, ..., :L]])
    sir, sii = jnp.swapaxes(sir, 2, 3)[:, :, :, None, :], jnp.swapaxes(sii, 2, 3)[:, :, :, None, :]
    tbr, tbi = jnp.swapaxes(bbr, 2, 3)[:, :, None], jnp.swapaxes(bbi, 2, 3)[:, :, None]
    w_sin = jnp.concatenate([sir * tbr - sii * tbi, sir * tbi + sii * tbr], axis=-1)
    w_sin = w_sin.reshape(2, g, L * H, 2 * P)

    sor = jnp.stack([apr[0, ..., 1:], jnp.flip(apr[1, ..., 1:], axis=-1)])
    soi = jnp.stack([api[0, ..., 1:], jnp.flip(api[1, ..., 1:], axis=-1)])
    tcr, tci = jnp.swapaxes(cr, 2, 3)[:, :, :, None, :], jnp.swapaxes(ci, 2, 3)[:, :, :, None, :]
    cnr = (tcr * sor[..., None] - tci * soi[..., None]).reshape(2, g, P, L * H)
    cni = (tcr * soi[..., None] + tci * sor[..., None]).reshape(2, g, P, L * H)
    w_so = jnp.concatenate([cnr, -cni], axis=2)

    al = jnp.stack([apr[..., L].reshape(2, npair, 2 * P), api[..., L].reshape(2, npair, 2 * P)], axis=2)
    al = al.reshape(2, 2 * npair, 1, 2 * P)
    return w_intra.astype(BF16), w_sin.astype(BF16), w_so.astype(BF16), al


def _atom_transpose_matrix():
    n = OCT * OCT * SSM_GROUP
    i = np.arange(n)
    j = ((i // SSM_GROUP) % OCT) * LANES + (i // LANES) * SSM_GROUP + i % SSM_GROUP
    p = np.zeros((n, n), np.float32)
    p[i, j] = 1.0
    return jnp.asarray(p, BF16)


def _ssm_in_kernel(us_ref, p_ref, w_ref, ut_ref, s_ref):
    n = ut_ref.shape[1]
    noct = us_ref.shape[0]
    half = SSM_CHUNK // 2
    src = []
    for o in range(noct):
        for hf in range(2):
            rows = [us_ref[o, pl.ds(half * hf + sl, n, stride=SSM_CHUNK), :] for sl in range(half)]
            src.append(jnp.concatenate(rows, axis=1).astype(BF16))
    perm = jnp.dot(jnp.concatenate(src, axis=0), p_ref[...], preferred_element_type=F32).astype(BF16)

    def group_chunk(gi):
        o, gl = gi // OCT, gi % OCT
        return jnp.concatenate([perm[(2 * o + hf) * n:(2 * o + hf + 1) * n, gl * LANES:(gl + 1) * LANES]
                                for hf in range(2)], axis=1)

    low = lax.broadcasted_iota(jnp.int32, (n, LANES), 1) < SSM_STATE
    for j in range(noct * OCT // 2):
        ug = [group_chunk(2 * j), group_chunk(2 * j + 1)]
        ut_ref[2 * j] = ug[0]
        ut_ref[2 * j + 1] = ug[1]
        for d in range(2):
            s0 = jnp.dot(ug[0], w_ref[d, 2 * j], preferred_element_type=F32)
            s1 = jnp.dot(ug[1], w_ref[d, 2 * j + 1], preferred_element_type=F32)
            s_ref[d, 2 * j] = jnp.where(low, s0, pltpu.roll(s1, SSM_STATE, 1))
            s_ref[d, 2 * j + 1] = jnp.where(low, pltpu.roll(s0, SSM_STATE, 1), s1)


def _ssm_scan_kernel(sc_ref, sl_ref, al_ref, hc_ref, hl_ref, *, b):
    d = pl.program_id(0)
    nblk = sc_ref.shape[1]
    ars = [al_ref[0, 2 * k] for k in range(nblk // 2)]
    ais = [al_ref[0, 2 * k + 1] for k in range(nblk // 2)]

    def phase(s_ref, h_ref, carry):
        nch = s_ref.shape[2] // b

        def body(i, carry):
            c = jnp.where(d == 0, i, nch - 1 - i)
            rows = pl.ds(c, b, stride=nch)
            new = []
            for k in range(nblk // 2):
                hr, hi = carry[2 * k], carry[2 * k + 1]
                h_ref[0, 2 * k, rows, :] = hr
                h_ref[0, 2 * k + 1, rows, :] = hi
                sr = s_ref[0, 2 * k, rows, :]
                si = s_ref[0, 2 * k + 1, rows, :]
                new.append(ars[k] * hr - ais[k] * hi + sr)
                new.append(ars[k] * hi + ais[k] * hr + si)
            return tuple(new)

        return lax.fori_loop(0, nch, body, carry, unroll=4)

    carry = tuple(jnp.zeros((b, LANES), F32) for _ in range(nblk))
    carry = phase(sc_ref, hc_ref, carry)
    phase(sl_ref, hl_ref, carry)


def _ssm_out_kernel(ut_ref, h_ref, wi_ref, wo_ref, p_ref, y_ref):
    n = ut_ref.shape[1]
    ng = ut_ref.shape[0]
    noct = ng // OCT
    half = SSM_CHUNK // 2
    low = lax.broadcasted_iota(jnp.int32, (n, LANES), 1) < SSM_STATE
    ys = []
    for j in range(ng // 2):
        hg = [[], []]
        for d in range(2):
            hr, hi = h_ref[d, 2 * j], h_ref[d, 2 * j + 1]
            hg[0].append(jnp.where(low, hr, pltpu.roll(hi, SSM_STATE, 1)).astype(BF16))
            hg[1].append(jnp.where(low, pltpu.roll(hr, SSM_STATE, 1), hi).astype(BF16))
        for e in range(2):
            gi = 2 * j + e
            ys.append(jnp.dot(ut_ref[gi], wi_ref[gi], preferred_element_type=F32)
                      + jnp.dot(hg[e][0], wo_ref[0, gi], preferred_element_type=F32)
                      + jnp.dot(hg[e][1], wo_ref[1, gi], preferred_element_type=F32))
    rows = []
    for o in range(noct):
        for hf in range(2):
            rows.append(jnp.concatenate([ys[OCT * o + gl][:, hf * LANES:(hf + 1) * LANES] for gl in range(OCT)],
                                        axis=1))
    ycat = jnp.concatenate(rows, axis=0)
    hi = ycat.astype(BF16)
    lo = (ycat - hi.astype(F32)).astype(BF16)
    res = (jnp.dot(hi, p_ref[...], preferred_element_type=F32)
           + jnp.dot(lo, p_ref[...], preferred_element_type=F32))
    for o in range(noct):
        for hf in range(2):
            blk = 2 * o + hf
            for tl in range(half):
                y_ref[o, pl.ds(half * hf + tl, n, stride=SSM_CHUNK), :] = (
                    res[blk * n:(blk + 1) * n, tl * LANES:(tl + 1) * LANES])


def _ssm_conv(us_lat, us_ctx, weights, perm, *, layer, b):
    w_intra, w_sin, w_so, al = weights
    P = SSM_STATE
    ng = w_intra.shape[1]
    nblk = ng

    def stage_in(us, steps):
        noct, rows, _ = us.shape
        nchunks = rows // SSM_CHUNK
        n = nchunks // steps
        return pl.pallas_call(
            _ssm_in_kernel,
            grid=(steps,),
            in_specs=[pl.BlockSpec((noct, n * SSM_CHUNK, LANES), lambda i: (0, i, 0)),
                      _resident(perm.shape), _resident_layer(w_sin.shape, layer)],
            out_specs=[pl.BlockSpec((ng, n, SSM_CW), lambda i: (0, i, 0)),
                       pl.BlockSpec((2, nblk, n, LANES), lambda i: (0, 0, i, 0))],
            out_shape=[jax.ShapeDtypeStruct((ng, nchunks, SSM_CW), BF16),
                       jax.ShapeDtypeStruct((2, nblk, nchunks, LANES), F32)],
            compiler_params=_cparams(("arbitrary",)),
            name="ssm_in",
        )(us, perm, w_sin)

    lat_steps = 2 * b
    ut_lat, s_lat = stage_in(us_lat, lat_steps)
    ut_ctx, s_ctx = stage_in(us_ctx, 1)

    cb = 16
    nc_rows, nl_rows = s_ctx.shape[2], s_lat.shape[2]
    h_ctx, h_lat = pl.pallas_call(
        functools.partial(_ssm_scan_kernel, b=b),
        grid=(2, nblk // cb),
        in_specs=[pl.BlockSpec((1, cb, nc_rows, LANES), lambda d, j: (d, j, 0, 0)),
                  pl.BlockSpec((1, cb, nl_rows, LANES), lambda d, j: (d, j, 0, 0)),
                  pl.BlockSpec((None, 1, cb, 1, 2 * P), lambda d, j: (layer, d, j, 0, 0))],
        out_specs=[pl.BlockSpec((1, cb, nc_rows, LANES), lambda d, j: (d, j, 0, 0)),
                   pl.BlockSpec((1, cb, nl_rows, LANES), lambda d, j: (d, j, 0, 0))],
        out_shape=[jax.ShapeDtypeStruct(s_ctx.shape, F32), jax.ShapeDtypeStruct(s_lat.shape, F32)],
        compiler_params=_cparams(("arbitrary", "arbitrary")),
        name="ssm_scan",
    )(s_ctx, s_lat, al)

    def stage_out(ut, hin, steps):
        _, nchunks, _ = ut.shape
        n = nchunks // steps
        noct = ng // OCT
        return pl.pallas_call(
            _ssm_out_kernel,
            grid=(steps,),
            in_specs=[pl.BlockSpec((ng, n, SSM_CW), lambda i: (0, i, 0)),
                      pl.BlockSpec((2, nblk, n, LANES), lambda i: (0, 0, i, 0)),
                      _resident_layer(w_intra.shape, layer), _resident_layer(w_so.shape, layer),
                      _resident(perm.shape)],
            out_specs=pl.BlockSpec((noct, n * SSM_CHUNK, LANES), lambda i: (0, i, 0)),
            out_shape=jax.ShapeDtypeStruct((noct, nchunks * SSM_CHUNK, LANES), F32),
            compiler_params=_cparams(("arbitrary",)),
            name="ssm_out",
        )(ut, hin, w_intra, w_so, perm)

    return stage_out(ut_lat, h_lat, lat_steps), stage_out(ut_ctx, h_ctx, 1)


def _fourier_kernel(ulo_ref, uup_ref, w_ref, cos_ref, sin_ref, alt_ref, s1_ref, b_ref,
                    lo_ref, up_ref, a_s, b_s, am_s, carry, *, groups):
    gc = ulo_ref.shape[1] // groups
    tm = lo_ref.shape[0]
    nt = a_s.shape[0] // tm
    row = lax.broadcasted_iota(jnp.int32, lo_ref.shape, 0)

    @pl.when(pl.program_id(1) == 0)
    def _():
        for jj in range(nt):
            pr = jnp.dot(s1_ref[...], uup_ref[(nt - 1 - jj) * tm:(nt - jj) * tm, :], preferred_element_type=F32)
            if jj > 0:
                first = uup_ref[(nt - jj) * tm:(nt - jj) * tm + am_s.shape[0], :][0:1, :].astype(F32)
                pr = jnp.where(row == 0, first, pr)
            pr = pr.astype(BF16)
            rows = slice(jj * tm, (jj + 1) * tm)
            for g in range(groups):
                cols = slice(g * gc, (g + 1) * gc)
                ab_lo = jnp.dot(ulo_ref[rows, cols], w_ref[g], preferred_element_type=F32)
                ab_pr = jnp.dot(pr[:, cols], w_ref[g], preferred_element_type=F32)
                a_s[rows, cols] = (ab_lo[:, :gc] + ab_pr[:, :gc]).astype(BF16)
                b_s[rows, cols] = (ab_lo[:, gc:] - ab_pr[:, gc:]).astype(BF16)
        for g in range(groups):
            cols = slice(g * gc, (g + 1) * gc)
            am_s[:, cols] = jnp.dot(uup_ref[0:am_s.shape[0], cols], w_ref[g, :, 0:gc], preferred_element_type=F32)
        mid = jnp.dot(alt_ref[...], a_s[...], preferred_element_type=F32)
        carry[...] = mid + am_s[0:carry.shape[0], :] + b_ref[...]

    a_mid = jnp.where((row & 1) == 0, am_s[0:1, :], -am_s[0:1, :])
    p = jnp.dot(cos_ref[...], a_s[...], preferred_element_type=F32) + a_mid + b_ref[...]
    q = jnp.dot(sin_ref[...], b_s[...], preferred_element_type=F32)
    lo_ref[...] = (p - q).astype(BF16)
    hi = (p + q).astype(BF16)
    rev = jnp.dot(s1_ref[...], hi, preferred_element_type=F32)
    up_ref[...] = jnp.where(row == 0, carry[0:1, :], rev).astype(BF16)
    carry[0:1, :] = hi[0:1, :].astype(F32)


def _dft_half_tables(n):
    m = math.isqrt(n)
    half = n // 2
    assert m * m == n and half % m == 0
    t = jnp.arange(half, dtype=jnp.int32)[None, :]
    k2 = jnp.arange(half // m, dtype=jnp.int32)[:, None]
    k1 = jnp.arange(m, dtype=jnp.int32)[:, None]
    alpha = ((m * k2 * t) % n).astype(F32) * (2.0 * math.pi / n)
    beta = ((k1 * t) % n).astype(F32) * (2.0 * math.pi / n)
    ca, sa = jnp.cos(alpha)[:, None, :], jnp.sin(alpha)[:, None, :]
    cb, sb = jnp.cos(beta)[None, :, :], jnp.sin(beta)[None, :, :]
    cos = (ca * cb - sa * sb).reshape(half, half)
    sin = (sa * cb + ca * sb).reshape(half, half)
    alt = jnp.broadcast_to(jnp.where((t & 1) == 0, 1.0, -1.0), (8, half))
    return cos.astype(BF16), sin.astype(BF16), alt.astype(BF16)


def _fourier(uf, w_cs, bias, tables, *, b, tm):
    nr, fw = uf.shape
    seq = nr // b
    half = seq // 2
    groups = w_cs.shape[0]
    cos_t, sin_t, alt = tables
    tm = min(tm, half)
    nt = half // tm
    r = np.arange(1, tm)
    s1 = np.zeros((tm, tm), np.float32)
    s1[r, tm - r] = 1.0
    s1 = jnp.asarray(s1, BF16)
    lo, up = pl.pallas_call(
        functools.partial(_fourier_kernel, groups=groups),
        grid=(b, nt),
        in_specs=[
            pl.BlockSpec((half, fw), lambda bi, i: (2 * bi, 0)),
            pl.BlockSpec((half, fw), lambda bi, i: (2 * bi + 1, 0)),
            pl.BlockSpec(w_cs.shape, lambda bi, i: (0, 0, 0)),
            pl.BlockSpec((tm, half), lambda bi, i: (nt - 1 - i, 0)),
            pl.BlockSpec((tm, half), lambda bi, i: (nt - 1 - i, 0)),
            pl.BlockSpec(alt.shape, lambda bi, i: (0, 0)),
            pl.BlockSpec((tm, tm), lambda bi, i: (0, 0)),
            pl.BlockSpec((1, fw), lambda bi, i: (0, 0)),
        ],
        out_specs=[pl.BlockSpec((tm, fw), lambda bi, i: (bi * nt + nt - 1 - i, 0)),
                   pl.BlockSpec((tm, fw), lambda bi, i: (bi * nt + i, 0))],
        out_shape=[jax.ShapeDtypeStruct((b * half, fw), BF16), jax.ShapeDtypeStruct((b * half, fw), BF16)],
        scratch_shapes=[pltpu.VMEM((half, fw), BF16), pltpu.VMEM((half, fw), BF16),
                        pltpu.VMEM((16, fw), F32), pltpu.VMEM((alt.shape[0], fw), F32)],
        compiler_params=_cparams(("arbitrary", "arbitrary")),
        name="fourier_%d" % seq,
    )(uf, uf, w_cs, cos_t, sin_t, alt, s1, bias)
    return jnp.concatenate([lo.reshape(b, half, fw), up.reshape(b, half, fw)], axis=1).reshape(nr, fw)


def _fourier_weights(w_four, seq):
    groups, gc, _ = w_four.shape
    k = np.arange(gc)
    ang = ((k[:, None] * k[None, :]) % gc).astype(np.float64) * (2.0 * np.pi / gc)
    norm = 1.0 / math.sqrt(seq * gc)
    cc = jnp.asarray(np.cos(ang) * norm, F32)
    sc = jnp.asarray(np.sin(ang) * norm, F32)
    hp = lax.Precision.HIGHEST
    wc = jnp.einsum('ck,gkd->gcd', cc, w_four.astype(F32), precision=hp)
    ws = jnp.einsum('ck,gkd->gcd', sc, w_four.astype(F32), precision=hp)
    return jnp.concatenate([wc, ws], axis=-1).astype(BF16)


def _outproj_kernel(att_ref, us_ref, yc_ref, uf_ref, x_ref, mod_ref, wo_ref, wg_ref, bg_ref, dsk_ref,
                    gpost_ref, gpre_ref, xo_ref, h2_ref, *, aw, sw):
    nblk = us_ref.shape[0]
    tm = x_ref.shape[0]
    for r0 in range(0, tm, tm // 2):
        rows = slice(r0, r0 + tm // 2)
        us = jnp.concatenate([us_ref[j, rows, :] for j in range(nblk)], axis=1)
        yc = jnp.concatenate([yc_ref[j, rows, :] for j in range(nblk)], axis=1)
        g = jax.nn.gelu(dsk_ref[...] * us + yc)
        z = jnp.dot(g.astype(BF16), wg_ref[...], preferred_element_type=F32) + bg_ref[...]
        ssm = (g * jax.nn.sigmoid(z)).astype(BF16)
        mix = (jnp.dot(att_ref[rows, :], wo_ref[0:aw, :], preferred_element_type=F32)
               + jnp.dot(ssm, wo_ref[aw:aw + sw, :], preferred_element_type=F32)
               + jnp.dot(uf_ref[rows, :], wo_ref[aw + sw:, :], preferred_element_type=F32))
        xn = x_ref[rows, :] + mod_ref[0, 2:3, :] * (_rms(mix, NORM_EPS) * gpost_ref[...])
        xo_ref[rows, :] = xn
        h2 = _rms(xn, NORM_EPS) * gpre_ref[...]
        h2_ref[rows, :] = (h2 * (1.0 + mod_ref[0, 4:5, :]) + mod_ref[0, 3:4, :]).astype(BF16)


def _outproj(att, us, yc, four, x, mod, w_out, w_glu, b_glu, dsk, g_post, g_pre, *, layer, mod_row, tm):
    n_rows, d = x.shape
    aw, fw = att.shape[1], four.shape[1]
    nblk = us.shape[0]
    sw = nblk * LANES

    def row(i):
        return (i, 0)

    def const(i):
        return (0, 0)

    blk3 = pl.BlockSpec((nblk, tm, LANES), lambda i: (0, i, 0))
    return pl.pallas_call(
        functools.partial(_outproj_kernel, aw=aw, sw=sw),
        grid=(n_rows // tm,),
        in_specs=[
            pl.BlockSpec((tm, aw), row), blk3, blk3,
            pl.BlockSpec((tm, fw), row), pl.BlockSpec((tm, d), row),
            pl.BlockSpec((1, N_MOD, d), lambda i: (mod_row(i), 0, 0)),
            _resident_layer(w_out.shape, layer), _resident_layer(w_glu.shape, layer),
            pl.BlockSpec((1, sw), const), pl.BlockSpec((1, sw), const),
            pl.BlockSpec((1, d), const), pl.BlockSpec((1, d), const),
        ],
        out_specs=[pl.BlockSpec((tm, d), row), pl.BlockSpec((tm, d), row)],
        out_shape=[jax.ShapeDtypeStruct((n_rows, d), F32), jax.ShapeDtypeStruct((n_rows, d), BF16)],
        compiler_params=_cparams(("arbitrary",)),
        name="outproj",
    )(att, us, yc, four, x, mod, w_out, w_glu, b_glu, dsk, g_post, g_pre)


def _ffn_kernel(h_ref, x_ref, mod_ref, g_ref, wg_ref, wu_ref, wd_ref, o_ref):
    k = pl.program_id(1)

    @pl.when(k == 0)
    def _():
        o_ref[...] = jnp.zeros(o_ref.shape, F32)

    h = h_ref[...]
    tf = wg_ref.shape[1]
    part = None
    for c0 in range(0, tf, tf // 2):
        cols = slice(c0, c0 + tf // 2)
        a = jnp.dot(h, wg_ref[:, cols], preferred_element_type=F32)
        u = jnp.dot(h, wu_ref[:, cols], preferred_element_type=F32)
        p = jnp.dot((_silu(a) * u).astype(BF16), wd_ref[cols, :], preferred_element_type=F32)
        part = p if part is None else part + p
    o_ref[...] += part

    @pl.when(k == pl.num_programs(1) - 1)
    def _():
        o_ref[...] = x_ref[...] + mod_ref[0, 5:6, :] * (_rms(o_ref[...], NORM_EPS) * g_ref[...])


def _ffn(h2, x_mid, mod, g_post, w_gate, w_up, w_down, *, layer, mod_row, tm, tf):
    n_rows, d = h2.shape
    dff = w_gate.shape[2]
    return pl.pallas_call(
        _ffn_kernel,
        grid=(n_rows // tm, dff // tf),
        in_specs=[
            pl.BlockSpec((tm, d), lambda i, k: (i, 0)),
            pl.BlockSpec((tm, d), lambda i, k: (i, 0)),
            pl.BlockSpec((1, N_MOD, d), lambda i, k: (mod_row(i), 0, 0)),
            pl.BlockSpec((1, d), lambda i, k: (0, 0)),
            pl.BlockSpec((None, d, tf), lambda i, k: (layer, 0, k)),
            pl.BlockSpec((None, d, tf), lambda i, k: (layer, 0, k)),
            pl.BlockSpec((None, tf, d), lambda i, k: (layer, k, 0)),
        ],
        out_specs=pl.BlockSpec((tm, d), lambda i, k: (i, 0)),
        out_shape=jax.ShapeDtypeStruct((n_rows, d), F32),
        compiler_params=_cparams(("arbitrary", "arbitrary")),
        name="ffn",
    )(h2, x_mid, mod, g_post, w_gate, w_up, w_down)


def kernel(x, c, ctx, c_ctx, w_mod, b_mod, g_mix_pre, g_mix_post, g_ffn_pre, g_ffn_post, w_in, w_out, lam_q1, lam_k1, lam_q2, lam_k2, g_subln, ssm_a_re, ssm_a_im, ssm_log_dt, ssm_b_re, ssm_b_im, ssm_c_re, ssm_c_im, ssm_d, w_glu, b_glu, w_four, b_four, w_gate, w_up, w_down):
    b, t, d = x.shape
    n_ctx = ctx.shape[1]
    depth = w_mod.shape[0]
    nl, nc = b * t, b * n_ctx
    aw = d // 2
    sw = ssm_d.shape[1]
    fw = d - aw - sw
    heads = aw // (2 * DA_HEAD_DIM)
    tm = 512
    assert t % tm == 0 and nc % tm == 0 and n_ctx % SSM_CHUNK == 0 and t % GRID_W == 0 and b + 1 <= MOD_ROWS
    assert sw % LANES == 0 and (sw // SSM_GROUP) % (2 * OCT) == 0

    cs = jnp.concatenate([c, c_ctx[None, :], jnp.zeros((MOD_ROWS - b - 1, d), F32)], axis=0)
    mod_all = _modulation(cs, w_mod, b_mod).reshape(depth, MOD_ROWS, N_MOD, d)

    per_b = t // tm
    lat_mod = lambda i: i // per_b
    ctx_mod = lambda i: b
    tabs = _rope_tables(t, tm)
    dft_lat = _dft_half_tables(t)
    dft_ctx = _dft_half_tables(n_ctx)
    perm = _atom_transpose_matrix()
    x_lat, x_ctx = x.reshape(nl, d), ctx.reshape(nc, d)
    w_in_b, w_out_b, w_glu_b = w_in.astype(BF16), w_out.astype(BF16), w_glu.astype(BF16)
    w_gate_b, w_up_b, w_down_b = w_gate.astype(BF16), w_up.astype(BF16), w_down.astype(BF16)
    ssm_w = jax.vmap(_ssm_weights)(ssm_a_re, ssm_a_im, ssm_log_dt, ssm_b_re, ssm_b_im, ssm_c_re, ssm_c_im)

    for l in range(depth):
        need_ctx = l < depth - 1
        lam_init = 0.8 - 0.6 * math.exp(-0.3 * l)
        mod = mod_all[l]
        g_pre = g_mix_pre[l][None, :]
        q, k, vt, us, uf = _inproj(x_lat, mod, g_pre, w_in_b, tabs, layer=l, mod_row=lat_mod,
                                   tab_blk=lambda i: i % per_b, aw=aw, sw=sw, fw=fw, tm=tm)
        qc, kc, vtc, usc, ufc = _inproj(x_ctx, mod, g_pre, w_in_b, tabs, layer=l, mod_row=ctx_mod,
                                        tab_blk=lambda i: per_b, aw=aw, sw=sw, fw=fw, tm=tm)

        lam4 = jnp.stack([lam_q1[l], lam_k1[l], lam_q2[l], lam_k2[l]]).astype(F32)
        gs = g_subln[l][None, :].astype(F32)
        att = _attention(lam4, gs, q, [(kc, vtc), (k, vt)], lam_init=lam_init, b=b, heads=heads, tq=1024,
                         n_split=1, name="attn_latent")
        yc, ycc = _ssm_conv(us, usc, ssm_w, perm, layer=l, b=b)
        bias = b_four[l].reshape(1, fw).astype(F32)
        four = _fourier(uf, _fourier_weights(w_four[l], t), bias, dft_lat, b=b, tm=tm)

        small = (b_glu[l][None, :].astype(F32), ssm_d[l][None, :].astype(F32), g_mix_post[l][None, :],
                 g_ffn_pre[l][None, :])
        ffn_w = (g_ffn_post[l][None, :], w_gate_b, w_up_b, w_down_b)
        x_mid, h2 = _outproj(att, us, yc, four, x_lat, mod, w_out_b, w_glu_b, *small, layer=l, mod_row=lat_mod,
                             tm=tm)
        x_lat = _ffn(h2, x_mid, mod, *ffn_w, layer=l, mod_row=lat_mod, tm=tm, tf=512)

        if need_ctx:
            att_c = _attention(lam4, gs, qc, [(kc, vtc)], lam_init=lam_init, b=b, heads=heads, tq=n_ctx,
                               n_split=1, name="attn_ctx")
            four_c = _fourier(ufc, _fourier_weights(w_four[l], n_ctx), bias, dft_ctx, b=b, tm=tm)
            xc_mid, h2c = _outproj(att_c, usc, ycc, four_c, x_ctx, mod, w_out_b, w_glu_b, *small, layer=l,
                                   mod_row=ctx_mod, tm=tm)
            x_ctx = _ffn(h2c, xc_mid, mod, *ffn_w, layer=l, mod_row=ctx_mod, tm=tm, tf=512)
    return x_lat.reshape(b, t, d)
```

```python
import functools
import math

import jax
import jax.numpy as jnp
import numpy as np
from jax import lax
from jax.experimental import pallas as pl
from jax.experimental.pallas import tpu as pltpu

F32 = jnp.float32
BF16 = jnp.bfloat16

LANES = 128
GRID_W = 64
DA_HEAD_DIM = 128
SSM_GROUP = 16
SSM_STATE = 64
N_MOD = 6
ROPE_BASE = 10000.0
ROPE_PAIRS = DA_HEAD_DIM // 4
NORM_EPS = 1e-6
SUBLN_EPS = 1e-5

SSM_CHUNK = 16
SSM_CW = SSM_CHUNK * SSM_GROUP
OCT = LANES // SSM_GROUP
MOD_ROWS = 8
VMEM_LIMIT = 56 * 1024 * 1024
NT_DIMS = (((1,), (1,)), ((), ()))


def _cparams(sem):
    return pltpu.CompilerParams(dimension_semantics=sem, vmem_limit_bytes=VMEM_LIMIT)


def _rms(x, eps):
    return x * lax.rsqrt(jnp.mean(x * x, axis=-1, keepdims=True) + eps)


def _silu(x):
    return x * jax.nn.sigmoid(x)


def _resident(shape):
    nd = len(shape)
    return pl.BlockSpec(shape, lambda *_: (0,) * nd, pipeline_mode=pl.Buffered(1))


def _resident_layer(shape, layer):
    nd = len(shape)
    return pl.BlockSpec((None,) + tuple(shape[1:]), lambda *_: (layer,) + (0,) * (nd - 1),
                        pipeline_mode=pl.Buffered(1))


def _mod_kernel(c_ref, w_ref, b_ref, o_ref):
    s = _silu(c_ref[...]).astype(BF16)
    o_ref[0] = jnp.dot(s, w_ref[0].astype(BF16), preferred_element_type=F32) + b_ref[0]


def _modulation(cs, w_mod, b_mod, tn=1024):
    depth, d, n = w_mod.shape
    return pl.pallas_call(
        _mod_kernel,
        grid=(depth, n // tn),
        in_specs=[
            pl.BlockSpec((MOD_ROWS, d), lambda l, j: (0, 0)),
            pl.BlockSpec((1, d, tn), lambda l, j: (l, 0, j)),
            pl.BlockSpec((1, 1, tn), lambda l, j: (l, 0, j)),
        ],
        out_specs=pl.BlockSpec((1, MOD_ROWS, tn), lambda l, j: (l, 0, j)),
        out_shape=jax.ShapeDtypeStruct((depth, MOD_ROWS, n), F32),
        compiler_params=_cparams(("arbitrary", "arbitrary")),
        name="modulation",
    )(cs, w_mod, b_mod.reshape(depth, 1, n))


def _inproj_kernel(x_ref, mod_ref, g_ref, w_ref, cq_ref, sq_ref, ck_ref, sk_ref,
                   q_ref, k_ref, vt_ref, us_ref, uf_ref, *, aw, sw):
    tm = x_ref.shape[0]
    h = _rms(x_ref[...], NORM_EPS) * g_ref[...]
    h = h * (1.0 + mod_ref[0, 1:2, :]) + mod_ref[0, 0:1, :]
    hb = h.astype(BF16)

    lane = lax.broadcasted_iota(jnp.int32, (tm, DA_HEAD_DIM), 1)
    first_half = (lane & (2 * ROPE_PAIRS - 1)) < ROPE_PAIRS

    def rope(z, c, s):
        partner = jnp.where(first_half, pltpu.roll(z, DA_HEAD_DIM - ROPE_PAIRS, 1), pltpu.roll(z, ROPE_PAIRS, 1))
        return z * c + partner * s

    zq = jnp.dot(hb, w_ref[:, 0:aw], preferred_element_type=F32)
    cq, sq = cq_ref[...], sq_ref[...]
    for j in range(aw // DA_HEAD_DIM):
        sl = slice(j * DA_HEAD_DIM, (j + 1) * DA_HEAD_DIM)
        q_ref[:, sl] = rope(zq[:, sl], cq, sq).astype(BF16)
    zk = jnp.dot(hb, w_ref[:, aw:2 * aw], preferred_element_type=F32)
    ck, sk = ck_ref[...], sk_ref[...]
    for j in range(aw // DA_HEAD_DIM):
        sl = slice(j * DA_HEAD_DIM, (j + 1) * DA_HEAD_DIM)
        k_ref[:, sl] = rope(zk[:, sl], ck, sk).astype(BF16)
    zv = jnp.dot(hb, w_ref[:, 2 * aw:3 * aw], preferred_element_type=F32)
    vt_ref[...] = jnp.transpose(zv.astype(BF16))
    zs = jnp.dot(hb, w_ref[:, 3 * aw:3 * aw + sw], preferred_element_type=F32)
    for j in range(sw // LANES):
        us_ref[j] = zs[:, j * LANES:(j + 1) * LANES]
    uf_ref[...] = jnp.dot(hb, w_ref[:, 3 * aw + sw:], preferred_element_type=F32).astype(BF16)


def _inproj(x, mod, g, w_in, tabs, *, layer, mod_row, tab_blk, aw, sw, fw, tm):
    nr, d = x.shape

    def row(i):
        return (i, 0)

    tab_spec = pl.BlockSpec((tm, DA_HEAD_DIM), lambda i: (tab_blk(i), 0))
    return pl.pallas_call(
        functools.partial(_inproj_kernel, aw=aw, sw=sw),
        grid=(nr // tm,),
        in_specs=[
            pl.BlockSpec((tm, d), row),
            pl.BlockSpec((1, N_MOD, d), lambda i: (mod_row(i), 0, 0)),
            pl.BlockSpec((1, d), lambda i: (0, 0)),
            _resident_layer(w_in.shape, layer),
            tab_spec, tab_spec, tab_spec, tab_spec,
        ],
        out_specs=[
            pl.BlockSpec((tm, aw), row), pl.BlockSpec((tm, aw), row), pl.BlockSpec((aw, tm), lambda i: (0, i)),
            pl.BlockSpec((sw // LANES, tm, LANES), lambda i: (0, i, 0)), pl.BlockSpec((tm, fw), row),
        ],
        out_shape=[
            jax.ShapeDtypeStruct((nr, aw), BF16), jax.ShapeDtypeStruct((nr, aw), BF16),
            jax.ShapeDtypeStruct((aw, nr), BF16), jax.ShapeDtypeStruct((sw // LANES, nr, LANES), F32),
            jax.ShapeDtypeStruct((nr, fw), BF16),
        ],
        compiler_params=_cparams(("arbitrary",)),
        name="inproj",
    )(x, mod, g, w_in, *tabs)


def _rope_tables(t, tm):
    rows = t // GRID_W
    r = np.repeat(np.arange(rows, dtype=np.float64), GRID_W)
    col = np.tile(np.arange(GRID_W, dtype=np.float64), rows)
    inv = ROPE_BASE ** (-np.arange(ROPE_PAIRS, dtype=np.float64) / ROPE_PAIRS)
    ar, ac = r[:, None] * inv, col[:, None] * inv
    cos = np.concatenate([np.cos(ar), np.cos(ar), np.cos(ac), np.cos(ac)], axis=1)
    sin = np.concatenate([-np.sin(ar), np.sin(ar), -np.sin(ac), np.sin(ac)], axis=1)
    cos = np.concatenate([cos, np.ones((tm, DA_HEAD_DIM))], axis=0)
    sin = np.concatenate([sin, np.zeros((tm, DA_HEAD_DIM))], axis=0)
    scale = DA_HEAD_DIM ** -0.5 * math.log2(math.e)
    return tuple(jnp.asarray(a, F32) for a in (cos * scale, sin * scale, cos, sin))


def _attn_kernel(lam_ref, gs_ref, q_ref, *refs, lam_init, n_sets, n_cast):
    k_refs, vt_refs = refs[0:2 * n_sets:2], refs[1:2 * n_sets:2]
    cast_in = refs[2 * n_sets:2 * n_sets + n_cast]
    o_ref = refs[2 * n_sets + n_cast]
    cast_out = refs[2 * n_sets + n_cast + 1:]
    for src, dst in zip(cast_in, cast_out):
        dst[...] = src[...].astype(BF16)

    hd = DA_HEAD_DIM
    s1 = jnp.sum(lam_ref[0:1, :] * lam_ref[1:2, :], axis=-1, keepdims=True)
    s2 = jnp.sum(lam_ref[2:3, :] * lam_ref[3:4, :], axis=-1, keepdims=True)
    lam = jnp.exp(s1) - jnp.exp(s2) + lam_init
    outs = []
    for idx in range(2):
        qi = q_ref[:, idx * hd:(idx + 1) * hd]
        ss = [lax.dot_general(kr[:, idx * hd:(idx + 1) * hd], qi, NT_DIMS, preferred_element_type=F32)
              for kr in k_refs]
        m = functools.reduce(jnp.maximum, [jnp.max(s, axis=0, keepdims=True) for s in ss])
        ps = [jnp.exp2(s - m) for s in ss]
        l = functools.reduce(jnp.add, [jnp.sum(p, axis=0, keepdims=True) for p in ps])
        acc = functools.reduce(jnp.add, [jnp.dot(vr[...], p.astype(BF16), preferred_element_type=F32)
                                         for vr, p in zip(vt_refs, ps)])
        outs.append(acc * (1.0 / l))
    o = jnp.transpose(outs[0] - outs[1] * lam)
    o_ref[...] = (_rms(o, SUBLN_EPS) * gs_ref[...] * (1.0 - lam_init)).astype(BF16)


def _attention(lam4, gs, q, kv_sets, *, lam_init, b, heads, tq, name, cast=None):
    nq, aw = q.shape
    vw = 2 * DA_HEAD_DIM
    per_b = nq // b // tq
    q_spec = pl.BlockSpec((tq, vw), lambda bi, h, i: (bi * per_b + i, h))
    in_specs = [pl.BlockSpec((4, DA_HEAD_DIM), lambda bi, h, i: (0, 0)),
                pl.BlockSpec((1, vw), lambda bi, h, i: (0, 0)),
                q_spec]
    args = [lam4, gs, q]
    for k, vt in kv_sets:
        sk = k.shape[0] // b
        in_specs += [pl.BlockSpec((sk, vw), lambda bi, h, i: (bi, h)),
                     pl.BlockSpec((vw, sk), lambda bi, h, i: (h, bi))]
        args += [k, vt]
    out_specs = [q_spec]
    out_shape = [jax.ShapeDtypeStruct((nq, aw), BF16)]
    n_cast = 0
    if cast is not None:
        layer, w_gate, w_up, w_down = cast
        _, d, dff = w_gate.shape
        n_steps = b * heads * per_b
        per = next(dv for dv in range(1, dff // LANES + 1) if (dff // LANES) % dv == 0 and dv * n_steps >= dff // LANES)
        cw = per * LANES
        nblk = dff // cw

        def blk(bi, h, i):
            return jnp.minimum((bi * heads + h) * per_b + i, nblk - 1)

        in_specs += [pl.BlockSpec((None, d, cw), lambda bi, h, i: (layer, 0, blk(bi, h, i))),
                     pl.BlockSpec((None, d, cw), lambda bi, h, i: (layer, 0, blk(bi, h, i))),
                     pl.BlockSpec((None, cw, d), lambda bi, h, i: (layer, blk(bi, h, i), 0))]
        args += [w_gate, w_up, w_down]
        out_specs += [pl.BlockSpec((d, cw), lambda bi, h, i: (0, blk(bi, h, i))),
                      pl.BlockSpec((d, cw), lambda bi, h, i: (0, blk(bi, h, i))),
                      pl.BlockSpec((cw, d), lambda bi, h, i: (blk(bi, h, i), 0))]
        out_shape += [jax.ShapeDtypeStruct((d, dff), BF16), jax.ShapeDtypeStruct((d, dff), BF16),
                      jax.ShapeDtypeStruct((dff, d), BF16)]
        n_cast = 3
    return pl.pallas_call(
        functools.partial(_attn_kernel, lam_init=lam_init, n_sets=len(kv_sets), n_cast=n_cast),
        grid=(b, heads, per_b),
        in_specs=in_specs,
        out_specs=out_specs,
        out_shape=out_shape,
        compiler_params=_cparams(("arbitrary", "arbitrary", "arbitrary")),
        name=name,
    )(*args)


def _ssm_weights(a_re, a_im, log_dt, b_re, b_im, c_re, c_im):
    L, H, P = SSM_CHUNK, SSM_GROUP, SSM_STATE
    g = a_re.shape[1]
    npair = g // 2
    hp = lax.Precision.HIGHEST
    ar, ai = a_re.astype(F32), a_im.astype(F32)
    dt = jnp.exp(log_dt.astype(F32))[..., None]
    n = jnp.arange(L + 1, dtype=F32)
    mag = jnp.exp((ar * dt)[..., None] * n)
    ang = (ai * dt)[..., None] * n
    apr, api = mag * jnp.cos(ang), mag * jnp.sin(ang)
    xr, xi = apr[..., 1] - 1.0, api[..., 1]
    den = ar * ar + ai * ai
    qr, qi = (xr * ar + xi * ai) / den, (xi * ar - xr * ai) / den
    br, bi = b_re.astype(F32), b_im.astype(F32)
    bbr = qr[..., None] * br - qi[..., None] * bi
    bbi = qr[..., None] * bi + qi[..., None] * br
    cr, ci = c_re.astype(F32), c_im.astype(F32)

    car = cr[..., None] * apr[:, :, None, :, :L] - ci[..., None] * api[:, :, None, :, :L]
    cai = cr[..., None] * api[:, :, None, :, :L] + ci[..., None] * apr[:, :, None, :, :L]
    kt = (jnp.einsum('dghpt,dgpk->dgkth', car, bbr, precision=hp)
          - jnp.einsum('dghpt,dgpk->dgkth', cai, bbi, precision=hp))
    kc = jnp.concatenate([jnp.flip(kt[1][:, :, 1:], axis=2), kt[0][:, :, :1] + kt[1][:, :, :1], kt[0][:, :, 1:]],
                         axis=2).reshape(g, H, (2 * L - 1) * H)
    w_intra = jnp.pad(kc, ((0, 0), (0, 0), (0, (-kc.shape[2]) % LANES)))

    sir = jnp.stack([jnp.flip(apr[0, ..., :L], axis=-1), apr[1, ..., :L]])
    sii = jnp.stack([jnp.flip(api[0, ..., :L], axis=-1), api[1, ..., :L]])
    sir, sii = jnp.swapaxes(sir, 2, 3)[:, :, :, None, :], jnp.swapaxes(sii, 2, 3)[:, :, :, None, :]
    tbr, tbi = jnp.swapaxes(bbr, 2, 3)[:, :, None], jnp.swapaxes(bbi, 2, 3)[:, :, None]
    w_sin = jnp.concatenate([sir * tbr - sii * tbi, sir * tbi + sii * tbr], axis=-1)
    w_sin = w_sin.reshape(2, g, L * H, 2 * P)

    sor = jnp.stack([apr[0, ..., 1:], jnp.flip(apr[1, ..., 1:], axis=-1)])
    soi = jnp.stack([api[0, ..., 1:], jnp.flip(api[1, ..., 1:], axis=-1)])
    tcr, tci = jnp.swapaxes(cr, 2, 3)[:, :, :, None, :], jnp.swapaxes(ci, 2, 3)[:, :, :, None, :]
    cnr = (tcr * sor[..., None] - tci * soi[..., None]).reshape(2, g, P, L * H)
    cni = (tcr * soi[..., None] + tci * sor[..., None]).reshape(2, g, P, L * H)
    w_so = jnp.concatenate([cnr, -cni], axis=2)

    al = jnp.stack([apr[..., L].reshape(2, npair, 2 * P), api[..., L].reshape(2, npair, 2 * P)], axis=2)
    al = al.reshape(2, 2 * npair, 1, 2 * P)
    return w_intra, w_sin.astype(BF16), w_so.astype(BF16), al


def _atom_transpose_matrix():
    n = OCT * OCT * SSM_GROUP
    i = np.arange(n)
    j = ((i // SSM_GROUP) % OCT) * LANES + (i // LANES) * SSM_GROUP + i % SSM_GROUP
    p = np.zeros((n, n), np.float32)
    p[i, j] = 1.0
    return jnp.asarray(p, BF16)


def _ssm_in_kernel(us_ref, p_ref, w_ref, ut_ref, s_ref):
    n = ut_ref.shape[1]
    noct = us_ref.shape[0]
    half = SSM_CHUNK // 2
    src = []
    for o in range(noct):
        for hf in range(2):
            rows = [us_ref[o, pl.ds(half * hf + sl, n, stride=SSM_CHUNK), :] for sl in range(half)]
            src.append(jnp.concatenate(rows, axis=1).astype(BF16))
    perm = jnp.dot(jnp.concatenate(src, axis=0), p_ref[...], preferred_element_type=F32).astype(BF16)

    def group_chunk(gi):
        o, gl = gi // OCT, gi % OCT
        return jnp.concatenate([perm[(2 * o + hf) * n:(2 * o + hf + 1) * n, gl * LANES:(gl + 1) * LANES]
                                for hf in range(2)], axis=1)

    low = lax.broadcasted_iota(jnp.int32, (n, LANES), 1) < SSM_STATE
    for j in range(noct * OCT // 2):
        ug = [group_chunk(2 * j), group_chunk(2 * j + 1)]
        ut_ref[2 * j] = ug[0]
        ut_ref[2 * j + 1] = ug[1]
        for d in range(2):
            s0 = jnp.dot(ug[0], w_ref[d, 2 * j], preferred_element_type=F32)
            s1 = jnp.dot(ug[1], w_ref[d, 2 * j + 1], preferred_element_type=F32)
            s_ref[d, 2 * j] = jnp.where(low, s0, pltpu.roll(s1, SSM_STATE, 1))
            s_ref[d, 2 * j + 1] = jnp.where(low, pltpu.roll(s0, SSM_STATE, 1), s1)


def _ssm_scan_kernel(sc_ref, sl_ref, al_ref, hc_ref, hl_ref, *, b):
    d = pl.program_id(0)
    nblk = sc_ref.shape[1]
    ars = [al_ref[0, 2 * k] for k in range(nblk // 2)]
    ais = [al_ref[0, 2 * k + 1] for k in range(nblk // 2)]

    def phase(s_ref, h_ref, carry):
        nch = s_ref.shape[2] // b

        def body(i, carry):
            c = jnp.where(d == 0, i, nch - 1 - i)
            rows = pl.ds(c, b, stride=nch)
            new = []
            for k in range(nblk // 2):
                hr, hi = carry[2 * k], carry[2 * k + 1]
                h_ref[0, 2 * k, rows, :] = hr
                h_ref[0, 2 * k + 1, rows, :] = hi
                sr = s_ref[0, 2 * k, rows, :]
                si = s_ref[0, 2 * k + 1, rows, :]
                new.append(ars[k] * hr - ais[k] * hi + sr)
                new.append(ars[k] * hi + ais[k] * hr + si)
            return tuple(new)

        return lax.fori_loop(0, nch, body, carry, unroll=4)

    carry = tuple(jnp.zeros((b, LANES), F32) for _ in range(nblk))
    carry = phase(sc_ref, hc_ref, carry)
    phase(sl_ref, hl_ref, carry)


def _ssm_out_kernel(ut_ref, h_ref, kc_ref, wo_ref, p_ref, y_ref, wi_ref):
    n = ut_ref.shape[1]
    ng = ut_ref.shape[0]
    noct = ng // OCT
    half = SSM_CHUNK // 2

    @pl.when(pl.program_id(0) == 0)
    def _():
        for gi in range(ng):
            kc = kc_ref[gi]
            wi_ref[gi] = jnp.concatenate(
                [kc[:, (SSM_CHUNK - 1 - s) * SSM_GROUP:(SSM_CHUNK - 1 - s) * SSM_GROUP + SSM_CW]
                 for s in range(SSM_CHUNK)], axis=0).astype(BF16)

    low = lax.broadcasted_iota(jnp.int32, (n, LANES), 1) < SSM_STATE
    ys = []
    for j in range(ng // 2):
        hg = [[], []]
        for d in range(2):
            hr, hi = h_ref[d, 2 * j], h_ref[d, 2 * j + 1]
            hg[0].append(jnp.where(low, hr, pltpu.roll(hi, SSM_STATE, 1)).astype(BF16))
            hg[1].append(jnp.where(low, pltpu.roll(hr, SSM_STATE, 1), hi).astype(BF16))
        for e in range(2):
            gi = 2 * j + e
            ys.append(jnp.dot(ut_ref[gi], wi_ref[gi], preferred_element_type=F32)
                      + jnp.dot(hg[e][0], wo_ref[0, gi], preferred_element_type=F32)
                      + jnp.dot(hg[e][1], wo_ref[1, gi], preferred_element_type=F32))
    rows = []
    for o in range(noct):
        for hf in range(2):
            rows.append(jnp.concatenate([ys[OCT * o + gl][:, hf * LANES:(hf + 1) * LANES] for gl in range(OCT)],
                                        axis=1))
    ycat = jnp.concatenate(rows, axis=0)
    hi = ycat.astype(BF16)
    lo = (ycat - hi.astype(F32)).astype(BF16)
    res = (jnp.dot(hi, p_ref[...], preferred_element_type=F32)
           + jnp.dot(lo, p_ref[...], preferred_element_type=F32))
    for o in range(noct):
        for hf in range(2):
            blk = 2 * o + hf
            for tl in range(half):
                y_ref[o, pl.ds(half * hf + tl, n, stride=SSM_CHUNK), :] = (
                    res[blk * n:(blk + 1) * n, tl * LANES:(tl + 1) * LANES])


def _ssm_conv(us_lat, us_ctx, weights, perm, *, layer, b):
    w_intra, w_sin, w_so, al = weights
    P = SSM_STATE
    ng = w_intra.shape[1]
    nblk = ng

    def stage_in(us, steps):
        noct, rows, _ = us.shape
        nchunks = rows // SSM_CHUNK
        n = nchunks // steps
        return pl.pallas_call(
            _ssm_in_kernel,
            grid=(steps,),
            in_specs=[pl.BlockSpec((noct, n * SSM_CHUNK, LANES), lambda i: (0, i, 0)),
                      _resident(perm.shape), _resident_layer(w_sin.shape, layer)],
            out_specs=[pl.BlockSpec((ng, n, SSM_CW), lambda i: (0, i, 0)),
                       pl.BlockSpec((2, nblk, n, LANES), lambda i: (0, 0, i, 0))],
            out_shape=[jax.ShapeDtypeStruct((ng, nchunks, SSM_CW), BF16),
                       jax.ShapeDtypeStruct((2, nblk, nchunks, LANES), F32)],
            compiler_params=_cparams(("arbitrary",)),
            name="ssm_in",
        )(us, perm, w_sin)

    lat_steps = 2 * b
    ut_lat, s_lat = stage_in(us_lat, lat_steps)
    ut_ctx, s_ctx = stage_in(us_ctx, 1)

    cb = 16
    nc_rows, nl_rows = s_ctx.shape[2], s_lat.shape[2]
    h_ctx, h_lat = pl.pallas_call(
        functools.partial(_ssm_scan_kernel, b=b),
        grid=(2, nblk // cb),
        in_specs=[pl.BlockSpec((1, cb, nc_rows, LANES), lambda d, j: (d, j, 0, 0)),
                  pl.BlockSpec((1, cb, nl_rows, LANES), lambda d, j: (d, j, 0, 0)),
                  pl.BlockSpec((None, 1, cb, 1, 2 * P), lambda d, j: (layer, d, j, 0, 0))],
        out_specs=[pl.BlockSpec((1, cb, nc_rows, LANES), lambda d, j: (d, j, 0, 0)),
                   pl.BlockSpec((1, cb, nl_rows, LANES), lambda d, j: (d, j, 0, 0))],
        out_shape=[jax.ShapeDtypeStruct(s_ctx.shape, F32), jax.ShapeDtypeStruct(s_lat.shape, F32)],
        compiler_params=_cparams(("arbitrary", "arbitrary")),
        name="ssm_scan",
    )(s_ctx, s_lat, al)

    def stage_out(ut, hin, steps):
        _, nchunks, _ = ut.shape
        n = nchunks // steps
        noct = ng // OCT
        return pl.pallas_call(
            _ssm_out_kernel,
            grid=(steps,),
            in_specs=[pl.BlockSpec((ng, n, SSM_CW), lambda i: (0, i, 0)),
                      pl.BlockSpec((2, nblk, n, LANES), lambda i: (0, 0, i, 0)),
                      _resident_layer(w_intra.shape, layer), _resident_layer(w_so.shape, layer),
                      _resident(perm.shape)],
            out_specs=pl.BlockSpec((noct, n * SSM_CHUNK, LANES), lambda i: (0, i, 0)),
            out_shape=jax.ShapeDtypeStruct((noct, nchunks * SSM_CHUNK, LANES), F32),
            scratch_shapes=[pltpu.VMEM((ng, SSM_CW, SSM_CW), BF16)],
            compiler_params=_cparams(("arbitrary",)),
            name="ssm_out",
        )(ut, hin, w_intra, w_so, perm)

    return stage_out(ut_lat, h_lat, lat_steps), stage_out(ut_ctx, h_ctx, 1)


def _fourier_kernel(ulo_ref, uup_ref, w_ref, cos_ref, sin_ref, alt_ref, s1_ref, b_ref,
                    lo_ref, up_ref, a_s, b_s, am_s, carry, *, groups):
    gc = ulo_ref.shape[1] // groups
    tm = lo_ref.shape[0]
    nt = a_s.shape[0] // tm
    row = lax.broadcasted_iota(jnp.int32, lo_ref.shape, 0)

    @pl.when(pl.program_id(1) == 0)
    def _():
        for jj in range(nt):
            pr = jnp.dot(s1_ref[...], uup_ref[(nt - 1 - jj) * tm:(nt - jj) * tm, :], preferred_element_type=F32)
            if jj > 0:
                first = uup_ref[(nt - jj) * tm:(nt - jj) * tm + am_s.shape[0], :][0:1, :].astype(F32)
                pr = jnp.where(row == 0, first, pr)
            pr = pr.astype(BF16)
            rows = slice(jj * tm, (jj + 1) * tm)
            for g in range(groups):
                cols = slice(g * gc, (g + 1) * gc)
                ab_lo = jnp.dot(ulo_ref[rows, cols], w_ref[g], preferred_element_type=F32)
                ab_pr = jnp.dot(pr[:, cols], w_ref[g], preferred_element_type=F32)
                a_s[rows, cols] = (ab_lo[:, :gc] + ab_pr[:, :gc]).astype(BF16)
                b_s[rows, cols] = (ab_lo[:, gc:] - ab_pr[:, gc:]).astype(BF16)
        for g in range(groups):
            cols = slice(g * gc, (g + 1) * gc)
            am_s[:, cols] = jnp.dot(uup_ref[0:am_s.shape[0], cols], w_ref[g, :, 0:gc], preferred_element_type=F32)
        mid = jnp.dot(alt_ref[...], a_s[...], preferred_element_type=F32)
        carry[...] = mid + am_s[0:carry.shape[0], :] + b_ref[...]

    a_mid = jnp.where((row & 1) == 0, am_s[0:1, :], -am_s[0:1, :])
    p = jnp.dot(cos_ref[...], a_s[...], preferred_element_type=F32) + a_mid + b_ref[...]
    q = jnp.dot(sin_ref[...], b_s[...], preferred_element_type=F32)
    lo_ref[...] = (p - q).astype(BF16)
    hi = (p + q).astype(BF16)
    rev = jnp.dot(s1_ref[...], hi, preferred_element_type=F32)
    up_ref[...] = jnp.where(row == 0, carry[0:1, :], rev).astype(BF16)
    carry[0:1, :] = hi[0:1, :].astype(F32)


def _dft_half_tables(n):
    half = n // 2
    k = np.arange(half, dtype=np.int64)
    ang = ((k[:, None] * k[None, :]) % n).astype(np.float64) * (2.0 * np.pi / n)
    alt = np.broadcast_to(np.where((k & 1) == 0, 1.0, -1.0), (8, half))
    return tuple(jnp.asarray(a, F32).astype(BF16) for a in (np.cos(ang), np.sin(ang), alt))


def _fourier(uf, w_cs, bias, tables, *, b, tm):
    nr, fw = uf.shape
    seq = nr // b
    half = seq // 2
    groups = w_cs.shape[0]
    cos_t, sin_t, alt = tables
    tm = min(tm, half)
    nt = half // tm
    r = np.arange(1, tm)
    s1 = np.zeros((tm, tm), np.float32)
    s1[r, tm - r] = 1.0
    s1 = jnp.asarray(s1, BF16)
    lo, up = pl.pallas_call(
        functools.partial(_fourier_kernel, groups=groups),
        grid=(b, nt),
        in_specs=[
            pl.BlockSpec((half, fw), lambda bi, i: (2 * bi, 0)),
            pl.BlockSpec((half, fw), lambda bi, i: (2 * bi + 1, 0)),
            pl.BlockSpec(w_cs.shape, lambda bi, i: (0, 0, 0)),
            pl.BlockSpec((tm, half), lambda bi, i: (nt - 1 - i, 0)),
            pl.BlockSpec((tm, half), lambda bi, i: (nt - 1 - i, 0)),
            pl.BlockSpec(alt.shape, lambda bi, i: (0, 0)),
            pl.BlockSpec((tm, tm), lambda bi, i: (0, 0)),
            pl.BlockSpec((1, fw), lambda bi, i: (0, 0)),
        ],
        out_specs=[pl.BlockSpec((tm, fw), lambda bi, i: (bi * nt + nt - 1 - i, 0)),
                   pl.BlockSpec((tm, fw), lambda bi, i: (bi * nt + i, 0))],
        out_shape=[jax.ShapeDtypeStruct((b * half, fw), BF16), jax.ShapeDtypeStruct((b * half, fw), BF16)],
        scratch_shapes=[pltpu.VMEM((half, fw), BF16), pltpu.VMEM((half, fw), BF16),
                        pltpu.VMEM((16, fw), F32), pltpu.VMEM((alt.shape[0], fw), F32)],
        compiler_params=_cparams(("arbitrary", "arbitrary")),
        name="fourier_%d" % seq,
    )(uf, uf, w_cs, cos_t, sin_t, alt, s1, bias)
    return jnp.concatenate([lo.reshape(b, half, fw), up.reshape(b, half, fw)], axis=1).reshape(nr, fw)


def _fourier_weights(w_four, seq):
    groups, gc, _ = w_four.shape
    k = np.arange(gc)
    ang = ((k[:, None] * k[None, :]) % gc).astype(np.float64) * (2.0 * np.pi / gc)
    norm = 1.0 / math.sqrt(seq * gc)
    cc = jnp.asarray(np.cos(ang) * norm, F32)
    sc = jnp.asarray(np.sin(ang) * norm, F32)
    hp = lax.Precision.HIGHEST
    wc = jnp.einsum('ck,gkd->gcd', cc, w_four.astype(F32), precision=hp)
    ws = jnp.einsum('ck,gkd->gcd', sc, w_four.astype(F32), precision=hp)
    return jnp.concatenate([wc, ws], axis=-1).astype(BF16)


def _outproj_kernel(att_ref, us_ref, yc_ref, uf_ref, x_ref, mod_ref, wo_ref, wg_ref, bg_ref, dsk_ref,
                    gpost_ref, gpre_ref, xo_ref, h2_ref, *, aw, sw):
    nblk = us_ref.shape[0]
    tm = x_ref.shape[0]
    for r0 in range(0, tm, tm // 2):
        rows = slice(r0, r0 + tm // 2)
        us = jnp.concatenate([us_ref[j, rows, :] for j in range(nblk)], axis=1)
        yc = jnp.concatenate([yc_ref[j, rows, :] for j in range(nblk)], axis=1)
        g = jax.nn.gelu(dsk_ref[...] * us + yc)
        z = jnp.dot(g.astype(BF16), wg_ref[...], preferred_element_type=F32) + bg_ref[...]
        ssm = (g * jax.nn.sigmoid(z)).astype(BF16)
        mix = (jnp.dot(att_ref[rows, :], wo_ref[0:aw, :], preferred_element_type=F32)
               + jnp.dot(ssm, wo_ref[aw:aw + sw, :], preferred_element_type=F32)
               + jnp.dot(uf_ref[rows, :], wo_ref[aw + sw:, :], preferred_element_type=F32))
        xn = x_ref[rows, :] + mod_ref[0, 2:3, :] * (_rms(mix, NORM_EPS) * gpost_ref[...])
        xo_ref[rows, :] = xn
        h2 = _rms(xn, NORM_EPS) * gpre_ref[...]
        h2_ref[rows, :] = (h2 * (1.0 + mod_ref[0, 4:5, :]) + mod_ref[0, 3:4, :]).astype(BF16)


def _outproj(att, us, yc, four, x, mod, w_out, w_glu, b_glu, dsk, g_post, g_pre, *, layer, mod_row, tm):
    n_rows, d = x.shape
    aw, fw = att.shape[1], four.shape[1]
    nblk = us.shape[0]
    sw = nblk * LANES

    def row(i):
        return (i, 0)

    def const(i):
        return (0, 0)

    blk3 = pl.BlockSpec((nblk, tm, LANES), lambda i: (0, i, 0))
    return pl.pallas_call(
        functools.partial(_outproj_kernel, aw=aw, sw=sw),
        grid=(n_rows // tm,),
        in_specs=[
            pl.BlockSpec((tm, aw), row), blk3, blk3,
            pl.BlockSpec((tm, fw), row), pl.BlockSpec((tm, d), row),
            pl.BlockSpec((1, N_MOD, d), lambda i: (mod_row(i), 0, 0)),
            _resident_layer(w_out.shape, layer), _resident_layer(w_glu.shape, layer),
            pl.BlockSpec((1, sw), const), pl.BlockSpec((1, sw), const),
            pl.BlockSpec((1, d), const), pl.BlockSpec((1, d), const),
        ],
        out_specs=[pl.BlockSpec((tm, d), row), pl.BlockSpec((tm, d), row)],
        out_shape=[jax.ShapeDtypeStruct((n_rows, d), F32), jax.ShapeDtypeStruct((n_rows, d), BF16)],
        compiler_params=_cparams(("arbitrary",)),
        name="outproj",
    )(att, us, yc, four, x, mod, w_out, w_glu, b_glu, dsk, g_post, g_pre)


def _ffn_kernel(h_ref, x_ref, mod_ref, g_ref, wg_ref, wu_ref, wd_ref, o_ref):
    k = pl.program_id(1)

    @pl.when(k == 0)
    def _():
        o_ref[...] = jnp.zeros(o_ref.shape, F32)

    h = h_ref[...]
    tf = wg_ref.shape[1]
    part = None
    for c0 in range(0, tf, tf // 2):
        cols = slice(c0, c0 + tf // 2)
        a = jnp.dot(h, wg_ref[:, cols], preferred_element_type=F32)
        u = jnp.dot(h, wu_ref[:, cols], preferred_element_type=F32)
        p = jnp.dot((_silu(a) * u).astype(BF16), wd_ref[cols, :], preferred_element_type=F32)
        part = p if part is None else part + p
    o_ref[...] += part

    @pl.when(k == pl.num_programs(1) - 1)
    def _():
        o_ref[...] = x_ref[...] + mod_ref[0, 5:6, :] * (_rms(o_ref[...], NORM_EPS) * g_ref[...])


def _ffn(h2, x_mid, mod, g_post, w_gate, w_up, w_down, *, mod_row, tm, tf):
    n_rows, d = h2.shape
    dff = w_gate.shape[1]
    return pl.pallas_call(
        _ffn_kernel,
        grid=(n_rows // tm, dff // tf),
        in_specs=[
            pl.BlockSpec((tm, d), lambda i, k: (i, 0)),
            pl.BlockSpec((tm, d), lambda i, k: (i, 0)),
            pl.BlockSpec((1, N_MOD, d), lambda i, k: (mod_row(i), 0, 0)),
            pl.BlockSpec((1, d), lambda i, k: (0, 0)),
            pl.BlockSpec((d, tf), lambda i, k: (0, k)),
            pl.BlockSpec((d, tf), lambda i, k: (0, k)),
            pl.BlockSpec((tf, d), lambda i, k: (k, 0)),
        ],
        out_specs=pl.BlockSpec((tm, d), lambda i, k: (i, 0)),
        out_shape=jax.ShapeDtypeStruct((n_rows, d), F32),
        compiler_params=_cparams(("arbitrary", "arbitrary")),
        name="ffn",
    )(h2, x_mid, mod, g_post, w_gate, w_up, w_down)


def kernel(x, c, ctx, c_ctx, w_mod, b_mod, g_mix_pre, g_mix_post, g_ffn_pre, g_ffn_post, w_in, w_out, lam_q1, lam_k1, lam_q2, lam_k2, g_subln, ssm_a_re, ssm_a_im, ssm_log_dt, ssm_b_re, ssm_b_im, ssm_c_re, ssm_c_im, ssm_d, w_glu, b_glu, w_four, b_four, w_gate, w_up, w_down):
    b, t, d = x.shape
    n_ctx = ctx.shape[1]
    depth = w_mod.shape[0]
    nl, nc = b * t, b * n_ctx
    aw = d // 2
    sw = ssm_d.shape[1]
    fw = d - aw - sw
    heads = aw // (2 * DA_HEAD_DIM)
    tm = 512
    assert t % tm == 0 and nc % tm == 0 and n_ctx % SSM_CHUNK == 0 and t % GRID_W == 0 and b + 1 <= MOD_ROWS
    assert sw % LANES == 0 and (sw // SSM_GROUP) % (2 * OCT) == 0

    cs = jnp.concatenate([c, c_ctx[None, :], jnp.zeros((MOD_ROWS - b - 1, d), F32)], axis=0)
    mod_all = _modulation(cs, w_mod, b_mod).reshape(depth, MOD_ROWS, N_MOD, d)

    per_b = t // tm
    lat_mod = lambda i: i // per_b
    ctx_mod = lambda i: b
    tabs = _rope_tables(t, tm)
    dft_lat = _dft_half_tables(t)
    dft_ctx = _dft_half_tables(n_ctx)
    perm = _atom_transpose_matrix()
    x_lat, x_ctx = x.reshape(nl, d), ctx.reshape(nc, d)
    w_in_b, w_out_b, w_glu_b = w_in.astype(BF16), w_out.astype(BF16), w_glu.astype(BF16)
    ssm_w = jax.vmap(_ssm_weights)(ssm_a_re, ssm_a_im, ssm_log_dt, ssm_b_re, ssm_b_im, ssm_c_re, ssm_c_im)

    for l in range(depth):
        need_ctx = l < depth - 1
        lam_init = 0.8 - 0.6 * math.exp(-0.3 * l)
        mod = mod_all[l]
        g_pre = g_mix_pre[l][None, :]
        q, k, vt, us, uf = _inproj(x_lat, mod, g_pre, w_in_b, tabs, layer=l, mod_row=lat_mod,
                                   tab_blk=lambda i: i % per_b, aw=aw, sw=sw, fw=fw, tm=tm)
        qc, kc, vtc, usc, ufc = _inproj(x_ctx, mod, g_pre, w_in_b, tabs, layer=l, mod_row=ctx_mod,
                                        tab_blk=lambda i: per_b, aw=aw, sw=sw, fw=fw, tm=tm)

        lam4 = jnp.stack([lam_q1[l], lam_k1[l], lam_q2[l], lam_k2[l]]).astype(F32)
        gs = g_subln[l][None, :].astype(F32)
        att, *ffn_wb = _attention(lam4, gs, q, [(kc, vtc), (k, vt)], lam_init=lam_init, b=b, heads=heads, tq=1024,
                                  name="attn_latent", cast=(l, w_gate, w_up, w_down))
        yc, ycc = _ssm_conv(us, usc, ssm_w, perm, layer=l, b=b)
        bias = b_four[l].reshape(1, fw).astype(F32)
        four = _fourier(uf, _fourier_weights(w_four[l], t), bias, dft_lat, b=b, tm=tm)

        small = (b_glu[l][None, :].astype(F32), ssm_d[l][None, :].astype(F32), g_mix_post[l][None, :],
                 g_ffn_pre[l][None, :])
        ffn_w = (g_ffn_post[l][None, :], *ffn_wb)
        x_mid, h2 = _outproj(att, us, yc, four, x_lat, mod, w_out_b, w_glu_b, *small, layer=l, mod_row=lat_mod,
                             tm=tm)
        x_lat = _ffn(h2, x_mid, mod, *ffn_w, mod_row=lat_mod, tm=tm, tf=512)

        if need_ctx:
            att_c, = _attention(lam4, gs, qc, [(kc, vtc)], lam_init=lam_init, b=b, heads=heads, tq=n_ctx,
                                name="attn_ctx")
            four_c = _fourier(ufc, _fourier_weights(w_four[l], n_ctx), bias, dft_ctx, b=b, tm=tm)
            xc_mid, h2c = _outproj(att_c, usc, ycc, four_c, x_ctx, mod, w_out_b, w_glu_b, *small, layer=l,
                                   mod_row=ctx_mod, tm=tm)
            x_ctx = _ffn(h2c, xc_mid, mod, *ffn_w, mod_row=ctx_mod, tm=tm, tf=512)
    return x_lat.reshape(b, t, d)
```

```python
import functools
import math

import jax
import jax.numpy as jnp
import numpy as np
from jax import lax
from jax.experimental import pallas as pl
from jax.experimental.pallas import tpu as pltpu

F32 = jnp.float32
BF16 = jnp.bfloat16

LANES = 128
GRID_W = 64
DA_HEAD_DIM = 128
SSM_GROUP = 16
SSM_STATE = 64
N_MOD = 6
ROPE_BASE = 10000.0
ROPE_PAIRS = DA_HEAD_DIM // 4
NORM_EPS = 1e-6
SUBLN_EPS = 1e-5

SSM_CHUNK = 16
SSM_CW = SSM_CHUNK * SSM_GROUP
OCT = LANES // SSM_GROUP
MOD_ROWS = 8
VMEM_LIMIT = 56 * 1024 * 1024
NT_DIMS = (((1,), (1,)), ((), ()))


def _cparams(sem):
    return pltpu.CompilerParams(dimension_semantics=sem, vmem_limit_bytes=VMEM_LIMIT)


def _rms(x, eps):
    return x * lax.rsqrt(jnp.mean(x * x, axis=-1, keepdims=True) + eps)


def _silu(x):
    return x * jax.nn.sigmoid(x)


def _resident(shape):
    nd = len(shape)
    return pl.BlockSpec(shape, lambda *_: (0,) * nd, pipeline_mode=pl.Buffered(1))


def _resident_layer(shape, layer):
    nd = len(shape)
    return pl.BlockSpec((None,) + tuple(shape[1:]), lambda *_: (layer,) + (0,) * (nd - 1),
                        pipeline_mode=pl.Buffered(1))


def _mod_kernel(c_ref, w_ref, b_ref, o_ref):
    s = _silu(c_ref[...]).astype(BF16)
    o_ref[0] = jnp.dot(s, w_ref[0].astype(BF16), preferred_element_type=F32) + b_ref[0]


def _modulation(cs, w_mod, b_mod, tn=1024):
    depth, d, n = w_mod.shape
    return pl.pallas_call(
        _mod_kernel,
        grid=(depth, n // tn),
        in_specs=[
            pl.BlockSpec((MOD_ROWS, d), lambda l, j: (0, 0)),
            pl.BlockSpec((1, d, tn), lambda l, j: (l, 0, j)),
            pl.BlockSpec((1, 1, tn), lambda l, j: (l, 0, j)),
        ],
        out_specs=pl.BlockSpec((1, MOD_ROWS, tn), lambda l, j: (l, 0, j)),
        out_shape=jax.ShapeDtypeStruct((depth, MOD_ROWS, n), F32),
        compiler_params=_cparams(("arbitrary", "arbitrary")),
        name="modulation",
    )(cs, w_mod, b_mod.reshape(depth, 1, n))


def _inproj_kernel(x_ref, mod_ref, g_ref, w_ref, cq_ref, sq_ref, ck_ref, sk_ref,
                   q_ref, k_ref, vt_ref, us_ref, uf_ref, *, aw, sw):
    tm = x_ref.shape[0]
    hb = (_rms(x_ref[...], NORM_EPS) * (g_ref[...] * (1.0 + mod_ref[0, 1:2, :])) + mod_ref[0, 0:1, :]).astype(BF16)

    lane = lax.broadcasted_iota(jnp.int32, (tm, DA_HEAD_DIM), 1)
    first_half = (lane & (2 * ROPE_PAIRS - 1)) < ROPE_PAIRS

    def rope(z, c, s):
        partner = jnp.where(first_half, pltpu.roll(z, DA_HEAD_DIM - ROPE_PAIRS, 1), pltpu.roll(z, ROPE_PAIRS, 1))
        return z * c + partner * s

    zq = jnp.dot(hb, w_ref[:, 0:aw], preferred_element_type=F32)
    cq, sq = cq_ref[...], sq_ref[...]
    for j in range(aw // DA_HEAD_DIM):
        sl = slice(j * DA_HEAD_DIM, (j + 1) * DA_HEAD_DIM)
        q_ref[:, sl] = rope(zq[:, sl], cq, sq).astype(BF16)
    zk = jnp.dot(hb, w_ref[:, aw:2 * aw], preferred_element_type=F32)
    ck, sk = ck_ref[...], sk_ref[...]
    for j in range(aw // DA_HEAD_DIM):
        sl = slice(j * DA_HEAD_DIM, (j + 1) * DA_HEAD_DIM)
        k_ref[:, sl] = rope(zk[:, sl], ck, sk).astype(BF16)
    zv = jnp.dot(hb, w_ref[:, 2 * aw:3 * aw], preferred_element_type=F32)
    vt_ref[...] = jnp.transpose(zv.astype(BF16))
    zs = jnp.dot(hb, w_ref[:, 3 * aw:3 * aw + sw], preferred_element_type=F32)
    for j in range(sw // LANES):
        us_ref[j] = zs[:, j * LANES:(j + 1) * LANES]
    uf_ref[...] = jnp.dot(hb, w_ref[:, 3 * aw + sw:], preferred_element_type=F32).astype(BF16)


def _inproj(x, mod, g, w_in, tabs, *, layer, mod_row, tab_blk, aw, sw, fw, tm):
    nr, d = x.shape

    def row(i):
        return (i, 0)

    tab_spec = pl.BlockSpec((tm, DA_HEAD_DIM), lambda i: (tab_blk(i), 0))
    return pl.pallas_call(
        functools.partial(_inproj_kernel, aw=aw, sw=sw),
        grid=(nr // tm,),
        in_specs=[
            pl.BlockSpec((tm, d), row),
            pl.BlockSpec((1, N_MOD, d), lambda i: (mod_row(i), 0, 0)),
            pl.BlockSpec((1, d), lambda i: (0, 0)),
            _resident_layer(w_in.shape, layer),
            tab_spec, tab_spec, tab_spec, tab_spec,
        ],
        out_specs=[
            pl.BlockSpec((tm, aw), row), pl.BlockSpec((tm, aw), row), pl.BlockSpec((aw, tm), lambda i: (0, i)),
            pl.BlockSpec((sw // LANES, tm, LANES), lambda i: (0, i, 0)), pl.BlockSpec((tm, fw), row),
        ],
        out_shape=[
            jax.ShapeDtypeStruct((nr, aw), BF16), jax.ShapeDtypeStruct((nr, aw), BF16),
            jax.ShapeDtypeStruct((aw, nr), BF16), jax.ShapeDtypeStruct((sw // LANES, nr, LANES), F32),
            jax.ShapeDtypeStruct((nr, fw), BF16),
        ],
        compiler_params=_cparams(("arbitrary",)),
        name="inproj",
    )(x, mod, g, w_in, *tabs)


def _rope_tables(t, tm):
    rows = t // GRID_W
    r = np.repeat(np.arange(rows, dtype=np.float64), GRID_W)
    col = np.tile(np.arange(GRID_W, dtype=np.float64), rows)
    inv = ROPE_BASE ** (-np.arange(ROPE_PAIRS, dtype=np.float64) / ROPE_PAIRS)
    ar, ac = r[:, None] * inv, col[:, None] * inv
    cos = np.concatenate([np.cos(ar), np.cos(ar), np.cos(ac), np.cos(ac)], axis=1)
    sin = np.concatenate([-np.sin(ar), np.sin(ar), -np.sin(ac), np.sin(ac)], axis=1)
    cos = np.concatenate([cos, np.ones((tm, DA_HEAD_DIM))], axis=0)
    sin = np.concatenate([sin, np.zeros((tm, DA_HEAD_DIM))], axis=0)
    scale = DA_HEAD_DIM ** -0.5 * math.log2(math.e)
    return tuple(jnp.asarray(a, F32) for a in (cos * scale, sin * scale, cos, sin))


def _attn_kernel(lam_ref, gs_ref, q_ref, *refs, lam_init, n_sets, n_cast):
    k_refs, vt_refs = refs[0:2 * n_sets:2], refs[1:2 * n_sets:2]
    cast_in = refs[2 * n_sets:2 * n_sets + n_cast]
    o_ref = refs[2 * n_sets + n_cast]
    cast_out = refs[2 * n_sets + n_cast + 1:]
    for src, dst in zip(cast_in, cast_out):
        dst[...] = src[...].astype(BF16)

    hd = DA_HEAD_DIM
    s1 = jnp.sum(lam_ref[0:1, :] * lam_ref[1:2, :], axis=-1, keepdims=True)
    s2 = jnp.sum(lam_ref[2:3, :] * lam_ref[3:4, :], axis=-1, keepdims=True)
    lam = jnp.exp(s1) - jnp.exp(s2) + lam_init
    outs = []
    for idx in range(2):
        qi = q_ref[:, idx * hd:(idx + 1) * hd]
        ss = [lax.dot_general(kr[:, idx * hd:(idx + 1) * hd], qi, NT_DIMS, preferred_element_type=F32)
              for kr in k_refs]
        m = functools.reduce(jnp.maximum, [jnp.max(s, axis=0, keepdims=True) for s in ss])
        ps = [jnp.exp2(s - m) for s in ss]
        l = functools.reduce(jnp.add, [jnp.sum(p, axis=0, keepdims=True) for p in ps])
        acc = functools.reduce(jnp.add, [jnp.dot(vr[...], p.astype(BF16), preferred_element_type=F32)
                                         for vr, p in zip(vt_refs, ps)])
        outs.append(acc * (1.0 / l))
    o = jnp.transpose(outs[0] - outs[1] * lam)
    o_ref[...] = (_rms(o, SUBLN_EPS) * gs_ref[...] * (1.0 - lam_init)).astype(BF16)


def _attention(lam4, gs, q, kv_sets, *, lam_init, b, heads, tq, name, cast=None):
    nq, aw = q.shape
    vw = 2 * DA_HEAD_DIM
    per_b = nq // b // tq
    q_spec = pl.BlockSpec((tq, vw), lambda bi, h, i: (bi * per_b + i, h))
    in_specs = [pl.BlockSpec((4, DA_HEAD_DIM), lambda bi, h, i: (0, 0)),
                pl.BlockSpec((1, vw), lambda bi, h, i: (0, 0)),
                q_spec]
    args = [lam4, gs, q]
    for k, vt in kv_sets:
        sk = k.shape[0] // b
        in_specs += [pl.BlockSpec((sk, vw), lambda bi, h, i: (bi, h)),
                     pl.BlockSpec((vw, sk), lambda bi, h, i: (h, bi))]
        args += [k, vt]
    out_specs = [q_spec]
    out_shape = [jax.ShapeDtypeStruct((nq, aw), BF16)]
    n_cast = 0
    if cast is not None:
        layer, w_gate, w_up, w_down = cast
        _, d, dff = w_gate.shape
        n_steps = b * heads * per_b
        per = next(dv for dv in range(1, dff // LANES + 1) if (dff // LANES) % dv == 0 and dv * n_steps >= dff // LANES)
        cw = per * LANES
        nblk = dff // cw

        def blk(bi, h, i):
            return jnp.minimum((bi * heads + h) * per_b + i, nblk - 1)

        in_specs += [pl.BlockSpec((None, d, cw), lambda bi, h, i: (layer, 0, blk(bi, h, i))),
                     pl.BlockSpec((None, d, cw), lambda bi, h, i: (layer, 0, blk(bi, h, i))),
                     pl.BlockSpec((None, cw, d), lambda bi, h, i: (layer, blk(bi, h, i), 0))]
        args += [w_gate, w_up, w_down]
        out_specs += [pl.BlockSpec((d, cw), lambda bi, h, i: (0, blk(bi, h, i))),
                      pl.BlockSpec((d, cw), lambda bi, h, i: (0, blk(bi, h, i))),
                      pl.BlockSpec((cw, d), lambda bi, h, i: (blk(bi, h, i), 0))]
        out_shape += [jax.ShapeDtypeStruct((d, dff), BF16), jax.ShapeDtypeStruct((d, dff), BF16),
                      jax.ShapeDtypeStruct((dff, d), BF16)]
        n_cast = 3
    return pl.pallas_call(
        functools.partial(_attn_kernel, lam_init=lam_init, n_sets=len(kv_sets), n_cast=n_cast),
        grid=(b, heads, per_b),
        in_specs=in_specs,
        out_specs=out_specs,
        out_shape=out_shape,
        compiler_params=_cparams(("arbitrary", "arbitrary", "arbitrary")),
        name=name,
    )(*args)


def _ssm_weights(a_re, a_im, log_dt, b_re, b_im, c_re, c_im):
    L, H, P = SSM_CHUNK, SSM_GROUP, SSM_STATE
    g = a_re.shape[1]
    npair = g // 2
    hp = lax.Precision.HIGHEST
    ar, ai = a_re.astype(F32), a_im.astype(F32)
    dt = jnp.exp(log_dt.astype(F32))[..., None]
    n = jnp.arange(L + 1, dtype=F32)
    mag = jnp.exp((ar * dt)[..., None] * n)
    ang = (ai * dt)[..., None] * n
    apr, api = mag * jnp.cos(ang), mag * jnp.sin(ang)
    xr, xi = apr[..., 1] - 1.0, api[..., 1]
    den = ar * ar + ai * ai
    qr, qi = (xr * ar + xi * ai) / den, (xi * ar - xr * ai) / den
    br, bi = b_re.astype(F32), b_im.astype(F32)
    bbr = qr[..., None] * br - qi[..., None] * bi
    bbi = qr[..., None] * bi + qi[..., None] * br
    cr, ci = c_re.astype(F32), c_im.astype(F32)

    car = cr[..., None] * apr[:, :, None, :, :L] - ci[..., None] * api[:, :, None, :, :L]
    cai = cr[..., None] * api[:, :, None, :, :L] + ci[..., None] * apr[:, :, None, :, :L]
    kt = (jnp.einsum('dghpt,dgpk->dgkth', car, bbr, precision=hp)
          - jnp.einsum('dghpt,dgpk->dgkth', cai, bbi, precision=hp))
    kc = jnp.concatenate([jnp.flip(kt[1][:, :, 1:], axis=2), kt[0][:, :, :1] + kt[1][:, :, :1], kt[0][:, :, 1:]],
                         axis=2).reshape(g, H, (2 * L - 1) * H)
    w_intra = jnp.pad(kc, ((0, 0), (0, 0), (0, (-kc.shape[2]) % LANES)))

    sir = jnp.stack([jnp.flip(apr[0, ..., :L], axis=-1), apr[1, ..., :L]])
    sii = jnp.stack([jnp.flip(api[0, ..., :L], axis=-1), api[1, ..., :L]])
    sir, sii = jnp.swapaxes(sir, 2, 3)[:, :, :, None, :], jnp.swapaxes(sii, 2, 3)[:, :, :, None, :]
    tbr, tbi = jnp.swapaxes(bbr, 2, 3)[:, :, None], jnp.swapaxes(bbi, 2, 3)[:, :, None]
    w_sin = jnp.concatenate([sir * tbr - sii * tbi, sir * tbi + sii * tbr], axis=-1)
    w_sin = w_sin.reshape(2, g, L * H, 2 * P)

    sor = jnp.stack([apr[0, ..., 1:], jnp.flip(apr[1, ..., 1:], axis=-1)])
    soi = jnp.stack([api[0, ..., 1:], jnp.flip(api[1, ..., 1:], axis=-1)])
    tcr, tci = jnp.swapaxes(cr, 2, 3)[:, :, :, None, :], jnp.swapaxes(ci, 2, 3)[:, :, :, None, :]
    cnr = (tcr * sor[..., None] - tci * soi[..., None]).reshape(2, g, P, L * H)
    cni = (tcr * soi[..., None] + tci * sor[..., None]).reshape(2, g, P, L * H)
    w_so = jnp.concatenate([cnr, -cni], axis=2)

    al = jnp.stack([apr[..., L].reshape(2, npair, 2 * P), api[..., L].reshape(2, npair, 2 * P)], axis=2)
    al = al.reshape(2, 2 * npair, 1, 2 * P)
    return w_intra, w_sin.astype(BF16), w_so.astype(BF16), al


def _atom_transpose_matrix():
    n = OCT * OCT * SSM_GROUP
    i = np.arange(n)
    j = ((i // SSM_GROUP) % OCT) * LANES + (i // LANES) * SSM_GROUP + i % SSM_GROUP
    p = np.zeros((n, n), np.float32)
    p[i, j] = 1.0
    return jnp.asarray(p, BF16)


def _ssm_in_kernel(us_ref, p_ref, w_ref, ut_ref, s_ref):
    n = ut_ref.shape[1]
    noct = us_ref.shape[0]
    half = SSM_CHUNK // 2
    src = []
    for o in range(noct):
        for hf in range(2):
            rows = [us_ref[o, pl.ds(half * hf + sl, n, stride=SSM_CHUNK), :] for sl in range(half)]
            src.append(jnp.concatenate(rows, axis=1).astype(BF16))
    perm = jnp.dot(jnp.concatenate(src, axis=0), p_ref[...], preferred_element_type=F32).astype(BF16)

    def group_chunk(gi):
        o, gl = gi // OCT, gi % OCT
        return jnp.concatenate([perm[(2 * o + hf) * n:(2 * o + hf + 1) * n, gl * LANES:(gl + 1) * LANES]
                                for hf in range(2)], axis=1)

    low = lax.broadcasted_iota(jnp.int32, (n, LANES), 1) < SSM_STATE
    for j in range(noct * OCT // 2):
        ug = [group_chunk(2 * j), group_chunk(2 * j + 1)]
        ut_ref[2 * j] = ug[0]
        ut_ref[2 * j + 1] = ug[1]
        for d in range(2):
            s0 = jnp.dot(ug[0], w_ref[d, 2 * j], preferred_element_type=F32)
            s1 = jnp.dot(ug[1], w_ref[d, 2 * j + 1], preferred_element_type=F32)
            s_ref[d, 2 * j] = jnp.where(low, s0, pltpu.roll(s1, SSM_STATE, 1))
            s_ref[d, 2 * j + 1] = jnp.where(low, pltpu.roll(s0, SSM_STATE, 1), s1)


def _ssm_scan_kernel(sc_ref, sl_ref, al_ref, hc_ref, hl_ref, *, b):
    d = pl.program_id(0)
    nblk = sc_ref.shape[1]
    ars = [al_ref[0, 2 * k] for k in range(nblk // 2)]
    ais = [al_ref[0, 2 * k + 1] for k in range(nblk // 2)]

    def phase(s_ref, h_ref, carry):
        nch = s_ref.shape[2] // b

        def body(i, carry):
            c = jnp.where(d == 0, i, nch - 1 - i)
            rows = pl.ds(c, b, stride=nch)
            new = []
            for k in range(nblk // 2):
                hr, hi = carry[2 * k], carry[2 * k + 1]
                h_ref[0, 2 * k, rows, :] = hr
                h_ref[0, 2 * k + 1, rows, :] = hi
                sr = s_ref[0, 2 * k, rows, :]
                si = s_ref[0, 2 * k + 1, rows, :]
                new.append(ars[k] * hr - ais[k] * hi + sr)
                new.append(ars[k] * hi + ais[k] * hr + si)
            return tuple(new)

        return lax.fori_loop(0, nch, body, carry, unroll=4)

    carry = tuple(jnp.zeros((b, LANES), F32) for _ in range(nblk))
    carry = phase(sc_ref, hc_ref, carry)
    phase(sl_ref, hl_ref, carry)


def _ssm_out_kernel(ut_ref, h_ref, kc_ref, wo_ref, p_ref, y_ref, wi_ref):
    n = ut_ref.shape[1]
    ng = ut_ref.shape[0]
    noct = ng // OCT
    half = SSM_CHUNK // 2

    @pl.when(pl.program_id(0) == 0)
    def _():
        for gi in range(ng):
            kc = kc_ref[gi]
            wi_ref[gi] = jnp.concatenate(
                [kc[:, (SSM_CHUNK - 1 - s) * SSM_GROUP:(SSM_CHUNK - 1 - s) * SSM_GROUP + SSM_CW]
                 for s in range(SSM_CHUNK)], axis=0).astype(BF16)

    low = lax.broadcasted_iota(jnp.int32, (n, LANES), 1) < SSM_STATE
    ys = []
    for j in range(ng // 2):
        hg = [[], []]
        for d in range(2):
            hr, hi = h_ref[d, 2 * j], h_ref[d, 2 * j + 1]
            hg[0].append(jnp.where(low, hr, pltpu.roll(hi, SSM_STATE, 1)).astype(BF16))
            hg[1].append(jnp.where(low, pltpu.roll(hr, SSM_STATE, 1), hi).astype(BF16))
        for e in range(2):
            gi = 2 * j + e
            ys.append(jnp.dot(ut_ref[gi], wi_ref[gi], preferred_element_type=F32)
                      + jnp.dot(hg[e][0], wo_ref[0, gi], preferred_element_type=F32)
                      + jnp.dot(hg[e][1], wo_ref[1, gi], preferred_element_type=F32))
    rows = []
    for o in range(noct):
        for hf in range(2):
            rows.append(jnp.concatenate([ys[OCT * o + gl][:, hf * LANES:(hf + 1) * LANES] for gl in range(OCT)],
                                        axis=1))
    ycat = jnp.concatenate(rows, axis=0)
    hi = ycat.astype(BF16)
    lo = (ycat - hi.astype(F32)).astype(BF16)
    res = (jnp.dot(hi, p_ref[...], preferred_element_type=F32)
           + jnp.dot(lo, p_ref[...], preferred_element_type=F32))
    for o in range(noct):
        for hf in range(2):
            blk = 2 * o + hf
            for tl in range(half):
                y_ref[o, pl.ds(half * hf + tl, n, stride=SSM_CHUNK), :] = (
                    res[blk * n:(blk + 1) * n, tl * LANES:(tl + 1) * LANES])


def _ssm_conv(us_lat, us_ctx, weights, perm, *, layer, b):
    w_intra, w_sin, w_so, al = weights
    P = SSM_STATE
    ng = w_intra.shape[1]
    nblk = ng

    def stage_in(us, steps):
        noct, rows, _ = us.shape
        nchunks = rows // SSM_CHUNK
        n = nchunks // steps
        return pl.pallas_call(
            _ssm_in_kernel,
            grid=(steps,),
            in_specs=[pl.BlockSpec((noct, n * SSM_CHUNK, LANES), lambda i: (0, i, 0)),
                      _resident(perm.shape), _resident_layer(w_sin.shape, layer)],
            out_specs=[pl.BlockSpec((ng, n, SSM_CW), lambda i: (0, i, 0)),
                       pl.BlockSpec((2, nblk, n, LANES), lambda i: (0, 0, i, 0))],
            out_shape=[jax.ShapeDtypeStruct((ng, nchunks, SSM_CW), BF16),
                       jax.ShapeDtypeStruct((2, nblk, nchunks, LANES), F32)],
            compiler_params=_cparams(("arbitrary",)),
            name="ssm_in",
        )(us, perm, w_sin)

    lat_steps = 2 * b
    ut_lat, s_lat = stage_in(us_lat, lat_steps)
    ut_ctx, s_ctx = stage_in(us_ctx, 1)

    cb = 16
    nc_rows, nl_rows = s_ctx.shape[2], s_lat.shape[2]
    h_ctx, h_lat = pl.pallas_call(
        functools.partial(_ssm_scan_kernel, b=b),
        grid=(2, nblk // cb),
        in_specs=[pl.BlockSpec((1, cb, nc_rows, LANES), lambda d, j: (d, j, 0, 0)),
                  pl.BlockSpec((1, cb, nl_rows, LANES), lambda d, j: (d, j, 0, 0)),
                  pl.BlockSpec((None, 1, cb, 1, 2 * P), lambda d, j: (layer, d, j, 0, 0))],
        out_specs=[pl.BlockSpec((1, cb, nc_rows, LANES), lambda d, j: (d, j, 0, 0)),
                   pl.BlockSpec((1, cb, nl_rows, LANES), lambda d, j: (d, j, 0, 0))],
        out_shape=[jax.ShapeDtypeStruct(s_ctx.shape, F32), jax.ShapeDtypeStruct(s_lat.shape, F32)],
        compiler_params=_cparams(("arbitrary", "arbitrary")),
        name="ssm_scan",
    )(s_ctx, s_lat, al)

    def stage_out(ut, hin, steps):
        _, nchunks, _ = ut.shape
        n = nchunks // steps
        noct = ng // OCT
        return pl.pallas_call(
            _ssm_out_kernel,
            grid=(steps,),
            in_specs=[pl.BlockSpec((ng, n, SSM_CW), lambda i: (0, i, 0)),
                      pl.BlockSpec((2, nblk, n, LANES), lambda i: (0, 0, i, 0)),
                      _resident_layer(w_intra.shape, layer), _resident_layer(w_so.shape, layer),
                      _resident(perm.shape)],
            out_specs=pl.BlockSpec((noct, n * SSM_CHUNK, LANES), lambda i: (0, i, 0)),
            out_shape=jax.ShapeDtypeStruct((noct, nchunks * SSM_CHUNK, LANES), F32),
            scratch_shapes=[pltpu.VMEM((ng, SSM_CW, SSM_CW), BF16)],
            compiler_params=_cparams(("arbitrary",)),
            name="ssm_out",
        )(ut, hin, w_intra, w_so, perm)

    return stage_out(ut_lat, h_lat, lat_steps), stage_out(ut_ctx, h_ctx, 1)


def _fourier_kernel(ulo_ref, uup_ref, w_ref, cos_ref, sin_ref, alt_ref, s1_ref, b_ref,
                    lo_ref, up_ref, a_s, b_s, am_s, carry, *, groups):
    gc = ulo_ref.shape[1] // groups
    tm = lo_ref.shape[0]
    nt = a_s.shape[0] // tm
    row = lax.broadcasted_iota(jnp.int32, lo_ref.shape, 0)

    @pl.when(pl.program_id(1) == 0)
    def _():
        for jj in range(nt):
            pr = jnp.dot(s1_ref[...], uup_ref[(nt - 1 - jj) * tm:(nt - jj) * tm, :], preferred_element_type=F32)
            if jj > 0:
                first = uup_ref[(nt - jj) * tm:(nt - jj) * tm + am_s.shape[0], :][0:1, :].astype(F32)
                pr = jnp.where(row == 0, first, pr)
            pr = pr.astype(BF16)
            rows = slice(jj * tm, (jj + 1) * tm)
            for g in range(groups):
                cols = slice(g * gc, (g + 1) * gc)
                ab_lo = jnp.dot(ulo_ref[rows, cols], w_ref[g], preferred_element_type=F32)
                ab_pr = jnp.dot(pr[:, cols], w_ref[g], preferred_element_type=F32)
                a_s[rows, cols] = (ab_lo[:, :gc] + ab_pr[:, :gc]).astype(BF16)
                b_s[rows, cols] = (ab_lo[:, gc:] - ab_pr[:, gc:]).astype(BF16)
        for g in range(groups):
            cols = slice(g * gc, (g + 1) * gc)
            am_s[:, cols] = jnp.dot(uup_ref[0:am_s.shape[0], cols], w_ref[g, :, 0:gc], preferred_element_type=F32)
        mid = jnp.dot(alt_ref[...], a_s[...], preferred_element_type=F32)
        carry[...] = mid + am_s[0:carry.shape[0], :] + b_ref[...]

    a_mid = jnp.where((row & 1) == 0, am_s[0:1, :], -am_s[0:1, :])
    p = jnp.dot(cos_ref[...], a_s[...], preferred_element_type=F32) + a_mid + b_ref[...]
    q = jnp.dot(sin_ref[...], b_s[...], preferred_element_type=F32)
    lo_ref[...] = (p - q).astype(BF16)
    hi = (p + q).astype(BF16)
    rev = jnp.dot(s1_ref[...], hi, preferred_element_type=F32)
    up_ref[...] = jnp.where(row == 0, carry[0:1, :], rev).astype(BF16)
    carry[0:1, :] = hi[0:1, :].astype(F32)


def _dft_half_tables(n):
    half = n // 2
    k = np.arange(half, dtype=np.int64)
    ang = ((k[:, None] * k[None, :]) % n).astype(np.float64) * (2.0 * np.pi / n)
    alt = np.broadcast_to(np.where((k & 1) == 0, 1.0, -1.0), (8, half))
    return tuple(jnp.asarray(a, F32).astype(BF16) for a in (np.cos(ang), np.sin(ang), alt))


def _fourier(uf, w_cs, bias, tables, *, b, tm):
    nr, fw = uf.shape
    seq = nr // b
    half = seq // 2
    groups = w_cs.shape[0]
    cos_t, sin_t, alt = tables
    tm = min(tm, half)
    nt = half // tm
    r = np.arange(1, tm)
    s1 = np.zeros((tm, tm), np.float32)
    s1[r, tm - r] = 1.0
    s1 = jnp.asarray(s1, BF16)
    lo, up = pl.pallas_call(
        functools.partial(_fourier_kernel, groups=groups),
        grid=(b, nt),
        in_specs=[
            pl.BlockSpec((half, fw), lambda bi, i: (2 * bi, 0)),
            pl.BlockSpec((half, fw), lambda bi, i: (2 * bi + 1, 0)),
            pl.BlockSpec(w_cs.shape, lambda bi, i: (0, 0, 0)),
            pl.BlockSpec((tm, half), lambda bi, i: (nt - 1 - i, 0)),
            pl.BlockSpec((tm, half), lambda bi, i: (nt - 1 - i, 0)),
            pl.BlockSpec(alt.shape, lambda bi, i: (0, 0)),
            pl.BlockSpec((tm, tm), lambda bi, i: (0, 0)),
            pl.BlockSpec((1, fw), lambda bi, i: (0, 0)),
        ],
        out_specs=[pl.BlockSpec((tm, fw), lambda bi, i: (bi * nt + nt - 1 - i, 0)),
                   pl.BlockSpec((tm, fw), lambda bi, i: (bi * nt + i, 0))],
        out_shape=[jax.ShapeDtypeStruct((b * half, fw), BF16), jax.ShapeDtypeStruct((b * half, fw), BF16)],
        scratch_shapes=[pltpu.VMEM((half, fw), BF16), pltpu.VMEM((half, fw), BF16),
                        pltpu.VMEM((16, fw), F32), pltpu.VMEM((alt.shape[0], fw), F32)],
        compiler_params=_cparams(("arbitrary", "arbitrary")),
        name="fourier_%d" % seq,
    )(uf, uf, w_cs, cos_t, sin_t, alt, s1, bias)
    return jnp.concatenate([lo.reshape(b, half, fw), up.reshape(b, half, fw)], axis=1).reshape(nr, fw)


def _fourier_weights(w_four, seq):
    groups, gc, _ = w_four.shape
    k = np.arange(gc)
    ang = ((k[:, None] * k[None, :]) % gc).astype(np.float64) * (2.0 * np.pi / gc)
    norm = 1.0 / math.sqrt(seq * gc)
    cc = jnp.asarray(np.cos(ang) * norm, F32)
    sc = jnp.asarray(np.sin(ang) * norm, F32)
    hp = lax.Precision.HIGHEST
    wc = jnp.einsum('ck,gkd->gcd', cc, w_four.astype(F32), precision=hp)
    ws = jnp.einsum('ck,gkd->gcd', sc, w_four.astype(F32), precision=hp)
    return jnp.concatenate([wc, ws], axis=-1).astype(BF16)


def _outproj_kernel(att_ref, us_ref, yc_ref, uf_ref, x_ref, mod_ref, wo_ref, wg_ref, bg_ref, dsk_ref,
                    gpost_ref, gpre_ref, xo_ref, h2_ref, *, aw, sw):
    nblk = us_ref.shape[0]
    tm = x_ref.shape[0]
    gate_g = mod_ref[0, 2:3, :] * gpost_ref[...]
    scale_g = gpre_ref[...] * (1.0 + mod_ref[0, 4:5, :])
    for r0 in range(0, tm, tm // 2):
        rows = slice(r0, r0 + tm // 2)
        us = jnp.concatenate([us_ref[j, rows, :] for j in range(nblk)], axis=1)
        yc = jnp.concatenate([yc_ref[j, rows, :] for j in range(nblk)], axis=1)
        g = jax.nn.gelu(dsk_ref[...] * us + yc)
        z = jnp.dot(g.astype(BF16), wg_ref[...], preferred_element_type=F32) + bg_ref[...]
        ssm = (g * jax.nn.sigmoid(z)).astype(BF16)
        mix = (jnp.dot(att_ref[rows, :], wo_ref[0:aw, :], preferred_element_type=F32)
               + jnp.dot(ssm, wo_ref[aw:aw + sw, :], preferred_element_type=F32)
               + jnp.dot(uf_ref[rows, :], wo_ref[aw + sw:, :], preferred_element_type=F32))
        xn = x_ref[rows, :] + _rms(mix, NORM_EPS) * gate_g
        xo_ref[rows, :] = xn
        h2_ref[rows, :] = (_rms(xn, NORM_EPS) * scale_g + mod_ref[0, 3:4, :]).astype(BF16)


def _outproj(att, us, yc, four, x, mod, w_out, w_glu, b_glu, dsk, g_post, g_pre, *, layer, mod_row, tm):
    n_rows, d = x.shape
    aw, fw = att.shape[1], four.shape[1]
    nblk = us.shape[0]
    sw = nblk * LANES

    def row(i):
        return (i, 0)

    def const(i):
        return (0, 0)

    blk3 = pl.BlockSpec((nblk, tm, LANES), lambda i: (0, i, 0))
    return pl.pallas_call(
        functools.partial(_outproj_kernel, aw=aw, sw=sw),
        grid=(n_rows // tm,),
        in_specs=[
            pl.BlockSpec((tm, aw), row), blk3, blk3,
            pl.BlockSpec((tm, fw), row), pl.BlockSpec((tm, d), row),
            pl.BlockSpec((1, N_MOD, d), lambda i: (mod_row(i), 0, 0)),
            _resident_layer(w_out.shape, layer), _resident_layer(w_glu.shape, layer),
            pl.BlockSpec((1, sw), const), pl.BlockSpec((1, sw), const),
            pl.BlockSpec((1, d), const), pl.BlockSpec((1, d), const),
        ],
        out_specs=[pl.BlockSpec((tm, d), row), pl.BlockSpec((tm, d), row)],
        out_shape=[jax.ShapeDtypeStruct((n_rows, d), F32), jax.ShapeDtypeStruct((n_rows, d), BF16)],
        compiler_params=_cparams(("arbitrary",)),
        name="outproj",
    )(att, us, yc, four, x, mod, w_out, w_glu, b_glu, dsk, g_post, g_pre)


def _ffn_kernel(h_ref, x_ref, mod_ref, g_ref, wg_ref, wu_ref, wd_ref, o_ref):
    k = pl.program_id(1)

    @pl.when(k == 0)
    def _():
        o_ref[...] = jnp.zeros(o_ref.shape, F32)

    h = h_ref[...]
    tf = wg_ref.shape[1]
    part = None
    for c0 in range(0, tf, tf // 2):
        cols = slice(c0, c0 + tf // 2)
        a = jnp.dot(h, wg_ref[:, cols], preferred_element_type=F32)
        u = jnp.dot(h, wu_ref[:, cols], preferred_element_type=F32)
        p = jnp.dot((_silu(a) * u).astype(BF16), wd_ref[cols, :], preferred_element_type=F32)
        part = p if part is None else part + p
    o_ref[...] += part

    @pl.when(k == pl.num_programs(1) - 1)
    def _():
        o_ref[...] = x_ref[...] + _rms(o_ref[...], NORM_EPS) * (mod_ref[0, 5:6, :] * g_ref[...])


def _ffn(h2, x_mid, mod, g_post, w_gate, w_up, w_down, *, mod_row, tm, tf):
    n_rows, d = h2.shape
    dff = w_gate.shape[1]
    return pl.pallas_call(
        _ffn_kernel,
        grid=(n_rows // tm, dff // tf),
        in_specs=[
            pl.BlockSpec((tm, d), lambda i, k: (i, 0)),
            pl.BlockSpec((tm, d), lambda i, k: (i, 0)),
            pl.BlockSpec((1, N_MOD, d), lambda i, k: (mod_row(i), 0, 0)),
            pl.BlockSpec((1, d), lambda i, k: (0, 0)),
            pl.BlockSpec((d, tf), lambda i, k: (0, k)),
            pl.BlockSpec((d, tf), lambda i, k: (0, k)),
            pl.BlockSpec((tf, d), lambda i, k: (k, 0)),
        ],
        out_specs=pl.BlockSpec((tm, d), lambda i, k: (i, 0)),
        out_shape=jax.ShapeDtypeStruct((n_rows, d), F32),
        compiler_params=_cparams(("arbitrary", "arbitrary")),
        name="ffn",
    )(h2, x_mid, mod, g_post, w_gate, w_up, w_down)


def kernel(x, c, ctx, c_ctx, w_mod, b_mod, g_mix_pre, g_mix_post, g_ffn_pre, g_ffn_post, w_in, w_out, lam_q1, lam_k1, lam_q2, lam_k2, g_subln, ssm_a_re, ssm_a_im, ssm_log_dt, ssm_b_re, ssm_b_im, ssm_c_re, ssm_c_im, ssm_d, w_glu, b_glu, w_four, b_four, w_gate, w_up, w_down):
    b, t, d = x.shape
    n_ctx = ctx.shape[1]
    depth = w_mod.shape[0]
    nl, nc = b * t, b * n_ctx
    aw = d // 2
    sw = ssm_d.shape[1]
    fw = d - aw - sw
    heads = aw // (2 * DA_HEAD_DIM)
    tm = 512
    assert t % tm == 0 and nc % tm == 0 and n_ctx % SSM_CHUNK == 0 and t % GRID_W == 0 and b + 1 <= MOD_ROWS
    assert sw % LANES == 0 and (sw // SSM_GROUP) % (2 * OCT) == 0

    cs = jnp.concatenate([c, c_ctx[None, :], jnp.zeros((MOD_ROWS - b - 1, d), F32)], axis=0)
    mod_all = _modulation(cs, w_mod, b_mod).reshape(depth, MOD_ROWS, N_MOD, d)

    per_b = t // tm
    lat_mod = lambda i: i // per_b
    ctx_mod = lambda i: b
    tabs = _rope_tables(t, tm)
    dft_lat = _dft_half_tables(t)
    dft_ctx = _dft_half_tables(n_ctx)
    perm = _atom_transpose_matrix()
    x_lat, x_ctx = x.reshape(nl, d), ctx.reshape(nc, d)
    w_in_b, w_out_b, w_glu_b = w_in.astype(BF16), w_out.astype(BF16), w_glu.astype(BF16)
    ssm_w = jax.vmap(_ssm_weights)(ssm_a_re, ssm_a_im, ssm_log_dt, ssm_b_re, ssm_b_im, ssm_c_re, ssm_c_im)

    for l in range(depth):
        need_ctx = l < depth - 1
        lam_init = 0.8 - 0.6 * math.exp(-0.3 * l)
        mod = mod_all[l]
        g_pre = g_mix_pre[l][None, :]
        q, k, vt, us, uf = _inproj(x_lat, mod, g_pre, w_in_b, tabs, layer=l, mod_row=lat_mod,
                                   tab_blk=lambda i: i % per_b, aw=aw, sw=sw, fw=fw, tm=tm)
        qc, kc, vtc, usc, ufc = _inproj(x_ctx, mod, g_pre, w_in_b, tabs, layer=l, mod_row=ctx_mod,
                                        tab_blk=lambda i: per_b, aw=aw, sw=sw, fw=fw, tm=tm)

        lam4 = jnp.stack([lam_q1[l], lam_k1[l], lam_q2[l], lam_k2[l]]).astype(F32)
        gs = g_subln[l][None, :].astype(F32)
        att, *ffn_wb = _attention(lam4, gs, q, [(kc, vtc), (k, vt)], lam_init=lam_init, b=b, heads=heads, tq=1024,
                                  name="attn_latent", cast=(l, w_gate, w_up, w_down))
        yc, ycc = _ssm_conv(us, usc, ssm_w, perm, layer=l, b=b)
        bias = b_four[l].reshape(1, fw).astype(F32)
        four = _fourier(uf, _fourier_weights(w_four[l], t), bias, dft_lat, b=b, tm=tm)

        small = (b_glu[l][None, :].astype(F32), ssm_d[l][None, :].astype(F32), g_mix_post[l][None, :],
                 g_ffn_pre[l][None, :])
        ffn_w = (g_ffn_post[l][None, :], *ffn_wb)
        x_mid, h2 = _outproj(att, us, yc, four, x_lat, mod, w_out_b, w_glu_b, *small, layer=l, mod_row=lat_mod,
                             tm=tm)
        x_lat = _ffn(h2, x_mid, mod, *ffn_w, mod_row=lat_mod, tm=tm, tf=512)

        if need_ctx:
            att_c, = _attention(lam4, gs, qc, [(kc, vtc)], lam_init=lam_init, b=b, heads=heads, tq=n_ctx,
                                name="attn_ctx")
            four_c = _fourier(ufc, _fourier_weights(w_four[l], n_ctx), bias, dft_ctx, b=b, tm=tm)
            xc_mid, h2c = _outproj(att_c, usc, ycc, four_c, x_ctx, mod, w_out_b, w_glu_b, *small, layer=l,
                                   mod_row=ctx_mod, tm=tm)
            x_ctx = _ffn(h2c, xc_mid, mod, *ffn_w, mod_row=ctx_mod, tm=tm, tf=512)
    return x_lat.reshape(b, t, d)
```

```python
import functools
import math

import jax
import jax.numpy as jnp
import numpy as np
from jax import lax
from jax.experimental import pallas as pl
from jax.experimental.pallas import tpu as pltpu

F32 = jnp.float32
BF16 = jnp.bfloat16

LANES = 128
GRID_W = 64
DA_HEAD_DIM = 128
SSM_GROUP = 16
SSM_STATE = 64
N_MOD = 6
ROPE_BASE = 10000.0
ROPE_PAIRS = DA_HEAD_DIM // 4
NORM_EPS = 1e-6
SUBLN_EPS = 1e-5

SSM_CHUNK = 16
SSM_CW = SSM_CHUNK * SSM_GROUP
OCT = LANES // SSM_GROUP
MOD_ROWS = 8
VMEM_LIMIT = 56 * 1024 * 1024
NT_DIMS = (((1,), (1,)), ((), ()))


def _cparams(sem):
    return pltpu.CompilerParams(dimension_semantics=sem, vmem_limit_bytes=VMEM_LIMIT)


def _rms(x, eps):
    return x * lax.rsqrt(jnp.mean(x * x, axis=-1, keepdims=True) + eps)


def _silu(x):
    return x * jax.nn.sigmoid(x)


def _resident(shape):
    nd = len(shape)
    return pl.BlockSpec(shape, lambda *_: (0,) * nd, pipeline_mode=pl.Buffered(1))


def _resident_layer(shape, layer):
    nd = len(shape)
    return pl.BlockSpec((None,) + tuple(shape[1:]), lambda *_: (layer,) + (0,) * (nd - 1),
                        pipeline_mode=pl.Buffered(1))


def _mod_kernel(c_ref, w_ref, b_ref, o_ref):
    s = _silu(c_ref[...]).astype(BF16)
    o_ref[0] = jnp.dot(s, w_ref[0].astype(BF16), preferred_element_type=F32) + b_ref[0]


def _modulation(cs, w_mod, b_mod, tn=1024):
    depth, d, n = w_mod.shape
    return pl.pallas_call(
        _mod_kernel,
        grid=(depth, n // tn),
        in_specs=[
            pl.BlockSpec((MOD_ROWS, d), lambda l, j: (0, 0)),
            pl.BlockSpec((1, d, tn), lambda l, j: (l, 0, j)),
            pl.BlockSpec((1, 1, tn), lambda l, j: (l, 0, j)),
        ],
        out_specs=pl.BlockSpec((1, MOD_ROWS, tn), lambda l, j: (l, 0, j)),
        out_shape=jax.ShapeDtypeStruct((depth, MOD_ROWS, n), F32),
        compiler_params=_cparams(("arbitrary", "arbitrary")),
        name="modulation",
    )(cs, w_mod, b_mod.reshape(depth, 1, n))


def _inproj_kernel(x_ref, mod_ref, g_ref, w_ref, cq_ref, sq_ref, ck_ref, sk_ref,
                   q_ref, k_ref, vt_ref, us_ref, uf_ref, *, aw, sw):
    tm = x_ref.shape[0]
    hb = (_rms(x_ref[...], NORM_EPS) * (g_ref[...] * (1.0 + mod_ref[0, 1:2, :])) + mod_ref[0, 0:1, :]).astype(BF16)

    lane = lax.broadcasted_iota(jnp.int32, (tm, DA_HEAD_DIM), 1)
    first_half = (lane & (2 * ROPE_PAIRS - 1)) < ROPE_PAIRS

    def rope(z, c, s):
        partner = jnp.where(first_half, pltpu.roll(z, DA_HEAD_DIM - ROPE_PAIRS, 1), pltpu.roll(z, ROPE_PAIRS, 1))
        return z * c + partner * s

    zq = jnp.dot(hb, w_ref[:, 0:aw], preferred_element_type=F32)
    cq, sq = cq_ref[...], sq_ref[...]
    for j in range(aw // DA_HEAD_DIM):
        sl = slice(j * DA_HEAD_DIM, (j + 1) * DA_HEAD_DIM)
        q_ref[:, sl] = rope(zq[:, sl], cq, sq).astype(BF16)
    zk = jnp.dot(hb, w_ref[:, aw:2 * aw], preferred_element_type=F32)
    ck, sk = ck_ref[...], sk_ref[...]
    for j in range(aw // DA_HEAD_DIM):
        sl = slice(j * DA_HEAD_DIM, (j + 1) * DA_HEAD_DIM)
        k_ref[:, sl] = rope(zk[:, sl], ck, sk).astype(BF16)
    zv = jnp.dot(hb, w_ref[:, 2 * aw:3 * aw], preferred_element_type=F32)
    vt_ref[...] = jnp.transpose(zv.astype(BF16))
    zs = jnp.dot(hb, w_ref[:, 3 * aw:3 * aw + sw], preferred_element_type=F32)
    for j in range(sw // LANES):
        us_ref[j] = zs[:, j * LANES:(j + 1) * LANES]
    uf_ref[...] = jnp.dot(hb, w_ref[:, 3 * aw + sw:], preferred_element_type=F32).astype(BF16)


def _inproj(x, mod, g, w_in, tabs, *, layer, mod_row, tab_blk, aw, sw, fw, tm):
    nr, d = x.shape

    def row(i):
        return (i, 0)

    tab_spec = pl.BlockSpec((tm, DA_HEAD_DIM), lambda i: (tab_blk(i), 0))
    return pl.pallas_call(
        functools.partial(_inproj_kernel, aw=aw, sw=sw),
        grid=(nr // tm,),
        in_specs=[
            pl.BlockSpec((tm, d), row),
            pl.BlockSpec((1, N_MOD, d), lambda i: (mod_row(i), 0, 0)),
            pl.BlockSpec((1, d), lambda i: (0, 0)),
            _resident_layer(w_in.shape, layer),
            tab_spec, tab_spec, tab_spec, tab_spec,
        ],
        out_specs=[
            pl.BlockSpec((tm, aw), row), pl.BlockSpec((tm, aw), row), pl.BlockSpec((aw, tm), lambda i: (0, i)),
            pl.BlockSpec((sw // LANES, tm, LANES), lambda i: (0, i, 0)), pl.BlockSpec((tm, fw), row),
        ],
        out_shape=[
            jax.ShapeDtypeStruct((nr, aw), BF16), jax.ShapeDtypeStruct((nr, aw), BF16),
            jax.ShapeDtypeStruct((aw, nr), BF16), jax.ShapeDtypeStruct((sw // LANES, nr, LANES), F32),
            jax.ShapeDtypeStruct((nr, fw), BF16),
        ],
        compiler_params=_cparams(("arbitrary",)),
        name="inproj",
    )(x, mod, g, w_in, *tabs)


def _rope_tables(t, tm):
    rows = t // GRID_W
    r = np.repeat(np.arange(rows, dtype=np.float64), GRID_W)
    col = np.tile(np.arange(GRID_W, dtype=np.float64), rows)
    inv = ROPE_BASE ** (-np.arange(ROPE_PAIRS, dtype=np.float64) / ROPE_PAIRS)
    ar, ac = r[:, None] * inv, col[:, None] * inv
    cos = np.concatenate([np.cos(ar), np.cos(ar), np.cos(ac), np.cos(ac)], axis=1)
    sin = np.concatenate([-np.sin(ar), np.sin(ar), -np.sin(ac), np.sin(ac)], axis=1)
    cos = np.concatenate([cos, np.ones((tm, DA_HEAD_DIM))], axis=0)
    sin = np.concatenate([sin, np.zeros((tm, DA_HEAD_DIM))], axis=0)
    scale = DA_HEAD_DIM ** -0.5 * math.log2(math.e)
    return tuple(jnp.asarray(a, F32) for a in (cos * scale, sin * scale, cos, sin))


def _attn_kernel(lam_ref, gs_ref, q_ref, *refs, lam_init, n_sets, n_cast):
    k_refs, vt_refs = refs[0:2 * n_sets:2], refs[1:2 * n_sets:2]
    cast_in = refs[2 * n_sets:2 * n_sets + n_cast]
    o_ref = refs[2 * n_sets + n_cast]
    cast_out = refs[2 * n_sets + n_cast + 1:]
    for src, dst in zip(cast_in, cast_out):
        dst[...] = src[...].astype(BF16)

    hd = DA_HEAD_DIM
    s1 = jnp.sum(lam_ref[0:1, :] * lam_ref[1:2, :], axis=-1, keepdims=True)
    s2 = jnp.sum(lam_ref[2:3, :] * lam_ref[3:4, :], axis=-1, keepdims=True)
    lam = jnp.exp(s1) - jnp.exp(s2) + lam_init
    outs = []
    for idx in range(2):
        qi = q_ref[:, idx * hd:(idx + 1) * hd]
        ss = [lax.dot_general(kr[:, idx * hd:(idx + 1) * hd], qi, NT_DIMS, preferred_element_type=F32)
              for kr in k_refs]
        m = functools.reduce(jnp.maximum, [jnp.max(s, axis=0, keepdims=True) for s in ss])
        ps = [jnp.exp2(s - m) for s in ss]
        l = functools.reduce(jnp.add, [jnp.sum(p, axis=0, keepdims=True) for p in ps])
        acc = functools.reduce(jnp.add, [jnp.dot(vr[...], p.astype(BF16), preferred_element_type=F32)
                                         for vr, p in zip(vt_refs, ps)])
        outs.append(acc * (1.0 / l))
    o = jnp.transpose(outs[0] - outs[1] * lam)
    o_ref[...] = (_rms(o, SUBLN_EPS) * gs_ref[...] * (1.0 - lam_init)).astype(BF16)


def _attention(lam4, gs, q, kv_sets, *, lam_init, b, heads, tq, name, cast=None):
    nq, aw = q.shape
    vw = 2 * DA_HEAD_DIM
    per_b = nq // b // tq
    q_spec = pl.BlockSpec((tq, vw), lambda bi, h, i: (bi * per_b + i, h))
    in_specs = [pl.BlockSpec((4, DA_HEAD_DIM), lambda bi, h, i: (0, 0)),
                pl.BlockSpec((1, vw), lambda bi, h, i: (0, 0)),
                q_spec]
    args = [lam4, gs, q]
    for k, vt in kv_sets:
        sk = k.shape[0] // b
        in_specs += [pl.BlockSpec((sk, vw), lambda bi, h, i: (bi, h)),
                     pl.BlockSpec((vw, sk), lambda bi, h, i: (h, bi))]
        args += [k, vt]
    out_specs = [q_spec]
    out_shape = [jax.ShapeDtypeStruct((nq, aw), BF16)]
    n_cast = 0
    if cast is not None:
        layer, w_gate, w_up, w_down = cast
        _, d, dff = w_gate.shape
        n_steps = b * heads * per_b
        per = next(dv for dv in range(1, dff // LANES + 1) if (dff // LANES) % dv == 0 and dv * n_steps >= dff // LANES)
        cw = per * LANES
        nblk = dff // cw

        def blk(bi, h, i):
            return jnp.minimum((bi * heads + h) * per_b + i, nblk - 1)

        in_specs += [pl.BlockSpec((None, d, cw), lambda bi, h, i: (layer, 0, blk(bi, h, i))),
                     pl.BlockSpec((None, d, cw), lambda bi, h, i: (layer, 0, blk(bi, h, i))),
                     pl.BlockSpec((None, cw, d), lambda bi, h, i: (layer, blk(bi, h, i), 0))]
        args += [w_gate, w_up, w_down]
        out_specs += [pl.BlockSpec((d, cw), lambda bi, h, i: (0, blk(bi, h, i))),
                      pl.BlockSpec((d, cw), lambda bi, h, i: (0, blk(bi, h, i))),
                      pl.BlockSpec((cw, d), lambda bi, h, i: (blk(bi, h, i), 0))]
        out_shape += [jax.ShapeDtypeStruct((d, dff), BF16), jax.ShapeDtypeStruct((d, dff), BF16),
                      jax.ShapeDtypeStruct((dff, d), BF16)]
        n_cast = 3
    return pl.pallas_call(
        functools.partial(_attn_kernel, lam_init=lam_init, n_sets=len(kv_sets), n_cast=n_cast),
        grid=(b, heads, per_b),
        in_specs=in_specs,
        out_specs=out_specs,
        out_shape=out_shape,
        compiler_params=_cparams(("arbitrary", "arbitrary", "arbitrary")),
        name=name,
    )(*args)


def _ssm_weights(a_re, a_im, log_dt, b_re, b_im, c_re, c_im):
    L, H, P = SSM_CHUNK, SSM_GROUP, SSM_STATE
    g = a_re.shape[1]
    npair = g // 2
    hp = lax.Precision.HIGHEST
    ar, ai = a_re.astype(F32), a_im.astype(F32)
    dt = jnp.exp(log_dt.astype(F32))[..., None]
    n = jnp.arange(L + 1, dtype=F32)
    mag = jnp.exp((ar * dt)[..., None] * n)
    ang = (ai * dt)[..., None] * n
    apr, api = mag * jnp.cos(ang), mag * jnp.sin(ang)
    xr, xi = apr[..., 1] - 1.0, api[..., 1]
    den = ar * ar + ai * ai
    qr, qi = (xr * ar + xi * ai) / den, (xi * ar - xr * ai) / den
    br, bi = b_re.astype(F32), b_im.astype(F32)
    bbr = qr[..., None] * br - qi[..., None] * bi
    bbi = qr[..., None] * bi + qi[..., None] * br
    cr, ci = c_re.astype(F32), c_im.astype(F32)

    car = cr[..., None] * apr[:, :, None, :, :L] - ci[..., None] * api[:, :, None, :, :L]
    cai = cr[..., None] * api[:, :, None, :, :L] + ci[..., None] * apr[:, :, None, :, :L]
    kt = (jnp.einsum('dghpt,dgpk->dgkth', car, bbr, precision=hp)
          - jnp.einsum('dghpt,dgpk->dgkth', cai, bbi, precision=hp))
    kc = jnp.concatenate([jnp.flip(kt[1][:, :, 1:], axis=2), kt[0][:, :, :1] + kt[1][:, :, :1], kt[0][:, :, 1:]],
                         axis=2).reshape(g, H, (2 * L - 1) * H)
    w_intra = jnp.pad(kc, ((0, 0), (0, 0), (0, (-kc.shape[2]) % LANES)))

    sir = jnp.stack([jnp.flip(apr[0, ..., :L], axis=-1), apr[1, ..., :L]])
    sii = jnp.stack([jnp.flip(api[0, ..., :L], axis=-1), api[1, ..., :L]])
    sir, sii = jnp.swapaxes(sir, 2, 3)[:, :, :, None, :], jnp.swapaxes(sii, 2, 3)[:, :, :, None, :]
    tbr, tbi = jnp.swapaxes(bbr, 2, 3)[:, :, None], jnp.swapaxes(bbi, 2, 3)[:, :, None]
    w_sin = jnp.concatenate([sir * tbr - sii * tbi, sir * tbi + sii * tbr], axis=-1)
    w_sin = w_sin.reshape(2, g, L * H, 2 * P)

    sor = jnp.stack([apr[0, ..., 1:], jnp.flip(apr[1, ..., 1:], axis=-1)])
    soi = jnp.stack([api[0, ..., 1:], jnp.flip(api[1, ..., 1:], axis=-1)])
    tcr, tci = jnp.swapaxes(cr, 2, 3)[:, :, :, None, :], jnp.swapaxes(ci, 2, 3)[:, :, :, None, :]
    cnr = (tcr * sor[..., None] - tci * soi[..., None]).reshape(2, g, P, L * H)
    cni = (tcr * soi[..., None] + tci * sor[..., None]).reshape(2, g, P, L * H)
    w_so = jnp.concatenate([cnr, -cni], axis=2)

    al = jnp.stack([apr[..., L].reshape(2, npair, 2 * P), api[..., L].reshape(2, npair, 2 * P)], axis=2)
    al = al.reshape(2, 2 * npair, 1, 2 * P)
    return w_intra, w_sin.astype(BF16), w_so.astype(BF16), al


def _atom_transpose_matrix():
    n = OCT * OCT * SSM_GROUP
    i = np.arange(n)
    j = ((i // SSM_GROUP) % OCT) * LANES + (i // LANES) * SSM_GROUP + i % SSM_GROUP
    p = np.zeros((n, n), np.float32)
    p[i, j] = 1.0
    return jnp.asarray(p, BF16)


def _ssm_in_kernel(us_ref, p_ref, w_ref, ut_ref, s_ref):
    noct, nb = us_ref.shape[0], us_ref.shape[1]
    nc = us_ref.shape[2] // SSM_CHUNK
    n = nb * nc
    half = SSM_CHUNK // 2
    src = []
    for o in range(noct):
        for hf in range(2):
            per_seq = [jnp.concatenate([us_ref[o, bi, pl.ds(half * hf + sl, nc, stride=SSM_CHUNK), :]
                                        for sl in range(half)], axis=1) for bi in range(nb)]
            src.append(jnp.concatenate(per_seq, axis=0).astype(BF16))
    perm = jnp.dot(jnp.concatenate(src, axis=0), p_ref[...], preferred_element_type=F32).astype(BF16)

    def group_chunk(gi):
        o, gl = gi // OCT, gi % OCT
        return jnp.concatenate([perm[(2 * o + hf) * n:(2 * o + hf + 1) * n, gl * LANES:(gl + 1) * LANES]
                                for hf in range(2)], axis=1)

    def store_chunk_major(d, blk, val):
        for bi in range(nb):
            s_ref[d, blk, pl.ds(bi, nc, stride=nb), :] = val[bi * nc:(bi + 1) * nc]

    low = lax.broadcasted_iota(jnp.int32, (n, LANES), 1) < SSM_STATE
    for j in range(noct * OCT // 2):
        ug = [group_chunk(2 * j), group_chunk(2 * j + 1)]
        ut_ref[2 * j] = ug[0]
        ut_ref[2 * j + 1] = ug[1]
        for d in range(2):
            s0 = jnp.dot(ug[0], w_ref[d, 2 * j], preferred_element_type=F32)
            s1 = jnp.dot(ug[1], w_ref[d, 2 * j + 1], preferred_element_type=F32)
            store_chunk_major(d, 2 * j, jnp.where(low, s0, pltpu.roll(s1, SSM_STATE, 1)))
            store_chunk_major(d, 2 * j + 1, jnp.where(low, pltpu.roll(s0, SSM_STATE, 1), s1))


def _ssm_scan_kernel(sc_ref, sl_ref, al_ref, hc_ref, hl_ref, *, b):
    d = pl.program_id(0)
    nblk = sc_ref.shape[1]
    tile = 8
    cpt = tile // b
    grp = lax.broadcasted_iota(jnp.int32, (tile, LANES), 0) // b
    ars = [al_ref[0, 2 * k] for k in range(nblk // 2)]
    ais = [al_ref[0, 2 * k + 1] for k in range(nblk // 2)]

    def spread(x, g):
        x = jnp.where(grp == g, x, 0.0)
        out = x
        for r in range(1, cpt):
            out = out + pltpu.roll(x, r * b, 0)
        return out

    def phase(s_ref, h_ref, carry):
        ntile = s_ref.shape[2] // tile

        def body(i, carry):
            t = jnp.where(d == 0, i, ntile - 1 - i)
            rows = pl.ds(pl.multiple_of(t * tile, tile), tile)
            new = []
            for k in range(nblk // 2):
                hr, hi = carry[2 * k], carry[2 * k + 1]
                sr, si = s_ref[0, 2 * k, rows, :], s_ref[0, 2 * k + 1, rows, :]
                hin_r, hin_i = hr, hi
                for step in range(cpt):
                    g = jnp.where(d == 0, step, cpt - 1 - step)
                    hin_r = jnp.where(grp == g, hr, hin_r)
                    hin_i = jnp.where(grp == g, hi, hin_i)
                    sgr, sgi = spread(sr, g), spread(si, g)
                    hr, hi = ars[k] * hr - ais[k] * hi + sgr, ars[k] * hi + ais[k] * hr + sgi
                h_ref[0, 2 * k, rows, :] = hin_r
                h_ref[0, 2 * k + 1, rows, :] = hin_i
                new += [hr, hi]
            return tuple(new)

        return lax.fori_loop(0, ntile, body, carry, unroll=2)

    carry = tuple(jnp.zeros((tile, LANES), F32) for _ in range(nblk))
    carry = phase(sc_ref, hc_ref, carry)
    phase(sl_ref, hl_ref, carry)


def _ssm_out_kernel(ut_ref, h_ref, kc_ref, wo_ref, p_ref, y_ref, wi_ref):
    n = ut_ref.shape[1]
    ng = ut_ref.shape[0]
    nb = y_ref.shape[1]
    nc = n // nb
    noct = ng // OCT
    half = SSM_CHUNK // 2

    @pl.when(pl.program_id(0) == 0)
    def _():
        for gi in range(ng):
            kc = kc_ref[gi]
            wi_ref[gi] = jnp.concatenate(
                [kc[:, (SSM_CHUNK - 1 - s) * SSM_GROUP:(SSM_CHUNK - 1 - s) * SSM_GROUP + SSM_CW]
                 for s in range(SSM_CHUNK)], axis=0).astype(BF16)

    def load_seq_major(d, blk):
        return jnp.concatenate([h_ref[d, blk, pl.ds(bi, nc, stride=nb), :] for bi in range(nb)], axis=0)

    low = lax.broadcasted_iota(jnp.int32, (n, LANES), 1) < SSM_STATE
    ys = []
    for j in range(ng // 2):
        hg = [[], []]
        for d in range(2):
            hr, hi = load_seq_major(d, 2 * j), load_seq_major(d, 2 * j + 1)
            hg[0].append(jnp.where(low, hr, pltpu.roll(hi, SSM_STATE, 1)).astype(BF16))
            hg[1].append(jnp.where(low, pltpu.roll(hr, SSM_STATE, 1), hi).astype(BF16))
        for e in range(2):
            gi = 2 * j + e
            ys.append(jnp.dot(ut_ref[gi], wi_ref[gi], preferred_element_type=F32)
                      + jnp.dot(hg[e][0], wo_ref[0, gi], preferred_element_type=F32)
                      + jnp.dot(hg[e][1], wo_ref[1, gi], preferred_element_type=F32))
    rows = []
    for o in range(noct):
        for hf in range(2):
            rows.append(jnp.concatenate([ys[OCT * o + gl][:, hf * LANES:(hf + 1) * LANES] for gl in range(OCT)],
                                        axis=1))
    ycat = jnp.concatenate(rows, axis=0)
    hi = ycat.astype(BF16)
    lo = (ycat - hi.astype(F32)).astype(BF16)
    res = (jnp.dot(hi, p_ref[...], preferred_element_type=F32)
           + jnp.dot(lo, p_ref[...], preferred_element_type=F32))
    for o in range(noct):
        for hf in range(2):
            blk = 2 * o + hf
            for tl in range(half):
                piece = res[blk * n:(blk + 1) * n, tl * LANES:(tl + 1) * LANES]
                for bi in range(nb):
                    y_ref[o, bi, pl.ds(half * hf + tl, nc, stride=SSM_CHUNK), :] = piece[bi * nc:(bi + 1) * nc]


def _ssm_conv(us_lat, us_ctx, weights, perm, *, layer, b):
    w_intra, w_sin, w_so, al = weights
    P = SSM_STATE
    ng = w_intra.shape[1]
    nblk = ng
    assert 8 % b == 0

    def stage_in(us, steps):
        noct, rows, _ = us.shape
        seq = rows // b
        tok = seq // steps
        n = b * tok // SSM_CHUNK
        nchunks = rows // SSM_CHUNK
        return pl.pallas_call(
            _ssm_in_kernel,
            grid=(steps,),
            in_specs=[pl.BlockSpec((noct, b, tok, LANES), lambda i: (0, 0, i, 0)),
                      _resident(perm.shape), _resident_layer(w_sin.shape, layer)],
            out_specs=[pl.BlockSpec((ng, n, SSM_CW), lambda i: (0, i, 0)),
                       pl.BlockSpec((2, nblk, n, LANES), lambda i: (0, 0, i, 0))],
            out_shape=[jax.ShapeDtypeStruct((ng, nchunks, SSM_CW), BF16),
                       jax.ShapeDtypeStruct((2, nblk, nchunks, LANES), F32)],
            compiler_params=_cparams(("arbitrary",)),
            name="ssm_in",
        )(us.reshape(noct, b, seq, LANES), perm, w_sin)

    lat_steps = 2 * b
    ut_lat, s_lat = stage_in(us_lat, lat_steps)
    ut_ctx, s_ctx = stage_in(us_ctx, 1)

    cb = 16
    nc_rows, nl_rows = s_ctx.shape[2], s_lat.shape[2]
    h_ctx, h_lat = pl.pallas_call(
        functools.partial(_ssm_scan_kernel, b=b),
        grid=(2, nblk // cb),
        in_specs=[pl.BlockSpec((1, cb, nc_rows, LANES), lambda d, j: (d, j, 0, 0)),
                  pl.BlockSpec((1, cb, nl_rows, LANES), lambda d, j: (d, j, 0, 0)),
                  pl.BlockSpec((None, 1, cb, 1, 2 * P), lambda d, j: (layer, d, j, 0, 0))],
        out_specs=[pl.BlockSpec((1, cb, nc_rows, LANES), lambda d, j: (d, j, 0, 0)),
                   pl.BlockSpec((1, cb, nl_rows, LANES), lambda d, j: (d, j, 0, 0))],
        out_shape=[jax.ShapeDtypeStruct(s_ctx.shape, F32), jax.ShapeDtypeStruct(s_lat.shape, F32)],
        compiler_params=_cparams(("arbitrary", "arbitrary")),
        name="ssm_scan",
    )(s_ctx, s_lat, al)

    def stage_out(ut, hin, steps):
        _, nchunks, _ = ut.shape
        n = nchunks // steps
        noct = ng // OCT
        seq = nchunks * SSM_CHUNK // b
        tok = seq // steps
        y = pl.pallas_call(
            _ssm_out_kernel,
            grid=(steps,),
            in_specs=[pl.BlockSpec((ng, n, SSM_CW), lambda i: (0, i, 0)),
                      pl.BlockSpec((2, nblk, n, LANES), lambda i: (0, 0, i, 0)),
                      _resident_layer(w_intra.shape, layer), _resident_layer(w_so.shape, layer),
                      _resident(perm.shape)],
            out_specs=pl.BlockSpec((noct, b, tok, LANES), lambda i: (0, 0, i, 0)),
            out_shape=jax.ShapeDtypeStruct((noct, b, seq, LANES), F32),
            scratch_shapes=[pltpu.VMEM((ng, SSM_CW, SSM_CW), BF16)],
            compiler_params=_cparams(("arbitrary",)),
            name="ssm_out",
        )(ut, hin, w_intra, w_so, perm)
        return y.reshape(noct, b * seq, LANES)

    return stage_out(ut_lat, h_lat, lat_steps), stage_out(ut_ctx, h_ctx, 1)


def _fourier_kernel(ulo_ref, uup_ref, w_ref, cos_ref, sin_ref, alt_ref, s1_ref, b_ref,
                    lo_ref, up_ref, a_s, b_s, am_s, carry, *, groups):
    gc = ulo_ref.shape[1] // groups
    tm = lo_ref.shape[0]
    nt = a_s.shape[0] // tm
    row = lax.broadcasted_iota(jnp.int32, lo_ref.shape, 0)

    @pl.when(pl.program_id(1) == 0)
    def _():
        for jj in range(nt):
            pr = jnp.dot(s1_ref[...], uup_ref[(nt - 1 - jj) * tm:(nt - jj) * tm, :], preferred_element_type=F32)
            if jj > 0:
                first = uup_ref[(nt - jj) * tm:(nt - jj) * tm + am_s.shape[0], :][0:1, :].astype(F32)
                pr = jnp.where(row == 0, first, pr)
            pr = pr.astype(BF16)
            rows = slice(jj * tm, (jj + 1) * tm)
            for g in range(groups):
                cols = slice(g * gc, (g + 1) * gc)
                ab_lo = jnp.dot(ulo_ref[rows, cols], w_ref[g], preferred_element_type=F32)
                ab_pr = jnp.dot(pr[:, cols], w_ref[g], preferred_element_type=F32)
                a_s[rows, cols] = (ab_lo[:, :gc] + ab_pr[:, :gc]).astype(BF16)
                b_s[rows, cols] = (ab_lo[:, gc:] - ab_pr[:, gc:]).astype(BF16)
        for g in range(groups):
            cols = slice(g * gc, (g + 1) * gc)
            am_s[:, cols] = jnp.dot(uup_ref[0:am_s.shape[0], cols], w_ref[g, :, 0:gc], preferred_element_type=F32)
        mid = jnp.dot(alt_ref[...], a_s[...], preferred_element_type=F32)
        carry[...] = mid + am_s[0:carry.shape[0], :] + b_ref[...]

    a_mid = jnp.where((row & 1) == 0, am_s[0:1, :], -am_s[0:1, :])
    p = jnp.dot(cos_ref[...], a_s[...], preferred_element_type=F32) + a_mid + b_ref[...]
    q = jnp.dot(sin_ref[...], b_s[...], preferred_element_type=F32)
    lo_ref[...] = (p - q).astype(BF16)
    hi = (p + q).astype(BF16)
    rev = jnp.dot(s1_ref[...], hi, preferred_element_type=F32)
    up_ref[...] = jnp.where(row == 0, carry[0:1, :], rev).astype(BF16)
    carry[0:1, :] = hi[0:1, :].astype(F32)


def _dft_half_tables(n):
    half = n // 2
    k = np.arange(half, dtype=np.int64)
    ang = ((k[:, None] * k[None, :]) % n).astype(np.float64) * (2.0 * np.pi / n)
    alt = np.broadcast_to(np.where((k & 1) == 0, 1.0, -1.0), (8, half))
    return tuple(jnp.asarray(a, F32).astype(BF16) for a in (np.cos(ang), np.sin(ang), alt))


def _fourier(uf, w_cs, bias, tables, *, b, tm):
    nr, fw = uf.shape
    seq = nr // b
    half = seq // 2
    groups = w_cs.shape[0]
    cos_t, sin_t, alt = tables
    tm = min(tm, half)
    nt = half // tm
    r = np.arange(1, tm)
    s1 = np.zeros((tm, tm), np.float32)
    s1[r, tm - r] = 1.0
    s1 = jnp.asarray(s1, BF16)
    lo, up = pl.pallas_call(
        functools.partial(_fourier_kernel, groups=groups),
        grid=(b, nt),
        in_specs=[
            pl.BlockSpec((half, fw), lambda bi, i: (2 * bi, 0)),
            pl.BlockSpec((half, fw), lambda bi, i: (2 * bi + 1, 0)),
            pl.BlockSpec(w_cs.shape, lambda bi, i: (0, 0, 0)),
            pl.BlockSpec((tm, half), lambda bi, i: (nt - 1 - i, 0)),
            pl.BlockSpec((tm, half), lambda bi, i: (nt - 1 - i, 0)),
            pl.BlockSpec(alt.shape, lambda bi, i: (0, 0)),
            pl.BlockSpec((tm, tm), lambda bi, i: (0, 0)),
            pl.BlockSpec((1, fw), lambda bi, i: (0, 0)),
        ],
        out_specs=[pl.BlockSpec((tm, fw), lambda bi, i: (bi * nt + nt - 1 - i, 0)),
                   pl.BlockSpec((tm, fw), lambda bi, i: (bi * nt + i, 0))],
        out_shape=[jax.ShapeDtypeStruct((b * half, fw), BF16), jax.ShapeDtypeStruct((b * half, fw), BF16)],
        scratch_shapes=[pltpu.VMEM((half, fw), BF16), pltpu.VMEM((half, fw), BF16),
                        pltpu.VMEM((16, fw), F32), pltpu.VMEM((alt.shape[0], fw), F32)],
        compiler_params=_cparams(("arbitrary", "arbitrary")),
        name="fourier_%d" % seq,
    )(uf, uf, w_cs, cos_t, sin_t, alt, s1, bias)
    return jnp.concatenate([lo.reshape(b, half, fw), up.reshape(b, half, fw)], axis=1).reshape(nr, fw)


def _fourier_weights(w_four, seq):
    groups, gc, _ = w_four.shape
    k = np.arange(gc)
    ang = ((k[:, None] * k[None, :]) % gc).astype(np.float64) * (2.0 * np.pi / gc)
    norm = 1.0 / math.sqrt(seq * gc)
    cc = jnp.asarray(np.cos(ang) * norm, F32)
    sc = jnp.asarray(np.sin(ang) * norm, F32)
    hp = lax.Precision.HIGHEST
    wc = jnp.einsum('ck,gkd->gcd', cc, w_four.astype(F32), precision=hp)
    ws = jnp.einsum('ck,gkd->gcd', sc, w_four.astype(F32), precision=hp)
    return jnp.concatenate([wc, ws], axis=-1).astype(BF16)


def _outproj_kernel(att_ref, us_ref, yc_ref, uf_ref, x_ref, mod_ref, wo_ref, wg_ref, bg_ref, dsk_ref,
                    gpost_ref, gpre_ref, xo_ref, h2_ref, *, aw, sw):
    nblk = us_ref.shape[0]
    tm = x_ref.shape[0]
    gate_g = mod_ref[0, 2:3, :] * gpost_ref[...]
    scale_g = gpre_ref[...] * (1.0 + mod_ref[0, 4:5, :])
    for r0 in range(0, tm, tm // 2):
        rows = slice(r0, r0 + tm // 2)
        us = jnp.concatenate([us_ref[j, rows, :] for j in range(nblk)], axis=1)
        yc = jnp.concatenate([yc_ref[j, rows, :] for j in range(nblk)], axis=1)
        g = jax.nn.gelu(dsk_ref[...] * us + yc)
        z = jnp.dot(g.astype(BF16), wg_ref[...], preferred_element_type=F32) + bg_ref[...]
        ssm = (g * jax.nn.sigmoid(z)).astype(BF16)
        mix = (jnp.dot(att_ref[rows, :], wo_ref[0:aw, :], preferred_element_type=F32)
               + jnp.dot(ssm, wo_ref[aw:aw + sw, :], preferred_element_type=F32)
               + jnp.dot(uf_ref[rows, :], wo_ref[aw + sw:, :], preferred_element_type=F32))
        xn = x_ref[rows, :] + _rms(mix, NORM_EPS) * gate_g
        xo_ref[rows, :] = xn
        h2_ref[rows, :] = (_rms(xn, NORM_EPS) * scale_g + mod_ref[0, 3:4, :]).astype(BF16)


def _outproj(att, us, yc, four, x, mod, w_out, w_glu, b_glu, dsk, g_post, g_pre, *, layer, mod_row, tm):
    n_rows, d = x.shape
    aw, fw = att.shape[1], four.shape[1]
    nblk = us.shape[0]
    sw = nblk * LANES

    def row(i):
        return (i, 0)

    def const(i):
        return (0, 0)

    blk3 = pl.BlockSpec((nblk, tm, LANES), lambda i: (0, i, 0))
    return pl.pallas_call(
        functools.partial(_outproj_kernel, aw=aw, sw=sw),
        grid=(n_rows // tm,),
        in_specs=[
            pl.BlockSpec((tm, aw), row), blk3, blk3,
            pl.BlockSpec((tm, fw), row), pl.BlockSpec((tm, d), row),
            pl.BlockSpec((1, N_MOD, d), lambda i: (mod_row(i), 0, 0)),
            _resident_layer(w_out.shape, layer), _resident_layer(w_glu.shape, layer),
            pl.BlockSpec((1, sw), const), pl.BlockSpec((1, sw), const),
            pl.BlockSpec((1, d), const), pl.BlockSpec((1, d), const),
        ],
        out_specs=[pl.BlockSpec((tm, d), row), pl.BlockSpec((tm, d), row)],
        out_shape=[jax.ShapeDtypeStruct((n_rows, d), F32), jax.ShapeDtypeStruct((n_rows, d), BF16)],
        compiler_params=_cparams(("arbitrary",)),
        name="outproj",
    )(att, us, yc, four, x, mod, w_out, w_glu, b_glu, dsk, g_post, g_pre)


def _ffn_kernel(h_ref, x_ref, mod_ref, g_ref, wg_ref, wu_ref, wd_ref, o_ref):
    k = pl.program_id(1)

    @pl.when(k == 0)
    def _():
        o_ref[...] = jnp.zeros(o_ref.shape, F32)

    h = h_ref[...]
    tf = wg_ref.shape[1]
    part = None
    for c0 in range(0, tf, tf // 2):
        cols = slice(c0, c0 + tf // 2)
        a = jnp.dot(h, wg_ref[:, cols], preferred_element_type=F32)
        u = jnp.dot(h, wu_ref[:, cols], preferred_element_type=F32)
        p = jnp.dot((_silu(a) * u).astype(BF16), wd_ref[cols, :], preferred_element_type=F32)
        part = p if part is None else part + p
    o_ref[...] += part

    @pl.when(k == pl.num_programs(1) - 1)
    def _():
        o_ref[...] = x_ref[...] + _rms(o_ref[...], NORM_EPS) * (mod_ref[0, 5:6, :] * g_ref[...])


def _ffn(h2, x_mid, mod, g_post, w_gate, w_up, w_down, *, mod_row, tm, tf):
    n_rows, d = h2.shape
    dff = w_gate.shape[1]
    return pl.pallas_call(
        _ffn_kernel,
        grid=(n_rows // tm, dff // tf),
        in_specs=[
            pl.BlockSpec((tm, d), lambda i, k: (i, 0)),
            pl.BlockSpec((tm, d), lambda i, k: (i, 0)),
            pl.BlockSpec((1, N_MOD, d), lambda i, k: (mod_row(i), 0, 0)),
            pl.BlockSpec((1, d), lambda i, k: (0, 0)),
            pl.BlockSpec((d, tf), lambda i, k: (0, k)),
            pl.BlockSpec((d, tf), lambda i, k: (0, k)),
            pl.BlockSpec((tf, d), lambda i, k: (k, 0)),
        ],
        out_specs=pl.BlockSpec((tm, d), lambda i, k: (i, 0)),
        out_shape=jax.ShapeDtypeStruct((n_rows, d), F32),
        compiler_params=_cparams(("arbitrary", "arbitrary")),
        name="ffn",
    )(h2, x_mid, mod, g_post, w_gate, w_up, w_down)


def kernel(x, c, ctx, c_ctx, w_mod, b_mod, g_mix_pre, g_mix_post, g_ffn_pre, g_ffn_post, w_in, w_out, lam_q1, lam_k1, lam_q2, lam_k2, g_subln, ssm_a_re, ssm_a_im, ssm_log_dt, ssm_b_re, ssm_b_im, ssm_c_re, ssm_c_im, ssm_d, w_glu, b_glu, w_four, b_four, w_gate, w_up, w_down):
    b, t, d = x.shape
    n_ctx = ctx.shape[1]
    depth = w_mod.shape[0]
    nl, nc = b * t, b * n_ctx
    aw = d // 2
    sw = ssm_d.shape[1]
    fw = d - aw - sw
    heads = aw // (2 * DA_HEAD_DIM)
    tm = 512
    assert t % tm == 0 and nc % tm == 0 and n_ctx % SSM_CHUNK == 0 and t % GRID_W == 0 and b + 1 <= MOD_ROWS
    assert sw % LANES == 0 and (sw // SSM_GROUP) % (2 * OCT) == 0

    cs = jnp.concatenate([c, c_ctx[None, :], jnp.zeros((MOD_ROWS - b - 1, d), F32)], axis=0)
    mod_all = _modulation(cs, w_mod, b_mod).reshape(depth, MOD_ROWS, N_MOD, d)

    per_b = t // tm
    lat_mod = lambda i: i // per_b
    ctx_mod = lambda i: b
    tabs = _rope_tables(t, tm)
    dft_lat = _dft_half_tables(t)
    dft_ctx = _dft_half_tables(n_ctx)
    perm = _atom_transpose_matrix()
    x_lat, x_ctx = x.reshape(nl, d), ctx.reshape(nc, d)
    w_in_b, w_out_b, w_glu_b = w_in.astype(BF16), w_out.astype(BF16), w_glu.astype(BF16)
    ssm_w = jax.vmap(_ssm_weights)(ssm_a_re, ssm_a_im, ssm_log_dt, ssm_b_re, ssm_b_im, ssm_c_re, ssm_c_im)

    for l in range(depth):
        need_ctx = l < depth - 1
        lam_init = 0.8 - 0.6 * math.exp(-0.3 * l)
        mod = mod_all[l]
        g_pre = g_mix_pre[l][None, :]
        q, k, vt, us, uf = _inproj(x_lat, mod, g_pre, w_in_b, tabs, layer=l, mod_row=lat_mod,
                                   tab_blk=lambda i: i % per_b, aw=aw, sw=sw, fw=fw, tm=tm)
        qc, kc, vtc, usc, ufc = _inproj(x_ctx, mod, g_pre, w_in_b, tabs, layer=l, mod_row=ctx_mod,
                                        tab_blk=lambda i: per_b, aw=aw, sw=sw, fw=fw, tm=tm)

        lam4 = jnp.stack([lam_q1[l], lam_k1[l], lam_q2[l], lam_k2[l]]).astype(F32)
        gs = g_subln[l][None, :].astype(F32)
        att, *ffn_wb = _attention(lam4, gs, q, [(kc, vtc), (k, vt)], lam_init=lam_init, b=b, heads=heads, tq=1024,
                                  name="attn_latent", cast=(l, w_gate, w_up, w_down))
        yc, ycc = _ssm_conv(us, usc, ssm_w, perm, layer=l, b=b)
        bias = b_four[l].reshape(1, fw).astype(F32)
        four = _fourier(uf, _fourier_weights(w_four[l], t), bias, dft_lat, b=b, tm=tm)

        small = (b_glu[l][None, :].astype(F32), ssm_d[l][None, :].astype(F32), g_mix_post[l][None, :],
                 g_ffn_pre[l][None, :])
        ffn_w = (g_ffn_post[l][None, :], *ffn_wb)
        x_mid, h2 = _outproj(att, us, yc, four, x_lat, mod, w_out_b, w_glu_b, *small, layer=l, mod_row=lat_mod,
                             tm=tm)
        x_lat = _ffn(h2, x_mid, mod, *ffn_w, mod_row=lat_mod, tm=tm, tf=512)

        if need_ctx:
            att_c, = _attention(lam4, gs, qc, [(kc, vtc)], lam_init=lam_init, b=b, heads=heads, tq=n_ctx,
                                name="attn_ctx")
            four_c = _fourier(ufc, _fourier_weights(w_four[l], n_ctx), bias, dft_ctx, b=b, tm=tm)
            xc_mid, h2c = _outproj(att_c, usc, ycc, four_c, x_ctx, mod, w_out_b, w_glu_b, *small, layer=l,
                                   mod_row=ctx_mod, tm=tm)
            x_ctx = _ffn(h2c, xc_mid, mod, *ffn_w, mod_row=ctx_mod, tm=tm, tf=512)
    return x_lat.reshape(b, t, d)
```

```python
import functools
import math

import jax
import jax.numpy as jnp
import numpy as np
from jax import lax
from jax.experimental import pallas as pl
from jax.experimental.pallas import tpu as pltpu

F32 = jnp.float32
BF16 = jnp.bfloat16

LANES = 128
GRID_W = 64
DA_HEAD_DIM = 128
SSM_GROUP = 16
SSM_STATE = 64
N_MOD = 6
ROPE_BASE = 10000.0
ROPE_PAIRS = DA_HEAD_DIM // 4
NORM_EPS = 1e-6
SUBLN_EPS = 1e-5

SSM_CHUNK = 16
SSM_CW = SSM_CHUNK * SSM_GROUP
OCT = LANES // SSM_GROUP
MOD_ROWS = 8
VMEM_LIMIT = 56 * 1024 * 1024
NT_DIMS = (((1,), (1,)), ((), ()))


def _cparams(sem):
    return pltpu.CompilerParams(dimension_semantics=sem, vmem_limit_bytes=VMEM_LIMIT)


def _rms(x, eps):
    return x * lax.rsqrt(jnp.mean(x * x, axis=-1, keepdims=True) + eps)


def _silu(x):
    return x * jax.nn.sigmoid(x)


def _resident(shape):
    nd = len(shape)
    return pl.BlockSpec(shape, lambda *_: (0,) * nd, pipeline_mode=pl.Buffered(1))


def _resident_layer(shape, layer):
    nd = len(shape)
    return pl.BlockSpec((None,) + tuple(shape[1:]), lambda *_: (layer,) + (0,) * (nd - 1),
                        pipeline_mode=pl.Buffered(1))


def _mod_kernel(c_ref, w_ref, b_ref, o_ref):
    s = _silu(c_ref[...]).astype(BF16)
    o_ref[0] = jnp.dot(s, w_ref[0].astype(BF16), preferred_element_type=F32) + b_ref[0]


def _modulation(cs, w_mod, b_mod, tn=1024):
    depth, d, n = w_mod.shape
    return pl.pallas_call(
        _mod_kernel,
        grid=(depth, n // tn),
        in_specs=[
            pl.BlockSpec((MOD_ROWS, d), lambda l, j: (0, 0)),
            pl.BlockSpec((1, d, tn), lambda l, j: (l, 0, j)),
            pl.BlockSpec((1, 1, tn), lambda l, j: (l, 0, j)),
        ],
        out_specs=pl.BlockSpec((1, MOD_ROWS, tn), lambda l, j: (l, 0, j)),
        out_shape=jax.ShapeDtypeStruct((depth, MOD_ROWS, n), F32),
        compiler_params=_cparams(("arbitrary", "arbitrary")),
        name="modulation",
    )(cs, w_mod, b_mod.reshape(depth, 1, n))


def _inproj_kernel(x_ref, mod_ref, g_ref, w_ref, cq_ref, sq_ref, ck_ref, sk_ref, *refs, aw, sw):
    n_cast = (len(refs) - 5) // 2
    q_ref, k_ref, vt_ref, us_ref, uf_ref = refs[n_cast:n_cast + 5]
    for src, dst in zip(refs[:n_cast], refs[n_cast + 5:]):
        dst[...] = src[...].astype(BF16)
    tm = x_ref.shape[0]
    hb = (_rms(x_ref[...], NORM_EPS) * (g_ref[...] * (1.0 + mod_ref[0, 1:2, :])) + mod_ref[0, 0:1, :]).astype(BF16)

    lane = lax.broadcasted_iota(jnp.int32, (tm, DA_HEAD_DIM), 1)
    first_half = (lane & (2 * ROPE_PAIRS - 1)) < ROPE_PAIRS

    def rope(z, c, s):
        partner = jnp.where(first_half, pltpu.roll(z, DA_HEAD_DIM - ROPE_PAIRS, 1), pltpu.roll(z, ROPE_PAIRS, 1))
        return z * c + partner * s

    zq = jnp.dot(hb, w_ref[:, 0:aw], preferred_element_type=F32)
    cq, sq = cq_ref[...], sq_ref[...]
    for j in range(aw // DA_HEAD_DIM):
        sl = slice(j * DA_HEAD_DIM, (j + 1) * DA_HEAD_DIM)
        q_ref[:, sl] = rope(zq[:, sl], cq, sq).astype(BF16)
    zk = jnp.dot(hb, w_ref[:, aw:2 * aw], preferred_element_type=F32)
    ck, sk = ck_ref[...], sk_ref[...]
    for j in range(aw // DA_HEAD_DIM):
        sl = slice(j * DA_HEAD_DIM, (j + 1) * DA_HEAD_DIM)
        k_ref[:, sl] = rope(zk[:, sl], ck, sk).astype(BF16)
    zv = jnp.dot(hb, w_ref[:, 2 * aw:3 * aw], preferred_element_type=F32)
    vt_ref[...] = jnp.transpose(zv.astype(BF16))
    zs = jnp.dot(hb, w_ref[:, 3 * aw:3 * aw + sw], preferred_element_type=F32)
    for j in range(sw // LANES):
        us_ref[j] = zs[:, j * LANES:(j + 1) * LANES]
    uf_ref[...] = jnp.dot(hb, w_ref[:, 3 * aw + sw:], preferred_element_type=F32).astype(BF16)


def _inproj(x, mod, g, w_in, tabs, *, mod_row, tab_blk, aw, sw, fw, tm, cast=()):
    nr, d = x.shape
    n_steps = nr // tm

    def row(i):
        return (i, 0)

    tab_spec = pl.BlockSpec((tm, DA_HEAD_DIM), lambda i: (tab_blk(i), 0))
    cast_in, cast_out, cast_shape = [], [], []
    for w, layer in cast:
        _, r, c = w.shape
        per = next(dv for dv in range(1, c // LANES + 1) if (c // LANES) % dv == 0 and dv * n_steps >= c // LANES)
        cw = per * LANES
        cast_in.append(pl.BlockSpec((None, r, cw), lambda i, layer=layer, nb=c // cw: (layer, 0, jnp.minimum(i, nb - 1))))
        cast_out.append(pl.BlockSpec((r, cw), lambda i, nb=c // cw: (0, jnp.minimum(i, nb - 1))))
        cast_shape.append(jax.ShapeDtypeStruct((r, c), BF16))
    return pl.pallas_call(
        functools.partial(_inproj_kernel, aw=aw, sw=sw),
        grid=(n_steps,),
        in_specs=[
            pl.BlockSpec((tm, d), row),
            pl.BlockSpec((1, N_MOD, d), lambda i: (mod_row(i), 0, 0)),
            pl.BlockSpec((1, d), lambda i: (0, 0)),
            _resident(w_in.shape),
            tab_spec, tab_spec, tab_spec, tab_spec,
        ] + cast_in,
        out_specs=[
            pl.BlockSpec((tm, aw), row), pl.BlockSpec((tm, aw), row), pl.BlockSpec((aw, tm), lambda i: (0, i)),
            pl.BlockSpec((sw // LANES, tm, LANES), lambda i: (0, i, 0)), pl.BlockSpec((tm, fw), row),
        ] + cast_out,
        out_shape=[
            jax.ShapeDtypeStruct((nr, aw), BF16), jax.ShapeDtypeStruct((nr, aw), BF16),
            jax.ShapeDtypeStruct((aw, nr), BF16), jax.ShapeDtypeStruct((sw // LANES, nr, LANES), F32),
            jax.ShapeDtypeStruct((nr, fw), BF16),
        ] + cast_shape,
        compiler_params=_cparams(("arbitrary",)),
        name="inproj",
    )(x, mod, g, w_in, *tabs, *[w for w, _ in cast])


def _rope_tables(t, tm):
    rows = t // GRID_W
    r = np.repeat(np.arange(rows, dtype=np.float64), GRID_W)
    col = np.tile(np.arange(GRID_W, dtype=np.float64), rows)
    inv = ROPE_BASE ** (-np.arange(ROPE_PAIRS, dtype=np.float64) / ROPE_PAIRS)
    ar, ac = r[:, None] * inv, col[:, None] * inv
    cos = np.concatenate([np.cos(ar), np.cos(ar), np.cos(ac), np.cos(ac)], axis=1)
    sin = np.concatenate([-np.sin(ar), np.sin(ar), -np.sin(ac), np.sin(ac)], axis=1)
    cos = np.concatenate([cos, np.ones((tm, DA_HEAD_DIM))], axis=0)
    sin = np.concatenate([sin, np.zeros((tm, DA_HEAD_DIM))], axis=0)
    scale = DA_HEAD_DIM ** -0.5 * math.log2(math.e)
    return tuple(jnp.asarray(a, F32) for a in (cos * scale, sin * scale, cos, sin))


def _attn_kernel(lam_ref, gs_ref, q_ref, *refs, lam_init, n_sets, n_cast):
    k_refs, vt_refs = refs[0:2 * n_sets:2], refs[1:2 * n_sets:2]
    cast_in = refs[2 * n_sets:2 * n_sets + n_cast]
    o_ref = refs[2 * n_sets + n_cast]
    cast_out = refs[2 * n_sets + n_cast + 1:]
    for src, dst in zip(cast_in, cast_out):
        dst[...] = src[...].astype(BF16)

    hd = DA_HEAD_DIM
    s1 = jnp.sum(lam_ref[0:1, :] * lam_ref[1:2, :], axis=-1, keepdims=True)
    s2 = jnp.sum(lam_ref[2:3, :] * lam_ref[3:4, :], axis=-1, keepdims=True)
    lam = jnp.exp(s1) - jnp.exp(s2) + lam_init
    outs = []
    for idx in range(2):
        qi = q_ref[:, idx * hd:(idx + 1) * hd]
        ss = [lax.dot_general(kr[:, idx * hd:(idx + 1) * hd], qi, NT_DIMS, preferred_element_type=F32)
              for kr in k_refs]
        m = functools.reduce(jnp.maximum, [jnp.max(s, axis=0, keepdims=True) for s in ss])
        ps = [jnp.exp2(s - m) for s in ss]
        l = functools.reduce(jnp.add, [jnp.sum(p, axis=0, keepdims=True) for p in ps])
        acc = functools.reduce(jnp.add, [jnp.dot(vr[...], p.astype(BF16), preferred_element_type=F32)
                                         for vr, p in zip(vt_refs, ps)])
        outs.append(acc * (1.0 / l))
    o = jnp.transpose(outs[0] - outs[1] * lam)
    o_ref[...] = (_rms(o, SUBLN_EPS) * gs_ref[...] * (1.0 - lam_init)).astype(BF16)


def _attention(lam4, gs, q, kv_sets, *, lam_init, b, heads, tq, name, cast=None):
    nq, aw = q.shape
    vw = 2 * DA_HEAD_DIM
    per_b = nq // b // tq
    q_spec = pl.BlockSpec((tq, vw), lambda bi, h, i: (bi * per_b + i, h))
    in_specs = [pl.BlockSpec((4, DA_HEAD_DIM), lambda bi, h, i: (0, 0)),
                pl.BlockSpec((1, vw), lambda bi, h, i: (0, 0)),
                q_spec]
    args = [lam4, gs, q]
    for k, vt in kv_sets:
        sk = k.shape[0] // b
        in_specs += [pl.BlockSpec((sk, vw), lambda bi, h, i: (bi, h)),
                     pl.BlockSpec((vw, sk), lambda bi, h, i: (h, bi))]
        args += [k, vt]
    out_specs = [q_spec]
    out_shape = [jax.ShapeDtypeStruct((nq, aw), BF16)]
    n_cast = 0
    if cast is not None:
        layer, w_gate, w_up, w_down = cast
        _, d, dff = w_gate.shape
        n_steps = b * heads * per_b
        per = next(dv for dv in range(1, dff // LANES + 1) if (dff // LANES) % dv == 0 and dv * n_steps >= dff // LANES)
        cw = per * LANES
        nblk = dff // cw

        def blk(bi, h, i):
            return jnp.minimum((bi * heads + h) * per_b + i, nblk - 1)

        in_specs += [pl.BlockSpec((None, d, cw), lambda bi, h, i: (layer, 0, blk(bi, h, i))),
                     pl.BlockSpec((None, d, cw), lambda bi, h, i: (layer, 0, blk(bi, h, i))),
                     pl.BlockSpec((None, cw, d), lambda bi, h, i: (layer, blk(bi, h, i), 0))]
        args += [w_gate, w_up, w_down]
        out_specs += [pl.BlockSpec((d, cw), lambda bi, h, i: (0, blk(bi, h, i))),
                      pl.BlockSpec((d, cw), lambda bi, h, i: (0, blk(bi, h, i))),
                      pl.BlockSpec((cw, d), lambda bi, h, i: (blk(bi, h, i), 0))]
        out_shape += [jax.ShapeDtypeStruct((d, dff), BF16), jax.ShapeDtypeStruct((d, dff), BF16),
                      jax.ShapeDtypeStruct((dff, d), BF16)]
        n_cast = 3
    return pl.pallas_call(
        functools.partial(_attn_kernel, lam_init=lam_init, n_sets=len(kv_sets), n_cast=n_cast),
        grid=(b, heads, per_b),
        in_specs=in_specs,
        out_specs=out_specs,
        out_shape=out_shape,
        compiler_params=_cparams(("arbitrary", "arbitrary", "arbitrary")),
        name=name,
    )(*args)


def _ssm_weights(a_re, a_im, log_dt, b_re, b_im, c_re, c_im):
    L, H, P = SSM_CHUNK, SSM_GROUP, SSM_STATE
    g = a_re.shape[1]
    npair = g // 2
    hp = lax.Precision.HIGHEST
    ar, ai = a_re.astype(F32), a_im.astype(F32)
    dt = jnp.exp(log_dt.astype(F32))[..., None]
    n = jnp.arange(L + 1, dtype=F32)
    mag = jnp.exp((ar * dt)[..., None] * n)
    ang = (ai * dt)[..., None] * n
    apr, api = mag * jnp.cos(ang), mag * jnp.sin(ang)
    xr, xi = apr[..., 1] - 1.0, api[..., 1]
    den = ar * ar + ai * ai
    qr, qi = (xr * ar + xi * ai) / den, (xi * ar - xr * ai) / den
    br, bi = b_re.astype(F32), b_im.astype(F32)
    bbr = qr[..., None] * br - qi[..., None] * bi
    bbi = qr[..., None] * bi + qi[..., None] * br
    cr, ci = c_re.astype(F32), c_im.astype(F32)

    car = cr[..., None] * apr[:, :, None, :, :L] - ci[..., None] * api[:, :, None, :, :L]
    cai = cr[..., None] * api[:, :, None, :, :L] + ci[..., None] * apr[:, :, None, :, :L]
    kt = (jnp.einsum('dghpt,dgpk->dgkth', car, bbr, precision=hp)
          - jnp.einsum('dghpt,dgpk->dgkth', cai, bbi, precision=hp))
    kc = jnp.concatenate([jnp.flip(kt[1][:, :, 1:], axis=2), kt[0][:, :, :1] + kt[1][:, :, :1], kt[0][:, :, 1:]],
                         axis=2).reshape(g, H, (2 * L - 1) * H)
    w_intra = jnp.pad(kc, ((0, 0), (0, 0), (0, (-kc.shape[2]) % LANES)))

    sir = jnp.stack([jnp.flip(apr[0, ..., :L], axis=-1), apr[1, ..., :L]])
    sii = jnp.stack([jnp.flip(api[0, ..., :L], axis=-1), api[1, ..., :L]])
    sir, sii = jnp.swapaxes(sir, 2, 3)[:, :, :, None, :], jnp.swapaxes(sii, 2, 3)[:, :, :, None, :]
    tbr, tbi = jnp.swapaxes(bbr, 2, 3)[:, :, None], jnp.swapaxes(bbi, 2, 3)[:, :, None]
    w_sin = jnp.concatenate([sir * tbr - sii * tbi, sir * tbi + sii * tbr], axis=-1)
    w_sin = w_sin.reshape(2, g, L * H, 2 * P)

    sor = jnp.stack([apr[0, ..., 1:], jnp.flip(apr[1, ..., 1:], axis=-1)])
    soi = jnp.stack([api[0, ..., 1:], jnp.flip(api[1, ..., 1:], axis=-1)])
    tcr, tci = jnp.swapaxes(cr, 2, 3)[:, :, :, None, :], jnp.swapaxes(ci, 2, 3)[:, :, :, None, :]
    cnr = (tcr * sor[..., None] - tci * soi[..., None]).reshape(2, g, P, L * H)
    cni = (tcr * soi[..., None] + tci * sor[..., None]).reshape(2, g, P, L * H)
    w_so = jnp.concatenate([cnr, -cni], axis=2)

    al = jnp.stack([apr[..., L].reshape(2, npair, 2 * P), api[..., L].reshape(2, npair, 2 * P)], axis=2)
    al = al.reshape(2, 2 * npair, 1, 2 * P)
    return w_intra, w_sin.astype(BF16), w_so.astype(BF16), al


def _atom_transpose_matrix():
    n = OCT * OCT * SSM_GROUP
    i = np.arange(n)
    j = ((i // SSM_GROUP) % OCT) * LANES + (i // LANES) * SSM_GROUP + i % SSM_GROUP
    p = np.zeros((n, n), np.float32)
    p[i, j] = 1.0
    return jnp.asarray(p, BF16)


def _ssm_in_kernel(us_ref, p_ref, w_ref, ut_ref, s_ref):
    noct, nb = us_ref.shape[0], us_ref.shape[1]
    nc = us_ref.shape[2] // SSM_CHUNK
    n = nb * nc
    half = SSM_CHUNK // 2
    src = []
    for o in range(noct):
        for hf in range(2):
            per_seq = [jnp.concatenate([us_ref[o, bi, pl.ds(half * hf + sl, nc, stride=SSM_CHUNK), :]
                                        for sl in range(half)], axis=1) for bi in range(nb)]
            src.append(jnp.concatenate(per_seq, axis=0).astype(BF16))
    perm = jnp.dot(jnp.concatenate(src, axis=0), p_ref[...], preferred_element_type=F32).astype(BF16)

    def group_chunk(gi):
        o, gl = gi // OCT, gi % OCT
        return jnp.concatenate([perm[(2 * o + hf) * n:(2 * o + hf + 1) * n, gl * LANES:(gl + 1) * LANES]
                                for hf in range(2)], axis=1)

    def store_chunk_major(d, blk, val):
        for bi in range(nb):
            s_ref[d, blk, pl.ds(bi, nc, stride=nb), :] = val[bi * nc:(bi + 1) * nc]

    low = lax.broadcasted_iota(jnp.int32, (n, LANES), 1) < SSM_STATE
    for j in range(noct * OCT // 2):
        ug = [group_chunk(2 * j), group_chunk(2 * j + 1)]
        ut_ref[2 * j] = ug[0]
        ut_ref[2 * j + 1] = ug[1]
        for d in range(2):
            s0 = jnp.dot(ug[0], w_ref[d, 2 * j], preferred_element_type=F32)
            s1 = jnp.dot(ug[1], w_ref[d, 2 * j + 1], preferred_element_type=F32)
            store_chunk_major(d, 2 * j, jnp.where(low, s0, pltpu.roll(s1, SSM_STATE, 1)))
            store_chunk_major(d, 2 * j + 1, jnp.where(low, pltpu.roll(s0, SSM_STATE, 1), s1))


def _ssm_scan_kernel(sc_ref, sl_ref, al_ref, hc_ref, hl_ref, *, b):
    d = pl.program_id(0)
    nblk = sc_ref.shape[1]
    tile = 8
    cpt = tile // b
    grp = lax.broadcasted_iota(jnp.int32, (tile, LANES), 0) // b
    ars = [al_ref[0, 2 * k] for k in range(nblk // 2)]
    ais = [al_ref[0, 2 * k + 1] for k in range(nblk // 2)]

    def spread(x, g):
        x = jnp.where(grp == g, x, 0.0)
        out = x
        for r in range(1, cpt):
            out = out + pltpu.roll(x, r * b, 0)
        return out

    def phase(s_ref, h_ref, carry):
        ntile = s_ref.shape[2] // tile

        def body(i, carry):
            t = jnp.where(d == 0, i, ntile - 1 - i)
            rows = pl.ds(pl.multiple_of(t * tile, tile), tile)
            new = []
            for k in range(nblk // 2):
                hr, hi = carry[2 * k], carry[2 * k + 1]
                sr, si = s_ref[0, 2 * k, rows, :], s_ref[0, 2 * k + 1, rows, :]
                hin_r, hin_i = hr, hi
                for step in range(cpt):
                    g = jnp.where(d == 0, step, cpt - 1 - step)
                    hin_r = jnp.where(grp == g, hr, hin_r)
                    hin_i = jnp.where(grp == g, hi, hin_i)
                    sgr, sgi = spread(sr, g), spread(si, g)
                    hr, hi = ars[k] * hr - ais[k] * hi + sgr, ars[k] * hi + ais[k] * hr + sgi
                h_ref[0, 2 * k, rows, :] = hin_r
                h_ref[0, 2 * k + 1, rows, :] = hin_i
                new += [hr, hi]
            return tuple(new)

        return lax.fori_loop(0, ntile, body, carry, unroll=2)

    carry = tuple(jnp.zeros((tile, LANES), F32) for _ in range(nblk))
    carry = phase(sc_ref, hc_ref, carry)
    phase(sl_ref, hl_ref, carry)


def _ssm_out_kernel(ut_ref, h_ref, kc_ref, wo_ref, p_ref, y_ref, wi_ref):
    n = ut_ref.shape[1]
    ng = ut_ref.shape[0]
    nb = y_ref.shape[1]
    nc = n // nb
    noct = ng // OCT
    half = SSM_CHUNK // 2

    @pl.when(pl.program_id(0) == 0)
    def _():
        for gi in range(ng):
            kc = kc_ref[gi]
            wi_ref[gi] = jnp.concatenate(
                [kc[:, (SSM_CHUNK - 1 - s) * SSM_GROUP:(SSM_CHUNK - 1 - s) * SSM_GROUP + SSM_CW]
                 for s in range(SSM_CHUNK)], axis=0).astype(BF16)

    def load_seq_major(d, blk):
        return jnp.concatenate([h_ref[d, blk, pl.ds(bi, nc, stride=nb), :] for bi in range(nb)], axis=0)

    low = lax.broadcasted_iota(jnp.int32, (n, LANES), 1) < SSM_STATE
    ys = []
    for j in range(ng // 2):
        hg = [[], []]
        for d in range(2):
            hr, hi = load_seq_major(d, 2 * j), load_seq_major(d, 2 * j + 1)
            hg[0].append(jnp.where(low, hr, pltpu.roll(hi, SSM_STATE, 1)).astype(BF16))
            hg[1].append(jnp.where(low, pltpu.roll(hr, SSM_STATE, 1), hi).astype(BF16))
        for e in range(2):
            gi = 2 * j + e
            ys.append(jnp.dot(ut_ref[gi], wi_ref[gi], preferred_element_type=F32)
                      + jnp.dot(hg[e][0], wo_ref[0, gi], preferred_element_type=F32)
                      + jnp.dot(hg[e][1], wo_ref[1, gi], preferred_element_type=F32))
    rows = []
    for o in range(noct):
        for hf in range(2):
            rows.append(jnp.concatenate([ys[OCT * o + gl][:, hf * LANES:(hf + 1) * LANES] for gl in range(OCT)],
                                        axis=1))
    ycat = jnp.concatenate(rows, axis=0)
    hi = ycat.astype(BF16)
    lo = (ycat - hi.astype(F32)).astype(BF16)
    res = (jnp.dot(hi, p_ref[...], preferred_element_type=F32)
           + jnp.dot(lo, p_ref[...], preferred_element_type=F32))
    for o in range(noct):
        for hf in range(2):
            blk = 2 * o + hf
            for tl in range(half):
                piece = res[blk * n:(blk + 1) * n, tl * LANES:(tl + 1) * LANES]
                for bi in range(nb):
                    y_ref[o, bi, pl.ds(half * hf + tl, nc, stride=SSM_CHUNK), :] = piece[bi * nc:(bi + 1) * nc]


def _ssm_conv(us_lat, us_ctx, weights, perm, *, layer, b):
    w_intra, w_sin, w_so, al = weights
    P = SSM_STATE
    ng = w_intra.shape[1]
    nblk = ng
    assert 8 % b == 0

    def stage_in(us, steps):
        noct, rows, _ = us.shape
        seq = rows // b
        tok = seq // steps
        n = b * tok // SSM_CHUNK
        nchunks = rows // SSM_CHUNK
        return pl.pallas_call(
            _ssm_in_kernel,
            grid=(steps,),
            in_specs=[pl.BlockSpec((noct, b, tok, LANES), lambda i: (0, 0, i, 0)),
                      _resident(perm.shape), _resident_layer(w_sin.shape, layer)],
            out_specs=[pl.BlockSpec((ng, n, SSM_CW), lambda i: (0, i, 0)),
                       pl.BlockSpec((2, nblk, n, LANES), lambda i: (0, 0, i, 0))],
            out_shape=[jax.ShapeDtypeStruct((ng, nchunks, SSM_CW), BF16),
                       jax.ShapeDtypeStruct((2, nblk, nchunks, LANES), F32)],
            compiler_params=_cparams(("arbitrary",)),
            name="ssm_in",
        )(us.reshape(noct, b, seq, LANES), perm, w_sin)

    lat_steps = 2 * b
    ut_lat, s_lat = stage_in(us_lat, lat_steps)
    ut_ctx, s_ctx = stage_in(us_ctx, 1)

    cb = 16
    nc_rows, nl_rows = s_ctx.shape[2], s_lat.shape[2]
    h_ctx, h_lat = pl.pallas_call(
        functools.partial(_ssm_scan_kernel, b=b),
        grid=(2, nblk // cb),
        in_specs=[pl.BlockSpec((1, cb, nc_rows, LANES), lambda d, j: (d, j, 0, 0)),
                  pl.BlockSpec((1, cb, nl_rows, LANES), lambda d, j: (d, j, 0, 0)),
                  pl.BlockSpec((None, 1, cb, 1, 2 * P), lambda d, j: (layer, d, j, 0, 0))],
        out_specs=[pl.BlockSpec((1, cb, nc_rows, LANES), lambda d, j: (d, j, 0, 0)),
                   pl.BlockSpec((1, cb, nl_rows, LANES), lambda d, j: (d, j, 0, 0))],
        out_shape=[jax.ShapeDtypeStruct(s_ctx.shape, F32), jax.ShapeDtypeStruct(s_lat.shape, F32)],
        compiler_params=_cparams(("arbitrary", "arbitrary")),
        name="ssm_scan",
    )(s_ctx, s_lat, al)

    def stage_out(ut, hin, steps):
        _, nchunks, _ = ut.shape
        n = nchunks // steps
        noct = ng // OCT
        seq = nchunks * SSM_CHUNK // b
        tok = seq // steps
        y = pl.pallas_call(
            _ssm_out_kernel,
            grid=(steps,),
            in_specs=[pl.BlockSpec((ng, n, SSM_CW), lambda i: (0, i, 0)),
                      pl.BlockSpec((2, nblk, n, LANES), lambda i: (0, 0, i, 0)),
                      _resident_layer(w_intra.shape, layer), _resident_layer(w_so.shape, layer),
                      _resident(perm.shape)],
            out_specs=pl.BlockSpec((noct, b, tok, LANES), lambda i: (0, 0, i, 0)),
            out_shape=jax.ShapeDtypeStruct((noct, b, seq, LANES), F32),
            scratch_shapes=[pltpu.VMEM((ng, SSM_CW, SSM_CW), BF16)],
            compiler_params=_cparams(("arbitrary",)),
            name="ssm_out",
        )(ut, hin, w_intra, w_so, perm)
        return y.reshape(noct, b * seq, LANES)

    return stage_out(ut_lat, h_lat, lat_steps), stage_out(ut_ctx, h_ctx, 1)


def _fourier_kernel(ulo_ref, uup_ref, w_ref, cos_ref, sin_ref, alt_ref, s1_ref, b_ref,
                    lo_ref, up_ref, a_s, b_s, am_s, carry, *, groups):
    gc = ulo_ref.shape[1] // groups
    tm = lo_ref.shape[0]
    nt = a_s.shape[0] // tm
    row = lax.broadcasted_iota(jnp.int32, lo_ref.shape, 0)

    @pl.when(pl.program_id(1) == 0)
    def _():
        for jj in range(nt):
            pr = jnp.dot(s1_ref[...], uup_ref[(nt - 1 - jj) * tm:(nt - jj) * tm, :], preferred_element_type=F32)
            if jj > 0:
                first = uup_ref[(nt - jj) * tm:(nt - jj) * tm + am_s.shape[0], :][0:1, :].astype(F32)
                pr = jnp.where(row == 0, first, pr)
            pr = pr.astype(BF16)
            rows = slice(jj * tm, (jj + 1) * tm)
            for g in range(groups):
                cols = slice(g * gc, (g + 1) * gc)
                ab_lo = jnp.dot(ulo_ref[rows, cols], w_ref[g], preferred_element_type=F32)
                ab_pr = jnp.dot(pr[:, cols], w_ref[g], preferred_element_type=F32)
                a_s[rows, cols] = (ab_lo[:, :gc] + ab_pr[:, :gc]).astype(BF16)
                b_s[rows, cols] = (ab_lo[:, gc:] - ab_pr[:, gc:]).astype(BF16)
        for g in range(groups):
            cols = slice(g * gc, (g + 1) * gc)
            am_s[:, cols] = jnp.dot(uup_ref[0:am_s.shape[0], cols], w_ref[g, :, 0:gc], preferred_element_type=F32)
        mid = jnp.dot(alt_ref[...], a_s[...], preferred_element_type=F32)
        carry[...] = mid + am_s[0:carry.shape[0], :] + b_ref[...]

    a_mid = jnp.where((row & 1) == 0, am_s[0:1, :], -am_s[0:1, :])
    p = jnp.dot(cos_ref[...], a_s[...], preferred_element_type=F32) + a_mid + b_ref[...]
    q = jnp.dot(sin_ref[...], b_s[...], preferred_element_type=F32)
    lo_ref[...] = (p - q).astype(BF16)
    hi = (p + q).astype(BF16)
    rev = jnp.dot(s1_ref[...], hi, preferred_element_type=F32)
    up_ref[...] = jnp.where(row == 0, carry[0:1, :], rev).astype(BF16)
    carry[0:1, :] = hi[0:1, :].astype(F32)


def _dft_half_tables(n):
    half = n // 2
    k = np.arange(half, dtype=np.int64)
    ang = ((k[:, None] * k[None, :]) % n).astype(np.float64) * (2.0 * np.pi / n)
    alt = np.broadcast_to(np.where((k & 1) == 0, 1.0, -1.0), (8, half))
    return tuple(jnp.asarray(a, F32).astype(BF16) for a in (np.cos(ang), np.sin(ang), alt))


def _fourier(uf, w_cs, bias, tables, *, b, tm):
    nr, fw = uf.shape
    seq = nr // b
    half = seq // 2
    groups = w_cs.shape[0]
    cos_t, sin_t, alt = tables
    tm = min(tm, half)
    nt = half // tm
    r = np.arange(1, tm)
    s1 = np.zeros((tm, tm), np.float32)
    s1[r, tm - r] = 1.0
    s1 = jnp.asarray(s1, BF16)
    lo, up = pl.pallas_call(
        functools.partial(_fourier_kernel, groups=groups),
        grid=(b, nt),
        in_specs=[
            pl.BlockSpec((half, fw), lambda bi, i: (2 * bi, 0)),
            pl.BlockSpec((half, fw), lambda bi, i: (2 * bi + 1, 0)),
            pl.BlockSpec(w_cs.shape, lambda bi, i: (0, 0, 0)),
            pl.BlockSpec((tm, half), lambda bi, i: (nt - 1 - i, 0)),
            pl.BlockSpec((tm, half), lambda bi, i: (nt - 1 - i, 0)),
            pl.BlockSpec(alt.shape, lambda bi, i: (0, 0)),
            pl.BlockSpec((tm, tm), lambda bi, i: (0, 0)),
            pl.BlockSpec((1, fw), lambda bi, i: (0, 0)),
        ],
        out_specs=[pl.BlockSpec((tm, fw), lambda bi, i: (bi * nt + nt - 1 - i, 0)),
                   pl.BlockSpec((tm, fw), lambda bi, i: (bi * nt + i, 0))],
        out_shape=[jax.ShapeDtypeStruct((b * half, fw), BF16), jax.ShapeDtypeStruct((b * half, fw), BF16)],
        scratch_shapes=[pltpu.VMEM((half, fw), BF16), pltpu.VMEM((half, fw), BF16),
                        pltpu.VMEM((16, fw), F32), pltpu.VMEM((alt.shape[0], fw), F32)],
        compiler_params=_cparams(("arbitrary", "arbitrary")),
        name="fourier_%d" % seq,
    )(uf, uf, w_cs, cos_t, sin_t, alt, s1, bias)
    return jnp.concatenate([lo.reshape(b, half, fw), up.reshape(b, half, fw)], axis=1).reshape(nr, fw)


def _fourier_weights(w_four, seq):
    groups, gc, _ = w_four.shape
    k = np.arange(gc)
    ang = ((k[:, None] * k[None, :]) % gc).astype(np.float64) * (2.0 * np.pi / gc)
    norm = 1.0 / math.sqrt(seq * gc)
    cc = jnp.asarray(np.cos(ang) * norm, F32)
    sc = jnp.asarray(np.sin(ang) * norm, F32)
    hp = lax.Precision.HIGHEST
    wc = jnp.einsum('ck,gkd->gcd', cc, w_four.astype(F32), precision=hp)
    ws = jnp.einsum('ck,gkd->gcd', sc, w_four.astype(F32), precision=hp)
    return jnp.concatenate([wc, ws], axis=-1).astype(BF16)


def _outproj_kernel(att_ref, us_ref, yc_ref, uf_ref, x_ref, mod_ref, wo_ref, wg_ref, bg_ref, dsk_ref,
                    gpost_ref, gpre_ref, xo_ref, h2_ref, *, aw, sw):
    nblk = us_ref.shape[0]
    tm = x_ref.shape[0]
    gate_g = mod_ref[0, 2:3, :] * gpost_ref[...]
    scale_g = gpre_ref[...] * (1.0 + mod_ref[0, 4:5, :])
    for r0 in range(0, tm, tm // 2):
        rows = slice(r0, r0 + tm // 2)
        us = jnp.concatenate([us_ref[j, rows, :] for j in range(nblk)], axis=1)
        yc = jnp.concatenate([yc_ref[j, rows, :] for j in range(nblk)], axis=1)
        g = jax.nn.gelu(dsk_ref[...] * us + yc)
        z = jnp.dot(g.astype(BF16), wg_ref[...], preferred_element_type=F32) + bg_ref[...]
        ssm = (g * jax.nn.sigmoid(z)).astype(BF16)
        mix = (jnp.dot(att_ref[rows, :], wo_ref[0:aw, :], preferred_element_type=F32)
               + jnp.dot(ssm, wo_ref[aw:aw + sw, :], preferred_element_type=F32)
               + jnp.dot(uf_ref[rows, :], wo_ref[aw + sw:, :], preferred_element_type=F32))
        xn = x_ref[rows, :] + _rms(mix, NORM_EPS) * gate_g
        xo_ref[rows, :] = xn
        h2_ref[rows, :] = (_rms(xn, NORM_EPS) * scale_g + mod_ref[0, 3:4, :]).astype(BF16)


def _outproj(att, us, yc, four, x, mod, w_out, w_glu, b_glu, dsk, g_post, g_pre, *, layer, mod_row, tm):
    n_rows, d = x.shape
    aw, fw = att.shape[1], four.shape[1]
    nblk = us.shape[0]
    sw = nblk * LANES

    def row(i):
        return (i, 0)

    def const(i):
        return (0, 0)

    blk3 = pl.BlockSpec((nblk, tm, LANES), lambda i: (0, i, 0))
    return pl.pallas_call(
        functools.partial(_outproj_kernel, aw=aw, sw=sw),
        grid=(n_rows // tm,),
        in_specs=[
            pl.BlockSpec((tm, aw), row), blk3, blk3,
            pl.BlockSpec((tm, fw), row), pl.BlockSpec((tm, d), row),
            pl.BlockSpec((1, N_MOD, d), lambda i: (mod_row(i), 0, 0)),
            _resident(w_out.shape), _resident_layer(w_glu.shape, layer),
            pl.BlockSpec((1, sw), const), pl.BlockSpec((1, sw), const),
            pl.BlockSpec((1, d), const), pl.BlockSpec((1, d), const),
        ],
        out_specs=[pl.BlockSpec((tm, d), row), pl.BlockSpec((tm, d), row)],
        out_shape=[jax.ShapeDtypeStruct((n_rows, d), F32), jax.ShapeDtypeStruct((n_rows, d), BF16)],
        compiler_params=_cparams(("arbitrary",)),
        name="outproj",
    )(att, us, yc, four, x, mod, w_out, w_glu, b_glu, dsk, g_post, g_pre)


def _ffn_kernel(h_ref, x_ref, mod_ref, g_ref, wg_ref, wu_ref, wd_ref, o_ref):
    k = pl.program_id(1)

    @pl.when(k == 0)
    def _():
        o_ref[...] = jnp.zeros(o_ref.shape, F32)

    h = h_ref[...]
    tf = wg_ref.shape[1]
    part = None
    for c0 in range(0, tf, tf // 2):
        cols = slice(c0, c0 + tf // 2)
        a = jnp.dot(h, wg_ref[:, cols], preferred_element_type=F32)
        u = jnp.dot(h, wu_ref[:, cols], preferred_element_type=F32)
        p = jnp.dot((_silu(a) * u).astype(BF16), wd_ref[cols, :], preferred_element_type=F32)
        part = p if part is None else part + p
    o_ref[...] += part

    @pl.when(k == pl.num_programs(1) - 1)
    def _():
        o_ref[...] = x_ref[...] + _rms(o_ref[...], NORM_EPS) * (mod_ref[0, 5:6, :] * g_ref[...])


def _ffn(h2, x_mid, mod, g_post, w_gate, w_up, w_down, *, mod_row, tm, tf):
    n_rows, d = h2.shape
    dff = w_gate.shape[1]
    return pl.pallas_call(
        _ffn_kernel,
        grid=(n_rows // tm, dff // tf),
        in_specs=[
            pl.BlockSpec((tm, d), lambda i, k: (i, 0)),
            pl.BlockSpec((tm, d), lambda i, k: (i, 0)),
            pl.BlockSpec((1, N_MOD, d), lambda i, k: (mod_row(i), 0, 0)),
            pl.BlockSpec((1, d), lambda i, k: (0, 0)),
            pl.BlockSpec((d, tf), lambda i, k: (0, k)),
            pl.BlockSpec((d, tf), lambda i, k: (0, k)),
            pl.BlockSpec((tf, d), lambda i, k: (k, 0)),
        ],
        out_specs=pl.BlockSpec((tm, d), lambda i, k: (i, 0)),
        out_shape=jax.ShapeDtypeStruct((n_rows, d), F32),
        compiler_params=_cparams(("arbitrary", "arbitrary")),
        name="ffn",
    )(h2, x_mid, mod, g_post, w_gate, w_up, w_down)


def kernel(x, c, ctx, c_ctx, w_mod, b_mod, g_mix_pre, g_mix_post, g_ffn_pre, g_ffn_post, w_in, w_out, lam_q1, lam_k1, lam_q2, lam_k2, g_subln, ssm_a_re, ssm_a_im, ssm_log_dt, ssm_b_re, ssm_b_im, ssm_c_re, ssm_c_im, ssm_d, w_glu, b_glu, w_four, b_four, w_gate, w_up, w_down):
    b, t, d = x.shape
    n_ctx = ctx.shape[1]
    depth = w_mod.shape[0]
    nl, nc = b * t, b * n_ctx
    aw = d // 2
    sw = ssm_d.shape[1]
    fw = d - aw - sw
    heads = aw // (2 * DA_HEAD_DIM)
    tm = 512
    assert t % tm == 0 and nc % tm == 0 and n_ctx % SSM_CHUNK == 0 and t % GRID_W == 0 and b + 1 <= MOD_ROWS
    assert sw % LANES == 0 and (sw // SSM_GROUP) % (2 * OCT) == 0

    cs = jnp.concatenate([c, c_ctx[None, :], jnp.zeros((MOD_ROWS - b - 1, d), F32)], axis=0)
    mod_all = _modulation(cs, w_mod, b_mod).reshape(depth, MOD_ROWS, N_MOD, d)

    per_b = t // tm
    lat_mod = lambda i: i // per_b
    ctx_mod = lambda i: b
    tabs = _rope_tables(t, tm)
    dft_lat = _dft_half_tables(t)
    dft_ctx = _dft_half_tables(n_ctx)
    perm = _atom_transpose_matrix()
    x_lat, x_ctx = x.reshape(nl, d), ctx.reshape(nc, d)
    w_glu_b = w_glu.astype(BF16)
    w_in_b = w_in[0].astype(BF16)
    ssm_w = jax.vmap(_ssm_weights)(ssm_a_re, ssm_a_im, ssm_log_dt, ssm_b_re, ssm_b_im, ssm_c_re, ssm_c_im)

    for l in range(depth):
        need_ctx = l < depth - 1
        lam_init = 0.8 - 0.6 * math.exp(-0.3 * l)
        mod = mod_all[l]
        g_pre = g_mix_pre[l][None, :]
        casts = [(w_out, l)] + ([(w_in, l + 1)] if l + 1 < depth else [])
        q, k, vt, us, uf, w_out_b, *w_in_next = _inproj(x_lat, mod, g_pre, w_in_b, tabs, mod_row=lat_mod,
                                                        tab_blk=lambda i: i % per_b, aw=aw, sw=sw, fw=fw, tm=tm,
                                                        cast=casts)
        qc, kc, vtc, usc, ufc = _inproj(x_ctx, mod, g_pre, w_in_b, tabs, mod_row=ctx_mod,
                                        tab_blk=lambda i: per_b, aw=aw, sw=sw, fw=fw, tm=tm)
        w_in_b = w_in_next[0] if w_in_next else None

        lam4 = jnp.stack([lam_q1[l], lam_k1[l], lam_q2[l], lam_k2[l]]).astype(F32)
        gs = g_subln[l][None, :].astype(F32)
        att, *ffn_wb = _attention(lam4, gs, q, [(kc, vtc), (k, vt)], lam_init=lam_init, b=b, heads=heads, tq=1024,
                                  name="attn_latent", cast=(l, w_gate, w_up, w_down))
        yc, ycc = _ssm_conv(us, usc, ssm_w, perm, layer=l, b=b)
        bias = b_four[l].reshape(1, fw).astype(F32)
        four = _fourier(uf, _fourier_weights(w_four[l], t), bias, dft_lat, b=b, tm=tm)

        small = (b_glu[l][None, :].astype(F32), ssm_d[l][None, :].astype(F32), g_mix_post[l][None, :],
                 g_ffn_pre[l][None, :])
        ffn_w = (g_ffn_post[l][None, :], *ffn_wb)
        x_mid, h2 = _outproj(att, us, yc, four, x_lat, mod, w_out_b, w_glu_b, *small, layer=l, mod_row=lat_mod,
                             tm=tm)
        x_lat = _ffn(h2, x_mid, mod, *ffn_w, mod_row=lat_mod, tm=tm, tf=512)

        if need_ctx:
            att_c, = _attention(lam4, gs, qc, [(kc, vtc)], lam_init=lam_init, b=b, heads=heads, tq=n_ctx,
                                name="attn_ctx")
            four_c = _fourier(ufc, _fourier_weights(w_four[l], n_ctx), bias, dft_ctx, b=b, tm=tm)
            xc_mid, h2c = _outproj(att_c, usc, ycc, four_c, x_ctx, mod, w_out_b, w_glu_b, *small, layer=l,
                                   mod_row=ctx_mod, tm=tm)
            x_ctx = _ffn(h2c, xc_mid, mod, *ffn_w, mod_row=ctx_mod, tm=tm, tf=512)
    return x_lat.reshape(b, t, d)
```

```python
import functools
import math

import jax
import jax.numpy as jnp
import numpy as np
from jax import lax
from jax.experimental import pallas as pl
from jax.experimental.pallas import tpu as pltpu

F32 = jnp.float32
BF16 = jnp.bfloat16

LANES = 128
GRID_W = 64
DA_HEAD_DIM = 128
SSM_GROUP = 16
SSM_STATE = 64
N_MOD = 6
ROPE_BASE = 10000.0
ROPE_PAIRS = DA_HEAD_DIM // 4
NORM_EPS = 1e-6
SUBLN_EPS = 1e-5

SSM_CHUNK = 16
SSM_CW = SSM_CHUNK * SSM_GROUP
OCT = LANES // SSM_GROUP
MOD_ROWS = 8
VMEM_LIMIT = 56 * 1024 * 1024
NT_DIMS = (((1,), (1,)), ((), ()))


def _cparams(sem):
    return pltpu.CompilerParams(dimension_semantics=sem, vmem_limit_bytes=VMEM_LIMIT)


def _rms(x, eps):
    return x * lax.rsqrt(jnp.mean(x * x, axis=-1, keepdims=True) + eps)


def _silu(x):
    return x * jax.nn.sigmoid(x)


def _resident(shape):
    nd = len(shape)
    return pl.BlockSpec(shape, lambda *_: (0,) * nd, pipeline_mode=pl.Buffered(1))


def _resident_layer(shape, layer):
    nd = len(shape)
    return pl.BlockSpec((None,) + tuple(shape[1:]), lambda *_: (layer,) + (0,) * (nd - 1),
                        pipeline_mode=pl.Buffered(1))


def _mod_kernel(c_ref, w_ref, b_ref, o_ref):
    s = _silu(c_ref[...]).astype(BF16)
    o_ref[0] = jnp.dot(s, w_ref[0].astype(BF16), preferred_element_type=F32) + b_ref[0]


def _modulation(cs, w_mod, b_mod, tn=1024):
    depth, d, n = w_mod.shape
    return pl.pallas_call(
        _mod_kernel,
        grid=(depth, n // tn),
        in_specs=[
            pl.BlockSpec((MOD_ROWS, d), lambda l, j: (0, 0)),
            pl.BlockSpec((1, d, tn), lambda l, j: (l, 0, j)),
            pl.BlockSpec((1, 1, tn), lambda l, j: (l, 0, j)),
        ],
        out_specs=pl.BlockSpec((1, MOD_ROWS, tn), lambda l, j: (l, 0, j)),
        out_shape=jax.ShapeDtypeStruct((depth, MOD_ROWS, n), F32),
        compiler_params=_cparams(("arbitrary", "arbitrary")),
        name="modulation",
    )(cs, w_mod, b_mod.reshape(depth, 1, n))


def _inproj_kernel(x_ref, mod_ref, g_ref, w_ref, cq_ref, sq_ref, ck_ref, sk_ref, *refs, aw, sw):
    n_cast = (len(refs) - 5) // 2
    q_ref, k_ref, vt_ref, us_ref, uf_ref = refs[n_cast:n_cast + 5]
    for src, dst in zip(refs[:n_cast], refs[n_cast + 5:]):
        dst[...] = src[...].astype(BF16)
    tm = x_ref.shape[0]
    hb = (_rms(x_ref[...], NORM_EPS) * (g_ref[...] * (1.0 + mod_ref[0, 1:2, :])) + mod_ref[0, 0:1, :]).astype(BF16)

    lane = lax.broadcasted_iota(jnp.int32, (tm, DA_HEAD_DIM), 1)
    first_half = (lane & (2 * ROPE_PAIRS - 1)) < ROPE_PAIRS

    def rope(z, c, s):
        partner = jnp.where(first_half, pltpu.roll(z, DA_HEAD_DIM - ROPE_PAIRS, 1), pltpu.roll(z, ROPE_PAIRS, 1))
        return z * c + partner * s

    zq = jnp.dot(hb, w_ref[:, 0:aw], preferred_element_type=F32)
    cq, sq = cq_ref[...], sq_ref[...]
    for j in range(aw // DA_HEAD_DIM):
        sl = slice(j * DA_HEAD_DIM, (j + 1) * DA_HEAD_DIM)
        q_ref[:, sl] = rope(zq[:, sl], cq, sq).astype(BF16)
    zk = jnp.dot(hb, w_ref[:, aw:2 * aw], preferred_element_type=F32)
    ck, sk = ck_ref[...], sk_ref[...]
    for j in range(aw // DA_HEAD_DIM):
        sl = slice(j * DA_HEAD_DIM, (j + 1) * DA_HEAD_DIM)
        k_ref[:, sl] = rope(zk[:, sl], ck, sk).astype(BF16)
    zv = jnp.dot(hb, w_ref[:, 2 * aw:3 * aw], preferred_element_type=F32)
    vt_ref[...] = jnp.transpose(zv.astype(BF16))
    zs = jnp.dot(hb, w_ref[:, 3 * aw:3 * aw + sw], preferred_element_type=F32)
    for j in range(sw // LANES):
        us_ref[j] = zs[:, j * LANES:(j + 1) * LANES]
    uf_ref[...] = jnp.dot(hb, w_ref[:, 3 * aw + sw:], preferred_element_type=F32).astype(BF16)


def _inproj(x, mod, g, w_in, tabs, *, mod_row, tab_blk, aw, sw, fw, tm, cast=()):
    nr, d = x.shape
    n_steps = nr // tm

    def row(i):
        return (i, 0)

    tab_spec = pl.BlockSpec((tm, DA_HEAD_DIM), lambda i: (tab_blk(i), 0))
    cast_in, cast_out, cast_shape = [], [], []
    for w, layer in cast:
        _, r, c = w.shape
        per = next(dv for dv in range(1, c // LANES + 1) if (c // LANES) % dv == 0 and dv * n_steps >= c // LANES)
        cw = per * LANES
        cast_in.append(pl.BlockSpec((None, r, cw), lambda i, layer=layer, nb=c // cw: (layer, 0, jnp.minimum(i, nb - 1))))
        cast_out.append(pl.BlockSpec((r, cw), lambda i, nb=c // cw: (0, jnp.minimum(i, nb - 1))))
        cast_shape.append(jax.ShapeDtypeStruct((r, c), BF16))
    return pl.pallas_call(
        functools.partial(_inproj_kernel, aw=aw, sw=sw),
        grid=(n_steps,),
        in_specs=[
            pl.BlockSpec((tm, d), row),
            pl.BlockSpec((1, N_MOD, d), lambda i: (mod_row(i), 0, 0)),
            pl.BlockSpec((1, d), lambda i: (0, 0)),
            _resident(w_in.shape),
            tab_spec, tab_spec, tab_spec, tab_spec,
        ] + cast_in,
        out_specs=[
            pl.BlockSpec((tm, aw), row), pl.BlockSpec((tm, aw), row), pl.BlockSpec((aw, tm), lambda i: (0, i)),
            pl.BlockSpec((sw // LANES, tm, LANES), lambda i: (0, i, 0)), pl.BlockSpec((tm, fw), row),
        ] + cast_out,
        out_shape=[
            jax.ShapeDtypeStruct((nr, aw), BF16), jax.ShapeDtypeStruct((nr, aw), BF16),
            jax.ShapeDtypeStruct((aw, nr), BF16), jax.ShapeDtypeStruct((sw // LANES, nr, LANES), F32),
            jax.ShapeDtypeStruct((nr, fw), BF16),
        ] + cast_shape,
        compiler_params=_cparams(("arbitrary",)),
        name="inproj",
    )(x, mod, g, w_in, *tabs, *[w for w, _ in cast])


def _rope_tables(t, tm):
    rows = t // GRID_W
    r = np.repeat(np.arange(rows, dtype=np.float64), GRID_W)
    col = np.tile(np.arange(GRID_W, dtype=np.float64), rows)
    inv = ROPE_BASE ** (-np.arange(ROPE_PAIRS, dtype=np.float64) / ROPE_PAIRS)
    ar, ac = r[:, None] * inv, col[:, None] * inv
    cos = np.concatenate([np.cos(ar), np.cos(ar), np.cos(ac), np.cos(ac)], axis=1)
    sin = np.concatenate([-np.sin(ar), np.sin(ar), -np.sin(ac), np.sin(ac)], axis=1)
    cos = np.concatenate([cos, np.ones((tm, DA_HEAD_DIM))], axis=0)
    sin = np.concatenate([sin, np.zeros((tm, DA_HEAD_DIM))], axis=0)
    scale = DA_HEAD_DIM ** -0.5 * math.log2(math.e)
    return tuple(jnp.asarray(a, F32) for a in (cos * scale, sin * scale, cos, sin))


def _attn_kernel(lam_ref, gs_ref, q_ref, *refs, lam_init, n_sets, n_cast):
    k_refs, vt_refs = refs[0:2 * n_sets:2], refs[1:2 * n_sets:2]
    cast_in = refs[2 * n_sets:2 * n_sets + n_cast]
    o_ref = refs[2 * n_sets + n_cast]
    cast_out = refs[2 * n_sets + n_cast + 1:]
    for src, dst in zip(cast_in, cast_out):
        dst[...] = src[...].astype(BF16)

    hd = DA_HEAD_DIM
    s1 = jnp.sum(lam_ref[0:1, :] * lam_ref[1:2, :], axis=-1, keepdims=True)
    s2 = jnp.sum(lam_ref[2:3, :] * lam_ref[3:4, :], axis=-1, keepdims=True)
    lam = jnp.exp(s1) - jnp.exp(s2) + lam_init
    outs = []
    for idx in range(2):
        qi = q_ref[:, idx * hd:(idx + 1) * hd]
        ss = [lax.dot_general(kr[:, idx * hd:(idx + 1) * hd], qi, NT_DIMS, preferred_element_type=F32)
              for kr in k_refs]
        m = functools.reduce(jnp.maximum, [jnp.max(s, axis=0, keepdims=True) for s in ss])
        ps = [jnp.exp2(s - m) for s in ss]
        l = functools.reduce(jnp.add, [jnp.sum(p, axis=0, keepdims=True) for p in ps])
        acc = functools.reduce(jnp.add, [jnp.dot(vr[...], p.astype(BF16), preferred_element_type=F32)
                                         for vr, p in zip(vt_refs, ps)])
        outs.append(acc * (1.0 / l))
    o = jnp.transpose(outs[0] - outs[1] * lam)
    o_ref[...] = (_rms(o, SUBLN_EPS) * gs_ref[...] * (1.0 - lam_init)).astype(BF16)


def _attention(lam4, gs, q, kv_sets, *, lam_init, b, heads, tq, name, cast=None):
    nq, aw = q.shape
    vw = 2 * DA_HEAD_DIM
    per_b = nq // b // tq
    q_spec = pl.BlockSpec((tq, vw), lambda bi, h, i: (bi * per_b + i, h))
    in_specs = [pl.BlockSpec((4, DA_HEAD_DIM), lambda bi, h, i: (0, 0)),
                pl.BlockSpec((1, vw), lambda bi, h, i: (0, 0)),
                q_spec]
    args = [lam4, gs, q]
    for k, vt in kv_sets:
        sk = k.shape[0] // b
        in_specs += [pl.BlockSpec((sk, vw), lambda bi, h, i: (bi, h)),
                     pl.BlockSpec((vw, sk), lambda bi, h, i: (h, bi))]
        args += [k, vt]
    out_specs = [q_spec]
    out_shape = [jax.ShapeDtypeStruct((nq, aw), BF16)]
    n_cast = 0
    if cast is not None:
        layer, w_gate, w_up, w_down = cast
        _, d, dff = w_gate.shape
        n_steps = b * heads * per_b
        per = next(dv for dv in range(1, dff // LANES + 1) if (dff // LANES) % dv == 0 and dv * n_steps >= dff // LANES)
        cw = per * LANES
        nblk = dff // cw

        def blk(bi, h, i):
            return jnp.minimum((bi * heads + h) * per_b + i, nblk - 1)

        in_specs += [pl.BlockSpec((None, d, cw), lambda bi, h, i: (layer, 0, blk(bi, h, i))),
                     pl.BlockSpec((None, d, cw), lambda bi, h, i: (layer, 0, blk(bi, h, i))),
                     pl.BlockSpec((None, cw, d), lambda bi, h, i: (layer, blk(bi, h, i), 0))]
        args += [w_gate, w_up, w_down]
        out_specs += [pl.BlockSpec((d, cw), lambda bi, h, i: (0, blk(bi, h, i))),
                      pl.BlockSpec((d, cw), lambda bi, h, i: (0, blk(bi, h, i))),
                      pl.BlockSpec((cw, d), lambda bi, h, i: (blk(bi, h, i), 0))]
        out_shape += [jax.ShapeDtypeStruct((d, dff), BF16), jax.ShapeDtypeStruct((d, dff), BF16),
                      jax.ShapeDtypeStruct((dff, d), BF16)]
        n_cast = 3
    return pl.pallas_call(
        functools.partial(_attn_kernel, lam_init=lam_init, n_sets=len(kv_sets), n_cast=n_cast),
        grid=(b, heads, per_b),
        in_specs=in_specs,
        out_specs=out_specs,
        out_shape=out_shape,
        compiler_params=_cparams(("arbitrary", "arbitrary", "arbitrary")),
        name=name,
    )(*args)


def _ssm_weights(a_re, a_im, log_dt, b_re, b_im, c_re, c_im):
    L, H, P = SSM_CHUNK, SSM_GROUP, SSM_STATE
    g = a_re.shape[1]
    npair = g // 2
    hp = lax.Precision.HIGHEST
    ar, ai = a_re.astype(F32), a_im.astype(F32)
    dt = jnp.exp(log_dt.astype(F32))[..., None]
    steps = np.arange(L, dtype=np.float32)

    def powers(fwd, bwd):
        e = jnp.asarray(np.stack([fwd, bwd]), F32)[:, None, None, :]
        mag = jnp.exp((ar * dt)[..., None] * e)
        ang = (ai * dt)[..., None] * e
        return mag * jnp.cos(ang), mag * jnp.sin(ang)

    one = np.ones(1, np.float32)
    a1r, a1i = powers(one, one)
    alr, ali = powers(L * one, L * one)
    xr, xi = a1r[..., 0] - 1.0, a1i[..., 0]
    den = ar * ar + ai * ai
    qr, qi = (xr * ar + xi * ai) / den, (xi * ar - xr * ai) / den
    br, bi = b_re.astype(F32), b_im.astype(F32)
    bbr = qr[..., None] * br - qi[..., None] * bi
    bbi = qr[..., None] * bi + qi[..., None] * br
    cr, ci = c_re.astype(F32), c_im.astype(F32)

    kpr, kpi = powers(steps, L - 1 - steps)
    car = cr[..., None] * kpr[:, :, None] - ci[..., None] * kpi[:, :, None]
    cai = cr[..., None] * kpi[:, :, None] + ci[..., None] * kpr[:, :, None]
    kt = (jnp.einsum('dghpt,dgpk->dgkth', car, bbr, precision=hp)
          - jnp.einsum('dghpt,dgpk->dgkth', cai, bbi, precision=hp)).reshape(2, g, H, L * H)
    kc = jnp.concatenate([kt[1][..., :(L - 1) * H], kt[0][..., :H] + kt[1][..., (L - 1) * H:], kt[0][..., H:]],
                         axis=-1)
    w_intra = jnp.pad(kc, ((0, 0), (0, 0), (0, (-kc.shape[2]) % LANES)))

    sir, sii = powers(L - 1 - steps, steps)
    sir, sii = jnp.swapaxes(sir, 2, 3)[:, :, :, None, :], jnp.swapaxes(sii, 2, 3)[:, :, :, None, :]
    tbr, tbi = jnp.swapaxes(bbr, 2, 3)[:, :, None], jnp.swapaxes(bbi, 2, 3)[:, :, None]
    w_sin = jnp.concatenate([sir * tbr - sii * tbi, sir * tbi + sii * tbr], axis=-1)
    w_sin = w_sin.reshape(2, g, L * H, 2 * P)

    sor, soi = powers(steps + 1, L - steps)
    tcr, tci = jnp.swapaxes(cr, 2, 3)[:, :, :, None, :], jnp.swapaxes(ci, 2, 3)[:, :, :, None, :]
    cnr = (tcr * sor[..., None] - tci * soi[..., None]).reshape(2, g, P, L * H)
    cni = (tcr * soi[..., None] + tci * sor[..., None]).reshape(2, g, P, L * H)
    w_so = jnp.concatenate([cnr, -cni], axis=2)

    al = jnp.stack([alr[..., 0].reshape(2, npair, 2 * P), ali[..., 0].reshape(2, npair, 2 * P)], axis=2)
    al = al.reshape(2, 2 * npair, 1, 2 * P)
    return w_intra, w_sin.astype(BF16), w_so.astype(BF16), al


def _atom_transpose_matrix():
    n = OCT * OCT * SSM_GROUP
    i = np.arange(n)
    j = ((i // SSM_GROUP) % OCT) * LANES + (i // LANES) * SSM_GROUP + i % SSM_GROUP
    p = np.zeros((n, n), np.float32)
    p[i, j] = 1.0
    return jnp.asarray(p, BF16)


def _ssm_in_kernel(us_ref, p_ref, w_ref, ut_ref, s_ref):
    noct, nb = us_ref.shape[0], us_ref.shape[1]
    nc = us_ref.shape[2] // SSM_CHUNK
    n = nb * nc
    half = SSM_CHUNK // 2
    src = []
    for o in range(noct):
        for hf in range(2):
            per_seq = [jnp.concatenate([us_ref[o, bi, pl.ds(half * hf + sl, nc, stride=SSM_CHUNK), :]
                                        for sl in range(half)], axis=1) for bi in range(nb)]
            src.append(jnp.concatenate(per_seq, axis=0).astype(BF16))
    perm = jnp.dot(jnp.concatenate(src, axis=0), p_ref[...], preferred_element_type=F32).astype(BF16)

    def group_chunk(gi):
        o, gl = gi // OCT, gi % OCT
        return jnp.concatenate([perm[(2 * o + hf) * n:(2 * o + hf + 1) * n, gl * LANES:(gl + 1) * LANES]
                                for hf in range(2)], axis=1)

    def store_chunk_major(d, blk, val):
        for bi in range(nb):
            s_ref[d, blk, pl.ds(bi, nc, stride=nb), :] = val[bi * nc:(bi + 1) * nc]

    low = lax.broadcasted_iota(jnp.int32, (n, LANES), 1) < SSM_STATE
    for j in range(noct * OCT // 2):
        ug = [group_chunk(2 * j), group_chunk(2 * j + 1)]
        ut_ref[2 * j] = ug[0]
        ut_ref[2 * j + 1] = ug[1]
        for d in range(2):
            s0 = jnp.dot(ug[0], w_ref[d, 2 * j], preferred_element_type=F32)
            s1 = jnp.dot(ug[1], w_ref[d, 2 * j + 1], preferred_element_type=F32)
            store_chunk_major(d, 2 * j, jnp.where(low, s0, pltpu.roll(s1, SSM_STATE, 1)))
            store_chunk_major(d, 2 * j + 1, jnp.where(low, pltpu.roll(s0, SSM_STATE, 1), s1))


def _ssm_scan_kernel(sc_ref, sl_ref, al_ref, hc_ref, hl_ref, *, b):
    d = pl.program_id(0)
    nblk = sc_ref.shape[1]
    tile = 8
    cpt = tile // b
    grp = lax.broadcasted_iota(jnp.int32, (tile, LANES), 0) // b
    ars = [al_ref[0, 2 * k] for k in range(nblk // 2)]
    ais = [al_ref[0, 2 * k + 1] for k in range(nblk // 2)]

    def spread(x, g):
        x = jnp.where(grp == g, x, 0.0)
        out = x
        for r in range(1, cpt):
            out = out + pltpu.roll(x, r * b, 0)
        return out

    def phase(s_ref, h_ref, carry):
        ntile = s_ref.shape[2] // tile

        def body(i, carry):
            t = jnp.where(d == 0, i, ntile - 1 - i)
            rows = pl.ds(pl.multiple_of(t * tile, tile), tile)
            new = []
            for k in range(nblk // 2):
                hr, hi = carry[2 * k], carry[2 * k + 1]
                sr, si = s_ref[0, 2 * k, rows, :], s_ref[0, 2 * k + 1, rows, :]
                hin_r, hin_i = hr, hi
                for step in range(cpt):
                    g = jnp.where(d == 0, step, cpt - 1 - step)
                    hin_r = jnp.where(grp == g, hr, hin_r)
                    hin_i = jnp.where(grp == g, hi, hin_i)
                    sgr, sgi = spread(sr, g), spread(si, g)
                    hr, hi = ars[k] * hr - ais[k] * hi + sgr, ars[k] * hi + ais[k] * hr + sgi
                h_ref[0, 2 * k, rows, :] = hin_r
                h_ref[0, 2 * k + 1, rows, :] = hin_i
                new += [hr, hi]
            return tuple(new)

        return lax.fori_loop(0, ntile, body, carry, unroll=2)

    carry = tuple(jnp.zeros((tile, LANES), F32) for _ in range(nblk))
    carry = phase(sc_ref, hc_ref, carry)
    phase(sl_ref, hl_ref, carry)


def _ssm_out_kernel(ut_ref, h_ref, kc_ref, wo_ref, p_ref, y_ref, wi_ref):
    n = ut_ref.shape[1]
    ng = ut_ref.shape[0]
    nb = y_ref.shape[1]
    nc = n // nb
    noct = ng // OCT
    half = SSM_CHUNK // 2

    @pl.when(pl.program_id(0) == 0)
    def _():
        for gi in range(ng):
            kc = kc_ref[gi]
            wi_ref[gi] = jnp.concatenate(
                [kc[:, (SSM_CHUNK - 1 - s) * SSM_GROUP:(SSM_CHUNK - 1 - s) * SSM_GROUP + SSM_CW]
                 for s in range(SSM_CHUNK)], axis=0).astype(BF16)

    def load_seq_major(d, blk):
        return jnp.concatenate([h_ref[d, blk, pl.ds(bi, nc, stride=nb), :] for bi in range(nb)], axis=0)

    low = lax.broadcasted_iota(jnp.int32, (n, LANES), 1) < SSM_STATE
    ys = []
    for j in range(ng // 2):
        hg = [[], []]
        for d in range(2):
            hr, hi = load_seq_major(d, 2 * j), load_seq_major(d, 2 * j + 1)
            hg[0].append(jnp.where(low, hr, pltpu.roll(hi, SSM_STATE, 1)).astype(BF16))
            hg[1].append(jnp.where(low, pltpu.roll(hr, SSM_STATE, 1), hi).astype(BF16))
        for e in range(2):
            gi = 2 * j + e
            ys.append(jnp.dot(ut_ref[gi], wi_ref[gi], preferred_element_type=F32)
                      + jnp.dot(hg[e][0], wo_ref[0, gi], preferred_element_type=F32)
                      + jnp.dot(hg[e][1], wo_ref[1, gi], preferred_element_type=F32))
    rows = []
    for o in range(noct):
        for hf in range(2):
            rows.append(jnp.concatenate([ys[OCT * o + gl][:, hf * LANES:(hf + 1) * LANES] for gl in range(OCT)],
                                        axis=1))
    ycat = jnp.concatenate(rows, axis=0)
    hi = ycat.astype(BF16)
    lo = (ycat - hi.astype(F32)).astype(BF16)
    res = (jnp.dot(hi, p_ref[...], preferred_element_type=F32)
           + jnp.dot(lo, p_ref[...], preferred_element_type=F32))
    for o in range(noct):
        for hf in range(2):
            blk = 2 * o + hf
            for tl in range(half):
                piece = res[blk * n:(blk + 1) * n, tl * LANES:(tl + 1) * LANES]
                for bi in range(nb):
                    y_ref[o, bi, pl.ds(half * hf + tl, nc, stride=SSM_CHUNK), :] = piece[bi * nc:(bi + 1) * nc]


def _ssm_conv(us_lat, us_ctx, weights, perm, *, layer, b):
    w_intra, w_sin, w_so, al = weights
    P = SSM_STATE
    ng = w_intra.shape[1]
    nblk = ng
    assert 8 % b == 0

    def stage_in(us, steps):
        noct, rows, _ = us.shape
        seq = rows // b
        tok = seq // steps
        n = b * tok // SSM_CHUNK
        nchunks = rows // SSM_CHUNK
        return pl.pallas_call(
            _ssm_in_kernel,
            grid=(steps,),
            in_specs=[pl.BlockSpec((noct, b, tok, LANES), lambda i: (0, 0, i, 0)),
                      _resident(perm.shape), _resident_layer(w_sin.shape, layer)],
            out_specs=[pl.BlockSpec((ng, n, SSM_CW), lambda i: (0, i, 0)),
                       pl.BlockSpec((2, nblk, n, LANES), lambda i: (0, 0, i, 0))],
            out_shape=[jax.ShapeDtypeStruct((ng, nchunks, SSM_CW), BF16),
                       jax.ShapeDtypeStruct((2, nblk, nchunks, LANES), F32)],
            compiler_params=_cparams(("arbitrary",)),
            name="ssm_in",
        )(us.reshape(noct, b, seq, LANES), perm, w_sin)

    lat_steps = 2 * b
    ut_lat, s_lat = stage_in(us_lat, lat_steps)
    ut_ctx, s_ctx = stage_in(us_ctx, 1)

    cb = 16
    nc_rows, nl_rows = s_ctx.shape[2], s_lat.shape[2]
    h_ctx, h_lat = pl.pallas_call(
        functools.partial(_ssm_scan_kernel, b=b),
        grid=(2, nblk // cb),
        in_specs=[pl.BlockSpec((1, cb, nc_rows, LANES), lambda d, j: (d, j, 0, 0)),
                  pl.BlockSpec((1, cb, nl_rows, LANES), lambda d, j: (d, j, 0, 0)),
                  pl.BlockSpec((None, 1, cb, 1, 2 * P), lambda d, j: (layer, d, j, 0, 0))],
        out_specs=[pl.BlockSpec((1, cb, nc_rows, LANES), lambda d, j: (d, j, 0, 0)),
                   pl.BlockSpec((1, cb, nl_rows, LANES), lambda d, j: (d, j, 0, 0))],
        out_shape=[jax.ShapeDtypeStruct(s_ctx.shape, F32), jax.ShapeDtypeStruct(s_lat.shape, F32)],
        compiler_params=_cparams(("arbitrary", "arbitrary")),
        name="ssm_scan",
    )(s_ctx, s_lat, al)

    def stage_out(ut, hin, steps):
        _, nchunks, _ = ut.shape
        n = nchunks // steps
        noct = ng // OCT
        seq = nchunks * SSM_CHUNK // b
        tok = seq // steps
        y = pl.pallas_call(
            _ssm_out_kernel,
            grid=(steps,),
            in_specs=[pl.BlockSpec((ng, n, SSM_CW), lambda i: (0, i, 0)),
                      pl.BlockSpec((2, nblk, n, LANES), lambda i: (0, 0, i, 0)),
                      _resident_layer(w_intra.shape, layer), _resident_layer(w_so.shape, layer),
                      _resident(perm.shape)],
            out_specs=pl.BlockSpec((noct, b, tok, LANES), lambda i: (0, 0, i, 0)),
            out_shape=jax.ShapeDtypeStruct((noct, b, seq, LANES), F32),
            scratch_shapes=[pltpu.VMEM((ng, SSM_CW, SSM_CW), BF16)],
            compiler_params=_cparams(("arbitrary",)),
            name="ssm_out",
        )(ut, hin, w_intra, w_so, perm)
        return y.reshape(noct, b * seq, LANES)

    return stage_out(ut_lat, h_lat, lat_steps), stage_out(ut_ctx, h_ctx, 1)


def _fourier_kernel(ulo_ref, uup_ref, w_ref, cos_ref, sin_ref, alt_ref, s1_ref, b_ref,
                    lo_ref, up_ref, a_s, b_s, am_s, carry, *, groups):
    gc = ulo_ref.shape[1] // groups
    tm = lo_ref.shape[0]
    nt = a_s.shape[0] // tm
    row = lax.broadcasted_iota(jnp.int32, lo_ref.shape, 0)

    @pl.when(pl.program_id(1) == 0)
    def _():
        for jj in range(nt):
            pr = jnp.dot(s1_ref[...], uup_ref[(nt - 1 - jj) * tm:(nt - jj) * tm, :], preferred_element_type=F32)
            if jj > 0:
                first = uup_ref[(nt - jj) * tm:(nt - jj) * tm + am_s.shape[0], :][0:1, :].astype(F32)
                pr = jnp.where(row == 0, first, pr)
            pr = pr.astype(BF16)
            rows = slice(jj * tm, (jj + 1) * tm)
            for g in range(groups):
                cols = slice(g * gc, (g + 1) * gc)
                ab_lo = jnp.dot(ulo_ref[rows, cols], w_ref[g], preferred_element_type=F32)
                ab_pr = jnp.dot(pr[:, cols], w_ref[g], preferred_element_type=F32)
                a_s[rows, cols] = (ab_lo[:, :gc] + ab_pr[:, :gc]).astype(BF16)
                b_s[rows, cols] = (ab_lo[:, gc:] - ab_pr[:, gc:]).astype(BF16)
        for g in range(groups):
            cols = slice(g * gc, (g + 1) * gc)
            am_s[:, cols] = jnp.dot(uup_ref[0:am_s.shape[0], cols], w_ref[g, :, 0:gc], preferred_element_type=F32)
        mid = jnp.dot(alt_ref[...], a_s[...], preferred_element_type=F32)
        carry[...] = mid + am_s[0:carry.shape[0], :] + b_ref[...]

    a_mid = jnp.where((row & 1) == 0, am_s[0:1, :], -am_s[0:1, :])
    p = jnp.dot(cos_ref[...], a_s[...], preferred_element_type=F32) + a_mid + b_ref[...]
    q = jnp.dot(sin_ref[...], b_s[...], preferred_element_type=F32)
    lo_ref[...] = (p - q).astype(BF16)
    hi = (p + q).astype(BF16)
    rev = jnp.dot(s1_ref[...], hi, preferred_element_type=F32)
    up_ref[...] = jnp.where(row == 0, carry[0:1, :], rev).astype(BF16)
    carry[0:1, :] = hi[0:1, :].astype(F32)


def _dft_half_tables(n):
    half = n // 2
    k = np.arange(half, dtype=np.int64)
    ang = ((k[:, None] * k[None, :]) % n).astype(np.float64) * (2.0 * np.pi / n)
    alt = np.broadcast_to(np.where((k & 1) == 0, 1.0, -1.0), (8, half))
    return tuple(jnp.asarray(a, F32).astype(BF16) for a in (np.cos(ang), np.sin(ang), alt))


def _fourier(uf, w_cs, bias, tables, *, b, tm):
    nr, fw = uf.shape
    seq = nr // b
    half = seq // 2
    groups = w_cs.shape[0]
    cos_t, sin_t, alt = tables
    tm = min(tm, half)
    nt = half // tm
    r = np.arange(1, tm)
    s1 = np.zeros((tm, tm), np.float32)
    s1[r, tm - r] = 1.0
    s1 = jnp.asarray(s1, BF16)
    lo, up = pl.pallas_call(
        functools.partial(_fourier_kernel, groups=groups),
        grid=(b, nt),
        in_specs=[
            pl.BlockSpec((half, fw), lambda bi, i: (2 * bi, 0)),
            pl.BlockSpec((half, fw), lambda bi, i: (2 * bi + 1, 0)),
            pl.BlockSpec(w_cs.shape, lambda bi, i: (0, 0, 0)),
            pl.BlockSpec((tm, half), lambda bi, i: (nt - 1 - i, 0)),
            pl.BlockSpec((tm, half), lambda bi, i: (nt - 1 - i, 0)),
            pl.BlockSpec(alt.shape, lambda bi, i: (0, 0)),
            pl.BlockSpec((tm, tm), lambda bi, i: (0, 0)),
            pl.BlockSpec((1, fw), lambda bi, i: (0, 0)),
        ],
        out_specs=[pl.BlockSpec((tm, fw), lambda bi, i: (bi * nt + nt - 1 - i, 0)),
                   pl.BlockSpec((tm, fw), lambda bi, i: (bi * nt + i, 0))],
        out_shape=[jax.ShapeDtypeStruct((b * half, fw), BF16), jax.ShapeDtypeStruct((b * half, fw), BF16)],
        scratch_shapes=[pltpu.VMEM((half, fw), BF16), pltpu.VMEM((half, fw), BF16),
                        pltpu.VMEM((16, fw), F32), pltpu.VMEM((alt.shape[0], fw), F32)],
        compiler_params=_cparams(("arbitrary", "arbitrary")),
        name="fourier_%d" % seq,
    )(uf, uf, w_cs, cos_t, sin_t, alt, s1, bias)
    return jnp.concatenate([lo.reshape(b, half, fw), up.reshape(b, half, fw)], axis=1).reshape(nr, fw)


def _fourier_weights(w_four, seq):
    groups, gc, _ = w_four.shape
    k = np.arange(gc)
    ang = ((k[:, None] * k[None, :]) % gc).astype(np.float64) * (2.0 * np.pi / gc)
    norm = 1.0 / math.sqrt(seq * gc)
    cc = jnp.asarray(np.cos(ang) * norm, F32)
    sc = jnp.asarray(np.sin(ang) * norm, F32)
    hp = lax.Precision.HIGHEST
    wc = jnp.einsum('ck,gkd->gcd', cc, w_four.astype(F32), precision=hp)
    ws = jnp.einsum('ck,gkd->gcd', sc, w_four.astype(F32), precision=hp)
    return jnp.concatenate([wc, ws], axis=-1).astype(BF16)


def _outproj_kernel(att_ref, us_ref, yc_ref, uf_ref, x_ref, mod_ref, wo_ref, wg_ref, bg_ref, dsk_ref,
                    gpost_ref, gpre_ref, xo_ref, h2_ref, *, aw, sw):
    nblk = us_ref.shape[0]
    tm = x_ref.shape[0]
    gate_g = mod_ref[0, 2:3, :] * gpost_ref[...]
    scale_g = gpre_ref[...] * (1.0 + mod_ref[0, 4:5, :])
    for r0 in range(0, tm, tm // 2):
        rows = slice(r0, r0 + tm // 2)
        us = jnp.concatenate([us_ref[j, rows, :] for j in range(nblk)], axis=1)
        yc = jnp.concatenate([yc_ref[j, rows, :] for j in range(nblk)], axis=1)
        g = jax.nn.gelu(dsk_ref[...] * us + yc)
        z = jnp.dot(g.astype(BF16), wg_ref[...], preferred_element_type=F32) + bg_ref[...]
        ssm = (g * jax.nn.sigmoid(z)).astype(BF16)
        mix = (jnp.dot(att_ref[rows, :], wo_ref[0:aw, :], preferred_element_type=F32)
               + jnp.dot(ssm, wo_ref[aw:aw + sw, :], preferred_element_type=F32)
               + jnp.dot(uf_ref[rows, :], wo_ref[aw + sw:, :], preferred_element_type=F32))
        xn = x_ref[rows, :] + _rms(mix, NORM_EPS) * gate_g
        xo_ref[rows, :] = xn
        h2_ref[rows, :] = (_rms(xn, NORM_EPS) * scale_g + mod_ref[0, 3:4, :]).astype(BF16)


def _outproj(att, us, yc, four, x, mod, w_out, w_glu, b_glu, dsk, g_post, g_pre, *, layer, mod_row, tm):
    n_rows, d = x.shape
    aw, fw = att.shape[1], four.shape[1]
    nblk = us.shape[0]
    sw = nblk * LANES

    def row(i):
        return (i, 0)

    def const(i):
        return (0, 0)

    blk3 = pl.BlockSpec((nblk, tm, LANES), lambda i: (0, i, 0))
    return pl.pallas_call(
        functools.partial(_outproj_kernel, aw=aw, sw=sw),
        grid=(n_rows // tm,),
        in_specs=[
            pl.BlockSpec((tm, aw), row), blk3, blk3,
            pl.BlockSpec((tm, fw), row), pl.BlockSpec((tm, d), row),
            pl.BlockSpec((1, N_MOD, d), lambda i: (mod_row(i), 0, 0)),
            _resident(w_out.shape), _resident_layer(w_glu.shape, layer),
            pl.BlockSpec((1, sw), const), pl.BlockSpec((1, sw), const),
            pl.BlockSpec((1, d), const), pl.BlockSpec((1, d), const),
        ],
        out_specs=[pl.BlockSpec((tm, d), row), pl.BlockSpec((tm, d), row)],
        out_shape=[jax.ShapeDtypeStruct((n_rows, d), F32), jax.ShapeDtypeStruct((n_rows, d), BF16)],
        compiler_params=_cparams(("arbitrary",)),
        name="outproj",
    )(att, us, yc, four, x, mod, w_out, w_glu, b_glu, dsk, g_post, g_pre)


def _ffn_kernel(h_ref, x_ref, mod_ref, g_ref, wg_ref, wu_ref, wd_ref, o_ref):
    k = pl.program_id(1)

    @pl.when(k == 0)
    def _():
        o_ref[...] = jnp.zeros(o_ref.shape, F32)

    h = h_ref[...]
    tf = wg_ref.shape[1]
    part = None
    for c0 in range(0, tf, tf // 2):
        cols = slice(c0, c0 + tf // 2)
        a = jnp.dot(h, wg_ref[:, cols], preferred_element_type=F32)
        u = jnp.dot(h, wu_ref[:, cols], preferred_element_type=F32)
        p = jnp.dot((_silu(a) * u).astype(BF16), wd_ref[cols, :], preferred_element_type=F32)
        part = p if part is None else part + p
    o_ref[...] += part

    @pl.when(k == pl.num_programs(1) - 1)
    def _():
        o_ref[...] = x_ref[...] + _rms(o_ref[...], NORM_EPS) * (mod_ref[0, 5:6, :] * g_ref[...])


def _ffn(h2, x_mid, mod, g_post, w_gate, w_up, w_down, *, mod_row, tm, tf):
    n_rows, d = h2.shape
    dff = w_gate.shape[1]
    return pl.pallas_call(
        _ffn_kernel,
        grid=(n_rows // tm, dff // tf),
        in_specs=[
            pl.BlockSpec((tm, d), lambda i, k: (i, 0)),
            pl.BlockSpec((tm, d), lambda i, k: (i, 0)),
            pl.BlockSpec((1, N_MOD, d), lambda i, k: (mod_row(i), 0, 0)),
            pl.BlockSpec((1, d), lambda i, k: (0, 0)),
            pl.BlockSpec((d, tf), lambda i, k: (0, k)),
            pl.BlockSpec((d, tf), lambda i, k: (0, k)),
            pl.BlockSpec((tf, d), lambda i, k: (k, 0)),
        ],
        out_specs=pl.BlockSpec((tm, d), lambda i, k: (i, 0)),
        out_shape=jax.ShapeDtypeStruct((n_rows, d), F32),
        compiler_params=_cparams(("arbitrary", "arbitrary")),
        name="ffn",
    )(h2, x_mid, mod, g_post, w_gate, w_up, w_down)


def kernel(x, c, ctx, c_ctx, w_mod, b_mod, g_mix_pre, g_mix_post, g_ffn_pre, g_ffn_post, w_in, w_out, lam_q1, lam_k1, lam_q2, lam_k2, g_subln, ssm_a_re, ssm_a_im, ssm_log_dt, ssm_b_re, ssm_b_im, ssm_c_re, ssm_c_im, ssm_d, w_glu, b_glu, w_four, b_four, w_gate, w_up, w_down):
    b, t, d = x.shape
    n_ctx = ctx.shape[1]
    depth = w_mod.shape[0]
    nl, nc = b * t, b * n_ctx
    aw = d // 2
    sw = ssm_d.shape[1]
    fw = d - aw - sw
    heads = aw // (2 * DA_HEAD_DIM)
    tm = 512
    assert t % tm == 0 and nc % tm == 0 and n_ctx % SSM_CHUNK == 0 and t % GRID_W == 0 and b + 1 <= MOD_ROWS
    assert sw % LANES == 0 and (sw // SSM_GROUP) % (2 * OCT) == 0

    cs = jnp.concatenate([c, c_ctx[None, :], jnp.zeros((MOD_ROWS - b - 1, d), F32)], axis=0)
    mod_all = _modulation(cs, w_mod, b_mod).reshape(depth, MOD_ROWS, N_MOD, d)

    per_b = t // tm
    lat_mod = lambda i: i // per_b
    ctx_mod = lambda i: b
    tabs = _rope_tables(t, tm)
    dft_lat = _dft_half_tables(t)
    dft_ctx = _dft_half_tables(n_ctx)
    perm = _atom_transpose_matrix()
    x_lat, x_ctx = x.reshape(nl, d), ctx.reshape(nc, d)
    w_glu_b = w_glu.astype(BF16)
    w_in_b = w_in[0].astype(BF16)
    ssm_w = jax.vmap(_ssm_weights)(ssm_a_re, ssm_a_im, ssm_log_dt, ssm_b_re, ssm_b_im, ssm_c_re, ssm_c_im)

    for l in range(depth):
        need_ctx = l < depth - 1
        lam_init = 0.8 - 0.6 * math.exp(-0.3 * l)
        mod = mod_all[l]
        g_pre = g_mix_pre[l][None, :]
        casts = [(w_out, l)] + ([(w_in, l + 1)] if l + 1 < depth else [])
        q, k, vt, us, uf, w_out_b, *w_in_next = _inproj(x_lat, mod, g_pre, w_in_b, tabs, mod_row=lat_mod,
                                                        tab_blk=lambda i: i % per_b, aw=aw, sw=sw, fw=fw, tm=tm,
                                                        cast=casts)
        qc, kc, vtc, usc, ufc = _inproj(x_ctx, mod, g_pre, w_in_b, tabs, mod_row=ctx_mod,
                                        tab_blk=lambda i: per_b, aw=aw, sw=sw, fw=fw, tm=tm)
        w_in_b = w_in_next[0] if w_in_next else None

        lam4 = jnp.stack([lam_q1[l], lam_k1[l], lam_q2[l], lam_k2[l]]).astype(F32)
        gs = g_subln[l][None, :].astype(F32)
        att, *ffn_wb = _attention(lam4, gs, q, [(kc, vtc), (k, vt)], lam_init=lam_init, b=b, heads=heads, tq=1024,
                                  name="attn_latent", cast=(l, w_gate, w_up, w_down))
        yc, ycc = _ssm_conv(us, usc, ssm_w, perm, layer=l, b=b)
        bias = b_four[l].reshape(1, fw).astype(F32)
        four = _fourier(uf, _fourier_weights(w_four[l], t), bias, dft_lat, b=b, tm=tm)

        small = (b_glu[l][None, :].astype(F32), ssm_d[l][None, :].astype(F32), g_mix_post[l][None, :],
                 g_ffn_pre[l][None, :])
        ffn_w = (g_ffn_post[l][None, :], *ffn_wb)
        x_mid, h2 = _outproj(att, us, yc, four, x_lat, mod, w_out_b, w_glu_b, *small, layer=l, mod_row=lat_mod,
                             tm=tm)
        x_lat = _ffn(h2, x_mid, mod, *ffn_w, mod_row=lat_mod, tm=tm, tf=512)

        if need_ctx:
            att_c, = _attention(lam4, gs, qc, [(kc, vtc)], lam_init=lam_init, b=b, heads=heads, tq=n_ctx,
                                name="attn_ctx")
            four_c = _fourier(ufc, _fourier_weights(w_four[l], n_ctx), bias, dft_ctx, b=b, tm=tm)
            xc_mid, h2c = _outproj(att_c, usc, ycc, four_c, x_ctx, mod, w_out_b, w_glu_b, *small, layer=l,
                                   mod_row=ctx_mod, tm=tm)
            x_ctx = _ffn(h2c, xc_mid, mod, *ffn_w, mod_row=ctx_mod, tm=tm, tf=512)
    return x_lat.reshape(b, t, d)
```

```python
import functools
import math

import jax
import jax.numpy as jnp
import numpy as np
from jax import lax
from jax.experimental import pallas as pl
from jax.experimental.pallas import tpu as pltpu

F32 = jnp.float32
BF16 = jnp.bfloat16

LANES = 128
GRID_W = 64
DA_HEAD_DIM = 128
SSM_GROUP = 16
SSM_STATE = 64
N_MOD = 6
ROPE_BASE = 10000.0
ROPE_PAIRS = DA_HEAD_DIM // 4
NORM_EPS = 1e-6
SUBLN_EPS = 1e-5

SSM_CHUNK = 16
SSM_CW = SSM_CHUNK * SSM_GROUP
OCT = LANES // SSM_GROUP
MOD_ROWS = 8
VMEM_LIMIT = 56 * 1024 * 1024
NT_DIMS = (((1,), (1,)), ((), ()))


def _cparams(sem):
    return pltpu.CompilerParams(dimension_semantics=sem, vmem_limit_bytes=VMEM_LIMIT)


def _rms(x, eps):
    return x * lax.rsqrt(jnp.mean(x * x, axis=-1, keepdims=True) + eps)


def _silu(x):
    return x * jax.nn.sigmoid(x)


def _resident(shape):
    nd = len(shape)
    return pl.BlockSpec(shape, lambda *_: (0,) * nd, pipeline_mode=pl.Buffered(1))


def _resident_layer(shape, layer):
    nd = len(shape)
    return pl.BlockSpec((None,) + tuple(shape[1:]), lambda *_: (layer,) + (0,) * (nd - 1),
                        pipeline_mode=pl.Buffered(1))


def _mod_kernel(c_ref, w_ref, b_ref, o_ref):
    s = _silu(c_ref[...]).astype(BF16)
    o_ref[0] = jnp.dot(s, w_ref[0].astype(BF16), preferred_element_type=F32) + b_ref[0]


def _modulation(cs, w_mod, b_mod, tn=1024):
    depth, d, n = w_mod.shape
    return pl.pallas_call(
        _mod_kernel,
        grid=(depth, n // tn),
        in_specs=[
            pl.BlockSpec((MOD_ROWS, d), lambda l, j: (0, 0)),
            pl.BlockSpec((1, d, tn), lambda l, j: (l, 0, j)),
            pl.BlockSpec((1, 1, tn), lambda l, j: (l, 0, j)),
        ],
        out_specs=pl.BlockSpec((1, MOD_ROWS, tn), lambda l, j: (l, 0, j)),
        out_shape=jax.ShapeDtypeStruct((depth, MOD_ROWS, n), F32),
        compiler_params=_cparams(("arbitrary", "arbitrary")),
        name="modulation",
    )(cs, w_mod, b_mod.reshape(depth, 1, n))


def _inproj_kernel(x_ref, mod_ref, g_ref, w_ref, cq_ref, sq_ref, ck_ref, sk_ref, *refs, aw, sw):
    n_cast = (len(refs) - 5) // 2
    q_ref, k_ref, vt_ref, us_ref, uf_ref = refs[n_cast:n_cast + 5]
    for src, dst in zip(refs[:n_cast], refs[n_cast + 5:]):
        dst[...] = src[...].astype(BF16)
    tm = x_ref.shape[0]
    hb = (_rms(x_ref[...], NORM_EPS) * (g_ref[...] * (1.0 + mod_ref[0, 1:2, :])) + mod_ref[0, 0:1, :]).astype(BF16)

    lane = lax.broadcasted_iota(jnp.int32, (tm, DA_HEAD_DIM), 1)
    first_half = (lane & (2 * ROPE_PAIRS - 1)) < ROPE_PAIRS

    def rope(z, c, s):
        partner = jnp.where(first_half, pltpu.roll(z, DA_HEAD_DIM - ROPE_PAIRS, 1), pltpu.roll(z, ROPE_PAIRS, 1))
        return z * c + partner * s

    zq = jnp.dot(hb, w_ref[:, 0:aw], preferred_element_type=F32)
    cq, sq = cq_ref[...], sq_ref[...]
    for j in range(aw // DA_HEAD_DIM):
        sl = slice(j * DA_HEAD_DIM, (j + 1) * DA_HEAD_DIM)
        q_ref[:, sl] = rope(zq[:, sl], cq, sq).astype(BF16)
    zk = jnp.dot(hb, w_ref[:, aw:2 * aw], preferred_element_type=F32)
    ck, sk = ck_ref[...], sk_ref[...]
    for j in range(aw // DA_HEAD_DIM):
        sl = slice(j * DA_HEAD_DIM, (j + 1) * DA_HEAD_DIM)
        k_ref[:, sl] = rope(zk[:, sl], ck, sk).astype(BF16)
    zv = jnp.dot(hb, w_ref[:, 2 * aw:3 * aw], preferred_element_type=F32)
    vt_ref[...] = jnp.transpose(zv.astype(BF16))
    zs = jnp.dot(hb, w_ref[:, 3 * aw:3 * aw + sw], preferred_element_type=F32)
    for j in range(sw // LANES):
        us_ref[j] = zs[:, j * LANES:(j + 1) * LANES]
    uf_ref[...] = jnp.dot(hb, w_ref[:, 3 * aw + sw:], preferred_element_type=F32).astype(BF16)


def _inproj(x, mod, g, w_in, tabs, *, mod_row, tab_blk, aw, sw, fw, tm, cast=()):
    nr, d = x.shape
    n_steps = nr // tm

    def row(i):
        return (i, 0)

    tab_spec = pl.BlockSpec((tm, DA_HEAD_DIM), lambda i: (tab_blk(i), 0))
    cast_in, cast_out, cast_shape = [], [], []
    for w, layer in cast:
        _, r, c = w.shape
        per = next(dv for dv in range(1, c // LANES + 1) if (c // LANES) % dv == 0 and dv * n_steps >= c // LANES)
        cw = per * LANES
        cast_in.append(pl.BlockSpec((None, r, cw), lambda i, layer=layer, nb=c // cw: (layer, 0, jnp.minimum(i, nb - 1))))
        cast_out.append(pl.BlockSpec((r, cw), lambda i, nb=c // cw: (0, jnp.minimum(i, nb - 1))))
        cast_shape.append(jax.ShapeDtypeStruct((r, c), BF16))
    return pl.pallas_call(
        functools.partial(_inproj_kernel, aw=aw, sw=sw),
        grid=(n_steps,),
        in_specs=[
            pl.BlockSpec((tm, d), row),
            pl.BlockSpec((1, N_MOD, d), lambda i: (mod_row(i), 0, 0)),
            pl.BlockSpec((1, d), lambda i: (0, 0)),
            _resident(w_in.shape),
            tab_spec, tab_spec, tab_spec, tab_spec,
        ] + cast_in,
        out_specs=[
            pl.BlockSpec((tm, aw), row), pl.BlockSpec((tm, aw), row), pl.BlockSpec((aw, tm), lambda i: (0, i)),
            pl.BlockSpec((sw // LANES, tm, LANES), lambda i: (0, i, 0)), pl.BlockSpec((tm, fw), row),
        ] + cast_out,
        out_shape=[
            jax.ShapeDtypeStruct((nr, aw), BF16), jax.ShapeDtypeStruct((nr, aw), BF16),
            jax.ShapeDtypeStruct((aw, nr), BF16), jax.ShapeDtypeStruct((sw // LANES, nr, LANES), F32),
            jax.ShapeDtypeStruct((nr, fw), BF16),
        ] + cast_shape,
        compiler_params=_cparams(("arbitrary",)),
        name="inproj",
    )(x, mod, g, w_in, *tabs, *[w for w, _ in cast])


def _rope_tables(t, tm):
    rows = t // GRID_W
    r = np.repeat(np.arange(rows, dtype=np.float64), GRID_W)
    col = np.tile(np.arange(GRID_W, dtype=np.float64), rows)
    inv = ROPE_BASE ** (-np.arange(ROPE_PAIRS, dtype=np.float64) / ROPE_PAIRS)
    ar, ac = r[:, None] * inv, col[:, None] * inv
    cos = np.concatenate([np.cos(ar), np.cos(ar), np.cos(ac), np.cos(ac)], axis=1)
    sin = np.concatenate([-np.sin(ar), np.sin(ar), -np.sin(ac), np.sin(ac)], axis=1)
    cos = np.concatenate([cos, np.ones((tm, DA_HEAD_DIM))], axis=0)
    sin = np.concatenate([sin, np.zeros((tm, DA_HEAD_DIM))], axis=0)
    scale = DA_HEAD_DIM ** -0.5 * math.log2(math.e)
    return tuple(jnp.asarray(a, F32) for a in (cos * scale, sin * scale, cos, sin))


def _attn_kernel(lam_ref, gs_ref, q_ref, *refs, lam_init, n_sets, n_cast):
    k_refs, vt_refs = refs[0:2 * n_sets:2], refs[1:2 * n_sets:2]
    cast_in = refs[2 * n_sets:2 * n_sets + n_cast]
    o_ref = refs[2 * n_sets + n_cast]
    cast_out = refs[2 * n_sets + n_cast + 1:]
    for src, dst in zip(cast_in, cast_out):
        dst[...] = src[...].astype(BF16)

    hd = DA_HEAD_DIM
    s1 = jnp.sum(lam_ref[0:1, :] * lam_ref[1:2, :], axis=-1, keepdims=True)
    s2 = jnp.sum(lam_ref[2:3, :] * lam_ref[3:4, :], axis=-1, keepdims=True)
    lam = jnp.exp(s1) - jnp.exp(s2) + lam_init
    outs = []
    for idx in range(2):
        qi = q_ref[:, idx * hd:(idx + 1) * hd]
        ss = [lax.dot_general(kr[:, idx * hd:(idx + 1) * hd], qi, NT_DIMS, preferred_element_type=F32)
              for kr in k_refs]
        m = functools.reduce(jnp.maximum, [jnp.max(s, axis=0, keepdims=True) for s in ss])
        ps = [jnp.exp2(s - m) for s in ss]
        l = functools.reduce(jnp.add, [jnp.sum(p, axis=0, keepdims=True) for p in ps])
        acc = functools.reduce(jnp.add, [jnp.dot(vr[...], p.astype(BF16), preferred_element_type=F32)
                                         for vr, p in zip(vt_refs, ps)])
        outs.append(acc * (1.0 / l))
    o = jnp.transpose(outs[0] - outs[1] * lam)
    o_ref[...] = (_rms(o, SUBLN_EPS) * gs_ref[...] * (1.0 - lam_init)).astype(BF16)


def _attention(lam4, gs, q, kv_sets, *, lam_init, b, heads, tq, name, cast=None):
    nq, aw = q.shape
    vw = 2 * DA_HEAD_DIM
    per_b = nq // b // tq
    q_spec = pl.BlockSpec((tq, vw), lambda bi, h, i: (bi * per_b + i, h))
    in_specs = [pl.BlockSpec((4, DA_HEAD_DIM), lambda bi, h, i: (0, 0)),
                pl.BlockSpec((1, vw), lambda bi, h, i: (0, 0)),
                q_spec]
    args = [lam4, gs, q]
    for k, vt in kv_sets:
        sk = k.shape[0] // b
        in_specs += [pl.BlockSpec((sk, vw), lambda bi, h, i: (bi, h)),
                     pl.BlockSpec((vw, sk), lambda bi, h, i: (h, bi))]
        args += [k, vt]
    out_specs = [q_spec]
    out_shape = [jax.ShapeDtypeStruct((nq, aw), BF16)]
    n_cast = 0
    if cast is not None:
        layer, w_gate, w_up, w_down = cast
        _, d, dff = w_gate.shape
        n_steps = b * heads * per_b
        per = next(dv for dv in range(1, dff // LANES + 1) if (dff // LANES) % dv == 0 and dv * n_steps >= dff // LANES)
        cw = per * LANES
        nblk = dff // cw

        def blk(bi, h, i):
            return jnp.minimum((bi * heads + h) * per_b + i, nblk - 1)

        in_specs += [pl.BlockSpec((None, d, cw), lambda bi, h, i: (layer, 0, blk(bi, h, i))),
                     pl.BlockSpec((None, d, cw), lambda bi, h, i: (layer, 0, blk(bi, h, i))),
                     pl.BlockSpec((None, cw, d), lambda bi, h, i: (layer, blk(bi, h, i), 0))]
        args += [w_gate, w_up, w_down]
        out_specs += [pl.BlockSpec((d, cw), lambda bi, h, i: (0, blk(bi, h, i))),
                      pl.BlockSpec((d, cw), lambda bi, h, i: (0, blk(bi, h, i))),
                      pl.BlockSpec((cw, d), lambda bi, h, i: (blk(bi, h, i), 0))]
        out_shape += [jax.ShapeDtypeStruct((d, dff), BF16), jax.ShapeDtypeStruct((d, dff), BF16),
                      jax.ShapeDtypeStruct((dff, d), BF16)]
        n_cast = 3
    return pl.pallas_call(
        functools.partial(_attn_kernel, lam_init=lam_init, n_sets=len(kv_sets), n_cast=n_cast),
        grid=(b, heads, per_b),
        in_specs=in_specs,
        out_specs=out_specs,
        out_shape=out_shape,
        compiler_params=_cparams(("arbitrary", "arbitrary", "arbitrary")),
        name=name,
    )(*args)


def _ssm_weights(a_re, a_im, log_dt, b_re, b_im, c_re, c_im):
    L, H, P = SSM_CHUNK, SSM_GROUP, SSM_STATE
    g = a_re.shape[1]
    npair = g // 2
    hp = lax.Precision.HIGHEST
    ar, ai = a_re.astype(F32), a_im.astype(F32)
    dt = jnp.exp(log_dt.astype(F32))[..., None]
    n = jnp.arange(L + 1, dtype=F32)
    mag = jnp.exp((ar * dt)[..., None] * n)
    ang = (ai * dt)[..., None] * n
    apr, api = mag * jnp.cos(ang), mag * jnp.sin(ang)
    xr, xi = apr[..., 1] - 1.0, api[..., 1]
    den = ar * ar + ai * ai
    qr, qi = (xr * ar + xi * ai) / den, (xi * ar - xr * ai) / den
    br, bi = b_re.astype(F32), b_im.astype(F32)
    bbr = qr[..., None] * br - qi[..., None] * bi
    bbi = qr[..., None] * bi + qi[..., None] * br
    cr, ci = c_re.astype(F32), c_im.astype(F32)

    car = cr[..., None] * apr[:, :, None, :, :L] - ci[..., None] * api[:, :, None, :, :L]
    cai = cr[..., None] * api[:, :, None, :, :L] + ci[..., None] * apr[:, :, None, :, :L]
    kt = (jnp.einsum('dghpt,dgpk->dgkth', car, bbr, precision=hp)
          - jnp.einsum('dghpt,dgpk->dgkth', cai, bbi, precision=hp))
    kc = jnp.concatenate([jnp.flip(kt[1][:, :, 1:], axis=2), kt[0][:, :, :1] + kt[1][:, :, :1], kt[0][:, :, 1:]],
                         axis=2).reshape(g, H, (2 * L - 1) * H)
    w_intra = jnp.pad(kc, ((0, 0), (0, 0), (0, (-kc.shape[2]) % LANES)))

    sir = jnp.stack([jnp.flip(apr[0, ..., :L], axis=-1), apr[1, ..., :L]])
    sii = jnp.stack([jnp.flip(api[0, ..., :L], axis=-1), api[1, ..., :L]])
    sir, sii = jnp.swapaxes(sir, 2, 3)[:, :, :, None, :], jnp.swapaxes(sii, 2, 3)[:, :, :, None, :]
    tbr, tbi = jnp.swapaxes(bbr, 2, 3)[:, :, None], jnp.swapaxes(bbi, 2, 3)[:, :, None]
    w_sin = jnp.concatenate([sir * tbr - sii * tbi, sir * tbi + sii * tbr], axis=-1)
    w_sin = w_sin.reshape(2, g, L * H, 2 * P)

    sor = jnp.stack([apr[0, ..., 1:], jnp.flip(apr[1, ..., 1:], axis=-1)])
    soi = jnp.stack([api[0, ..., 1:], jnp.flip(api[1, ..., 1:], axis=-1)])
    tcr, tci = jnp.swapaxes(cr, 2, 3)[:, :, :, None, :], jnp.swapaxes(ci, 2, 3)[:, :, :, None, :]
    cnr = (tcr * sor[..., None] - tci * soi[..., None]).reshape(2, g, P, L * H)
    cni = (tcr * soi[..., None] + tci * sor[..., None]).reshape(2, g, P, L * H)
    w_so = jnp.concatenate([cnr, -cni], axis=2)

    al = jnp.stack([apr[..., L].reshape(2, npair, 2 * P), api[..., L].reshape(2, npair, 2 * P)], axis=2)
    al = al.reshape(2, 2 * npair, 1, 2 * P)
    return w_intra, w_sin.astype(BF16), w_so.astype(BF16), al


def _atom_transpose_matrix():
    n = OCT * OCT * SSM_GROUP
    i = np.arange(n)
    j = ((i // SSM_GROUP) % OCT) * LANES + (i // LANES) * SSM_GROUP + i % SSM_GROUP
    p = np.zeros((n, n), np.float32)
    p[i, j] = 1.0
    return jnp.asarray(p, BF16)


def _ssm_in_kernel(us_ref, p_ref, w_ref, ut_ref, s_ref):
    noct, nb = us_ref.shape[0], us_ref.shape[1]
    nc = us_ref.shape[2] // SSM_CHUNK
    n = nb * nc
    half = SSM_CHUNK // 2
    src = []
    for o in range(noct):
        for hf in range(2):
            per_seq = [jnp.concatenate([us_ref[o, bi, pl.ds(half * hf + sl, nc, stride=SSM_CHUNK), :]
                                        for sl in range(half)], axis=1) for bi in range(nb)]
            src.append(jnp.concatenate(per_seq, axis=0).astype(BF16))
    perm = jnp.dot(jnp.concatenate(src, axis=0), p_ref[...], preferred_element_type=F32).astype(BF16)

    def group_chunk(gi):
        o, gl = gi // OCT, gi % OCT
        return jnp.concatenate([perm[(2 * o + hf) * n:(2 * o + hf + 1) * n, gl * LANES:(gl + 1) * LANES]
                                for hf in range(2)], axis=1)

    def store_chunk_major(d, blk, val):
        for bi in range(nb):
            s_ref[d, blk, pl.ds(bi, nc, stride=nb), :] = val[bi * nc:(bi + 1) * nc]

    low = lax.broadcasted_iota(jnp.int32, (n, LANES), 1) < SSM_STATE
    for j in range(noct * OCT // 2):
        ug = [group_chunk(2 * j), group_chunk(2 * j + 1)]
        ut_ref[2 * j] = ug[0]
        ut_ref[2 * j + 1] = ug[1]
        for d in range(2):
            s0 = jnp.dot(ug[0], w_ref[d, 2 * j], preferred_element_type=F32)
            s1 = jnp.dot(ug[1], w_ref[d, 2 * j + 1], preferred_element_type=F32)
            store_chunk_major(d, 2 * j, jnp.where(low, s0, pltpu.roll(s1, SSM_STATE, 1)))
            store_chunk_major(d, 2 * j + 1, jnp.where(low, pltpu.roll(s0, SSM_STATE, 1), s1))


def _ssm_scan_kernel(sc_ref, sl_ref, al_ref, hc_ref, hl_ref, *, b):
    d = pl.program_id(0)
    nblk = sc_ref.shape[1]
    tile = 8
    cpt = tile // b
    grp = lax.broadcasted_iota(jnp.int32, (tile, LANES), 0) // b
    ars = [al_ref[0, 2 * k] for k in range(nblk // 2)]
    ais = [al_ref[0, 2 * k + 1] for k in range(nblk // 2)]

    def spread(x, g):
        x = jnp.where(grp == g, x, 0.0)
        out = x
        for r in range(1, cpt):
            out = out + pltpu.roll(x, r * b, 0)
        return out

    def phase(s_ref, h_ref, carry):
        ntile = s_ref.shape[2] // tile

        def body(i, carry):
            t = jnp.where(d == 0, i, ntile - 1 - i)
            rows = pl.ds(pl.multiple_of(t * tile, tile), tile)
            new = []
            for k in range(nblk // 2):
                hr, hi = carry[2 * k], carry[2 * k + 1]
                sr, si = s_ref[0, 2 * k, rows, :], s_ref[0, 2 * k + 1, rows, :]
                hin_r, hin_i = hr, hi
                for step in range(cpt):
                    g = jnp.where(d == 0, step, cpt - 1 - step)
                    hin_r = jnp.where(grp == g, hr, hin_r)
                    hin_i = jnp.where(grp == g, hi, hin_i)
                    sgr, sgi = spread(sr, g), spread(si, g)
                    hr, hi = ars[k] * hr - ais[k] * hi + sgr, ars[k] * hi + ais[k] * hr + sgi
                h_ref[0, 2 * k, rows, :] = hin_r
                h_ref[0, 2 * k + 1, rows, :] = hin_i
                new += [hr, hi]
            return tuple(new)

        return lax.fori_loop(0, ntile, body, carry, unroll=2)

    carry = tuple(jnp.zeros((tile, LANES), F32) for _ in range(nblk))
    carry = phase(sc_ref, hc_ref, carry)
    phase(sl_ref, hl_ref, carry)


def _ssm_out_kernel(ut_ref, h_ref, kc_ref, wo_ref, p_ref, y_ref, wi_ref):
    n = ut_ref.shape[1]
    ng = ut_ref.shape[0]
    nb = y_ref.shape[1]
    nc = n // nb
    noct = ng // OCT
    half = SSM_CHUNK // 2

    @pl.when(pl.program_id(0) == 0)
    def _():
        for gi in range(ng):
            kc = kc_ref[gi]
            wi_ref[gi] = jnp.concatenate(
                [kc[:, (SSM_CHUNK - 1 - s) * SSM_GROUP:(SSM_CHUNK - 1 - s) * SSM_GROUP + SSM_CW]
                 for s in range(SSM_CHUNK)], axis=0).astype(BF16)

    def load_seq_major(d, blk):
        return jnp.concatenate([h_ref[d, blk, pl.ds(bi, nc, stride=nb), :] for bi in range(nb)], axis=0)

    low = lax.broadcasted_iota(jnp.int32, (n, LANES), 1) < SSM_STATE
    ys = []
    for j in range(ng // 2):
        hg = [[], []]
        for d in range(2):
            hr, hi = load_seq_major(d, 2 * j), load_seq_major(d, 2 * j + 1)
            hg[0].append(jnp.where(low, hr, pltpu.roll(hi, SSM_STATE, 1)).astype(BF16))
            hg[1].append(jnp.where(low, pltpu.roll(hr, SSM_STATE, 1), hi).astype(BF16))
        for e in range(2):
            gi = 2 * j + e
            ys.append(jnp.dot(ut_ref[gi], wi_ref[gi], preferred_element_type=F32)
                      + jnp.dot(hg[e][0], wo_ref[0, gi], preferred_element_type=F32)
                      + jnp.dot(hg[e][1], wo_ref[1, gi], preferred_element_type=F32))
    rows = []
    for o in range(noct):
        for hf in range(2):
            rows.append(jnp.concatenate([ys[OCT * o + gl][:, hf * LANES:(hf + 1) * LANES] for gl in range(OCT)],
                                        axis=1))
    ycat = jnp.concatenate(rows, axis=0)
    hi = ycat.astype(BF16)
    lo = (ycat - hi.astype(F32)).astype(BF16)
    res = (jnp.dot(hi, p_ref[...], preferred_element_type=F32)
           + jnp.dot(lo, p_ref[...], preferred_element_type=F32))
    for o in range(noct):
        for hf in range(2):
            blk = 2 * o + hf
            for tl in range(half):
                piece = res[blk * n:(blk + 1) * n, tl * LANES:(tl + 1) * LANES]
                for bi in range(nb):
                    y_ref[o, bi, pl.ds(half * hf + tl, nc, stride=SSM_CHUNK), :] = piece[bi * nc:(bi + 1) * nc]


def _ssm_conv(us_lat, us_ctx, weights, perm, *, layer, b):
    w_intra, w_sin, w_so, al = weights
    P = SSM_STATE
    ng = w_intra.shape[1]
    nblk = ng
    assert 8 % b == 0

    def stage_in(us, steps):
        noct, rows, _ = us.shape
        seq = rows // b
        tok = seq // steps
        n = b * tok // SSM_CHUNK
        nchunks = rows // SSM_CHUNK
        return pl.pallas_call(
            _ssm_in_kernel,
            grid=(steps,),
            in_specs=[pl.BlockSpec((noct, b, tok, LANES), lambda i: (0, 0, i, 0)),
                      _resident(perm.shape), _resident_layer(w_sin.shape, layer)],
            out_specs=[pl.BlockSpec((ng, n, SSM_CW), lambda i: (0, i, 0)),
                       pl.BlockSpec((2, nblk, n, LANES), lambda i: (0, 0, i, 0))],
            out_shape=[jax.ShapeDtypeStruct((ng, nchunks, SSM_CW), BF16),
                       jax.ShapeDtypeStruct((2, nblk, nchunks, LANES), F32)],
            compiler_params=_cparams(("arbitrary",)),
            name="ssm_in",
        )(us.reshape(noct, b, seq, LANES), perm, w_sin)

    lat_steps = 2 * b
    ut_lat, s_lat = stage_in(us_lat, lat_steps)
    ut_ctx, s_ctx = stage_in(us_ctx, 1)

    cb = 16
    nc_rows, nl_rows = s_ctx.shape[2], s_lat.shape[2]
    h_ctx, h_lat = pl.pallas_call(
        functools.partial(_ssm_scan_kernel, b=b),
        grid=(2, nblk // cb),
        in_specs=[pl.BlockSpec((1, cb, nc_rows, LANES), lambda d, j: (d, j, 0, 0)),
                  pl.BlockSpec((1, cb, nl_rows, LANES), lambda d, j: (d, j, 0, 0)),
                  pl.BlockSpec((None, 1, cb, 1, 2 * P), lambda d, j: (layer, d, j, 0, 0))],
        out_specs=[pl.BlockSpec((1, cb, nc_rows, LANES), lambda d, j: (d, j, 0, 0)),
                   pl.BlockSpec((1, cb, nl_rows, LANES), lambda d, j: (d, j, 0, 0))],
        out_shape=[jax.ShapeDtypeStruct(s_ctx.shape, F32), jax.ShapeDtypeStruct(s_lat.shape, F32)],
        compiler_params=_cparams(("arbitrary", "arbitrary")),
        name="ssm_scan",
    )(s_ctx, s_lat, al)

    def stage_out(ut, hin, steps):
        _, nchunks, _ = ut.shape
        n = nchunks // steps
        noct = ng // OCT
        seq = nchunks * SSM_CHUNK // b
        tok = seq // steps
        y = pl.pallas_call(
            _ssm_out_kernel,
            grid=(steps,),
            in_specs=[pl.BlockSpec((ng, n, SSM_CW), lambda i: (0, i, 0)),
                      pl.BlockSpec((2, nblk, n, LANES), lambda i: (0, 0, i, 0)),
                      _resident_layer(w_intra.shape, layer), _resident_layer(w_so.shape, layer),
                      _resident(perm.shape)],
            out_specs=pl.BlockSpec((noct, b, tok, LANES), lambda i: (0, 0, i, 0)),
            out_shape=jax.ShapeDtypeStruct((noct, b, seq, LANES), F32),
            scratch_shapes=[pltpu.VMEM((ng, SSM_CW, SSM_CW), BF16)],
            compiler_params=_cparams(("arbitrary",)),
            name="ssm_out",
        )(ut, hin, w_intra, w_so, perm)
        return y.reshape(noct, b * seq, LANES)

    return stage_out(ut_lat, h_lat, lat_steps), stage_out(ut_ctx, h_ctx, 1)


def _fourier_kernel(ulo_ref, uup_ref, w_ref, cos_ref, sin_ref, alt_ref, s1_ref, b_ref,
                    lo_ref, up_ref, a_s, b_s, am_s, carry, *, groups):
    gc = ulo_ref.shape[1] // groups
    tm = lo_ref.shape[0]
    nt = a_s.shape[0] // tm
    row = lax.broadcasted_iota(jnp.int32, lo_ref.shape, 0)

    @pl.when(pl.program_id(1) == 0)
    def _():
        for jj in range(nt):
            pr = jnp.dot(s1_ref[...], uup_ref[(nt - 1 - jj) * tm:(nt - jj) * tm, :], preferred_element_type=F32)
            if jj > 0:
                first = uup_ref[(nt - jj) * tm:(nt - jj) * tm + am_s.shape[0], :][0:1, :].astype(F32)
                pr = jnp.where(row == 0, first, pr)
            pr = pr.astype(BF16)
            rows = slice(jj * tm, (jj + 1) * tm)
            for g in range(groups):
                cols = slice(g * gc, (g + 1) * gc)
                ab_lo = jnp.dot(ulo_ref[rows, cols], w_ref[g], preferred_element_type=F32)
                ab_pr = jnp.dot(pr[:, cols], w_ref[g], preferred_element_type=F32)
                a_s[rows, cols] = (ab_lo[:, :gc] + ab_pr[:, :gc]).astype(BF16)
                b_s[rows, cols] = (ab_lo[:, gc:] - ab_pr[:, gc:]).astype(BF16)
        for g in range(groups):
            cols = slice(g * gc, (g + 1) * gc)
            am_s[:, cols] = jnp.dot(uup_ref[0:am_s.shape[0], cols], w_ref[g, :, 0:gc], preferred_element_type=F32)
        mid = jnp.dot(alt_ref[...], a_s[...], preferred_element_type=F32)
        carry[...] = mid + am_s[0:carry.shape[0], :] + b_ref[...]

    a_mid = jnp.where((row & 1) == 0, am_s[0:1, :], -am_s[0:1, :])
    p = jnp.dot(cos_ref[...], a_s[...], preferred_element_type=F32) + a_mid + b_ref[...]
    q = jnp.dot(sin_ref[...], b_s[...], preferred_element_type=F32)
    lo_ref[...] = (p - q).astype(BF16)
    hi = (p + q).astype(BF16)
    rev = jnp.dot(s1_ref[...], hi, preferred_element_type=F32)
    up_ref[...] = jnp.where(row == 0, carry[0:1, :], rev).astype(BF16)
    carry[0:1, :] = hi[0:1, :].astype(F32)


def _dft_half_tables(n):
    half = n // 2
    k = np.arange(half, dtype=np.int64)
    ang = ((k[:, None] * k[None, :]) % n).astype(np.float64) * (2.0 * np.pi / n)
    alt = np.broadcast_to(np.where((k & 1) == 0, 1.0, -1.0), (8, half))
    return tuple(jnp.asarray(a, F32).astype(BF16) for a in (np.cos(ang), np.sin(ang), alt))


def _fourier(uf, w_cs, bias, tables, *, b, tm):
    nr, fw = uf.shape
    seq = nr // b
    half = seq // 2
    groups = w_cs.shape[0]
    cos_t, sin_t, alt = tables
    tm = min(tm, half)
    nt = half // tm
    r = np.arange(1, tm)
    s1 = np.zeros((tm, tm), np.float32)
    s1[r, tm - r] = 1.0
    s1 = jnp.asarray(s1, BF16)
    lo, up = pl.pallas_call(
        functools.partial(_fourier_kernel, groups=groups),
        grid=(b, nt),
        in_specs=[
            pl.BlockSpec((half, fw), lambda bi, i: (2 * bi, 0)),
            pl.BlockSpec((half, fw), lambda bi, i: (2 * bi + 1, 0)),
            pl.BlockSpec(w_cs.shape, lambda bi, i: (0, 0, 0)),
            pl.BlockSpec((tm, half), lambda bi, i: (nt - 1 - i, 0)),
            pl.BlockSpec((tm, half), lambda bi, i: (nt - 1 - i, 0)),
            pl.BlockSpec(alt.shape, lambda bi, i: (0, 0)),
            pl.BlockSpec((tm, tm), lambda bi, i: (0, 0)),
            pl.BlockSpec((1, fw), lambda bi, i: (0, 0)),
        ],
        out_specs=[pl.BlockSpec((tm, fw), lambda bi, i: (bi * nt + nt - 1 - i, 0)),
                   pl.BlockSpec((tm, fw), lambda bi, i: (bi * nt + i, 0))],
        out_shape=[jax.ShapeDtypeStruct((b * half, fw), BF16), jax.ShapeDtypeStruct((b * half, fw), BF16)],
        scratch_shapes=[pltpu.VMEM((half, fw), BF16), pltpu.VMEM((half, fw), BF16),
                        pltpu.VMEM((16, fw), F32), pltpu.VMEM((alt.shape[0], fw), F32)],
        compiler_params=_cparams(("arbitrary", "arbitrary")),
        name="fourier_%d" % seq,
    )(uf, uf, w_cs, cos_t, sin_t, alt, s1, bias)
    return lo, up


def _fourier_weights(w_four, seq):
    groups, gc, _ = w_four.shape
    k = np.arange(gc)
    ang = ((k[:, None] * k[None, :]) % gc).astype(np.float64) * (2.0 * np.pi / gc)
    norm = 1.0 / math.sqrt(seq * gc)
    cc = jnp.asarray(np.cos(ang) * norm, F32)
    sc = jnp.asarray(np.sin(ang) * norm, F32)
    hp = lax.Precision.HIGHEST
    wc = jnp.einsum('ck,gkd->gcd', cc, w_four.astype(F32), precision=hp)
    ws = jnp.einsum('ck,gkd->gcd', sc, w_four.astype(F32), precision=hp)
    return jnp.concatenate([wc, ws], axis=-1).astype(BF16)


def _outproj_kernel(att_ref, us_ref, yc_ref, x_ref, mod_ref, wo_ref, wg_ref, bg_ref, dsk_ref,
                    gpost_ref, gpre_ref, *refs, aw, sw, four_part):
    uf_refs, (xo_ref, h2_ref) = refs[:-2], refs[-2:]
    part = four_part(pl.program_id(0))
    nblk = us_ref.shape[0]
    tm = x_ref.shape[0]
    gate_g = mod_ref[0, 2:3, :] * gpost_ref[...]
    scale_g = gpre_ref[...] * (1.0 + mod_ref[0, 4:5, :])
    for r0 in range(0, tm, tm // 2):
        rows = slice(r0, r0 + tm // 2)
        us = jnp.concatenate([us_ref[j, rows, :] for j in range(nblk)], axis=1)
        yc = jnp.concatenate([yc_ref[j, rows, :] for j in range(nblk)], axis=1)
        g = jax.nn.gelu(dsk_ref[...] * us + yc)
        z = jnp.dot(g.astype(BF16), wg_ref[...], preferred_element_type=F32) + bg_ref[...]
        ssm = (g * jax.nn.sigmoid(z)).astype(BF16)
        uf = uf_refs[0][rows, :]
        for pi in range(1, len(uf_refs)):
            uf = jnp.where(part == pi, uf_refs[pi][rows, :], uf)
        mix = (jnp.dot(att_ref[rows, :], wo_ref[0:aw, :], preferred_element_type=F32)
               + jnp.dot(ssm, wo_ref[aw:aw + sw, :], preferred_element_type=F32)
               + jnp.dot(uf, wo_ref[aw + sw:, :], preferred_element_type=F32))
        xn = x_ref[rows, :] + _rms(mix, NORM_EPS) * gate_g
        xo_ref[rows, :] = xn
        h2_ref[rows, :] = (_rms(xn, NORM_EPS) * scale_g + mod_ref[0, 3:4, :]).astype(BF16)


def _outproj(att, us, yc, four_parts, x, mod, w_out, w_glu, b_glu, dsk, g_post, g_pre, *, layer, mod_row, tm,
             four_tile=None):
    n_rows, d = x.shape
    aw, fw = att.shape[1], four_parts[0].shape[1]
    if four_tile is None:
        four_tile = lambda i: (0, i)
    nblk = us.shape[0]
    sw = nblk * LANES

    def row(i):
        return (i, 0)

    def const(i):
        return (0, 0)

    blk3 = pl.BlockSpec((nblk, tm, LANES), lambda i: (0, i, 0))
    def part_spec(pi):
        nblocks = four_parts[pi].shape[0] // tm
        return pl.BlockSpec((tm, fw), lambda i: (jnp.where(four_tile(i)[0] == pi,
                                                           jnp.clip(four_tile(i)[1], 0, nblocks - 1), 0), 0))

    return pl.pallas_call(
        functools.partial(_outproj_kernel, aw=aw, sw=sw, four_part=lambda i: four_tile(i)[0]),
        grid=(n_rows // tm,),
        in_specs=[
            pl.BlockSpec((tm, aw), row), blk3, blk3,
            pl.BlockSpec((tm, d), row),
            pl.BlockSpec((1, N_MOD, d), lambda i: (mod_row(i), 0, 0)),
            _resident(w_out.shape), _resident_layer(w_glu.shape, layer),
            pl.BlockSpec((1, sw), const), pl.BlockSpec((1, sw), const),
            pl.BlockSpec((1, d), const), pl.BlockSpec((1, d), const),
        ] + [part_spec(pi) for pi in range(len(four_parts))],
        out_specs=[pl.BlockSpec((tm, d), row), pl.BlockSpec((tm, d), row)],
        out_shape=[jax.ShapeDtypeStruct((n_rows, d), F32), jax.ShapeDtypeStruct((n_rows, d), BF16)],
        compiler_params=_cparams(("arbitrary",)),
        name="outproj",
    )(att, us, yc, x, mod, w_out, w_glu, b_glu, dsk, g_post, g_pre, *four_parts)


def _ffn_kernel(h_ref, x_ref, mod_ref, g_ref, wg_ref, wu_ref, wd_ref, o_ref):
    k = pl.program_id(1)

    @pl.when(k == 0)
    def _():
        o_ref[...] = jnp.zeros(o_ref.shape, F32)

    h = h_ref[...]
    tf = wg_ref.shape[1]
    part = None
    for c0 in range(0, tf, tf // 2):
        cols = slice(c0, c0 + tf // 2)
        a = jnp.dot(h, wg_ref[:, cols], preferred_element_type=F32)
        u = jnp.dot(h, wu_ref[:, cols], preferred_element_type=F32)
        p = jnp.dot((_silu(a) * u).astype(BF16), wd_ref[cols, :], preferred_element_type=F32)
        part = p if part is None else part + p
    o_ref[...] += part

    @pl.when(k == pl.num_programs(1) - 1)
    def _():
        o_ref[...] = x_ref[...] + _rms(o_ref[...], NORM_EPS) * (mod_ref[0, 5:6, :] * g_ref[...])


def _ffn(h2, x_mid, mod, g_post, w_gate, w_up, w_down, *, mod_row, tm, tf):
    n_rows, d = h2.shape
    dff = w_gate.shape[1]
    return pl.pallas_call(
        _ffn_kernel,
        grid=(n_rows // tm, dff // tf),
        in_specs=[
            pl.BlockSpec((tm, d), lambda i, k: (i, 0)),
            pl.BlockSpec((tm, d), lambda i, k: (i, 0)),
            pl.BlockSpec((1, N_MOD, d), lambda i, k: (mod_row(i), 0, 0)),
            pl.BlockSpec((1, d), lambda i, k: (0, 0)),
            pl.BlockSpec((d, tf), lambda i, k: (0, k)),
            pl.BlockSpec((d, tf), lambda i, k: (0, k)),
            pl.BlockSpec((tf, d), lambda i, k: (k, 0)),
        ],
        out_specs=pl.BlockSpec((tm, d), lambda i, k: (i, 0)),
        out_shape=jax.ShapeDtypeStruct((n_rows, d), F32),
        compiler_params=_cparams(("arbitrary", "arbitrary")),
        name="ffn",
    )(h2, x_mid, mod, g_post, w_gate, w_up, w_down)


def kernel(x, c, ctx, c_ctx, w_mod, b_mod, g_mix_pre, g_mix_post, g_ffn_pre, g_ffn_post, w_in, w_out, lam_q1, lam_k1, lam_q2, lam_k2, g_subln, ssm_a_re, ssm_a_im, ssm_log_dt, ssm_b_re, ssm_b_im, ssm_c_re, ssm_c_im, ssm_d, w_glu, b_glu, w_four, b_four, w_gate, w_up, w_down):
    b, t, d = x.shape
    n_ctx = ctx.shape[1]
    depth = w_mod.shape[0]
    nl, nc = b * t, b * n_ctx
    aw = d // 2
    sw = ssm_d.shape[1]
    fw = d - aw - sw
    heads = aw // (2 * DA_HEAD_DIM)
    tm = 512
    assert t % tm == 0 and nc % tm == 0 and n_ctx % SSM_CHUNK == 0 and t % GRID_W == 0 and b + 1 <= MOD_ROWS
    assert sw % LANES == 0 and (sw // SSM_GROUP) % (2 * OCT) == 0

    cs = jnp.concatenate([c, c_ctx[None, :], jnp.zeros((MOD_ROWS - b - 1, d), F32)], axis=0)
    mod_all = _modulation(cs, w_mod, b_mod).reshape(depth, MOD_ROWS, N_MOD, d)

    per_b = t // tm
    lat_mod = lambda i: i // per_b
    ctx_mod = lambda i: b
    tabs = _rope_tables(t, tm)
    dft_lat = _dft_half_tables(t)
    dft_ctx = _dft_half_tables(n_ctx)
    perm = _atom_transpose_matrix()
    x_lat, x_ctx = x.reshape(nl, d), ctx.reshape(nc, d)
    w_glu_b = w_glu.astype(BF16)
    w_in_b = w_in[0].astype(BF16)
    ssm_w = jax.vmap(_ssm_weights)(ssm_a_re, ssm_a_im, ssm_log_dt, ssm_b_re, ssm_b_im, ssm_c_re, ssm_c_im)

    for l in range(depth):
        need_ctx = l < depth - 1
        lam_init = 0.8 - 0.6 * math.exp(-0.3 * l)
        mod = mod_all[l]
        g_pre = g_mix_pre[l][None, :]
        casts = [(w_out, l)] + ([(w_in, l + 1)] if l + 1 < depth else [])
        q, k, vt, us, uf, w_out_b, *w_in_next = _inproj(x_lat, mod, g_pre, w_in_b, tabs, mod_row=lat_mod,
                                                        tab_blk=lambda i: i % per_b, aw=aw, sw=sw, fw=fw, tm=tm,
                                                        cast=casts)
        qc, kc, vtc, usc, ufc = _inproj(x_ctx, mod, g_pre, w_in_b, tabs, mod_row=ctx_mod,
                                        tab_blk=lambda i: per_b, aw=aw, sw=sw, fw=fw, tm=tm)
        w_in_b = w_in_next[0] if w_in_next else None

        lam4 = jnp.stack([lam_q1[l], lam_k1[l], lam_q2[l], lam_k2[l]]).astype(F32)
        gs = g_subln[l][None, :].astype(F32)
        att, *ffn_wb = _attention(lam4, gs, q, [(kc, vtc), (k, vt)], lam_init=lam_init, b=b, heads=heads, tq=1024,
                                  name="attn_latent", cast=(l, w_gate, w_up, w_down))
        yc, ycc = _ssm_conv(us, usc, ssm_w, perm, layer=l, b=b)
        bias = b_four[l].reshape(1, fw).astype(F32)
        four = _fourier(uf, _fourier_weights(w_four[l], t), bias, dft_lat, b=b, tm=tm)

        small = (b_glu[l][None, :].astype(F32), ssm_d[l][None, :].astype(F32), g_mix_post[l][None, :],
                 g_ffn_pre[l][None, :])
        ffn_w = (g_ffn_post[l][None, :], *ffn_wb)
        half_t = per_b // 2

        def four_tile(i):
            pos = i % per_b
            return pos // half_t, (i // per_b) * half_t + pos % half_t

        x_mid, h2 = _outproj(att, us, yc, four, x_lat, mod, w_out_b, w_glu_b, *small, layer=l, mod_row=lat_mod,
                             tm=tm, four_tile=four_tile)
        x_lat = _ffn(h2, x_mid, mod, *ffn_w, mod_row=lat_mod, tm=tm, tf=512)

        if need_ctx:
            att_c, = _attention(lam4, gs, qc, [(kc, vtc)], lam_init=lam_init, b=b, heads=heads, tq=n_ctx,
                                name="attn_ctx")
            lo_c, up_c = _fourier(ufc, _fourier_weights(w_four[l], n_ctx), bias, dft_ctx, b=b, tm=tm)
            four_c = [jnp.concatenate([lo_c.reshape(b, -1, fw), up_c.reshape(b, -1, fw)], axis=1).reshape(nc, fw)]
            xc_mid, h2c = _outproj(att_c, usc, ycc, four_c, x_ctx, mod, w_out_b, w_glu_b, *small, layer=l,
                                   mod_row=ctx_mod, tm=tm)
            x_ctx = _ffn(h2c, xc_mid, mod, *ffn_w, mod_row=ctx_mod, tm=tm, tf=512)
    return x_lat.reshape(b, t, d)
```

```python
import functools
import math

import jax
import jax.numpy as jnp
import numpy as np
from jax import lax
from jax.experimental import pallas as pl
from jax.experimental.pallas import tpu as pltpu

F32 = jnp.float32
BF16 = jnp.bfloat16

LANES = 128
GRID_W = 64
DA_HEAD_DIM = 128
SSM_GROUP = 16
SSM_STATE = 64
N_MOD = 6
ROPE_BASE = 10000.0
ROPE_PAIRS = DA_HEAD_DIM // 4
NORM_EPS = 1e-6
SUBLN_EPS = 1e-5

SSM_CHUNK = 16
SSM_CW = SSM_CHUNK * SSM_GROUP
OCT = LANES // SSM_GROUP
MOD_ROWS = 8
VMEM_LIMIT = 56 * 1024 * 1024
NT_DIMS = (((1,), (1,)), ((), ()))


def _cparams(sem):
    return pltpu.CompilerParams(dimension_semantics=sem, vmem_limit_bytes=VMEM_LIMIT)


def _rms(x, eps):
    return x * lax.rsqrt(jnp.mean(x * x, axis=-1, keepdims=True) + eps)


def _silu(x):
    return x * jax.nn.sigmoid(x)


def _resident(shape):
    nd = len(shape)
    return pl.BlockSpec(shape, lambda *_: (0,) * nd, pipeline_mode=pl.Buffered(1))


def _resident_layer(shape, layer):
    nd = len(shape)
    return pl.BlockSpec((None,) + tuple(shape[1:]), lambda *_: (layer,) + (0,) * (nd - 1),
                        pipeline_mode=pl.Buffered(1))


def _mod_kernel(c_ref, w_ref, b_ref, o_ref):
    s = _silu(c_ref[...]).astype(BF16)
    o_ref[0] = jnp.dot(s, w_ref[0].astype(BF16), preferred_element_type=F32) + b_ref[0]


def _modulation(cs, w_mod, b_mod, tn=1024):
    depth, d, n = w_mod.shape
    return pl.pallas_call(
        _mod_kernel,
        grid=(depth, n // tn),
        in_specs=[
            pl.BlockSpec((MOD_ROWS, d), lambda l, j: (0, 0)),
            pl.BlockSpec((1, d, tn), lambda l, j: (l, 0, j)),
            pl.BlockSpec((1, 1, tn), lambda l, j: (l, 0, j)),
        ],
        out_specs=pl.BlockSpec((1, MOD_ROWS, tn), lambda l, j: (l, 0, j)),
        out_shape=jax.ShapeDtypeStruct((depth, MOD_ROWS, n), F32),
        compiler_params=_cparams(("arbitrary", "arbitrary")),
        name="modulation",
    )(cs, w_mod, b_mod.reshape(depth, 1, n))


def _inproj_kernel(x_ref, mod_ref, g_ref, w_ref, cq_ref, sq_ref, ck_ref, sk_ref, *refs, aw, sw):
    n_cast = (len(refs) - 5) // 2
    q_ref, k_ref, vt_ref, us_ref, uf_ref = refs[n_cast:n_cast + 5]
    for src, dst in zip(refs[:n_cast], refs[n_cast + 5:]):
        dst[...] = src[...].astype(BF16)
    tm = x_ref.shape[0]
    hb = (_rms(x_ref[...], NORM_EPS) * (g_ref[...] * (1.0 + mod_ref[0, 1:2, :])) + mod_ref[0, 0:1, :]).astype(BF16)

    lane = lax.broadcasted_iota(jnp.int32, (tm, DA_HEAD_DIM), 1)
    first_half = (lane & (2 * ROPE_PAIRS - 1)) < ROPE_PAIRS

    def rope(z, c, s):
        partner = jnp.where(first_half, pltpu.roll(z, DA_HEAD_DIM - ROPE_PAIRS, 1), pltpu.roll(z, ROPE_PAIRS, 1))
        return z * c + partner * s

    zq = jnp.dot(hb, w_ref[:, 0:aw], preferred_element_type=F32)
    cq, sq = cq_ref[...], sq_ref[...]
    for j in range(aw // DA_HEAD_DIM):
        sl = slice(j * DA_HEAD_DIM, (j + 1) * DA_HEAD_DIM)
        q_ref[:, sl] = rope(zq[:, sl], cq, sq).astype(BF16)
    zk = jnp.dot(hb, w_ref[:, aw:2 * aw], preferred_element_type=F32)
    ck, sk = ck_ref[...], sk_ref[...]
    for j in range(aw // DA_HEAD_DIM):
        sl = slice(j * DA_HEAD_DIM, (j + 1) * DA_HEAD_DIM)
        k_ref[:, sl] = rope(zk[:, sl], ck, sk).astype(BF16)
    zv = jnp.dot(hb, w_ref[:, 2 * aw:3 * aw], preferred_element_type=F32)
    vt_ref[...] = jnp.transpose(zv.astype(BF16))
    zs = jnp.dot(hb, w_ref[:, 3 * aw:3 * aw + sw], preferred_element_type=F32)
    for j in range(sw // LANES):
        us_ref[j] = zs[:, j * LANES:(j + 1) * LANES]
    uf_ref[...] = jnp.dot(hb, w_ref[:, 3 * aw + sw:], preferred_element_type=F32).astype(BF16)


def _inproj(x, mod, g, w_in, tabs, *, mod_row, tab_blk, aw, sw, fw, tm, cast=()):
    nr, d = x.shape
    n_steps = nr // tm

    def row(i):
        return (i, 0)

    tab_spec = pl.BlockSpec((tm, DA_HEAD_DIM), lambda i: (tab_blk(i), 0))
    cast_in, cast_out, cast_shape = [], [], []
    for w, layer in cast:
        _, r, c = w.shape
        per = next(dv for dv in range(1, c // LANES + 1) if (c // LANES) % dv == 0 and dv * n_steps >= c // LANES)
        cw = per * LANES
        cast_in.append(pl.BlockSpec((None, r, cw), lambda i, layer=layer, nb=c // cw: (layer, 0, jnp.minimum(i, nb - 1))))
        cast_out.append(pl.BlockSpec((r, cw), lambda i, nb=c // cw: (0, jnp.minimum(i, nb - 1))))
        cast_shape.append(jax.ShapeDtypeStruct((r, c), BF16))
    return pl.pallas_call(
        functools.partial(_inproj_kernel, aw=aw, sw=sw),
        grid=(n_steps,),
        in_specs=[
            pl.BlockSpec((tm, d), row),
            pl.BlockSpec((1, N_MOD, d), lambda i: (mod_row(i), 0, 0)),
            pl.BlockSpec((1, d), lambda i: (0, 0)),
            _resident(w_in.shape),
            tab_spec, tab_spec, tab_spec, tab_spec,
        ] + cast_in,
        out_specs=[
            pl.BlockSpec((tm, aw), row), pl.BlockSpec((tm, aw), row), pl.BlockSpec((aw, tm), lambda i: (0, i)),
            pl.BlockSpec((sw // LANES, tm, LANES), lambda i: (0, i, 0)), pl.BlockSpec((tm, fw), row),
        ] + cast_out,
        out_shape=[
            jax.ShapeDtypeStruct((nr, aw), BF16), jax.ShapeDtypeStruct((nr, aw), BF16),
            jax.ShapeDtypeStruct((aw, nr), BF16), jax.ShapeDtypeStruct((sw // LANES, nr, LANES), F32),
            jax.ShapeDtypeStruct((nr, fw), BF16),
        ] + cast_shape,
        compiler_params=_cparams(("arbitrary",)),
        name="inproj",
    )(x, mod, g, w_in, *tabs, *[w for w, _ in cast])


def _rope_tables(t, tm):
    rows = t // GRID_W
    r = np.repeat(np.arange(rows, dtype=np.float64), GRID_W)
    col = np.tile(np.arange(GRID_W, dtype=np.float64), rows)
    inv = ROPE_BASE ** (-np.arange(ROPE_PAIRS, dtype=np.float64) / ROPE_PAIRS)
    ar, ac = r[:, None] * inv, col[:, None] * inv
    cos = np.concatenate([np.cos(ar), np.cos(ar), np.cos(ac), np.cos(ac)], axis=1)
    sin = np.concatenate([-np.sin(ar), np.sin(ar), -np.sin(ac), np.sin(ac)], axis=1)
    cos = np.concatenate([cos, np.ones((tm, DA_HEAD_DIM))], axis=0)
    sin = np.concatenate([sin, np.zeros((tm, DA_HEAD_DIM))], axis=0)
    scale = DA_HEAD_DIM ** -0.5 * math.log2(math.e)
    return tuple(jnp.asarray(a, F32) for a in (cos * scale, sin * scale, cos, sin))


def _attn_kernel(lam_ref, gs_ref, q_ref, *refs, lam_init, n_sets, n_cast):
    k_refs, vt_refs = refs[0:2 * n_sets:2], refs[1:2 * n_sets:2]
    cast_in = refs[2 * n_sets:2 * n_sets + n_cast]
    o_ref = refs[2 * n_sets + n_cast]
    cast_out = refs[2 * n_sets + n_cast + 1:]
    for src, dst in zip(cast_in, cast_out):
        dst[...] = src[...].astype(BF16)

    hd = DA_HEAD_DIM
    s1 = jnp.sum(lam_ref[0:1, :] * lam_ref[1:2, :], axis=-1, keepdims=True)
    s2 = jnp.sum(lam_ref[2:3, :] * lam_ref[3:4, :], axis=-1, keepdims=True)
    lam = jnp.exp(s1) - jnp.exp(s2) + lam_init
    outs = []
    for idx in range(2):
        qi = q_ref[:, idx * hd:(idx + 1) * hd]
        ss = [lax.dot_general(kr[:, idx * hd:(idx + 1) * hd], qi, NT_DIMS, preferred_element_type=F32)
              for kr in k_refs]
        m = functools.reduce(jnp.maximum, [jnp.max(s, axis=0, keepdims=True) for s in ss])
        ps = [jnp.exp2(s - m) for s in ss]
        l = functools.reduce(jnp.add, [jnp.sum(p, axis=0, keepdims=True) for p in ps])
        acc = functools.reduce(jnp.add, [jnp.dot(vr[...], p.astype(BF16), preferred_element_type=F32)
                                         for vr, p in zip(vt_refs, ps)])
        outs.append(acc * (1.0 / l))
    o = jnp.transpose(outs[0] - outs[1] * lam)
    o_ref[...] = (_rms(o, SUBLN_EPS) * gs_ref[...] * (1.0 - lam_init)).astype(BF16)


def _attention(lam4, gs, q, kv_sets, *, lam_init, b, heads, tq, name, cast=None):
    nq, aw = q.shape
    vw = 2 * DA_HEAD_DIM
    per_b = nq // b // tq
    q_spec = pl.BlockSpec((tq, vw), lambda bi, h, i: (bi * per_b + i, h))
    in_specs = [pl.BlockSpec((4, DA_HEAD_DIM), lambda bi, h, i: (0, 0)),
                pl.BlockSpec((1, vw), lambda bi, h, i: (0, 0)),
                q_spec]
    args = [lam4, gs, q]
    for k, vt in kv_sets:
        sk = k.shape[0] // b
        in_specs += [pl.BlockSpec((sk, vw), lambda bi, h, i: (bi, h)),
                     pl.BlockSpec((vw, sk), lambda bi, h, i: (h, bi))]
        args += [k, vt]
    out_specs = [q_spec]
    out_shape = [jax.ShapeDtypeStruct((nq, aw), BF16)]
    n_cast = 0
    if cast is not None:
        layer, w_gate, w_up, w_down = cast
        _, d, dff = w_gate.shape
        n_steps = b * heads * per_b
        per = next(dv for dv in range(1, dff // LANES + 1) if (dff // LANES) % dv == 0 and dv * n_steps >= dff // LANES)
        cw = per * LANES
        nblk = dff // cw

        def blk(bi, h, i):
            return jnp.minimum((bi * heads + h) * per_b + i, nblk - 1)

        in_specs += [pl.BlockSpec((None, d, cw), lambda bi, h, i: (layer, 0, blk(bi, h, i))),
                     pl.BlockSpec((None, d, cw), lambda bi, h, i: (layer, 0, blk(bi, h, i))),
                     pl.BlockSpec((None, cw, d), lambda bi, h, i: (layer, blk(bi, h, i), 0))]
        args += [w_gate, w_up, w_down]
        out_specs += [pl.BlockSpec((d, cw), lambda bi, h, i: (0, blk(bi, h, i))),
                      pl.BlockSpec((d, cw), lambda bi, h, i: (0, blk(bi, h, i))),
                      pl.BlockSpec((cw, d), lambda bi, h, i: (blk(bi, h, i), 0))]
        out_shape += [jax.ShapeDtypeStruct((d, dff), BF16), jax.ShapeDtypeStruct((d, dff), BF16),
                      jax.ShapeDtypeStruct((dff, d), BF16)]
        n_cast = 3
    return pl.pallas_call(
        functools.partial(_attn_kernel, lam_init=lam_init, n_sets=len(kv_sets), n_cast=n_cast),
        grid=(b, heads, per_b),
        in_specs=in_specs,
        out_specs=out_specs,
        out_shape=out_shape,
        compiler_params=_cparams(("arbitrary", "arbitrary", "arbitrary")),
        name=name,
    )(*args)


def _ssm_weights(a_re, a_im, log_dt, b_re, b_im, c_re, c_im):
    L, H, P = SSM_CHUNK, SSM_GROUP, SSM_STATE
    g = a_re.shape[1]
    npair = g // 2
    hp = lax.Precision.HIGHEST
    ar, ai = a_re.astype(F32), a_im.astype(F32)
    dt = jnp.exp(log_dt.astype(F32))[..., None]
    n = jnp.arange(L + 1, dtype=F32)
    mag = jnp.exp((ar * dt)[..., None] * n)
    ang = (ai * dt)[..., None] * n
    apr, api = mag * jnp.cos(ang), mag * jnp.sin(ang)
    xr, xi = apr[..., 1] - 1.0, api[..., 1]
    den = ar * ar + ai * ai
    qr, qi = (xr * ar + xi * ai) / den, (xi * ar - xr * ai) / den
    br, bi = b_re.astype(F32), b_im.astype(F32)
    bbr = qr[..., None] * br - qi[..., None] * bi
    bbi = qr[..., None] * bi + qi[..., None] * br
    cr, ci = c_re.astype(F32), c_im.astype(F32)

    car = cr[..., None] * apr[:, :, None, :, :L] - ci[..., None] * api[:, :, None, :, :L]
    cai = cr[..., None] * api[:, :, None, :, :L] + ci[..., None] * apr[:, :, None, :, :L]
    kt = (jnp.einsum('dghpt,dgpk->dgkth', car, bbr, precision=hp)
          - jnp.einsum('dghpt,dgpk->dgkth', cai, bbi, precision=hp))
    kc = jnp.concatenate([jnp.flip(kt[1][:, :, 1:], axis=2), kt[0][:, :, :1] + kt[1][:, :, :1], kt[0][:, :, 1:]],
                         axis=2).reshape(g, H, (2 * L - 1) * H)
    w_intra = jnp.pad(kc, ((0, 0), (0, 0), (0, (-kc.shape[2]) % LANES)))

    sir = jnp.stack([jnp.flip(apr[0, ..., :L], axis=-1), apr[1, ..., :L]])
    sii = jnp.stack([jnp.flip(api[0, ..., :L], axis=-1), api[1, ..., :L]])
    sir, sii = jnp.swapaxes(sir, 2, 3)[:, :, :, None, :], jnp.swapaxes(sii, 2, 3)[:, :, :, None, :]
    tbr, tbi = jnp.swapaxes(bbr, 2, 3)[:, :, None], jnp.swapaxes(bbi, 2, 3)[:, :, None]
    w_sin = jnp.concatenate([sir * tbr - sii * tbi, sir * tbi + sii * tbr], axis=-1)
    w_sin = w_sin.reshape(2, g, L * H, 2 * P)

    sor = jnp.stack([apr[0, ..., 1:], jnp.flip(apr[1, ..., 1:], axis=-1)])
    soi = jnp.stack([api[0, ..., 1:], jnp.flip(api[1, ..., 1:], axis=-1)])
    tcr, tci = jnp.swapaxes(cr, 2, 3)[:, :, :, None, :], jnp.swapaxes(ci, 2, 3)[:, :, :, None, :]
    cnr = (tcr * sor[..., None] - tci * soi[..., None]).reshape(2, g, P, L * H)
    cni = (tcr * soi[..., None] + tci * sor[..., None]).reshape(2, g, P, L * H)
    w_so = jnp.concatenate([cnr, -cni], axis=2)

    al = jnp.stack([apr[..., L].reshape(2, npair, 2 * P), api[..., L].reshape(2, npair, 2 * P)], axis=2)
    al = al.reshape(2, 2 * npair, 1, 2 * P)
    return w_intra, w_sin.astype(BF16), w_so.astype(BF16), al


def _atom_transpose(vs):
    atom = lax.broadcasted_iota(jnp.int32, vs[0].shape, 1) // SSM_GROUP
    cur = list(vs)
    d = OCT // 2
    while d:
        hi = (atom & d) != 0
        nxt = list(cur)
        for i in range(OCT):
            if i & d:
                continue
            a, b = cur[i], cur[i + d]
            nxt[i] = jnp.where(hi, pltpu.roll(b, d * SSM_GROUP, 1), a)
            nxt[i + d] = jnp.where(hi, b, pltpu.roll(a, LANES - d * SSM_GROUP, 1))
        cur = nxt
        d //= 2
    return cur


def _atom_transpose_matrix():
    n = OCT * OCT * SSM_GROUP
    i = np.arange(n)
    j = ((i // SSM_GROUP) % OCT) * LANES + (i // LANES) * SSM_GROUP + i % SSM_GROUP
    p = np.zeros((n, n), np.float32)
    p[i, j] = 1.0
    return jnp.asarray(p, BF16)


def _ssm_in_kernel(us_ref, p_ref, w_ref, ut_ref, s_ref):
    noct, nb = us_ref.shape[0], us_ref.shape[1]
    nc = us_ref.shape[2] // SSM_CHUNK
    n = nb * nc
    half = SSM_CHUNK // 2
    src = []
    for o in range(noct):
        for hf in range(2):
            per_seq = [jnp.concatenate([us_ref[o, bi, pl.ds(half * hf + sl, nc, stride=SSM_CHUNK), :]
                                        for sl in range(half)], axis=1) for bi in range(nb)]
            src.append(jnp.concatenate(per_seq, axis=0).astype(BF16))
    perm = jnp.dot(jnp.concatenate(src, axis=0), p_ref[...], preferred_element_type=F32).astype(BF16)

    def group_chunk(gi):
        o, gl = gi // OCT, gi % OCT
        return jnp.concatenate([perm[(2 * o + hf) * n:(2 * o + hf + 1) * n, gl * LANES:(gl + 1) * LANES]
                                for hf in range(2)], axis=1)

    def store_chunk_major(d, blk, val):
        for bi in range(nb):
            s_ref[d, blk, pl.ds(bi, nc, stride=nb), :] = val[bi * nc:(bi + 1) * nc]

    low = lax.broadcasted_iota(jnp.int32, (n, LANES), 1) < SSM_STATE
    for j in range(noct * OCT // 2):
        ug = [group_chunk(2 * j), group_chunk(2 * j + 1)]
        ut_ref[2 * j] = ug[0]
        ut_ref[2 * j + 1] = ug[1]
        for d in range(2):
            s0 = jnp.dot(ug[0], w_ref[d, 2 * j], preferred_element_type=F32)
            s1 = jnp.dot(ug[1], w_ref[d, 2 * j + 1], preferred_element_type=F32)
            store_chunk_major(d, 2 * j, jnp.where(low, s0, pltpu.roll(s1, SSM_STATE, 1)))
            store_chunk_major(d, 2 * j + 1, jnp.where(low, pltpu.roll(s0, SSM_STATE, 1), s1))


def _ssm_scan_kernel(sc_ref, sl_ref, al_ref, hc_ref, hl_ref, *, b):
    d = pl.program_id(0)
    nblk = sc_ref.shape[1]
    tile = 8
    cpt = tile // b
    grp = lax.broadcasted_iota(jnp.int32, (tile, LANES), 0) // b
    ars = [al_ref[0, 2 * k] for k in range(nblk // 2)]
    ais = [al_ref[0, 2 * k + 1] for k in range(nblk // 2)]

    def spread(x, g):
        x = jnp.where(grp == g, x, 0.0)
        out = x
        for r in range(1, cpt):
            out = out + pltpu.roll(x, r * b, 0)
        return out

    def phase(s_ref, h_ref, carry):
        ntile = s_ref.shape[2] // tile

        def body(i, carry):
            t = jnp.where(d == 0, i, ntile - 1 - i)
            rows = pl.ds(pl.multiple_of(t * tile, tile), tile)
            new = []
            for k in range(nblk // 2):
                hr, hi = carry[2 * k], carry[2 * k + 1]
                sr, si = s_ref[0, 2 * k, rows, :], s_ref[0, 2 * k + 1, rows, :]
                hin_r, hin_i = hr, hi
                for step in range(cpt):
                    g = jnp.where(d == 0, step, cpt - 1 - step)
                    hin_r = jnp.where(grp == g, hr, hin_r)
                    hin_i = jnp.where(grp == g, hi, hin_i)
                    sgr, sgi = spread(sr, g), spread(si, g)
                    hr, hi = ars[k] * hr - ais[k] * hi + sgr, ars[k] * hi + ais[k] * hr + sgi
                h_ref[0, 2 * k, rows, :] = hin_r
                h_ref[0, 2 * k + 1, rows, :] = hin_i
                new += [hr, hi]
            return tuple(new)

        return lax.fori_loop(0, ntile, body, carry, unroll=2)

    carry = tuple(jnp.zeros((tile, LANES), F32) for _ in range(nblk))
    carry = phase(sc_ref, hc_ref, carry)
    phase(sl_ref, hl_ref, carry)


def _ssm_out_kernel(ut_ref, h_ref, kc_ref, wo_ref, y_ref, wi_ref):
    n = ut_ref.shape[1]
    ng = ut_ref.shape[0]
    nb = y_ref.shape[1]
    nc = n // nb
    noct = ng // OCT
    half = SSM_CHUNK // 2

    @pl.when(pl.program_id(0) == 0)
    def _():
        for gi in range(ng):
            kc = kc_ref[gi]
            wi_ref[gi] = jnp.concatenate(
                [kc[:, (SSM_CHUNK - 1 - s) * SSM_GROUP:(SSM_CHUNK - 1 - s) * SSM_GROUP + SSM_CW]
                 for s in range(SSM_CHUNK)], axis=0).astype(BF16)

    def load_seq_major(d, blk):
        return jnp.concatenate([h_ref[d, blk, pl.ds(bi, nc, stride=nb), :] for bi in range(nb)], axis=0)

    low = lax.broadcasted_iota(jnp.int32, (n, LANES), 1) < SSM_STATE
    ys = []
    for j in range(ng // 2):
        hg = [[], []]
        for d in range(2):
            hr, hi = load_seq_major(d, 2 * j), load_seq_major(d, 2 * j + 1)
            hg[0].append(jnp.where(low, hr, pltpu.roll(hi, SSM_STATE, 1)).astype(BF16))
            hg[1].append(jnp.where(low, pltpu.roll(hr, SSM_STATE, 1), hi).astype(BF16))
        for e in range(2):
            gi = 2 * j + e
            ys.append(jnp.dot(ut_ref[gi], wi_ref[gi], preferred_element_type=F32)
                      + jnp.dot(hg[e][0], wo_ref[0, gi], preferred_element_type=F32)
                      + jnp.dot(hg[e][1], wo_ref[1, gi], preferred_element_type=F32))
    for o in range(noct):
        for hf in range(2):
            toks = _atom_transpose([ys[OCT * o + gl][:, hf * LANES:(hf + 1) * LANES] for gl in range(OCT)])
            for tl in range(half):
                for bi in range(nb):
                    y_ref[o, bi, pl.ds(half * hf + tl, nc, stride=SSM_CHUNK), :] = toks[tl][bi * nc:(bi + 1) * nc]


def _ssm_conv(us_lat, us_ctx, weights, *, layer, b):
    w_intra, w_sin, w_so, al = weights
    perm = _atom_transpose_matrix()
    P = SSM_STATE
    ng = w_intra.shape[1]
    nblk = ng
    assert 8 % b == 0

    def stage_in(us, steps):
        noct, rows, _ = us.shape
        seq = rows // b
        tok = seq // steps
        n = b * tok // SSM_CHUNK
        nchunks = rows // SSM_CHUNK
        return pl.pallas_call(
            _ssm_in_kernel,
            grid=(steps,),
            in_specs=[pl.BlockSpec((noct, b, tok, LANES), lambda i: (0, 0, i, 0)),
                      _resident(perm.shape), _resident_layer(w_sin.shape, layer)],
            out_specs=[pl.BlockSpec((ng, n, SSM_CW), lambda i: (0, i, 0)),
                       pl.BlockSpec((2, nblk, n, LANES), lambda i: (0, 0, i, 0))],
            out_shape=[jax.ShapeDtypeStruct((ng, nchunks, SSM_CW), BF16),
                       jax.ShapeDtypeStruct((2, nblk, nchunks, LANES), F32)],
            compiler_params=_cparams(("arbitrary",)),
            name="ssm_in",
        )(us.reshape(noct, b, seq, LANES), perm, w_sin)

    lat_steps = 2 * b
    ut_lat, s_lat = stage_in(us_lat, lat_steps)
    ut_ctx, s_ctx = stage_in(us_ctx, 1)

    cb = 16
    nc_rows, nl_rows = s_ctx.shape[2], s_lat.shape[2]
    h_ctx, h_lat = pl.pallas_call(
        functools.partial(_ssm_scan_kernel, b=b),
        grid=(2, nblk // cb),
        in_specs=[pl.BlockSpec((1, cb, nc_rows, LANES), lambda d, j: (d, j, 0, 0)),
                  pl.BlockSpec((1, cb, nl_rows, LANES), lambda d, j: (d, j, 0, 0)),
                  pl.BlockSpec((None, 1, cb, 1, 2 * P), lambda d, j: (layer, d, j, 0, 0))],
        out_specs=[pl.BlockSpec((1, cb, nc_rows, LANES), lambda d, j: (d, j, 0, 0)),
                   pl.BlockSpec((1, cb, nl_rows, LANES), lambda d, j: (d, j, 0, 0))],
        out_shape=[jax.ShapeDtypeStruct(s_ctx.shape, F32), jax.ShapeDtypeStruct(s_lat.shape, F32)],
        compiler_params=_cparams(("arbitrary", "arbitrary")),
        name="ssm_scan",
    )(s_ctx, s_lat, al)

    def stage_out(ut, hin, steps):
        _, nchunks, _ = ut.shape
        n = nchunks // steps
        noct = ng // OCT
        seq = nchunks * SSM_CHUNK // b
        tok = seq // steps
        y = pl.pallas_call(
            _ssm_out_kernel,
            grid=(steps,),
            in_specs=[pl.BlockSpec((ng, n, SSM_CW), lambda i: (0, i, 0)),
                      pl.BlockSpec((2, nblk, n, LANES), lambda i: (0, 0, i, 0)),
                      _resident_layer(w_intra.shape, layer), _resident_layer(w_so.shape, layer)],
            out_specs=pl.BlockSpec((noct, b, tok, LANES), lambda i: (0, 0, i, 0)),
            out_shape=jax.ShapeDtypeStruct((noct, b, seq, LANES), F32),
            scratch_shapes=[pltpu.VMEM((ng, SSM_CW, SSM_CW), BF16)],
            compiler_params=_cparams(("arbitrary",)),
            name="ssm_out",
        )(ut, hin, w_intra, w_so)
        return y.reshape(noct, b * seq, LANES)

    return stage_out(ut_lat, h_lat, lat_steps), stage_out(ut_ctx, h_ctx, 1)


def _fourier_kernel(ulo_ref, uup_ref, w_ref, cos_ref, sin_ref, alt_ref, s1_ref, b_ref,
                    lo_ref, up_ref, a_s, b_s, am_s, carry, *, groups):
    gc = ulo_ref.shape[1] // groups
    tm = lo_ref.shape[0]
    nt = a_s.shape[0] // tm
    row = lax.broadcasted_iota(jnp.int32, lo_ref.shape, 0)

    @pl.when(pl.program_id(1) == 0)
    def _():
        for jj in range(nt):
            pr = jnp.dot(s1_ref[...], uup_ref[(nt - 1 - jj) * tm:(nt - jj) * tm, :], preferred_element_type=F32)
            if jj > 0:
                first = uup_ref[(nt - jj) * tm:(nt - jj) * tm + am_s.shape[0], :][0:1, :].astype(F32)
                pr = jnp.where(row == 0, first, pr)
            pr = pr.astype(BF16)
            rows = slice(jj * tm, (jj + 1) * tm)
            for g in range(groups):
                cols = slice(g * gc, (g + 1) * gc)
                ab_lo = jnp.dot(ulo_ref[rows, cols], w_ref[g], preferred_element_type=F32)
                ab_pr = jnp.dot(pr[:, cols], w_ref[g], preferred_element_type=F32)
                a_s[rows, cols] = (ab_lo[:, :gc] + ab_pr[:, :gc]).astype(BF16)
                b_s[rows, cols] = (ab_lo[:, gc:] - ab_pr[:, gc:]).astype(BF16)
        for g in range(groups):
            cols = slice(g * gc, (g + 1) * gc)
            am_s[:, cols] = jnp.dot(uup_ref[0:am_s.shape[0], cols], w_ref[g, :, 0:gc], preferred_element_type=F32)
        mid = jnp.dot(alt_ref[...], a_s[...], preferred_element_type=F32)
        carry[...] = mid + am_s[0:carry.shape[0], :] + b_ref[...]

    a_mid = jnp.where((row & 1) == 0, am_s[0:1, :], -am_s[0:1, :])
    p = jnp.dot(cos_ref[...], a_s[...], preferred_element_type=F32) + a_mid + b_ref[...]
    q = jnp.dot(sin_ref[...], b_s[...], preferred_element_type=F32)
    lo_ref[...] = (p - q).astype(BF16)
    hi = (p + q).astype(BF16)
    rev = jnp.dot(s1_ref[...], hi, preferred_element_type=F32)
    up_ref[...] = jnp.where(row == 0, carry[0:1, :], rev).astype(BF16)
    carry[0:1, :] = hi[0:1, :].astype(F32)


def _dft_half_tables(n):
    half = n // 2
    k = np.arange(half, dtype=np.int64)
    ang = ((k[:, None] * k[None, :]) % n).astype(np.float64) * (2.0 * np.pi / n)
    alt = np.broadcast_to(np.where((k & 1) == 0, 1.0, -1.0), (8, half))
    return tuple(jnp.asarray(a, F32).astype(BF16) for a in (np.cos(ang), np.sin(ang), alt))


def _fourier(uf, w_cs, bias, tables, *, b, tm):
    nr, fw = uf.shape
    seq = nr // b
    half = seq // 2
    groups = w_cs.shape[0]
    cos_t, sin_t, alt = tables
    tm = min(tm, half)
    nt = half // tm
    r = np.arange(1, tm)
    s1 = np.zeros((tm, tm), np.float32)
    s1[r, tm - r] = 1.0
    s1 = jnp.asarray(s1, BF16)
    lo, up = pl.pallas_call(
        functools.partial(_fourier_kernel, groups=groups),
        grid=(b, nt),
        in_specs=[
            pl.BlockSpec((half, fw), lambda bi, i: (2 * bi, 0)),
            pl.BlockSpec((half, fw), lambda bi, i: (2 * bi + 1, 0)),
            pl.BlockSpec(w_cs.shape, lambda bi, i: (0, 0, 0)),
            pl.BlockSpec((tm, half), lambda bi, i: (nt - 1 - i, 0)),
            pl.BlockSpec((tm, half), lambda bi, i: (nt - 1 - i, 0)),
            pl.BlockSpec(alt.shape, lambda bi, i: (0, 0)),
            pl.BlockSpec((tm, tm), lambda bi, i: (0, 0)),
            pl.BlockSpec((1, fw), lambda bi, i: (0, 0)),
        ],
        out_specs=[pl.BlockSpec((tm, fw), lambda bi, i: (bi * nt + nt - 1 - i, 0)),
                   pl.BlockSpec((tm, fw), lambda bi, i: (bi * nt + i, 0))],
        out_shape=[jax.ShapeDtypeStruct((b * half, fw), BF16), jax.ShapeDtypeStruct((b * half, fw), BF16)],
        scratch_shapes=[pltpu.VMEM((half, fw), BF16), pltpu.VMEM((half, fw), BF16),
                        pltpu.VMEM((16, fw), F32), pltpu.VMEM((alt.shape[0], fw), F32)],
        compiler_params=_cparams(("arbitrary", "arbitrary")),
        name="fourier_%d" % seq,
    )(uf, uf, w_cs, cos_t, sin_t, alt, s1, bias)
    return lo, up


def _fourier_weights(w_four, seq):
    groups, gc, _ = w_four.shape
    k = np.arange(gc)
    ang = ((k[:, None] * k[None, :]) % gc).astype(np.float64) * (2.0 * np.pi / gc)
    norm = 1.0 / math.sqrt(seq * gc)
    cc = jnp.asarray(np.cos(ang) * norm, F32)
    sc = jnp.asarray(np.sin(ang) * norm, F32)
    hp = lax.Precision.HIGHEST
    wc = jnp.einsum('ck,gkd->gcd', cc, w_four.astype(F32), precision=hp)
    ws = jnp.einsum('ck,gkd->gcd', sc, w_four.astype(F32), precision=hp)
    return jnp.concatenate([wc, ws], axis=-1).astype(BF16)


def _outproj_kernel(att_ref, us_ref, yc_ref, x_ref, mod_ref, wo_ref, wg_ref, bg_ref, dsk_ref,
                    gpost_ref, gpre_ref, *refs, aw, sw, four_part):
    uf_refs, (xo_ref, h2_ref) = refs[:-2], refs[-2:]
    part = four_part(pl.program_id(0))
    nblk = us_ref.shape[0]
    tm = x_ref.shape[0]
    gate_g = mod_ref[0, 2:3, :] * gpost_ref[...]
    scale_g = gpre_ref[...] * (1.0 + mod_ref[0, 4:5, :])
    for r0 in range(0, tm, tm // 2):
        rows = slice(r0, r0 + tm // 2)
        us = jnp.concatenate([us_ref[j, rows, :] for j in range(nblk)], axis=1)
        yc = jnp.concatenate([yc_ref[j, rows, :] for j in range(nblk)], axis=1)
        g = jax.nn.gelu(dsk_ref[...] * us + yc)
        z = jnp.dot(g.astype(BF16), wg_ref[...], preferred_element_type=F32) + bg_ref[...]
        ssm = (g * jax.nn.sigmoid(z)).astype(BF16)
        uf = uf_refs[0][rows, :]
        for pi in range(1, len(uf_refs)):
            uf = jnp.where(part == pi, uf_refs[pi][rows, :], uf)
        mix = (jnp.dot(att_ref[rows, :], wo_ref[0:aw, :], preferred_element_type=F32)
               + jnp.dot(ssm, wo_ref[aw:aw + sw, :], preferred_element_type=F32)
               + jnp.dot(uf, wo_ref[aw + sw:, :], preferred_element_type=F32))
        xn = x_ref[rows, :] + _rms(mix, NORM_EPS) * gate_g
        xo_ref[rows, :] = xn
        h2_ref[rows, :] = (_rms(xn, NORM_EPS) * scale_g + mod_ref[0, 3:4, :]).astype(BF16)


def _outproj(att, us, yc, four_parts, x, mod, w_out, w_glu, b_glu, dsk, g_post, g_pre, *, layer, mod_row, tm,
             four_tile=None):
    n_rows, d = x.shape
    aw, fw = att.shape[1], four_parts[0].shape[1]
    if four_tile is None:
        four_tile = lambda i: (0, i)
    nblk = us.shape[0]
    sw = nblk * LANES

    def row(i):
        return (i, 0)

    def const(i):
        return (0, 0)

    blk3 = pl.BlockSpec((nblk, tm, LANES), lambda i: (0, i, 0))
    def part_spec(pi):
        nblocks = four_parts[pi].shape[0] // tm
        return pl.BlockSpec((tm, fw), lambda i: (jnp.where(four_tile(i)[0] == pi,
                                                           jnp.clip(four_tile(i)[1], 0, nblocks - 1), 0), 0))

    return pl.pallas_call(
        functools.partial(_outproj_kernel, aw=aw, sw=sw, four_part=lambda i: four_tile(i)[0]),
        grid=(n_rows // tm,),
        in_specs=[
            pl.BlockSpec((tm, aw), row), blk3, blk3,
            pl.BlockSpec((tm, d), row),
            pl.BlockSpec((1, N_MOD, d), lambda i: (mod_row(i), 0, 0)),
            _resident(w_out.shape), _resident_layer(w_glu.shape, layer),
            pl.BlockSpec((1, sw), const), pl.BlockSpec((1, sw), const),
            pl.BlockSpec((1, d), const), pl.BlockSpec((1, d), const),
        ] + [part_spec(pi) for pi in range(len(four_parts))],
        out_specs=[pl.BlockSpec((tm, d), row), pl.BlockSpec((tm, d), row)],
        out_shape=[jax.ShapeDtypeStruct((n_rows, d), F32), jax.ShapeDtypeStruct((n_rows, d), BF16)],
        compiler_params=_cparams(("arbitrary",)),
        name="outproj",
    )(att, us, yc, x, mod, w_out, w_glu, b_glu, dsk, g_post, g_pre, *four_parts)


def _ffn_kernel(h_ref, x_ref, mod_ref, g_ref, wg_ref, wu_ref, wd_ref, o_ref):
    k = pl.program_id(1)

    @pl.when(k == 0)
    def _():
        o_ref[...] = jnp.zeros(o_ref.shape, F32)

    h = h_ref[...]
    tf = wg_ref.shape[1]
    part = None
    for c0 in range(0, tf, tf // 2):
        cols = slice(c0, c0 + tf // 2)
        a = jnp.dot(h, wg_ref[:, cols], preferred_element_type=F32)
        u = jnp.dot(h, wu_ref[:, cols], preferred_element_type=F32)
        p = jnp.dot((_silu(a) * u).astype(BF16), wd_ref[cols, :], preferred_element_type=F32)
        part = p if part is None else part + p
    o_ref[...] += part

    @pl.when(k == pl.num_programs(1) - 1)
    def _():
        o_ref[...] = x_ref[...] + _rms(o_ref[...], NORM_EPS) * (mod_ref[0, 5:6, :] * g_ref[...])


def _ffn(h2, x_mid, mod, g_post, w_gate, w_up, w_down, *, mod_row, tm, tf):
    n_rows, d = h2.shape
    dff = w_gate.shape[1]
    return pl.pallas_call(
        _ffn_kernel,
        grid=(n_rows // tm, dff // tf),
        in_specs=[
            pl.BlockSpec((tm, d), lambda i, k: (i, 0)),
            pl.BlockSpec((tm, d), lambda i, k: (i, 0)),
            pl.BlockSpec((1, N_MOD, d), lambda i, k: (mod_row(i), 0, 0)),
            pl.BlockSpec((1, d), lambda i, k: (0, 0)),
            pl.BlockSpec((d, tf), lambda i, k: (0, k)),
            pl.BlockSpec((d, tf), lambda i, k: (0, k)),
            pl.BlockSpec((tf, d), lambda i, k: (k, 0)),
        ],
        out_specs=pl.BlockSpec((tm, d), lambda i, k: (i, 0)),
        out_shape=jax.ShapeDtypeStruct((n_rows, d), F32),
        compiler_params=_cparams(("arbitrary", "arbitrary")),
        name="ffn",
    )(h2, x_mid, mod, g_post, w_gate, w_up, w_down)


def kernel(x, c, ctx, c_ctx, w_mod, b_mod, g_mix_pre, g_mix_post, g_ffn_pre, g_ffn_post, w_in, w_out, lam_q1, lam_k1, lam_q2, lam_k2, g_subln, ssm_a_re, ssm_a_im, ssm_log_dt, ssm_b_re, ssm_b_im, ssm_c_re, ssm_c_im, ssm_d, w_glu, b_glu, w_four, b_four, w_gate, w_up, w_down):
    b, t, d = x.shape
    n_ctx = ctx.shape[1]
    depth = w_mod.shape[0]
    nl, nc = b * t, b * n_ctx
    aw = d // 2
    sw = ssm_d.shape[1]
    fw = d - aw - sw
    heads = aw // (2 * DA_HEAD_DIM)
    tm = 512
    assert t % tm == 0 and nc % tm == 0 and n_ctx % SSM_CHUNK == 0 and t % GRID_W == 0 and b + 1 <= MOD_ROWS
    assert sw % LANES == 0 and (sw // SSM_GROUP) % (2 * OCT) == 0

    cs = jnp.concatenate([c, c_ctx[None, :], jnp.zeros((MOD_ROWS - b - 1, d), F32)], axis=0)
    mod_all = _modulation(cs, w_mod, b_mod).reshape(depth, MOD_ROWS, N_MOD, d)

    per_b = t // tm
    lat_mod = lambda i: i // per_b
    ctx_mod = lambda i: b
    tabs = _rope_tables(t, tm)
    dft_lat = _dft_half_tables(t)
    dft_ctx = _dft_half_tables(n_ctx)
    x_lat, x_ctx = x.reshape(nl, d), ctx.reshape(nc, d)
    w_glu_b = w_glu.astype(BF16)
    w_in_b = w_in[0].astype(BF16)
    ssm_w = jax.vmap(_ssm_weights)(ssm_a_re, ssm_a_im, ssm_log_dt, ssm_b_re, ssm_b_im, ssm_c_re, ssm_c_im)

    for l in range(depth):
        need_ctx = l < depth - 1
        lam_init = 0.8 - 0.6 * math.exp(-0.3 * l)
        mod = mod_all[l]
        g_pre = g_mix_pre[l][None, :]
        casts = [(w_out, l)] + ([(w_in, l + 1)] if l + 1 < depth else [])
        q, k, vt, us, uf, w_out_b, *w_in_next = _inproj(x_lat, mod, g_pre, w_in_b, tabs, mod_row=lat_mod,
                                                        tab_blk=lambda i: i % per_b, aw=aw, sw=sw, fw=fw, tm=tm,
                                                        cast=casts)
        qc, kc, vtc, usc, ufc = _inproj(x_ctx, mod, g_pre, w_in_b, tabs, mod_row=ctx_mod,
                                        tab_blk=lambda i: per_b, aw=aw, sw=sw, fw=fw, tm=tm)
        w_in_b = w_in_next[0] if w_in_next else None

        lam4 = jnp.stack([lam_q1[l], lam_k1[l], lam_q2[l], lam_k2[l]]).astype(F32)
        gs = g_subln[l][None, :].astype(F32)
        att, *ffn_wb = _attention(lam4, gs, q, [(kc, vtc), (k, vt)], lam_init=lam_init, b=b, heads=heads, tq=1024,
                                  name="attn_latent", cast=(l, w_gate, w_up, w_down))
        yc, ycc = _ssm_conv(us, usc, ssm_w, layer=l, b=b)
        bias = b_four[l].reshape(1, fw).astype(F32)
        four = _fourier(uf, _fourier_weights(w_four[l], t), bias, dft_lat, b=b, tm=tm)

        small = (b_glu[l][None, :].astype(F32), ssm_d[l][None, :].astype(F32), g_mix_post[l][None, :],
                 g_ffn_pre[l][None, :])
        ffn_w = (g_ffn_post[l][None, :], *ffn_wb)
        half_t = per_b // 2

        def four_tile(i):
            pos = i % per_b
            return pos // half_t, (i // per_b) * half_t + pos % half_t

        x_mid, h2 = _outproj(att, us, yc, four, x_lat, mod, w_out_b, w_glu_b, *small, layer=l, mod_row=lat_mod,
                             tm=tm, four_tile=four_tile)
        x_lat = _ffn(h2, x_mid, mod, *ffn_w, mod_row=lat_mod, tm=tm, tf=512)

        if need_ctx:
            att_c, = _attention(lam4, gs, qc, [(kc, vtc)], lam_init=lam_init, b=b, heads=heads, tq=n_ctx,
                                name="attn_ctx")
            lo_c, up_c = _fourier(ufc, _fourier_weights(w_four[l], n_ctx), bias, dft_ctx, b=b, tm=tm)
            four_c = [jnp.concatenate([lo_c.reshape(b, -1, fw), up_c.reshape(b, -1, fw)], axis=1).reshape(nc, fw)]
            xc_mid, h2c = _outproj(att_c, usc, ycc, four_c, x_ctx, mod, w_out_b, w_glu_b, *small, layer=l,
                                   mod_row=ctx_mod, tm=tm)
            x_ctx = _ffn(h2c, xc_mid, mod, *ffn_w, mod_row=ctx_mod, tm=tm, tf=512)
    return x_lat.reshape(b, t, d)
```

```python
import functools
import math

import jax
import jax.numpy as jnp
import numpy as np
from jax import lax
from jax.experimental import pallas as pl
from jax.experimental.pallas import tpu as pltpu

F32 = jnp.float32
BF16 = jnp.bfloat16

LANES = 128
GRID_W = 64
DA_HEAD_DIM = 128
SSM_GROUP = 16
SSM_STATE = 64
N_MOD = 6
ROPE_BASE = 10000.0
ROPE_PAIRS = DA_HEAD_DIM // 4
NORM_EPS = 1e-6
SUBLN_EPS = 1e-5

SSM_CHUNK = 16
SSM_CW = SSM_CHUNK * SSM_GROUP
OCT = LANES // SSM_GROUP
MOD_ROWS = 8
VMEM_LIMIT = 56 * 1024 * 1024
NT_DIMS = (((1,), (1,)), ((), ()))


def _cparams(sem):
    return pltpu.CompilerParams(dimension_semantics=sem, vmem_limit_bytes=VMEM_LIMIT)


def _rms(x, eps):
    return x * lax.rsqrt(jnp.mean(x * x, axis=-1, keepdims=True) + eps)


def _silu(x):
    return x * jax.nn.sigmoid(x)


def _resident(shape):
    nd = len(shape)
    return pl.BlockSpec(shape, lambda *_: (0,) * nd, pipeline_mode=pl.Buffered(1))


def _resident_layer(shape, layer):
    nd = len(shape)
    return pl.BlockSpec((None,) + tuple(shape[1:]), lambda *_: (layer,) + (0,) * (nd - 1),
                        pipeline_mode=pl.Buffered(1))


def _mod_kernel(c_ref, w_ref, b_ref, o_ref):
    s = _silu(c_ref[...]).astype(BF16)
    o_ref[0] = jnp.dot(s, w_ref[0].astype(BF16), preferred_element_type=F32) + b_ref[0]


def _modulation(cs, w_mod, b_mod, tn=1024):
    depth, d, n = w_mod.shape
    return pl.pallas_call(
        _mod_kernel,
        grid=(depth, n // tn),
        in_specs=[
            pl.BlockSpec((MOD_ROWS, d), lambda l, j: (0, 0)),
            pl.BlockSpec((1, d, tn), lambda l, j: (l, 0, j)),
            pl.BlockSpec((1, 1, tn), lambda l, j: (l, 0, j)),
        ],
        out_specs=pl.BlockSpec((1, MOD_ROWS, tn), lambda l, j: (l, 0, j)),
        out_shape=jax.ShapeDtypeStruct((depth, MOD_ROWS, n), F32),
        compiler_params=_cparams(("arbitrary", "arbitrary")),
        name="modulation",
    )(cs, w_mod, b_mod.reshape(depth, 1, n))


def _inproj_kernel(x_ref, mod_ref, g_ref, w_ref, cq_ref, sq_ref, ck_ref, sk_ref, *refs, aw, sw):
    n_cast = (len(refs) - 5) // 2
    q_ref, k_ref, vt_ref, us_ref, uf_ref = refs[n_cast:n_cast + 5]
    for src, dst in zip(refs[:n_cast], refs[n_cast + 5:]):
        dst[...] = src[...].astype(BF16)
    tm = x_ref.shape[0]
    hb = (_rms(x_ref[...], NORM_EPS) * (g_ref[...] * (1.0 + mod_ref[0, 1:2, :])) + mod_ref[0, 0:1, :]).astype(BF16)

    lane = lax.broadcasted_iota(jnp.int32, (tm, DA_HEAD_DIM), 1)
    first_half = (lane & (2 * ROPE_PAIRS - 1)) < ROPE_PAIRS

    def rope(z, c, s):
        partner = jnp.where(first_half, pltpu.roll(z, DA_HEAD_DIM - ROPE_PAIRS, 1), pltpu.roll(z, ROPE_PAIRS, 1))
        return z * c + partner * s

    zq = jnp.dot(hb, w_ref[:, 0:aw], preferred_element_type=F32)
    cq, sq = cq_ref[...], sq_ref[...]
    for j in range(aw // DA_HEAD_DIM):
        sl = slice(j * DA_HEAD_DIM, (j + 1) * DA_HEAD_DIM)
        q_ref[:, sl] = rope(zq[:, sl], cq, sq).astype(BF16)
    zk = jnp.dot(hb, w_ref[:, aw:2 * aw], preferred_element_type=F32)
    ck, sk = ck_ref[...], sk_ref[...]
    for j in range(aw // DA_HEAD_DIM):
        sl = slice(j * DA_HEAD_DIM, (j + 1) * DA_HEAD_DIM)
        k_ref[:, sl] = rope(zk[:, sl], ck, sk).astype(BF16)
    zv = jnp.dot(hb, w_ref[:, 2 * aw:3 * aw], preferred_element_type=F32)
    vt_ref[...] = jnp.transpose(zv.astype(BF16))
    zs = jnp.dot(hb, w_ref[:, 3 * aw:3 * aw + sw], preferred_element_type=F32)
    for j in range(sw // LANES):
        us_ref[j] = zs[:, j * LANES:(j + 1) * LANES]
    uf_ref[...] = jnp.dot(hb, w_ref[:, 3 * aw + sw:], preferred_element_type=F32).astype(BF16)


def _inproj(x, mod, g, w_in, tabs, *, mod_row, tab_blk, aw, sw, fw, tm, cast=()):
    nr, d = x.shape
    n_steps = nr // tm

    def row(i):
        return (i, 0)

    tab_spec = pl.BlockSpec((tm, DA_HEAD_DIM), lambda i: (tab_blk(i), 0))
    cast_in, cast_out, cast_shape = [], [], []
    for w, layer in cast:
        _, r, c = w.shape
        per = next(dv for dv in range(1, c // LANES + 1) if (c // LANES) % dv == 0 and dv * n_steps >= c // LANES)
        cw = per * LANES
        cast_in.append(pl.BlockSpec((None, r, cw), lambda i, layer=layer, nb=c // cw: (layer, 0, jnp.minimum(i, nb - 1))))
        cast_out.append(pl.BlockSpec((r, cw), lambda i, nb=c // cw: (0, jnp.minimum(i, nb - 1))))
        cast_shape.append(jax.ShapeDtypeStruct((r, c), BF16))
    return pl.pallas_call(
        functools.partial(_inproj_kernel, aw=aw, sw=sw),
        grid=(n_steps,),
        in_specs=[
            pl.BlockSpec((tm, d), row),
            pl.BlockSpec((1, N_MOD, d), lambda i: (mod_row(i), 0, 0)),
            pl.BlockSpec((1, d), lambda i: (0, 0)),
            _resident(w_in.shape),
            tab_spec, tab_spec, tab_spec, tab_spec,
        ] + cast_in,
        out_specs=[
            pl.BlockSpec((tm, aw), row), pl.BlockSpec((tm, aw), row), pl.BlockSpec((aw, tm), lambda i: (0, i)),
            pl.BlockSpec((sw // LANES, tm, LANES), lambda i: (0, i, 0)), pl.BlockSpec((tm, fw), row),
        ] + cast_out,
        out_shape=[
            jax.ShapeDtypeStruct((nr, aw), BF16), jax.ShapeDtypeStruct((nr, aw), BF16),
            jax.ShapeDtypeStruct((aw, nr), BF16), jax.ShapeDtypeStruct((sw // LANES, nr, LANES), F32),
            jax.ShapeDtypeStruct((nr, fw), BF16),
        ] + cast_shape,
        compiler_params=_cparams(("arbitrary",)),
        name="inproj",
    )(x, mod, g, w_in, *tabs, *[w for w, _ in cast])


def _rope_tables(t, tm):
    rows = t // GRID_W
    r = np.repeat(np.arange(rows, dtype=np.float64), GRID_W)
    col = np.tile(np.arange(GRID_W, dtype=np.float64), rows)
    inv = ROPE_BASE ** (-np.arange(ROPE_PAIRS, dtype=np.float64) / ROPE_PAIRS)
    ar, ac = r[:, None] * inv, col[:, None] * inv
    cos = np.concatenate([np.cos(ar), np.cos(ar), np.cos(ac), np.cos(ac)], axis=1)
    sin = np.concatenate([-np.sin(ar), np.sin(ar), -np.sin(ac), np.sin(ac)], axis=1)
    cos = np.concatenate([cos, np.ones((tm, DA_HEAD_DIM))], axis=0)
    sin = np.concatenate([sin, np.zeros((tm, DA_HEAD_DIM))], axis=0)
    scale = DA_HEAD_DIM ** -0.5 * math.log2(math.e)
    return tuple(jnp.asarray(a, F32) for a in (cos * scale, sin * scale, cos, sin))


def _attn_kernel(lam_ref, gs_ref, q_ref, *refs, lam_init, n_sets, n_cast):
    k_refs, vt_refs = refs[0:2 * n_sets:2], refs[1:2 * n_sets:2]
    cast_in = refs[2 * n_sets:2 * n_sets + n_cast]
    o_ref = refs[2 * n_sets + n_cast]
    cast_out = refs[2 * n_sets + n_cast + 1:]
    for src, dst in zip(cast_in, cast_out):
        dst[...] = src[...].astype(BF16)

    hd = DA_HEAD_DIM
    s1 = jnp.sum(lam_ref[0:1, :] * lam_ref[1:2, :], axis=-1, keepdims=True)
    s2 = jnp.sum(lam_ref[2:3, :] * lam_ref[3:4, :], axis=-1, keepdims=True)
    lam = jnp.exp(s1) - jnp.exp(s2) + lam_init
    outs = []
    for idx in range(2):
        qi = q_ref[:, idx * hd:(idx + 1) * hd]
        ss = [lax.dot_general(kr[:, idx * hd:(idx + 1) * hd], qi, NT_DIMS, preferred_element_type=F32)
              for kr in k_refs]
        m = functools.reduce(jnp.maximum, [jnp.max(s, axis=0, keepdims=True) for s in ss])
        ps = [jnp.exp2(s - m) for s in ss]
        l = functools.reduce(jnp.add, [jnp.sum(p, axis=0, keepdims=True) for p in ps])
        acc = functools.reduce(jnp.add, [jnp.dot(vr[...], p.astype(BF16), preferred_element_type=F32)
                                         for vr, p in zip(vt_refs, ps)])
        outs.append(acc * (1.0 / l))
    o = jnp.transpose(outs[0] - outs[1] * lam)
    o_ref[...] = (_rms(o, SUBLN_EPS) * gs_ref[...] * (1.0 - lam_init)).astype(BF16)


def _attention(lam4, gs, q, kv_sets, *, lam_init, b, heads, tq, name, cast=None):
    nq, aw = q.shape
    vw = 2 * DA_HEAD_DIM
    per_b = nq // b // tq
    q_spec = pl.BlockSpec((tq, vw), lambda bi, h, i: (bi * per_b + i, h))
    in_specs = [pl.BlockSpec((4, DA_HEAD_DIM), lambda bi, h, i: (0, 0)),
                pl.BlockSpec((1, vw), lambda bi, h, i: (0, 0)),
                q_spec]
    args = [lam4, gs, q]
    for k, vt in kv_sets:
        sk = k.shape[0] // b
        in_specs += [pl.BlockSpec((sk, vw), lambda bi, h, i: (bi, h)),
                     pl.BlockSpec((vw, sk), lambda bi, h, i: (h, bi))]
        args += [k, vt]
    out_specs = [q_spec]
    out_shape = [jax.ShapeDtypeStruct((nq, aw), BF16)]
    n_cast = 0
    if cast is not None:
        layer, w_gate, w_up, w_down = cast
        _, d, dff = w_gate.shape
        n_steps = b * heads * per_b
        per = next(dv for dv in range(1, dff // LANES + 1) if (dff // LANES) % dv == 0 and dv * n_steps >= dff // LANES)
        cw = per * LANES
        nblk = dff // cw

        def blk(bi, h, i):
            return jnp.minimum((bi * heads + h) * per_b + i, nblk - 1)

        in_specs += [pl.BlockSpec((None, d, cw), lambda bi, h, i: (layer, 0, blk(bi, h, i))),
                     pl.BlockSpec((None, d, cw), lambda bi, h, i: (layer, 0, blk(bi, h, i))),
                     pl.BlockSpec((None, cw, d), lambda bi, h, i: (layer, blk(bi, h, i), 0))]
        args += [w_gate, w_up, w_down]
        out_specs += [pl.BlockSpec((d, cw), lambda bi, h, i: (0, blk(bi, h, i))),
                      pl.BlockSpec((d, cw), lambda bi, h, i: (0, blk(bi, h, i))),
                      pl.BlockSpec((cw, d), lambda bi, h, i: (blk(bi, h, i), 0))]
        out_shape += [jax.ShapeDtypeStruct((d, dff), BF16), jax.ShapeDtypeStruct((d, dff), BF16),
                      jax.ShapeDtypeStruct((dff, d), BF16)]
        n_cast = 3
    return pl.pallas_call(
        functools.partial(_attn_kernel, lam_init=lam_init, n_sets=len(kv_sets), n_cast=n_cast),
        grid=(b, heads, per_b),
        in_specs=in_specs,
        out_specs=out_specs,
        out_shape=out_shape,
        compiler_params=_cparams(("arbitrary", "arbitrary", "arbitrary")),
        name=name,
    )(*args)


def _ssm_weights(a_re, a_im, log_dt, b_re, b_im, c_re, c_im):
    L, H, P = SSM_CHUNK, SSM_GROUP, SSM_STATE
    g = a_re.shape[1]
    npair = g // 2
    hp = lax.Precision.HIGHEST
    ar, ai = a_re.astype(F32), a_im.astype(F32)
    dt = jnp.exp(log_dt.astype(F32))[..., None]
    n = jnp.arange(L + 1, dtype=F32)
    mag = jnp.exp((ar * dt)[..., None] * n)
    ang = (ai * dt)[..., None] * n
    apr, api = mag * jnp.cos(ang), mag * jnp.sin(ang)
    xr, xi = apr[..., 1] - 1.0, api[..., 1]
    den = ar * ar + ai * ai
    qr, qi = (xr * ar + xi * ai) / den, (xi * ar - xr * ai) / den
    br, bi = b_re.astype(F32), b_im.astype(F32)
    bbr = qr[..., None] * br - qi[..., None] * bi
    bbi = qr[..., None] * bi + qi[..., None] * br
    cr, ci = c_re.astype(F32), c_im.astype(F32)

    car = cr[..., None] * apr[:, :, None, :, :L] - ci[..., None] * api[:, :, None, :, :L]
    cai = cr[..., None] * api[:, :, None, :, :L] + ci[..., None] * apr[:, :, None, :, :L]
    kt = (jnp.einsum('dghpt,dgpk->dgkth', car, bbr, precision=hp)
          - jnp.einsum('dghpt,dgpk->dgkth', cai, bbi, precision=hp))
    kc = jnp.concatenate([jnp.flip(kt[1][:, :, 1:], axis=2), kt[0][:, :, :1] + kt[1][:, :, :1], kt[0][:, :, 1:]],
                         axis=2).reshape(g, H, (2 * L - 1) * H)
    w_intra = jnp.pad(kc, ((0, 0), (0, 0), (0, (-kc.shape[2]) % LANES)))

    sir = jnp.stack([jnp.flip(apr[0, ..., :L], axis=-1), apr[1, ..., :L]])
    sii = jnp.stack([jnp.flip(api[0, ..., :L], axis=-1), api[1, ..., :L]])
    sir, sii = jnp.swapaxes(sir, 2, 3)[:, :, :, None, :], jnp.swapaxes(sii, 2, 3)[:, :, :, None, :]
    tbr, tbi = jnp.swapaxes(bbr, 2, 3)[:, :, None], jnp.swapaxes(bbi, 2, 3)[:, :, None]
    w_sin = jnp.concatenate([sir * tbr - sii * tbi, sir * tbi + sii * tbr], axis=-1)
    w_sin = w_sin.reshape(2, g, L * H, 2 * P)

    sor = jnp.stack([apr[0, ..., 1:], jnp.flip(apr[1, ..., 1:], axis=-1)])
    soi = jnp.stack([api[0, ..., 1:], jnp.flip(api[1, ..., 1:], axis=-1)])
    tcr, tci = jnp.swapaxes(cr, 2, 3)[:, :, :, None, :], jnp.swapaxes(ci, 2, 3)[:, :, :, None, :]
    cnr = (tcr * sor[..., None] - tci * soi[..., None]).reshape(2, g, P, L * H)
    cni = (tcr * soi[..., None] + tci * sor[..., None]).reshape(2, g, P, L * H)
    w_so = jnp.concatenate([cnr, -cni], axis=2)

    al = jnp.stack([apr[..., L].reshape(2, npair, 2 * P), api[..., L].reshape(2, npair, 2 * P)], axis=2)
    al = al.reshape(2, 2 * npair, 1, 2 * P)
    return w_intra, w_sin.astype(BF16), w_so.astype(BF16), al


def _atom_transpose(vs):
    atom = lax.broadcasted_iota(jnp.int32, vs[0].shape, 1) // SSM_GROUP
    cur = list(vs)
    d = OCT // 2
    while d:
        hi = (atom & d) != 0
        nxt = list(cur)
        for i in range(OCT):
            if i & d:
                continue
            a, b = cur[i], cur[i + d]
            nxt[i] = jnp.where(hi, pltpu.roll(b, d * SSM_GROUP, 1), a)
            nxt[i + d] = jnp.where(hi, b, pltpu.roll(a, LANES - d * SSM_GROUP, 1))
        cur = nxt
        d //= 2
    return cur


def _atom_transpose_matrix():
    n = OCT * OCT * SSM_GROUP
    i = np.arange(n)
    j = ((i // SSM_GROUP) % OCT) * LANES + (i // LANES) * SSM_GROUP + i % SSM_GROUP
    p = np.zeros((n, n), np.float32)
    p[i, j] = 1.0
    return jnp.asarray(p, BF16)


def _ssm_in_kernel(us_ref, p_ref, w_ref, ut_ref, s_ref):
    noct, nb = us_ref.shape[0], us_ref.shape[1]
    nc = us_ref.shape[2] // SSM_CHUNK
    n = nb * nc
    half = SSM_CHUNK // 2
    src = []
    for o in range(noct):
        for hf in range(2):
            per_seq = [jnp.concatenate([us_ref[o, bi, pl.ds(half * hf + sl, nc, stride=SSM_CHUNK), :]
                                        for sl in range(half)], axis=1) for bi in range(nb)]
            src.append(jnp.concatenate(per_seq, axis=0).astype(BF16))
    perm = jnp.dot(jnp.concatenate(src, axis=0), p_ref[...], preferred_element_type=F32).astype(BF16)

    def group_chunk(gi):
        o, gl = gi // OCT, gi % OCT
        return jnp.concatenate([perm[(2 * o + hf) * n:(2 * o + hf + 1) * n, gl * LANES:(gl + 1) * LANES]
                                for hf in range(2)], axis=1)

    def store_chunk_major(d, blk, val):
        for bi in range(nb):
            s_ref[d, blk, pl.ds(bi, nc, stride=nb), :] = val[bi * nc:(bi + 1) * nc]

    low = lax.broadcasted_iota(jnp.int32, (n, LANES), 1) < SSM_STATE
    for j in range(noct * OCT // 2):
        ug = [group_chunk(2 * j), group_chunk(2 * j + 1)]
        ut_ref[2 * j] = ug[0]
        ut_ref[2 * j + 1] = ug[1]
        for d in range(2):
            s0 = jnp.dot(ug[0], w_ref[d, 2 * j], preferred_element_type=F32)
            s1 = jnp.dot(ug[1], w_ref[d, 2 * j + 1], preferred_element_type=F32)
            store_chunk_major(d, 2 * j, jnp.where(low, s0, pltpu.roll(s1, SSM_STATE, 1)))
            store_chunk_major(d, 2 * j + 1, jnp.where(low, pltpu.roll(s0, SSM_STATE, 1), s1))


def _ssm_scan_kernel(sc_ref, sl_ref, al_ref, hc_ref, hl_ref, *, b):
    d = pl.program_id(0)
    nblk = sc_ref.shape[1]
    tile = 8
    cpt = tile // b
    grp = lax.broadcasted_iota(jnp.int32, (tile, LANES), 0) // b
    ars = [al_ref[0, 2 * k] for k in range(nblk // 2)]
    ais = [al_ref[0, 2 * k + 1] for k in range(nblk // 2)]

    def spread(x, g):
        x = jnp.where(grp == g, x, 0.0)
        out = x
        for r in range(1, cpt):
            out = out + pltpu.roll(x, r * b, 0)
        return out

    def phase(s_ref, h_ref, carry):
        ntile = s_ref.shape[2] // tile

        def body(i, carry):
            t = jnp.where(d == 0, i, ntile - 1 - i)
            rows = pl.ds(pl.multiple_of(t * tile, tile), tile)
            new = []
            for k in range(nblk // 2):
                hr, hi = carry[2 * k], carry[2 * k + 1]
                sr, si = s_ref[0, 2 * k, rows, :], s_ref[0, 2 * k + 1, rows, :]
                hin_r, hin_i = hr, hi
                for step in range(cpt):
                    g = jnp.where(d == 0, step, cpt - 1 - step)
                    hin_r = jnp.where(grp == g, hr, hin_r)
                    hin_i = jnp.where(grp == g, hi, hin_i)
                    sgr, sgi = spread(sr, g), spread(si, g)
                    hr, hi = ars[k] * hr - ais[k] * hi + sgr, ars[k] * hi + ais[k] * hr + sgi
                h_ref[0, 2 * k, rows, :] = hin_r
                h_ref[0, 2 * k + 1, rows, :] = hin_i
                new += [hr, hi]
            return tuple(new)

        return lax.fori_loop(0, ntile, body, carry, unroll=2)

    carry = tuple(jnp.zeros((tile, LANES), F32) for _ in range(nblk))
    carry = phase(sc_ref, hc_ref, carry)
    phase(sl_ref, hl_ref, carry)


def _ssm_out_kernel(ut_ref, h_ref, kc_ref, wo_ref, y_ref, wi_ref):
    n = ut_ref.shape[1]
    ng = ut_ref.shape[0]
    nb = y_ref.shape[1]
    nc = n // nb
    noct = ng // OCT
    half = SSM_CHUNK // 2

    @pl.when(pl.program_id(0) == 0)
    def _():
        for gi in range(ng):
            kc = kc_ref[gi]
            wi_ref[gi] = jnp.concatenate(
                [kc[:, (SSM_CHUNK - 1 - s) * SSM_GROUP:(SSM_CHUNK - 1 - s) * SSM_GROUP + SSM_CW]
                 for s in range(SSM_CHUNK)], axis=0).astype(BF16)

    def load_seq_major(d, blk):
        return jnp.concatenate([h_ref[d, blk, pl.ds(bi, nc, stride=nb), :] for bi in range(nb)], axis=0)

    low = lax.broadcasted_iota(jnp.int32, (n, LANES), 1) < SSM_STATE
    ys = []
    for j in range(ng // 2):
        hg = [[], []]
        for d in range(2):
            hr, hi = load_seq_major(d, 2 * j), load_seq_major(d, 2 * j + 1)
            hg[0].append(jnp.where(low, hr, pltpu.roll(hi, SSM_STATE, 1)).astype(BF16))
            hg[1].append(jnp.where(low, pltpu.roll(hr, SSM_STATE, 1), hi).astype(BF16))
        for e in range(2):
            gi = 2 * j + e
            ys.append(jnp.dot(ut_ref[gi], wi_ref[gi], preferred_element_type=F32)
                      + jnp.dot(hg[e][0], wo_ref[0, gi], preferred_element_type=F32)
                      + jnp.dot(hg[e][1], wo_ref[1, gi], preferred_element_type=F32))
    for o in range(noct):
        for hf in range(2):
            toks = _atom_transpose([ys[OCT * o + gl][:, hf * LANES:(hf + 1) * LANES] for gl in range(OCT)])
            for tl in range(half):
                for bi in range(nb):
                    y_ref[o, bi, pl.ds(half * hf + tl, nc, stride=SSM_CHUNK), :] = toks[tl][bi * nc:(bi + 1) * nc]


def _ssm_conv(us_lat, us_ctx, weights, *, layer, b):
    w_intra, w_sin, w_so, al = weights
    perm = _atom_transpose_matrix()
    P = SSM_STATE
    ng = w_intra.shape[1]
    nblk = ng
    assert 8 % b == 0

    def stage_in(us, steps):
        noct, rows, _ = us.shape
        seq = rows // b
        tok = seq // steps
        n = b * tok // SSM_CHUNK
        nchunks = rows // SSM_CHUNK
        return pl.pallas_call(
            _ssm_in_kernel,
            grid=(steps,),
            in_specs=[pl.BlockSpec((noct, b, tok, LANES), lambda i: (0, 0, i, 0)),
                      _resident(perm.shape), _resident_layer(w_sin.shape, layer)],
            out_specs=[pl.BlockSpec((ng, n, SSM_CW), lambda i: (0, i, 0)),
                       pl.BlockSpec((2, nblk, n, LANES), lambda i: (0, 0, i, 0))],
            out_shape=[jax.ShapeDtypeStruct((ng, nchunks, SSM_CW), BF16),
                       jax.ShapeDtypeStruct((2, nblk, nchunks, LANES), F32)],
            compiler_params=_cparams(("arbitrary",)),
            name="ssm_in",
        )(us.reshape(noct, b, seq, LANES), perm, w_sin)

    lat_steps = 2 * b
    ut_lat, s_lat = stage_in(us_lat, lat_steps)
    ut_ctx, s_ctx = stage_in(us_ctx, 1)

    cb = 16
    nc_rows, nl_rows = s_ctx.shape[2], s_lat.shape[2]
    h_ctx, h_lat = pl.pallas_call(
        functools.partial(_ssm_scan_kernel, b=b),
        grid=(2, nblk // cb),
        in_specs=[pl.BlockSpec((1, cb, nc_rows, LANES), lambda d, j: (d, j, 0, 0)),
                  pl.BlockSpec((1, cb, nl_rows, LANES), lambda d, j: (d, j, 0, 0)),
                  pl.BlockSpec((None, 1, cb, 1, 2 * P), lambda d, j: (layer, d, j, 0, 0))],
        out_specs=[pl.BlockSpec((1, cb, nc_rows, LANES), lambda d, j: (d, j, 0, 0)),
                   pl.BlockSpec((1, cb, nl_rows, LANES), lambda d, j: (d, j, 0, 0))],
        out_shape=[jax.ShapeDtypeStruct(s_ctx.shape, F32), jax.ShapeDtypeStruct(s_lat.shape, F32)],
        compiler_params=_cparams(("arbitrary", "arbitrary")),
        name="ssm_scan",
    )(s_ctx, s_lat, al)

    def stage_out(ut, hin, steps):
        _, nchunks, _ = ut.shape
        n = nchunks // steps
        noct = ng // OCT
        seq = nchunks * SSM_CHUNK // b
        tok = seq // steps
        y = pl.pallas_call(
            _ssm_out_kernel,
            grid=(steps,),
            in_specs=[pl.BlockSpec((ng, n, SSM_CW), lambda i: (0, i, 0)),
                      pl.BlockSpec((2, nblk, n, LANES), lambda i: (0, 0, i, 0)),
                      _resident_layer(w_intra.shape, layer), _resident_layer(w_so.shape, layer)],
            out_specs=pl.BlockSpec((noct, b, tok, LANES), lambda i: (0, 0, i, 0)),
            out_shape=jax.ShapeDtypeStruct((noct, b, seq, LANES), F32),
            scratch_shapes=[pltpu.VMEM((ng, SSM_CW, SSM_CW), BF16)],
            compiler_params=_cparams(("arbitrary",)),
            name="ssm_out",
        )(ut, hin, w_intra, w_so)
        return y.reshape(noct, b * seq, LANES)

    return stage_out(ut_lat, h_lat, lat_steps), stage_out(ut_ctx, h_ctx, 1)


def _fourier_kernel(ulo_ref, uup_ref, w_ref, cos_ref, sin_ref, alt_ref, s1_ref, b_ref,
                    lo_ref, up_ref, a_s, b_s, am_s, carry, *, groups):
    gc = ulo_ref.shape[1] // groups
    tm = lo_ref.shape[0]
    nt = a_s.shape[0] // tm
    row = lax.broadcasted_iota(jnp.int32, lo_ref.shape, 0)

    @pl.when(pl.program_id(1) == 0)
    def _():
        for jj in range(nt):
            pr = jnp.dot(s1_ref[...], uup_ref[(nt - 1 - jj) * tm:(nt - jj) * tm, :], preferred_element_type=F32)
            if jj > 0:
                first = uup_ref[(nt - jj) * tm:(nt - jj) * tm + am_s.shape[0], :][0:1, :].astype(F32)
                pr = jnp.where(row == 0, first, pr)
            pr = pr.astype(BF16)
            rows = slice(jj * tm, (jj + 1) * tm)
            for g in range(groups):
                cols = slice(g * gc, (g + 1) * gc)
                ab_lo = jnp.dot(ulo_ref[rows, cols], w_ref[g], preferred_element_type=F32)
                ab_pr = jnp.dot(pr[:, cols], w_ref[g], preferred_element_type=F32)
                a_s[rows, cols] = (ab_lo[:, :gc] + ab_pr[:, :gc]).astype(BF16)
                b_s[rows, cols] = (ab_lo[:, gc:] - ab_pr[:, gc:]).astype(BF16)
        for g in range(groups):
            cols = slice(g * gc, (g + 1) * gc)
            am_s[:, cols] = jnp.dot(uup_ref[0:am_s.shape[0], cols], w_ref[g, :, 0:gc], preferred_element_type=F32)
        mid = jnp.dot(alt_ref[...], a_s[...], preferred_element_type=F32)
        carry[...] = mid + am_s[0:carry.shape[0], :] + b_ref[...]

    a_mid = jnp.where((row & 1) == 0, am_s[0:1, :], -am_s[0:1, :])
    p = jnp.dot(cos_ref[...], a_s[...], preferred_element_type=F32) + a_mid + b_ref[...]
    q = jnp.dot(sin_ref[...], b_s[...], preferred_element_type=F32)
    lo_ref[...] = (p - q).astype(BF16)
    hi = (p + q).astype(BF16)
    rev = jnp.dot(s1_ref[...], hi, preferred_element_type=F32)
    up_ref[...] = jnp.where(row == 0, carry[0:1, :], rev).astype(BF16)
    carry[0:1, :] = hi[0:1, :].astype(F32)


def _dft_half_tables(n):
    half = n // 2
    k = np.arange(half, dtype=np.int64)
    ang = ((k[:, None] * k[None, :]) % n).astype(np.float64) * (2.0 * np.pi / n)
    alt = np.broadcast_to(np.where((k & 1) == 0, 1.0, -1.0), (8, half))
    return tuple(jnp.asarray(a, F32).astype(BF16) for a in (np.cos(ang), np.sin(ang), alt))


def _fourier(uf, w_cs, bias, tables, *, b, tm):
    nr, fw = uf.shape
    seq = nr // b
    half = seq // 2
    groups = w_cs.shape[0]
    cos_t, sin_t, alt = tables
    tm = min(tm, half)
    nt = half // tm
    r = np.arange(1, tm)
    s1 = np.zeros((tm, tm), np.float32)
    s1[r, tm - r] = 1.0
    s1 = jnp.asarray(s1, BF16)
    lo, up = pl.pallas_call(
        functools.partial(_fourier_kernel, groups=groups),
        grid=(b, nt),
        in_specs=[
            pl.BlockSpec((half, fw), lambda bi, i: (2 * bi, 0)),
            pl.BlockSpec((half, fw), lambda bi, i: (2 * bi + 1, 0)),
            pl.BlockSpec(w_cs.shape, lambda bi, i: (0, 0, 0)),
            pl.BlockSpec((tm, half), lambda bi, i: (nt - 1 - i, 0)),
            pl.BlockSpec((tm, half), lambda bi, i: (nt - 1 - i, 0)),
            pl.BlockSpec(alt.shape, lambda bi, i: (0, 0)),
            pl.BlockSpec((tm, tm), lambda bi, i: (0, 0)),
            pl.BlockSpec((1, fw), lambda bi, i: (0, 0)),
        ],
        out_specs=[pl.BlockSpec((tm, fw), lambda bi, i: (bi * nt + nt - 1 - i, 0)),
                   pl.BlockSpec((tm, fw), lambda bi, i: (bi * nt + i, 0))],
        out_shape=[jax.ShapeDtypeStruct((b * half, fw), BF16), jax.ShapeDtypeStruct((b * half, fw), BF16)],
        scratch_shapes=[pltpu.VMEM((half, fw), BF16), pltpu.VMEM((half, fw), BF16),
                        pltpu.VMEM((16, fw), F32), pltpu.VMEM((alt.shape[0], fw), F32)],
        compiler_params=_cparams(("arbitrary", "arbitrary")),
        name="fourier_%d" % seq,
    )(uf, uf, w_cs, cos_t, sin_t, alt, s1, bias)
    return lo, up


def _fourier_weights(w_four, seq):
    groups, gc, _ = w_four.shape
    k = np.arange(gc)
    ang = ((k[:, None] * k[None, :]) % gc).astype(np.float64) * (2.0 * np.pi / gc)
    norm = 1.0 / math.sqrt(seq * gc)
    cc = jnp.asarray(np.cos(ang) * norm, F32)
    sc = jnp.asarray(np.sin(ang) * norm, F32)
    hp = lax.Precision.HIGHEST
    wc = jnp.einsum('ck,gkd->gcd', cc, w_four.astype(F32), precision=hp)
    ws = jnp.einsum('ck,gkd->gcd', sc, w_four.astype(F32), precision=hp)
    return jnp.concatenate([wc, ws], axis=-1).astype(BF16)


def _outproj_kernel(att_ref, us_ref, yc_ref, x_ref, mod_ref, wo_ref, wg_ref, bg_ref, dsk_ref,
                    gpost_ref, gpre_ref, *refs, aw, sw, four_part):
    uf_refs, (xo_ref, h2_ref) = refs[:-2], refs[-2:]
    part = four_part(pl.program_id(0))
    nblk = us_ref.shape[0]
    tm = x_ref.shape[0]
    gate_g = mod_ref[0, 2:3, :] * gpost_ref[...]
    scale_g = gpre_ref[...] * (1.0 + mod_ref[0, 4:5, :])
    for r0 in range(0, tm, tm // 2):
        rows = slice(r0, r0 + tm // 2)
        us = jnp.concatenate([us_ref[j, rows, :] for j in range(nblk)], axis=1)
        yc = jnp.concatenate([yc_ref[j, rows, :] for j in range(nblk)], axis=1)
        g = jax.nn.gelu(dsk_ref[...] * us + yc)
        z = jnp.dot(g.astype(BF16), wg_ref[...], preferred_element_type=F32) + bg_ref[...]
        ssm = (g * jax.nn.sigmoid(z)).astype(BF16)
        uf = uf_refs[0][rows, :]
        for pi in range(1, len(uf_refs)):
            uf = jnp.where(part == pi, uf_refs[pi][rows, :], uf)
        mix = (jnp.dot(att_ref[rows, :], wo_ref[0:aw, :], preferred_element_type=F32)
               + jnp.dot(ssm, wo_ref[aw:aw + sw, :], preferred_element_type=F32)
               + jnp.dot(uf, wo_ref[aw + sw:, :], preferred_element_type=F32))
        xn = x_ref[rows, :] + _rms(mix, NORM_EPS) * gate_g
        xo_ref[rows, :] = xn
        h2_ref[rows, :] = (_rms(xn, NORM_EPS) * scale_g + mod_ref[0, 3:4, :]).astype(BF16)


def _outproj(att, us, yc, four_parts, x, mod, w_out, w_glu, b_glu, dsk, g_post, g_pre, *, layer, mod_row, tm,
             four_tile=None):
    n_rows, d = x.shape
    aw, fw = att.shape[1], four_parts[0].shape[1]
    if four_tile is None:
        four_tile = lambda i: (0, i)
    nblk = us.shape[0]
    sw = nblk * LANES

    def row(i):
        return (i, 0)

    def const(i):
        return (0, 0)

    blk3 = pl.BlockSpec((nblk, tm, LANES), lambda i: (0, i, 0))
    def part_spec(pi):
        nblocks = four_parts[pi].shape[0] // tm
        return pl.BlockSpec((tm, fw), lambda i: (jnp.where(four_tile(i)[0] == pi,
                                                           jnp.clip(four_tile(i)[1], 0, nblocks - 1), 0), 0))

    return pl.pallas_call(
        functools.partial(_outproj_kernel, aw=aw, sw=sw, four_part=lambda i: four_tile(i)[0]),
        grid=(n_rows // tm,),
        in_specs=[
            pl.BlockSpec((tm, aw), row), blk3, blk3,
            pl.BlockSpec((tm, d), row),
            pl.BlockSpec((1, N_MOD, d), lambda i: (mod_row(i), 0, 0)),
            _resident(w_out.shape), _resident_layer(w_glu.shape, layer),
            pl.BlockSpec((1, sw), const), pl.BlockSpec((1, sw), const),
            pl.BlockSpec((1, d), const), pl.BlockSpec((1, d), const),
        ] + [part_spec(pi) for pi in range(len(four_parts))],
        out_specs=[pl.BlockSpec((tm, d), row), pl.BlockSpec((tm, d), row)],
        out_shape=[jax.ShapeDtypeStruct((n_rows, d), F32), jax.ShapeDtypeStruct((n_rows, d), BF16)],
        compiler_params=_cparams(("arbitrary",)),
        name="outproj",
    )(att, us, yc, x, mod, w_out, w_glu, b_glu, dsk, g_post, g_pre, *four_parts)


def _ffn_up_kernel(h_ref, wg_ref, wu_ref, t_ref):
    h = h_ref[...]
    tf = wg_ref.shape[1]
    for c0 in range(0, tf, tf // 2):
        cols = slice(c0, c0 + tf // 2)
        a = jnp.dot(h, wg_ref[:, cols], preferred_element_type=F32)
        u = jnp.dot(h, wu_ref[:, cols], preferred_element_type=F32)
        t_ref[:, cols] = (_silu(a) * u).astype(BF16)


def _ffn_down_kernel(t_ref, x_ref, mod_ref, g_ref, wd_ref, o_ref):
    f = jnp.dot(t_ref[...], wd_ref[...], preferred_element_type=F32)
    o_ref[...] = x_ref[...] + _rms(f, NORM_EPS) * (mod_ref[0, 5:6, :] * g_ref[...])


def _ffn(h2, x_mid, mod, g_post, w_gate, w_up, w_down, *, mod_row, tm, tf):
    n_rows, d = h2.shape
    dff = w_gate.shape[1]
    hidden = pl.pallas_call(
        _ffn_up_kernel,
        grid=(n_rows // tm, dff // tf),
        in_specs=[
            pl.BlockSpec((tm, d), lambda i, k: (i, 0)),
            pl.BlockSpec((d, tf), lambda i, k: (0, k)),
            pl.BlockSpec((d, tf), lambda i, k: (0, k)),
        ],
        out_specs=pl.BlockSpec((tm, tf), lambda i, k: (i, k)),
        out_shape=jax.ShapeDtypeStruct((n_rows, dff), BF16),
        compiler_params=_cparams(("arbitrary", "arbitrary")),
        name="ffn_up",
    )(h2, w_gate, w_up)
    tm2 = tm // 2
    return pl.pallas_call(
        _ffn_down_kernel,
        grid=(n_rows // tm2,),
        in_specs=[
            pl.BlockSpec((tm2, dff), lambda i: (i, 0)),
            pl.BlockSpec((tm2, d), lambda i: (i, 0)),
            pl.BlockSpec((1, N_MOD, d), lambda i: (mod_row(i // 2), 0, 0)),
            pl.BlockSpec((1, d), lambda i: (0, 0)),
            _resident(w_down.shape),
        ],
        out_specs=pl.BlockSpec((tm2, d), lambda i: (i, 0)),
        out_shape=jax.ShapeDtypeStruct((n_rows, d), F32),
        compiler_params=_cparams(("arbitrary",)),
        name="ffn_down",
    )(hidden, x_mid, mod, g_post, w_down)


def kernel(x, c, ctx, c_ctx, w_mod, b_mod, g_mix_pre, g_mix_post, g_ffn_pre, g_ffn_post, w_in, w_out, lam_q1, lam_k1, lam_q2, lam_k2, g_subln, ssm_a_re, ssm_a_im, ssm_log_dt, ssm_b_re, ssm_b_im, ssm_c_re, ssm_c_im, ssm_d, w_glu, b_glu, w_four, b_four, w_gate, w_up, w_down):
    b, t, d = x.shape
    n_ctx = ctx.shape[1]
    depth = w_mod.shape[0]
    nl, nc = b * t, b * n_ctx
    aw = d // 2
    sw = ssm_d.shape[1]
    fw = d - aw - sw
    heads = aw // (2 * DA_HEAD_DIM)
    tm = 512
    assert t % tm == 0 and nc % tm == 0 and n_ctx % SSM_CHUNK == 0 and t % GRID_W == 0 and b + 1 <= MOD_ROWS
    assert sw % LANES == 0 and (sw // SSM_GROUP) % (2 * OCT) == 0

    cs = jnp.concatenate([c, c_ctx[None, :], jnp.zeros((MOD_ROWS - b - 1, d), F32)], axis=0)
    mod_all = _modulation(cs, w_mod, b_mod).reshape(depth, MOD_ROWS, N_MOD, d)

    per_b = t // tm
    lat_mod = lambda i: i // per_b
    ctx_mod = lambda i: b
    tabs = _rope_tables(t, tm)
    dft_lat = _dft_half_tables(t)
    dft_ctx = _dft_half_tables(n_ctx)
    x_lat, x_ctx = x.reshape(nl, d), ctx.reshape(nc, d)
    w_glu_b = w_glu.astype(BF16)
    w_in_b = w_in[0].astype(BF16)
    ssm_w = jax.vmap(_ssm_weights)(ssm_a_re, ssm_a_im, ssm_log_dt, ssm_b_re, ssm_b_im, ssm_c_re, ssm_c_im)

    for l in range(depth):
        need_ctx = l < depth - 1
        lam_init = 0.8 - 0.6 * math.exp(-0.3 * l)
        mod = mod_all[l]
        g_pre = g_mix_pre[l][None, :]
        casts = [(w_out, l)] + ([(w_in, l + 1)] if l + 1 < depth else [])
        q, k, vt, us, uf, w_out_b, *w_in_next = _inproj(x_lat, mod, g_pre, w_in_b, tabs, mod_row=lat_mod,
                                                        tab_blk=lambda i: i % per_b, aw=aw, sw=sw, fw=fw, tm=tm,
                                                        cast=casts)
        qc, kc, vtc, usc, ufc = _inproj(x_ctx, mod, g_pre, w_in_b, tabs, mod_row=ctx_mod,
                                        tab_blk=lambda i: per_b, aw=aw, sw=sw, fw=fw, tm=tm)
        w_in_b = w_in_next[0] if w_in_next else None

        lam4 = jnp.stack([lam_q1[l], lam_k1[l], lam_q2[l], lam_k2[l]]).astype(F32)
        gs = g_subln[l][None, :].astype(F32)
        att, *ffn_wb = _attention(lam4, gs, q, [(kc, vtc), (k, vt)], lam_init=lam_init, b=b, heads=heads, tq=1024,
                                  name="attn_latent", cast=(l, w_gate, w_up, w_down))
        yc, ycc = _ssm_conv(us, usc, ssm_w, layer=l, b=b)
        bias = b_four[l].reshape(1, fw).astype(F32)
        four = _fourier(uf, _fourier_weights(w_four[l], t), bias, dft_lat, b=b, tm=tm)

        small = (b_glu[l][None, :].astype(F32), ssm_d[l][None, :].astype(F32), g_mix_post[l][None, :],
                 g_ffn_pre[l][None, :])
        ffn_w = (g_ffn_post[l][None, :], *ffn_wb)
        half_t = per_b // 2

        def four_tile(i):
            pos = i % per_b
            return pos // half_t, (i // per_b) * half_t + pos % half_t

        x_mid, h2 = _outproj(att, us, yc, four, x_lat, mod, w_out_b, w_glu_b, *small, layer=l, mod_row=lat_mod,
                             tm=tm, four_tile=four_tile)
        x_lat = _ffn(h2, x_mid, mod, *ffn_w, mod_row=lat_mod, tm=tm, tf=512)

        if need_ctx:
            att_c, = _attention(lam4, gs, qc, [(kc, vtc)], lam_init=lam_init, b=b, heads=heads, tq=n_ctx,
                                name="attn_ctx")
            lo_c, up_c = _fourier(ufc, _fourier_weights(w_four[l], n_ctx), bias, dft_ctx, b=b, tm=tm)
            four_c = [jnp.concatenate([lo_c.reshape(b, -1, fw), up_c.reshape(b, -1, fw)], axis=1).reshape(nc, fw)]
            xc_mid, h2c = _outproj(att_c, usc, ycc, four_c, x_ctx, mod, w_out_b, w_glu_b, *small, layer=l,
                                   mod_row=ctx_mod, tm=tm)
            x_ctx = _ffn(h2c, xc_mid, mod, *ffn_w, mod_row=ctx_mod, tm=tm, tf=512)
    return x_lat.reshape(b, t, d)
```

```python
import functools
import math

import jax
import jax.numpy as jnp
import numpy as np
from jax import lax
from jax.experimental import pallas as pl
from jax.experimental.pallas import tpu as pltpu

F32 = jnp.float32
BF16 = jnp.bfloat16

LANES = 128
GRID_W = 64
DA_HEAD_DIM = 128
SSM_GROUP = 16
SSM_STATE = 64
N_MOD = 6
ROPE_BASE = 10000.0
ROPE_PAIRS = DA_HEAD_DIM // 4
NORM_EPS = 1e-6
SUBLN_EPS = 1e-5

SSM_CHUNK = 16
SSM_CW = SSM_CHUNK * SSM_GROUP
OCT = LANES // SSM_GROUP
MOD_ROWS = 8
VMEM_LIMIT = 56 * 1024 * 1024
NT_DIMS = (((1,), (1,)), ((), ()))


def _cparams(sem):
    return pltpu.CompilerParams(dimension_semantics=sem, vmem_limit_bytes=VMEM_LIMIT)


def _rms(x, eps):
    return x * lax.rsqrt(jnp.mean(x * x, axis=-1, keepdims=True) + eps)


def _silu(x):
    return x * jax.nn.sigmoid(x)


def _resident(shape):
    nd = len(shape)
    return pl.BlockSpec(shape, lambda *_: (0,) * nd, pipeline_mode=pl.Buffered(1))


def _resident_layer(shape, layer):
    nd = len(shape)
    return pl.BlockSpec((None,) + tuple(shape[1:]), lambda *_: (layer,) + (0,) * (nd - 1),
                        pipeline_mode=pl.Buffered(1))


def _mod_kernel(c_ref, w_ref, b_ref, o_ref):
    s = _silu(c_ref[...]).astype(BF16)
    o_ref[0] = jnp.dot(s, w_ref[0].astype(BF16), preferred_element_type=F32) + b_ref[0]


def _modulation(cs, w_mod, b_mod, tn=1024):
    depth, d, n = w_mod.shape
    return pl.pallas_call(
        _mod_kernel,
        grid=(depth, n // tn),
        in_specs=[
            pl.BlockSpec((MOD_ROWS, d), lambda l, j: (0, 0)),
            pl.BlockSpec((1, d, tn), lambda l, j: (l, 0, j)),
            pl.BlockSpec((1, 1, tn), lambda l, j: (l, 0, j)),
        ],
        out_specs=pl.BlockSpec((1, MOD_ROWS, tn), lambda l, j: (l, 0, j)),
        out_shape=jax.ShapeDtypeStruct((depth, MOD_ROWS, n), F32),
        compiler_params=_cparams(("arbitrary", "arbitrary")),
        name="modulation",
    )(cs, w_mod, b_mod.reshape(depth, 1, n))


def _inproj_kernel(x_ref, mod_ref, g_ref, w_ref, cq_ref, sq_ref, ck_ref, sk_ref, *refs, aw, sw):
    n_cast = (len(refs) - 5) // 2
    q_ref, k_ref, vt_ref, us_ref, uf_ref = refs[n_cast:n_cast + 5]
    for src, dst in zip(refs[:n_cast], refs[n_cast + 5:]):
        dst[...] = src[...].astype(BF16)
    tm = x_ref.shape[0]
    hb = (_rms(x_ref[...], NORM_EPS) * (g_ref[...] * (1.0 + mod_ref[0, 1:2, :])) + mod_ref[0, 0:1, :]).astype(BF16)

    lane = lax.broadcasted_iota(jnp.int32, (tm, DA_HEAD_DIM), 1)
    first_half = (lane & (2 * ROPE_PAIRS - 1)) < ROPE_PAIRS

    def rope(z, c, s):
        partner = jnp.where(first_half, pltpu.roll(z, DA_HEAD_DIM - ROPE_PAIRS, 1), pltpu.roll(z, ROPE_PAIRS, 1))
        return z * c + partner * s

    zq = jnp.dot(hb, w_ref[:, 0:aw], preferred_element_type=F32)
    cq, sq = cq_ref[...], sq_ref[...]
    for j in range(aw // DA_HEAD_DIM):
        sl = slice(j * DA_HEAD_DIM, (j + 1) * DA_HEAD_DIM)
        q_ref[:, sl] = rope(zq[:, sl], cq, sq).astype(BF16)
    zk = jnp.dot(hb, w_ref[:, aw:2 * aw], preferred_element_type=F32)
    ck, sk = ck_ref[...], sk_ref[...]
    for j in range(aw // DA_HEAD_DIM):
        sl = slice(j * DA_HEAD_DIM, (j + 1) * DA_HEAD_DIM)
        k_ref[:, sl] = rope(zk[:, sl], ck, sk).astype(BF16)
    zv = jnp.dot(hb, w_ref[:, 2 * aw:3 * aw], preferred_element_type=F32)
    vt_ref[...] = jnp.transpose(zv.astype(BF16))
    zs = jnp.dot(hb, w_ref[:, 3 * aw:3 * aw + sw], preferred_element_type=F32)
    for j in range(sw // LANES):
        us_ref[j] = zs[:, j * LANES:(j + 1) * LANES]
    uf_ref[...] = jnp.dot(hb, w_ref[:, 3 * aw + sw:], preferred_element_type=F32).astype(BF16)


def _inproj(x, mod, g, w_in, tabs, *, mod_row, tab_blk, aw, sw, fw, tm, cast=()):
    nr, d = x.shape
    n_steps = nr // tm

    def row(i):
        return (i, 0)

    tab_spec = pl.BlockSpec((tm, DA_HEAD_DIM), lambda i: (tab_blk(i), 0))
    cast_in, cast_out, cast_shape = [], [], []
    for w, layer in cast:
        _, r, c = w.shape
        per = next(dv for dv in range(1, c // LANES + 1) if (c // LANES) % dv == 0 and dv * n_steps >= c // LANES)
        cw = per * LANES
        cast_in.append(pl.BlockSpec((None, r, cw), lambda i, layer=layer, nb=c // cw: (layer, 0, jnp.minimum(i, nb - 1))))
        cast_out.append(pl.BlockSpec((r, cw), lambda i, nb=c // cw: (0, jnp.minimum(i, nb - 1))))
        cast_shape.append(jax.ShapeDtypeStruct((r, c), BF16))
    return pl.pallas_call(
        functools.partial(_inproj_kernel, aw=aw, sw=sw),
        grid=(n_steps,),
        in_specs=[
            pl.BlockSpec((tm, d), row),
            pl.BlockSpec((1, N_MOD, d), lambda i: (mod_row(i), 0, 0)),
            pl.BlockSpec((1, d), lambda i: (0, 0)),
            _resident(w_in.shape),
            tab_spec, tab_spec, tab_spec, tab_spec,
        ] + cast_in,
        out_specs=[
            pl.BlockSpec((tm, aw), row), pl.BlockSpec((tm, aw), row), pl.BlockSpec((aw, tm), lambda i: (0, i)),
            pl.BlockSpec((sw // LANES, tm, LANES), lambda i: (0, i, 0)), pl.BlockSpec((tm, fw), row),
        ] + cast_out,
        out_shape=[
            jax.ShapeDtypeStruct((nr, aw), BF16), jax.ShapeDtypeStruct((nr, aw), BF16),
            jax.ShapeDtypeStruct((aw, nr), BF16), jax.ShapeDtypeStruct((sw // LANES, nr, LANES), F32),
            jax.ShapeDtypeStruct((nr, fw), BF16),
        ] + cast_shape,
        compiler_params=_cparams(("arbitrary",)),
        name="inproj",
    )(x, mod, g, w_in, *tabs, *[w for w, _ in cast])


def _rope_tables(t, tm):
    rows = t // GRID_W
    r = np.repeat(np.arange(rows, dtype=np.float64), GRID_W)
    col = np.tile(np.arange(GRID_W, dtype=np.float64), rows)
    inv = ROPE_BASE ** (-np.arange(ROPE_PAIRS, dtype=np.float64) / ROPE_PAIRS)
    ar, ac = r[:, None] * inv, col[:, None] * inv
    cos = np.concatenate([np.cos(ar), np.cos(ar), np.cos(ac), np.cos(ac)], axis=1)
    sin = np.concatenate([-np.sin(ar), np.sin(ar), -np.sin(ac), np.sin(ac)], axis=1)
    cos = np.concatenate([cos, np.ones((tm, DA_HEAD_DIM))], axis=0)
    sin = np.concatenate([sin, np.zeros((tm, DA_HEAD_DIM))], axis=0)
    scale = DA_HEAD_DIM ** -0.5 * math.log2(math.e)
    return tuple(jnp.asarray(a, F32) for a in (cos * scale, sin * scale, cos, sin))


def _attn_kernel(lam_ref, gs_ref, q_ref, *refs, lam_init, n_sets, n_cast):
    k_refs, vt_refs = refs[0:2 * n_sets:2], refs[1:2 * n_sets:2]
    cast_in = refs[2 * n_sets:2 * n_sets + n_cast]
    o_ref = refs[2 * n_sets + n_cast]
    cast_out = refs[2 * n_sets + n_cast + 1:]
    for src, dst in zip(cast_in, cast_out):
        dst[...] = src[...].astype(BF16)

    hd = DA_HEAD_DIM
    s1 = jnp.sum(lam_ref[0:1, :] * lam_ref[1:2, :], axis=-1, keepdims=True)
    s2 = jnp.sum(lam_ref[2:3, :] * lam_ref[3:4, :], axis=-1, keepdims=True)
    lam = jnp.exp(s1) - jnp.exp(s2) + lam_init
    outs = []
    for idx in range(2):
        qi = q_ref[:, idx * hd:(idx + 1) * hd]
        ss = [lax.dot_general(kr[:, idx * hd:(idx + 1) * hd], qi, NT_DIMS, preferred_element_type=F32)
              for kr in k_refs]
        m = functools.reduce(jnp.maximum, [jnp.max(s, axis=0, keepdims=True) for s in ss])
        ps = [jnp.exp2(s - m) for s in ss]
        l = functools.reduce(jnp.add, [jnp.sum(p, axis=0, keepdims=True) for p in ps])
        acc = functools.reduce(jnp.add, [jnp.dot(vr[...], p.astype(BF16), preferred_element_type=F32)
                                         for vr, p in zip(vt_refs, ps)])
        outs.append(acc * (1.0 / l))
    o = jnp.transpose(outs[0] - outs[1] * lam)
    o_ref[...] = (_rms(o, SUBLN_EPS) * gs_ref[...] * (1.0 - lam_init)).astype(BF16)


def _attention(lam4, gs, q, kv_sets, *, lam_init, b, heads, tq, name, cast=None):
    nq, aw = q.shape
    vw = 2 * DA_HEAD_DIM
    per_b = nq // b // tq
    q_spec = pl.BlockSpec((tq, vw), lambda bi, h, i: (bi * per_b + i, h))
    in_specs = [pl.BlockSpec((4, DA_HEAD_DIM), lambda bi, h, i: (0, 0)),
                pl.BlockSpec((1, vw), lambda bi, h, i: (0, 0)),
                q_spec]
    args = [lam4, gs, q]
    for k, vt in kv_sets:
        sk = k.shape[0] // b
        in_specs += [pl.BlockSpec((sk, vw), lambda bi, h, i: (bi, h)),
                     pl.BlockSpec((vw, sk), lambda bi, h, i: (h, bi))]
        args += [k, vt]
    out_specs = [q_spec]
    out_shape = [jax.ShapeDtypeStruct((nq, aw), BF16)]
    n_cast = 0
    if cast is not None:
        layer, w_gate, w_up, w_down = cast
        _, d, dff = w_gate.shape
        n_steps = b * heads * per_b
        per = next(dv for dv in range(1, dff // LANES + 1) if (dff // LANES) % dv == 0 and dv * n_steps >= dff // LANES)
        cw = per * LANES
        nblk = dff // cw

        def blk(bi, h, i):
            return jnp.minimum((bi * heads + h) * per_b + i, nblk - 1)

        in_specs += [pl.BlockSpec((None, d, cw), lambda bi, h, i: (layer, 0, blk(bi, h, i))),
                     pl.BlockSpec((None, d, cw), lambda bi, h, i: (layer, 0, blk(bi, h, i))),
                     pl.BlockSpec((None, cw, d), lambda bi, h, i: (layer, blk(bi, h, i), 0))]
        args += [w_gate, w_up, w_down]
        out_specs += [pl.BlockSpec((d, cw), lambda bi, h, i: (0, blk(bi, h, i))),
                      pl.BlockSpec((d, cw), lambda bi, h, i: (0, blk(bi, h, i))),
                      pl.BlockSpec((cw, d), lambda bi, h, i: (blk(bi, h, i), 0))]
        out_shape += [jax.ShapeDtypeStruct((d, dff), BF16), jax.ShapeDtypeStruct((d, dff), BF16),
                      jax.ShapeDtypeStruct((dff, d), BF16)]
        n_cast = 3
    return pl.pallas_call(
        functools.partial(_attn_kernel, lam_init=lam_init, n_sets=len(kv_sets), n_cast=n_cast),
        grid=(b, heads, per_b),
        in_specs=in_specs,
        out_specs=out_specs,
        out_shape=out_shape,
        compiler_params=_cparams(("arbitrary", "arbitrary", "arbitrary")),
        name=name,
    )(*args)


def _ssm_weights(a_re, a_im, log_dt, b_re, b_im, c_re, c_im):
    L, H, P = SSM_CHUNK, SSM_GROUP, SSM_STATE
    g = a_re.shape[1]
    npair = g // 2
    hp = lax.Precision.HIGHEST
    ar, ai = a_re.astype(F32), a_im.astype(F32)
    dt = jnp.exp(log_dt.astype(F32))[..., None]
    n = jnp.arange(L + 1, dtype=F32)
    mag = jnp.exp((ar * dt)[..., None] * n)
    ang = (ai * dt)[..., None] * n
    apr, api = mag * jnp.cos(ang), mag * jnp.sin(ang)
    xr, xi = apr[..., 1] - 1.0, api[..., 1]
    den = ar * ar + ai * ai
    qr, qi = (xr * ar + xi * ai) / den, (xi * ar - xr * ai) / den
    br, bi = b_re.astype(F32), b_im.astype(F32)
    bbr = qr[..., None] * br - qi[..., None] * bi
    bbi = qr[..., None] * bi + qi[..., None] * br
    cr, ci = c_re.astype(F32), c_im.astype(F32)

    car = cr[..., None] * apr[:, :, None, :, :L] - ci[..., None] * api[:, :, None, :, :L]
    cai = cr[..., None] * api[:, :, None, :, :L] + ci[..., None] * apr[:, :, None, :, :L]
    kt = (jnp.einsum('dghpt,dgpk->dgkth', car, bbr, precision=hp)
          - jnp.einsum('dghpt,dgpk->dgkth', cai, bbi, precision=hp))
    kc = jnp.concatenate([jnp.flip(kt[1][:, :, 1:], axis=2), kt[0][:, :, :1] + kt[1][:, :, :1], kt[0][:, :, 1:]],
                         axis=2).reshape(g, H, (2 * L - 1) * H)
    w_intra = jnp.pad(kc, ((0, 0), (0, 0), (0, (-kc.shape[2]) % LANES)))

    sir = jnp.stack([jnp.flip(apr[0, ..., :L], axis=-1), apr[1, ..., :L]])
    sii = jnp.stack([jnp.flip(api[0, ..., :L], axis=-1), api[1, ..., :L]])
    sir, sii = jnp.swapaxes(sir, 2, 3)[:, :, :, None, :], jnp.swapaxes(sii, 2, 3)[:, :, :, None, :]
    tbr, tbi = jnp.swapaxes(bbr, 2, 3)[:, :, None], jnp.swapaxes(bbi, 2, 3)[:, :, None]
    w_sin = jnp.concatenate([sir * tbr - sii * tbi, sir * tbi + sii * tbr], axis=-1)
    w_sin = w_sin.reshape(2, g, L * H, 2 * P)

    sor = jnp.stack([apr[0, ..., 1:], jnp.flip(apr[1, ..., 1:], axis=-1)])
    soi = jnp.stack([api[0, ..., 1:], jnp.flip(api[1, ..., 1:], axis=-1)])
    tcr, tci = jnp.swapaxes(cr, 2, 3)[:, :, :, None, :], jnp.swapaxes(ci, 2, 3)[:, :, :, None, :]
    cnr = (tcr * sor[..., None] - tci * soi[..., None]).reshape(2, g, P, L * H)
    cni = (tcr * soi[..., None] + tci * sor[..., None]).reshape(2, g, P, L * H)
    w_so = jnp.concatenate([cnr, -cni], axis=2)

    al = jnp.stack([apr[..., L].reshape(2, npair, 2 * P), api[..., L].reshape(2, npair, 2 * P)], axis=2)
    al = al.reshape(2, 2 * npair, 1, 2 * P)
    return w_intra, w_sin.astype(BF16), w_so.astype(BF16), al


def _atom_transpose(vs):
    atom = lax.broadcasted_iota(jnp.int32, vs[0].shape, 1) // SSM_GROUP
    cur = list(vs)
    d = OCT // 2
    while d:
        hi = (atom & d) != 0
        nxt = list(cur)
        for i in range(OCT):
            if i & d:
                continue
            a, b = cur[i], cur[i + d]
            nxt[i] = jnp.where(hi, pltpu.roll(b, d * SSM_GROUP, 1), a)
            nxt[i + d] = jnp.where(hi, b, pltpu.roll(a, LANES - d * SSM_GROUP, 1))
        cur = nxt
        d //= 2
    return cur


def _atom_transpose_matrix():
    n = OCT * OCT * SSM_GROUP
    i = np.arange(n)
    j = ((i // SSM_GROUP) % OCT) * LANES + (i // LANES) * SSM_GROUP + i % SSM_GROUP
    p = np.zeros((n, n), np.float32)
    p[i, j] = 1.0
    return jnp.asarray(p, BF16)


def _ssm_in_kernel(us_ref, p_ref, w_ref, ut_ref, s_ref):
    noct, nb = us_ref.shape[0], us_ref.shape[1]
    nc = us_ref.shape[2] // SSM_CHUNK
    n = nb * nc
    half = SSM_CHUNK // 2
    src = []
    for o in range(noct):
        for hf in range(2):
            per_seq = [jnp.concatenate([us_ref[o, bi, pl.ds(half * hf + sl, nc, stride=SSM_CHUNK), :]
                                        for sl in range(half)], axis=1) for bi in range(nb)]
            src.append(jnp.concatenate(per_seq, axis=0).astype(BF16))
    perm = jnp.dot(jnp.concatenate(src, axis=0), p_ref[...], preferred_element_type=F32).astype(BF16)

    def group_chunk(gi):
        o, gl = gi // OCT, gi % OCT
        return jnp.concatenate([perm[(2 * o + hf) * n:(2 * o + hf + 1) * n, gl * LANES:(gl + 1) * LANES]
                                for hf in range(2)], axis=1)

    def store_chunk_major(d, blk, val):
        for bi in range(nb):
            s_ref[d, blk, pl.ds(bi, nc, stride=nb), :] = val[bi * nc:(bi + 1) * nc]

    low = lax.broadcasted_iota(jnp.int32, (n, LANES), 1) < SSM_STATE
    for j in range(noct * OCT // 2):
        ug = [group_chunk(2 * j), group_chunk(2 * j + 1)]
        ut_ref[2 * j] = ug[0]
        ut_ref[2 * j + 1] = ug[1]
        for d in range(2):
            s0 = jnp.dot(ug[0], w_ref[d, 2 * j], preferred_element_type=F32)
            s1 = jnp.dot(ug[1], w_ref[d, 2 * j + 1], preferred_element_type=F32)
            store_chunk_major(d, 2 * j, jnp.where(low, s0, pltpu.roll(s1, SSM_STATE, 1)))
            store_chunk_major(d, 2 * j + 1, jnp.where(low, pltpu.roll(s0, SSM_STATE, 1), s1))


def _ssm_scan_kernel(sc_ref, sl_ref, al_ref, hc_ref, hl_ref, *, b):
    d = pl.program_id(0)
    nblk = sc_ref.shape[1]
    tile = 8
    cpt = tile // b
    grp = lax.broadcasted_iota(jnp.int32, (tile, LANES), 0) // b
    ars = [al_ref[0, 2 * k] for k in range(nblk // 2)]
    ais = [al_ref[0, 2 * k + 1] for k in range(nblk // 2)]

    def spread(x, g):
        x = jnp.where(grp == g, x, 0.0)
        out = x
        for r in range(1, cpt):
            out = out + pltpu.roll(x, r * b, 0)
        return out

    def phase(s_ref, h_ref, carry):
        ntile = s_ref.shape[2] // tile

        def body(i, carry):
            t = jnp.where(d == 0, i, ntile - 1 - i)
            rows = pl.ds(pl.multiple_of(t * tile, tile), tile)
            new = []
            for k in range(nblk // 2):
                hr, hi = carry[2 * k], carry[2 * k + 1]
                sr, si = s_ref[0, 2 * k, rows, :], s_ref[0, 2 * k + 1, rows, :]
                hin_r, hin_i = hr, hi
                for step in range(cpt):
                    g = jnp.where(d == 0, step, cpt - 1 - step)
                    hin_r = jnp.where(grp == g, hr, hin_r)
                    hin_i = jnp.where(grp == g, hi, hin_i)
                    sgr, sgi = spread(sr, g), spread(si, g)
                    hr, hi = ars[k] * hr - ais[k] * hi + sgr, ars[k] * hi + ais[k] * hr + sgi
                h_ref[0, 2 * k, rows, :] = hin_r
                h_ref[0, 2 * k + 1, rows, :] = hin_i
                new += [hr, hi]
            return tuple(new)

        return lax.fori_loop(0, ntile, body, carry, unroll=2)

    carry = tuple(jnp.zeros((tile, LANES), F32) for _ in range(nblk))
    carry = phase(sc_ref, hc_ref, carry)
    phase(sl_ref, hl_ref, carry)


def _ssm_out_kernel(ut_ref, h_ref, kc_ref, wo_ref, y_ref, wi_ref):
    n = ut_ref.shape[1]
    ng = ut_ref.shape[0]
    nb = y_ref.shape[1]
    nc = n // nb
    noct = ng // OCT
    half = SSM_CHUNK // 2

    @pl.when(pl.program_id(0) == 0)
    def _():
        for gi in range(ng):
            kc = kc_ref[gi]
            wi_ref[gi] = jnp.concatenate(
                [kc[:, (SSM_CHUNK - 1 - s) * SSM_GROUP:(SSM_CHUNK - 1 - s) * SSM_GROUP + SSM_CW]
                 for s in range(SSM_CHUNK)], axis=0).astype(BF16)

    def load_seq_major(d, blk):
        return jnp.concatenate([h_ref[d, blk, pl.ds(bi, nc, stride=nb), :] for bi in range(nb)], axis=0)

    low = lax.broadcasted_iota(jnp.int32, (n, LANES), 1) < SSM_STATE
    ys = []
    for j in range(ng // 2):
        hg = [[], []]
        for d in range(2):
            hr, hi = load_seq_major(d, 2 * j), load_seq_major(d, 2 * j + 1)
            hg[0].append(jnp.where(low, hr, pltpu.roll(hi, SSM_STATE, 1)).astype(BF16))
            hg[1].append(jnp.where(low, pltpu.roll(hr, SSM_STATE, 1), hi).astype(BF16))
        for e in range(2):
            gi = 2 * j + e
            ys.append(jnp.dot(ut_ref[gi], wi_ref[gi], preferred_element_type=F32)
                      + jnp.dot(hg[e][0], wo_ref[0, gi], preferred_element_type=F32)
                      + jnp.dot(hg[e][1], wo_ref[1, gi], preferred_element_type=F32))
    for o in range(noct):
        for hf in range(2):
            toks = _atom_transpose([ys[OCT * o + gl][:, hf * LANES:(hf + 1) * LANES] for gl in range(OCT)])
            for tl in range(half):
                for bi in range(nb):
                    y_ref[o, bi, pl.ds(half * hf + tl, nc, stride=SSM_CHUNK), :] = toks[tl][bi * nc:(bi + 1) * nc]


def _ssm_conv(us_lat, us_ctx, weights, *, layer, b):
    w_intra, w_sin, w_so, al = weights
    perm = _atom_transpose_matrix()
    P = SSM_STATE
    ng = w_intra.shape[1]
    nblk = ng
    assert 8 % b == 0

    def stage_in(us, steps):
        noct, rows, _ = us.shape
        seq = rows // b
        tok = seq // steps
        n = b * tok // SSM_CHUNK
        nchunks = rows // SSM_CHUNK
        return pl.pallas_call(
            _ssm_in_kernel,
            grid=(steps,),
            in_specs=[pl.BlockSpec((noct, b, tok, LANES), lambda i: (0, 0, i, 0)),
                      _resident(perm.shape), _resident_layer(w_sin.shape, layer)],
            out_specs=[pl.BlockSpec((ng, n, SSM_CW), lambda i: (0, i, 0)),
                       pl.BlockSpec((2, nblk, n, LANES), lambda i: (0, 0, i, 0))],
            out_shape=[jax.ShapeDtypeStruct((ng, nchunks, SSM_CW), BF16),
                       jax.ShapeDtypeStruct((2, nblk, nchunks, LANES), F32)],
            compiler_params=_cparams(("arbitrary",)),
            name="ssm_in",
        )(us.reshape(noct, b, seq, LANES), perm, w_sin)

    lat_steps = 2 * b
    ut_lat, s_lat = stage_in(us_lat, lat_steps)
    ut_ctx, s_ctx = stage_in(us_ctx, 1)

    cb = 16
    nc_rows, nl_rows = s_ctx.shape[2], s_lat.shape[2]
    h_ctx, h_lat = pl.pallas_call(
        functools.partial(_ssm_scan_kernel, b=b),
        grid=(2, nblk // cb),
        in_specs=[pl.BlockSpec((1, cb, nc_rows, LANES), lambda d, j: (d, j, 0, 0)),
                  pl.BlockSpec((1, cb, nl_rows, LANES), lambda d, j: (d, j, 0, 0)),
                  pl.BlockSpec((None, 1, cb, 1, 2 * P), lambda d, j: (layer, d, j, 0, 0))],
        out_specs=[pl.BlockSpec((1, cb, nc_rows, LANES), lambda d, j: (d, j, 0, 0)),
                   pl.BlockSpec((1, cb, nl_rows, LANES), lambda d, j: (d, j, 0, 0))],
        out_shape=[jax.ShapeDtypeStruct(s_ctx.shape, F32), jax.ShapeDtypeStruct(s_lat.shape, F32)],
        compiler_params=_cparams(("arbitrary", "arbitrary")),
        name="ssm_scan",
    )(s_ctx, s_lat, al)

    def stage_out(ut, hin, steps):
        _, nchunks, _ = ut.shape
        n = nchunks // steps
        noct = ng // OCT
        seq = nchunks * SSM_CHUNK // b
        tok = seq // steps
        y = pl.pallas_call(
            _ssm_out_kernel,
            grid=(steps,),
            in_specs=[pl.BlockSpec((ng, n, SSM_CW), lambda i: (0, i, 0)),
                      pl.BlockSpec((2, nblk, n, LANES), lambda i: (0, 0, i, 0)),
                      _resident_layer(w_intra.shape, layer), _resident_layer(w_so.shape, layer)],
            out_specs=pl.BlockSpec((noct, b, tok, LANES), lambda i: (0, 0, i, 0)),
            out_shape=jax.ShapeDtypeStruct((noct, b, seq, LANES), F32),
            scratch_shapes=[pltpu.VMEM((ng, SSM_CW, SSM_CW), BF16)],
            compiler_params=_cparams(("arbitrary",)),
            name="ssm_out",
        )(ut, hin, w_intra, w_so)
        return y.reshape(noct, b * seq, LANES)

    return stage_out(ut_lat, h_lat, lat_steps), stage_out(ut_ctx, h_ctx, 1)


def _fourier_kernel(ulo_ref, uup_ref, w_ref, cos_ref, sin_ref, alt_ref, s1_ref, b_ref,
                    lo_ref, up_ref, a_s, b_s, am_s, carry, *, groups):
    gc = ulo_ref.shape[1] // groups
    tm = lo_ref.shape[0]
    nt = a_s.shape[0] // tm
    row = lax.broadcasted_iota(jnp.int32, lo_ref.shape, 0)

    @pl.when(pl.program_id(1) == 0)
    def _():
        for jj in range(nt):
            pr = jnp.dot(s1_ref[...], uup_ref[(nt - 1 - jj) * tm:(nt - jj) * tm, :], preferred_element_type=F32)
            if jj > 0:
                first = uup_ref[(nt - jj) * tm:(nt - jj) * tm + am_s.shape[0], :][0:1, :].astype(F32)
                pr = jnp.where(row == 0, first, pr)
            pr = pr.astype(BF16)
            rows = slice(jj * tm, (jj + 1) * tm)
            for g in range(groups):
                cols = slice(g * gc, (g + 1) * gc)
                ab_lo = jnp.dot(ulo_ref[rows, cols], w_ref[g], preferred_element_type=F32)
                ab_pr = jnp.dot(pr[:, cols], w_ref[g], preferred_element_type=F32)
                a_s[rows, cols] = (ab_lo[:, :gc] + ab_pr[:, :gc]).astype(BF16)
                b_s[rows, cols] = (ab_lo[:, gc:] - ab_pr[:, gc:]).astype(BF16)
        for g in range(groups):
            cols = slice(g * gc, (g + 1) * gc)
            am_s[:, cols] = jnp.dot(uup_ref[0:am_s.shape[0], cols], w_ref[g, :, 0:gc], preferred_element_type=F32)
        mid = jnp.dot(alt_ref[...], a_s[...], preferred_element_type=F32)
        carry[...] = mid + am_s[0:carry.shape[0], :] + b_ref[...]

    a_mid = jnp.where((row & 1) == 0, am_s[0:1, :], -am_s[0:1, :])
    p = jnp.dot(cos_ref[...], a_s[...], preferred_element_type=F32) + a_mid + b_ref[...]
    q = jnp.dot(sin_ref[...], b_s[...], preferred_element_type=F32)
    lo_ref[...] = (p - q).astype(BF16)
    hi = (p + q).astype(BF16)
    rev = jnp.dot(s1_ref[...], hi, preferred_element_type=F32)
    up_ref[...] = jnp.where(row == 0, carry[0:1, :], rev).astype(BF16)
    carry[0:1, :] = hi[0:1, :].astype(F32)


def _dft_half_tables(n):
    half = n // 2
    k = np.arange(half, dtype=np.int64)
    ang = ((k[:, None] * k[None, :]) % n).astype(np.float64) * (2.0 * np.pi / n)
    alt = np.broadcast_to(np.where((k & 1) == 0, 1.0, -1.0), (8, half))
    return tuple(jnp.asarray(a, F32).astype(BF16) for a in (np.cos(ang), np.sin(ang), alt))


def _fourier(uf, w_cs, bias, tables, *, b, tm):
    nr, fw = uf.shape
    seq = nr // b
    half = seq // 2
    groups = w_cs.shape[0]
    cos_t, sin_t, alt = tables
    tm = min(tm, half)
    nt = half // tm
    r = np.arange(1, tm)
    s1 = np.zeros((tm, tm), np.float32)
    s1[r, tm - r] = 1.0
    s1 = jnp.asarray(s1, BF16)
    lo, up = pl.pallas_call(
        functools.partial(_fourier_kernel, groups=groups),
        grid=(b, nt),
        in_specs=[
            pl.BlockSpec((half, fw), lambda bi, i: (2 * bi, 0)),
            pl.BlockSpec((half, fw), lambda bi, i: (2 * bi + 1, 0)),
            pl.BlockSpec(w_cs.shape, lambda bi, i: (0, 0, 0)),
            pl.BlockSpec((tm, half), lambda bi, i: (nt - 1 - i, 0)),
            pl.BlockSpec((tm, half), lambda bi, i: (nt - 1 - i, 0)),
            pl.BlockSpec(alt.shape, lambda bi, i: (0, 0)),
            pl.BlockSpec((tm, tm), lambda bi, i: (0, 0)),
            pl.BlockSpec((1, fw), lambda bi, i: (0, 0)),
        ],
        out_specs=[pl.BlockSpec((tm, fw), lambda bi, i: (bi * nt + nt - 1 - i, 0)),
                   pl.BlockSpec((tm, fw), lambda bi, i: (bi * nt + i, 0))],
        out_shape=[jax.ShapeDtypeStruct((b * half, fw), BF16), jax.ShapeDtypeStruct((b * half, fw), BF16)],
        scratch_shapes=[pltpu.VMEM((half, fw), BF16), pltpu.VMEM((half, fw), BF16),
                        pltpu.VMEM((16, fw), F32), pltpu.VMEM((alt.shape[0], fw), F32)],
        compiler_params=_cparams(("arbitrary", "arbitrary")),
        name="fourier_%d" % seq,
    )(uf, uf, w_cs, cos_t, sin_t, alt, s1, bias)
    return lo, up


def _fourier_weights(w_four, seq):
    groups, gc, _ = w_four.shape
    k = np.arange(gc)
    ang = ((k[:, None] * k[None, :]) % gc).astype(np.float64) * (2.0 * np.pi / gc)
    norm = 1.0 / math.sqrt(seq * gc)
    cc = jnp.asarray(np.cos(ang) * norm, F32)
    sc = jnp.asarray(np.sin(ang) * norm, F32)
    hp = lax.Precision.HIGHEST
    wc = jnp.einsum('ck,gkd->gcd', cc, w_four.astype(F32), precision=hp)
    ws = jnp.einsum('ck,gkd->gcd', sc, w_four.astype(F32), precision=hp)
    return jnp.concatenate([wc, ws], axis=-1).astype(BF16)


def _outproj_kernel(att_ref, us_ref, yc_ref, x_ref, mod_ref, wo_ref, wg_ref, bg_ref, dsk_ref,
                    gpost_ref, gpre_ref, *refs, aw, sw, four_part):
    uf_refs, (xo_ref, h2_ref) = refs[:-2], refs[-2:]
    part = four_part(pl.program_id(0))
    nblk = us_ref.shape[0]
    tm = x_ref.shape[0]
    gate_g = mod_ref[0, 2:3, :] * gpost_ref[...]
    scale_g = gpre_ref[...] * (1.0 + mod_ref[0, 4:5, :])
    for r0 in range(0, tm, tm // 2):
        rows = slice(r0, r0 + tm // 2)
        us = jnp.concatenate([us_ref[j, rows, :] for j in range(nblk)], axis=1)
        yc = jnp.concatenate([yc_ref[j, rows, :] for j in range(nblk)], axis=1)
        g = jax.nn.gelu(dsk_ref[...] * us + yc)
        z = jnp.dot(g.astype(BF16), wg_ref[...], preferred_element_type=F32) + bg_ref[...]
        ssm = (g * jax.nn.sigmoid(z)).astype(BF16)
        uf = uf_refs[0][rows, :]
        for pi in range(1, len(uf_refs)):
            uf = jnp.where(part == pi, uf_refs[pi][rows, :], uf)
        mix = (jnp.dot(att_ref[rows, :], wo_ref[0:aw, :], preferred_element_type=F32)
               + jnp.dot(ssm, wo_ref[aw:aw + sw, :], preferred_element_type=F32)
               + jnp.dot(uf, wo_ref[aw + sw:, :], preferred_element_type=F32))
        xn = x_ref[rows, :] + _rms(mix, NORM_EPS) * gate_g
        xo_ref[rows, :] = xn
        h2_ref[rows, :] = (_rms(xn, NORM_EPS) * scale_g + mod_ref[0, 3:4, :]).astype(BF16)


def _outproj(att, us, yc, four_parts, x, mod, w_out, w_glu, b_glu, dsk, g_post, g_pre, *, layer, mod_row, tm,
             four_tile=None):
    n_rows, d = x.shape
    aw, fw = att.shape[1], four_parts[0].shape[1]
    if four_tile is None:
        four_tile = lambda i: (0, i)
    nblk = us.shape[0]
    sw = nblk * LANES

    def row(i):
        return (i, 0)

    def const(i):
        return (0, 0)

    blk3 = pl.BlockSpec((nblk, tm, LANES), lambda i: (0, i, 0))
    def part_spec(pi):
        nblocks = four_parts[pi].shape[0] // tm
        return pl.BlockSpec((tm, fw), lambda i: (jnp.where(four_tile(i)[0] == pi,
                                                           jnp.clip(four_tile(i)[1], 0, nblocks - 1), 0), 0))

    return pl.pallas_call(
        functools.partial(_outproj_kernel, aw=aw, sw=sw, four_part=lambda i: four_tile(i)[0]),
        grid=(n_rows // tm,),
        in_specs=[
            pl.BlockSpec((tm, aw), row), blk3, blk3,
            pl.BlockSpec((tm, d), row),
            pl.BlockSpec((1, N_MOD, d), lambda i: (mod_row(i), 0, 0)),
            _resident(w_out.shape), _resident_layer(w_glu.shape, layer),
            pl.BlockSpec((1, sw), const), pl.BlockSpec((1, sw), const),
            pl.BlockSpec((1, d), const), pl.BlockSpec((1, d), const),
        ] + [part_spec(pi) for pi in range(len(four_parts))],
        out_specs=[pl.BlockSpec((tm, d), row), pl.BlockSpec((tm, d), row)],
        out_shape=[jax.ShapeDtypeStruct((n_rows, d), F32), jax.ShapeDtypeStruct((n_rows, d), BF16)],
        compiler_params=_cparams(("parallel",)),
        name="outproj",
    )(att, us, yc, x, mod, w_out, w_glu, b_glu, dsk, g_post, g_pre, *four_parts)


def _ffn_kernel(h_ref, x_ref, mod_ref, g_ref, wg_ref, wu_ref, wd_ref, o_ref):
    k = pl.program_id(1)

    @pl.when(k == 0)
    def _():
        o_ref[...] = jnp.zeros(o_ref.shape, F32)

    h = h_ref[...]
    tf = wg_ref.shape[1]
    part = None
    for c0 in range(0, tf, tf // 2):
        cols = slice(c0, c0 + tf // 2)
        a = jnp.dot(h, wg_ref[:, cols], preferred_element_type=F32)
        u = jnp.dot(h, wu_ref[:, cols], preferred_element_type=F32)
        p = jnp.dot((_silu(a) * u).astype(BF16), wd_ref[cols, :], preferred_element_type=F32)
        part = p if part is None else part + p
    o_ref[...] += part

    @pl.when(k == pl.num_programs(1) - 1)
    def _():
        o_ref[...] = x_ref[...] + _rms(o_ref[...], NORM_EPS) * (mod_ref[0, 5:6, :] * g_ref[...])


def _ffn(h2, x_mid, mod, g_post, w_gate, w_up, w_down, *, mod_row, tm, tf):
    n_rows, d = h2.shape
    dff = w_gate.shape[1]
    return pl.pallas_call(
        _ffn_kernel,
        grid=(n_rows // tm, dff // tf),
        in_specs=[
            pl.BlockSpec((tm, d), lambda i, k: (i, 0)),
            pl.BlockSpec((tm, d), lambda i, k: (i, 0)),
            pl.BlockSpec((1, N_MOD, d), lambda i, k: (mod_row(i), 0, 0)),
            pl.BlockSpec((1, d), lambda i, k: (0, 0)),
            pl.BlockSpec((d, tf), lambda i, k: (0, k)),
            pl.BlockSpec((d, tf), lambda i, k: (0, k)),
            pl.BlockSpec((tf, d), lambda i, k: (k, 0)),
        ],
        out_specs=pl.BlockSpec((tm, d), lambda i, k: (i, 0)),
        out_shape=jax.ShapeDtypeStruct((n_rows, d), F32),
        compiler_params=_cparams(("parallel", "arbitrary")),
        name="ffn",
    )(h2, x_mid, mod, g_post, w_gate, w_up, w_down)


def kernel(x, c, ctx, c_ctx, w_mod, b_mod, g_mix_pre, g_mix_post, g_ffn_pre, g_ffn_post, w_in, w_out, lam_q1, lam_k1, lam_q2, lam_k2, g_subln, ssm_a_re, ssm_a_im, ssm_log_dt, ssm_b_re, ssm_b_im, ssm_c_re, ssm_c_im, ssm_d, w_glu, b_glu, w_four, b_four, w_gate, w_up, w_down):
    b, t, d = x.shape
    n_ctx = ctx.shape[1]
    depth = w_mod.shape[0]
    nl, nc = b * t, b * n_ctx
    aw = d // 2
    sw = ssm_d.shape[1]
    fw = d - aw - sw
    heads = aw // (2 * DA_HEAD_DIM)
    tm = 512
    assert t % tm == 0 and nc % tm == 0 and n_ctx % SSM_CHUNK == 0 and t % GRID_W == 0 and b + 1 <= MOD_ROWS
    assert sw % LANES == 0 and (sw // SSM_GROUP) % (2 * OCT) == 0

    cs = jnp.concatenate([c, c_ctx[None, :], jnp.zeros((MOD_ROWS - b - 1, d), F32)], axis=0)
    mod_all = _modulation(cs, w_mod, b_mod).reshape(depth, MOD_ROWS, N_MOD, d)

    per_b = t // tm
    lat_mod = lambda i: i // per_b
    ctx_mod = lambda i: b
    tabs = _rope_tables(t, tm)
    dft_lat = _dft_half_tables(t)
    dft_ctx = _dft_half_tables(n_ctx)
    x_lat, x_ctx = x.reshape(nl, d), ctx.reshape(nc, d)
    w_glu_b = w_glu.astype(BF16)
    w_in_b = w_in[0].astype(BF16)
    ssm_w = jax.vmap(_ssm_weights)(ssm_a_re, ssm_a_im, ssm_log_dt, ssm_b_re, ssm_b_im, ssm_c_re, ssm_c_im)

    for l in range(depth):
        need_ctx = l < depth - 1
        lam_init = 0.8 - 0.6 * math.exp(-0.3 * l)
        mod = mod_all[l]
        g_pre = g_mix_pre[l][None, :]
        casts = [(w_out, l)] + ([(w_in, l + 1)] if l + 1 < depth else [])
        q, k, vt, us, uf, w_out_b, *w_in_next = _inproj(x_lat, mod, g_pre, w_in_b, tabs, mod_row=lat_mod,
                                                        tab_blk=lambda i: i % per_b, aw=aw, sw=sw, fw=fw, tm=tm,
                                                        cast=casts)
        qc, kc, vtc, usc, ufc = _inproj(x_ctx, mod, g_pre, w_in_b, tabs, mod_row=ctx_mod,
                                        tab_blk=lambda i: per_b, aw=aw, sw=sw, fw=fw, tm=tm)
        w_in_b = w_in_next[0] if w_in_next else None

        lam4 = jnp.stack([lam_q1[l], lam_k1[l], lam_q2[l], lam_k2[l]]).astype(F32)
        gs = g_subln[l][None, :].astype(F32)
        att, *ffn_wb = _attention(lam4, gs, q, [(kc, vtc), (k, vt)], lam_init=lam_init, b=b, heads=heads, tq=1024,
                                  name="attn_latent", cast=(l, w_gate, w_up, w_down))
        yc, ycc = _ssm_conv(us, usc, ssm_w, layer=l, b=b)
        bias = b_four[l].reshape(1, fw).astype(F32)
        four = _fourier(uf, _fourier_weights(w_four[l], t), bias, dft_lat, b=b, tm=tm)

        small = (b_glu[l][None, :].astype(F32), ssm_d[l][None, :].astype(F32), g_mix_post[l][None, :],
                 g_ffn_pre[l][None, :])
        ffn_w = (g_ffn_post[l][None, :], *ffn_wb)
        half_t = per_b // 2

        def four_tile(i):
            pos = i % per_b
            return pos // half_t, (i // per_b) * half_t + pos % half_t

        x_mid, h2 = _outproj(att, us, yc, four, x_lat, mod, w_out_b, w_glu_b, *small, layer=l, mod_row=lat_mod,
                             tm=tm, four_tile=four_tile)
        x_lat = _ffn(h2, x_mid, mod, *ffn_w, mod_row=lat_mod, tm=tm, tf=512)

        if need_ctx:
            att_c, = _attention(lam4, gs, qc, [(kc, vtc)], lam_init=lam_init, b=b, heads=heads, tq=n_ctx,
                                name="attn_ctx")
            lo_c, up_c = _fourier(ufc, _fourier_weights(w_four[l], n_ctx), bias, dft_ctx, b=b, tm=tm)
            four_c = [jnp.concatenate([lo_c.reshape(b, -1, fw), up_c.reshape(b, -1, fw)], axis=1).reshape(nc, fw)]
            xc_mid, h2c = _outproj(att_c, usc, ycc, four_c, x_ctx, mod, w_out_b, w_glu_b, *small, layer=l,
                                   mod_row=ctx_mod, tm=tm)
            x_ctx = _ffn(h2c, xc_mid, mod, *ffn_w, mod_row=ctx_mod, tm=tm, tf=512)
    return x_lat.reshape(b, t, d)
```
